```python
import math
import jax, jax.numpy as jnp
from jax import lax
import numpy as np

D_MODEL = 1024
BATCH = 8
SEQ = 4096
DEPTH = 1

N_META = 16
CONV_WIDTH = 3
D_CONV = D_MODEL
CONV_GROUPS = 16
SB_HEAD_DIM = 64
SB_HEADS = 16
D_ATTN = SB_HEADS * SB_HEAD_DIM
N_BRANCH = 2
D_IN = 3 * D_CONV + 3 * D_ATTN + N_BRANCH * D_MODEL
D_FF = 2816
Q_BLOCK = 128
RMS_EPS = 1e-6

kernel_name = "hybrid_shortconv_stickbreaking_convffn_block"


def rms_norm(x, g):
    xf = x.astype(jnp.float32)
    y = xf * lax.rsqrt(jnp.mean(xf * xf, axis=-1, keepdims=True) + RMS_EPS)
    return (y * g.astype(jnp.float32)).astype(x.dtype)


def causal_dwconv(x, w):
    k, c = w.shape
    return lax.conv_general_dilated(
        x, w.reshape(k, 1, c).astype(x.dtype),
        window_strides=(1,), padding=[(k - 1, 0)],
        dimension_numbers=("NWC", "WIO", "NWC"),
        feature_group_count=c)


def stick_breaking_attention(q, k, v):
    seq_len = q.shape[1]
    scale = 1.0 / math.sqrt(q.shape[-1])
    bounds = [(0, N_META)] + [(s, min(s + Q_BLOCK, seq_len))
                              for s in range(N_META, seq_len, Q_BLOCK)]
    outs = []
    for q0, q1 in bounds:
        qb = q[:, q0:q1]
        kb = k[:, :q1]
        vb = v[:, :q1].astype(jnp.float32)
        z = jnp.einsum("bqhd,bkhd->bhqk", qb, kb,
                       preferred_element_type=jnp.float32) * scale
        t_idx = jnp.arange(q0, q1)[:, None]
        s_idx = jnp.arange(q1)[None, :]
        causal = s_idx < t_idx
        log_beta = jax.nn.log_sigmoid(z)
        log_fail = jnp.where(causal, log_beta - z, 0.0)
        survive = lax.cumsum(log_fail, axis=3, reverse=True) - log_fail
        a = jnp.where(causal, jnp.exp(log_beta + survive), 0.0)
        o = jnp.einsum("bhqk,bkhd->bqhd", a, vb)
        outs.append(o.astype(v.dtype))
    return jnp.concatenate(outs, axis=1)


def hybrid_mixer(xn, w_in, conv_w_mix, w_proj_conv, w_proj_attn, b_gate, w_out):
    bsz, seq_len, _ = xn.shape
    h = xn @ w_in
    splits = np.cumsum([D_CONV, D_CONV, D_CONV, D_ATTN, D_ATTN, D_ATTN, D_MODEL])
    b_c, c_c, h_c, q, k, v, g_conv, g_attn = jnp.split(h, list(splits), axis=-1)

    y_conv = b_c * causal_dwconv(c_c * h_c, conv_w_mix)
    branch_conv = y_conv @ w_proj_conv

    hd = (bsz, seq_len, SB_HEADS, SB_HEAD_DIM)
    o = stick_breaking_attention(q.reshape(hd), k.reshape(hd), v.reshape(hd))
    branch_attn = o.reshape(bsz, seq_len, D_ATTN) @ w_proj_attn

    gate_conv = jax.nn.sigmoid(g_conv + b_gate[0])
    gate_attn = jax.nn.sigmoid(g_attn + b_gate[1])
    merged = gate_conv * branch_conv + gate_attn * branch_attn
    return merged @ w_out


def conv_gated_mlp(xn, w_up_gate, conv_w_ffn, w_down):
    u, g = jnp.split(xn @ w_up_gate, 2, axis=-1)
    u = causal_dwconv(u, conv_w_ffn)
    return (jax.nn.gelu(u) * g) @ w_down


def _fwd_setup_inputs(seed: int = 0) -> dict:
    key = jax.random.key(seed)
    ks = jax.random.split(key, 16)
    f32 = jnp.float32

    def normal(k, shape, scale):
        return jax.random.normal(k, shape, f32) * scale

    def gain(k):
        return 1.0 + normal(k, (DEPTH, D_MODEL), 0.05)

    return {
        "x": normal(ks[0], (BATCH, SEQ, D_MODEL), 1.0),
        "meta_tokens": normal(ks[1], (N_META, D_MODEL), 1.0),
        "g_pre_mix": gain(ks[2]),
        "w_in": normal(ks[3], (DEPTH, D_MODEL, D_IN), D_MODEL ** -0.5),
        "conv_w_mix": normal(ks[4], (DEPTH, CONV_WIDTH, D_CONV), CONV_WIDTH ** -0.5),
        "w_proj_conv": normal(ks[5], (DEPTH, D_CONV, D_MODEL), D_CONV ** -0.5),
        "w_proj_attn": normal(ks[6], (DEPTH, D_ATTN, D_MODEL), D_ATTN ** -0.5),
        "b_gate": normal(ks[7], (DEPTH, N_BRANCH, D_MODEL), 0.02),
        "w_out": normal(ks[8], (DEPTH, D_MODEL, D_MODEL), D_MODEL ** -0.5),
        "g_post_mix": gain(ks[9]),
        "g_pre_ffn": gain(ks[10]),
        "w_up_gate": normal(ks[11], (DEPTH, D_MODEL, 2 * D_FF), D_MODEL ** -0.5),
        "conv_w_ffn": normal(ks[12], (DEPTH, CONV_WIDTH, D_FF), CONV_WIDTH ** -0.5),
        "w_down": normal(ks[13], (DEPTH, D_FF, D_MODEL), D_FF ** -0.5),
        "g_post_ffn": gain(ks[14]),
    }


def _fwd_reference(x, meta_tokens, g_pre_mix, w_in, conv_w_mix, w_proj_conv, w_proj_attn,
              b_gate, w_out, g_post_mix, g_pre_ffn, w_up_gate, conv_w_ffn, w_down,
              g_post_ffn):
    bsz = x.shape[0]
    meta = jnp.broadcast_to(meta_tokens[None].astype(x.dtype), (bsz, N_META, D_MODEL))
    h = jnp.concatenate([meta, x], axis=1)
    for layer in range(DEPTH):
        mix = hybrid_mixer(rms_norm(h, g_pre_mix[layer]), w_in[layer], conv_w_mix[layer],
                           w_proj_conv[layer], w_proj_attn[layer], b_gate[layer],
                           w_out[layer])
        h = h + rms_norm(mix, g_post_mix[layer])
        ffn = conv_gated_mlp(rms_norm(h, g_pre_ffn[layer]), w_up_gate[layer],
                             conv_w_ffn[layer], w_down[layer])
        h = h + rms_norm(ffn, g_post_ffn[layer])
    return h[:, N_META:, :]


import jax as _jax
import jax.numpy as _jnp

TWIN_FORMAT = 'train_step'
FWD_PARAMS = ['x', 'meta_tokens', 'g_pre_mix', 'w_in', 'conv_w_mix', 'w_proj_conv', 'w_proj_attn', 'b_gate', 'w_out', 'g_post_mix', 'g_pre_ffn', 'w_up_gate', 'conv_w_ffn', 'w_down', 'g_post_ffn']
TWIN_WEIGHTS = ['meta_tokens', 'g_pre_mix', 'w_in', 'conv_w_mix', 'w_proj_conv', 'w_proj_attn', 'b_gate', 'w_out', 'g_post_mix', 'g_pre_ffn', 'w_up_gate', 'conv_w_ffn', 'w_down', 'g_post_ffn']
TWIN_DIFF_INPUT = 'x'
TWIN_INPUTS = ['x', 'meta_tokens', 'g_pre_mix', 'w_in', 'conv_w_mix', 'w_proj_conv', 'w_proj_attn', 'b_gate', 'w_out', 'g_post_mix', 'g_pre_ffn', 'w_up_gate', 'conv_w_ffn', 'w_down', 'g_post_ffn', 'loss_target', 'm_meta_tokens', 'm_g_pre_mix', 'm_w_in', 'm_conv_w_mix', 'm_w_proj_conv', 'm_w_proj_attn', 'm_b_gate', 'm_w_out', 'm_g_post_mix', 'm_g_pre_ffn', 'm_w_up_gate', 'm_conv_w_ffn', 'm_w_down', 'm_g_post_ffn', 'v_meta_tokens', 'v_g_pre_mix', 'v_w_in', 'v_conv_w_mix', 'v_w_proj_conv', 'v_w_proj_attn', 'v_b_gate', 'v_w_out', 'v_g_post_mix', 'v_g_pre_ffn', 'v_w_up_gate', 'v_conv_w_ffn', 'v_w_down', 'v_g_post_ffn']
TWIN_OUTPUTS = ['loss', 'grad_x', 'grad_meta_tokens', 'grad_g_pre_mix', 'grad_w_in', 'grad_conv_w_mix', 'grad_w_proj_conv', 'grad_w_proj_attn', 'grad_b_gate', 'grad_w_out', 'grad_g_post_mix', 'grad_g_pre_ffn', 'grad_w_up_gate', 'grad_conv_w_ffn', 'grad_w_down', 'grad_g_post_ffn', 'delta_meta_tokens', 'delta_g_pre_mix', 'delta_w_in', 'delta_conv_w_mix', 'delta_w_proj_conv', 'delta_w_proj_attn', 'delta_b_gate', 'delta_w_out', 'delta_g_post_mix', 'delta_g_pre_ffn', 'delta_w_up_gate', 'delta_conv_w_ffn', 'delta_w_down', 'delta_g_post_ffn', 'new_m_meta_tokens', 'new_m_g_pre_mix', 'new_m_w_in', 'new_m_conv_w_mix', 'new_m_w_proj_conv', 'new_m_w_proj_attn', 'new_m_b_gate', 'new_m_w_out', 'new_m_g_post_mix', 'new_m_g_pre_ffn', 'new_m_w_up_gate', 'new_m_conv_w_ffn', 'new_m_w_down', 'new_m_g_post_ffn', 'new_v_meta_tokens', 'new_v_g_pre_mix', 'new_v_w_in', 'new_v_conv_w_mix', 'new_v_w_proj_conv', 'new_v_w_proj_attn', 'new_v_b_gate', 'new_v_w_out', 'new_v_g_post_mix', 'new_v_g_pre_ffn', 'new_v_w_up_gate', 'new_v_conv_w_ffn', 'new_v_w_down', 'new_v_g_post_ffn']
TWIN_LEAF_KINDS = {'loss': 'loss', 'grad_x': 'grad_x', 'grad_meta_tokens': 'grad_w', 'grad_g_pre_mix': 'grad_w', 'grad_w_in': 'grad_w', 'grad_conv_w_mix': 'grad_w', 'grad_w_proj_conv': 'grad_w', 'grad_w_proj_attn': 'grad_w', 'grad_b_gate': 'grad_w', 'grad_w_out': 'grad_w', 'grad_g_post_mix': 'grad_w', 'grad_g_pre_ffn': 'grad_w', 'grad_w_up_gate': 'grad_w', 'grad_conv_w_ffn': 'grad_w', 'grad_w_down': 'grad_w', 'grad_g_post_ffn': 'grad_w', 'delta_meta_tokens': 'delta_w', 'delta_g_pre_mix': 'delta_w', 'delta_w_in': 'delta_w', 'delta_conv_w_mix': 'delta_w', 'delta_w_proj_conv': 'delta_w', 'delta_w_proj_attn': 'delta_w', 'delta_b_gate': 'delta_w', 'delta_w_out': 'delta_w', 'delta_g_post_mix': 'delta_w', 'delta_g_pre_ffn': 'delta_w', 'delta_w_up_gate': 'delta_w', 'delta_conv_w_ffn': 'delta_w', 'delta_w_down': 'delta_w', 'delta_g_post_ffn': 'delta_w', 'new_m_meta_tokens': 'new_m', 'new_m_g_pre_mix': 'new_m', 'new_m_w_in': 'new_m', 'new_m_conv_w_mix': 'new_m', 'new_m_w_proj_conv': 'new_m', 'new_m_w_proj_attn': 'new_m', 'new_m_b_gate': 'new_m', 'new_m_w_out': 'new_m', 'new_m_g_post_mix': 'new_m', 'new_m_g_pre_ffn': 'new_m', 'new_m_w_up_gate': 'new_m', 'new_m_conv_w_ffn': 'new_m', 'new_m_w_down': 'new_m', 'new_m_g_post_ffn': 'new_m', 'new_v_meta_tokens': 'new_v', 'new_v_g_pre_mix': 'new_v', 'new_v_w_in': 'new_v', 'new_v_conv_w_mix': 'new_v', 'new_v_w_proj_conv': 'new_v', 'new_v_w_proj_attn': 'new_v', 'new_v_b_gate': 'new_v', 'new_v_w_out': 'new_v', 'new_v_g_post_mix': 'new_v', 'new_v_g_pre_ffn': 'new_v', 'new_v_w_up_gate': 'new_v', 'new_v_conv_w_ffn': 'new_v', 'new_v_w_down': 'new_v', 'new_v_g_post_ffn': 'new_v'}


def _forward(args):
    return _fwd_reference(*[args[k] for k in FWD_PARAMS])


def _output_shape():
    out = _jax.eval_shape(lambda: _forward(_fwd_setup_inputs(0)))
    return out.shape, out.dtype

N_MICROBATCH = 1
ADAM_LR = 0.001
ADAM_B1 = 0.9
ADAM_B2 = 0.999
ADAM_EPS = 1e-08
ADAM_WD = 0.01
ADAM_STEP = 10
PER_EXAMPLE_BATCH_AXIS = {'x': 0, 'loss_target': 0}
SHARED_INPUTS = []
_WEIGHT_DTYPES = {'meta_tokens': _jnp.float32, 'g_pre_mix': _jnp.float32, 'w_in': _jnp.float32, 'conv_w_mix': _jnp.float32, 'w_proj_conv': _jnp.float32, 'w_proj_attn': _jnp.float32, 'b_gate': _jnp.float32, 'w_out': _jnp.float32, 'g_post_mix': _jnp.float32, 'g_pre_ffn': _jnp.float32, 'w_up_gate': _jnp.float32, 'conv_w_ffn': _jnp.float32, 'w_down': _jnp.float32, 'g_post_ffn': _jnp.float32}
MOMENT_SCALE = {'meta_tokens': 1.125977e-02, 'g_pre_mix': 7.619618e-01, 'w_in': 2.570555e-01, 'conv_w_mix': 4.012394e-01, 'w_proj_conv': 4.144841e-01, 'w_proj_attn': 2.547495e-01, 'b_gate': 1.508470e-01, 'w_out': 5.543618e-01, 'g_post_mix': 3.193633e+01, 'g_pre_ffn': 6.501605e-01, 'w_up_gate': 2.653493e-01, 'conv_w_ffn': 2.796317e-01, 'w_down': 5.595283e-01, 'g_post_ffn': 3.210369e+01}


def _to_microbatches(a, axis):
    t = _jnp.moveaxis(a, axis, 0)
    t = t.reshape((N_MICROBATCH, t.shape[0] // N_MICROBATCH) + t.shape[1:])
    return _jnp.moveaxis(t, 1, axis + 1)


def setup_inputs(seed: int = 0) -> dict:
    inp = _fwd_setup_inputs(seed)
    key = _jax.random.fold_in(_jax.random.key(seed), 7919)
    shape, _ = _output_shape()
    out = dict(inp)
    out["loss_target"] = _jax.random.normal(_jax.random.fold_in(key, 0), shape, _jnp.float32)
    for i, name in enumerate(TWIN_WEIGHTS):
        w = inp[name].astype(_jnp.float32)
        if MOMENT_SCALE is None:
            s = _jnp.sqrt(_jnp.mean(_jnp.square(w)) + 1e-30)
        else:
            s = MOMENT_SCALE[name]
        km, kv = _jax.random.split(_jax.random.fold_in(key, i + 1))
        out[name] = w
        out["m_" + name] = s * _jax.random.normal(km, w.shape, _jnp.float32)
        out["v_" + name] = (s * s) * _jax.random.uniform(kv, w.shape, _jnp.float32, 0.5, 1.5)
    if N_MICROBATCH > 1:
        for name, axis in PER_EXAMPLE_BATCH_AXIS.items():
            out[name] = _to_microbatches(out[name], axis)
    return {'x': out['x'], 'meta_tokens': out['meta_tokens'], 'g_pre_mix': out['g_pre_mix'], 'w_in': out['w_in'], 'conv_w_mix': out['conv_w_mix'], 'w_proj_conv': out['w_proj_conv'], 'w_proj_attn': out['w_proj_attn'], 'b_gate': out['b_gate'], 'w_out': out['w_out'], 'g_post_mix': out['g_post_mix'], 'g_pre_ffn': out['g_pre_ffn'], 'w_up_gate': out['w_up_gate'], 'conv_w_ffn': out['conv_w_ffn'], 'w_down': out['w_down'], 'g_post_ffn': out['g_post_ffn'], 'loss_target': out['loss_target'], 'm_meta_tokens': out['m_meta_tokens'], 'm_g_pre_mix': out['m_g_pre_mix'], 'm_w_in': out['m_w_in'], 'm_conv_w_mix': out['m_conv_w_mix'], 'm_w_proj_conv': out['m_w_proj_conv'], 'm_w_proj_attn': out['m_w_proj_attn'], 'm_b_gate': out['m_b_gate'], 'm_w_out': out['m_w_out'], 'm_g_post_mix': out['m_g_post_mix'], 'm_g_pre_ffn': out['m_g_pre_ffn'], 'm_w_up_gate': out['m_w_up_gate'], 'm_conv_w_ffn': out['m_conv_w_ffn'], 'm_w_down': out['m_w_down'], 'm_g_post_ffn': out['m_g_post_ffn'], 'v_meta_tokens': out['v_meta_tokens'], 'v_g_pre_mix': out['v_g_pre_mix'], 'v_w_in': out['v_w_in'], 'v_conv_w_mix': out['v_conv_w_mix'], 'v_w_proj_conv': out['v_w_proj_conv'], 'v_w_proj_attn': out['v_w_proj_attn'], 'v_b_gate': out['v_b_gate'], 'v_w_out': out['v_w_out'], 'v_g_post_mix': out['v_g_post_mix'], 'v_g_pre_ffn': out['v_g_pre_ffn'], 'v_w_up_gate': out['v_w_up_gate'], 'v_conv_w_ffn': out['v_conv_w_ffn'], 'v_w_down': out['v_w_down'], 'v_g_post_ffn': out['v_g_post_ffn']}


def _loss(weights, diff, rest, loss_target):
    with _jax.named_scope("forward"):
        args = {**rest, TWIN_DIFF_INPUT: diff, **{k: w.astype(_WEIGHT_DTYPES[k]) for k, w in weights.items()}}
        y = _forward(args)
    with _jax.named_scope("loss_head"):
        err = _jnp.square(y.astype(_jnp.float32) - loss_target)
        return 0.5 * _jnp.sum(_jnp.mean(err, axis=-1)) if err.ndim else 0.5 * err


def _adamw(w, g, m, v):
    m = ADAM_B1 * m + (1.0 - ADAM_B1) * g
    v = ADAM_B2 * v + (1.0 - ADAM_B2) * _jnp.square(g)
    m_hat = m / (1.0 - ADAM_B1 ** ADAM_STEP)
    v_hat = v / (1.0 - ADAM_B2 ** ADAM_STEP)
    delta = -ADAM_LR * (m_hat / (_jnp.sqrt(v_hat) + ADAM_EPS) + ADAM_WD * w)
    return delta, m, v


def reference(x, meta_tokens, g_pre_mix, w_in, conv_w_mix, w_proj_conv, w_proj_attn, b_gate, w_out, g_post_mix, g_pre_ffn, w_up_gate, conv_w_ffn, w_down, g_post_ffn, loss_target, m_meta_tokens, m_g_pre_mix, m_w_in, m_conv_w_mix, m_w_proj_conv, m_w_proj_attn, m_b_gate, m_w_out, m_g_post_mix, m_g_pre_ffn, m_w_up_gate, m_conv_w_ffn, m_w_down, m_g_post_ffn, v_meta_tokens, v_g_pre_mix, v_w_in, v_conv_w_mix, v_w_proj_conv, v_w_proj_attn, v_b_gate, v_w_out, v_g_post_mix, v_g_pre_ffn, v_w_up_gate, v_conv_w_ffn, v_w_down, v_g_post_ffn):
    given = dict(x=x, meta_tokens=meta_tokens, g_pre_mix=g_pre_mix, w_in=w_in, conv_w_mix=conv_w_mix, w_proj_conv=w_proj_conv, w_proj_attn=w_proj_attn, b_gate=b_gate, w_out=w_out, g_post_mix=g_post_mix, g_pre_ffn=g_pre_ffn, w_up_gate=w_up_gate, conv_w_ffn=conv_w_ffn, w_down=w_down, g_post_ffn=g_post_ffn, loss_target=loss_target, m_meta_tokens=m_meta_tokens, m_g_pre_mix=m_g_pre_mix, m_w_in=m_w_in, m_conv_w_mix=m_conv_w_mix, m_w_proj_conv=m_w_proj_conv, m_w_proj_attn=m_w_proj_attn, m_b_gate=m_b_gate, m_w_out=m_w_out, m_g_post_mix=m_g_post_mix, m_g_pre_ffn=m_g_pre_ffn, m_w_up_gate=m_w_up_gate, m_conv_w_ffn=m_conv_w_ffn, m_w_down=m_w_down, m_g_post_ffn=m_g_post_ffn, v_meta_tokens=v_meta_tokens, v_g_pre_mix=v_g_pre_mix, v_w_in=v_w_in, v_conv_w_mix=v_conv_w_mix, v_w_proj_conv=v_w_proj_conv, v_w_proj_attn=v_w_proj_attn, v_b_gate=v_b_gate, v_w_out=v_w_out, v_g_post_mix=v_g_post_mix, v_g_pre_ffn=v_g_pre_ffn, v_w_up_gate=v_w_up_gate, v_conv_w_ffn=v_conv_w_ffn, v_w_down=v_w_down, v_g_post_ffn=v_g_post_ffn)
    weights = {n: given[n] for n in TWIN_WEIGHTS}
    shared = {n: given[n] for n in SHARED_INPUTS}
    per_example = {n: given[n] for n in ['x']}
    grad_fn = _jax.value_and_grad(_loss, argnums=(0, 1))

    def one_microbatch(ex, loss_target):
        ex = dict(ex)
        diff = ex.pop(TWIN_DIFF_INPUT)
        return grad_fn(weights, diff, {**shared, **ex}, loss_target)

    if N_MICROBATCH == 1:
        loss, (grad_w, grad_x) = one_microbatch(per_example, given["loss_target"])
    else:
        def body(carry, xs):
            loss_sum, grad_sum = carry
            l_k, (gw_k, gx_k) = one_microbatch(xs[0], xs[1])
            with _jax.named_scope("update"):
                return (loss_sum + l_k, _jax.tree.map(_jnp.add, grad_sum, gw_k)), gx_k

        init = (_jnp.zeros((), _jnp.float32), _jax.tree.map(_jnp.zeros_like, weights))
        (loss, grad_w), grad_x = _jax.lax.scan(body, init, (per_example, given["loss_target"]))
    with _jax.named_scope("update"):
        delta_w, new_m, new_v = {}, {}, {}
        for n in TWIN_WEIGHTS:
            delta_w[n], new_m[n], new_v[n] = _adamw(weights[n], grad_w[n], given["m_" + n], given["v_" + n])
    return (loss, grad_x, *[grad_w[n] for n in TWIN_WEIGHTS], *[delta_w[n] for n in TWIN_WEIGHTS],
            *[new_m[n] for n in TWIN_WEIGHTS], *[new_v[n] for n in TWIN_WEIGHTS])
```

```python
import functools
import math

import jax
import jax.numpy as jnp
from jax import lax
from jax.experimental import pallas as pl
from jax.experimental.pallas import tpu as pltpu

F32 = jnp.float32
BF16 = jnp.bfloat16

D = 1024
SEQ = 4096
N_META = 16
PAD = 112
OFF = PAD + N_META
LP = OFF + SEQ
QB = 128
TM = 384
HALO = 16
N_DEV = 8
D_FF = 2816
FB = 704
N_FB = D_FF // FB
RMS_EPS = 1e-6
SCALE = 0.125
HEAD_LANES = 64
VMEM_LIMIT = 56 * 1024 * 1024

ADAM_LR = 0.001
ADAM_B1 = 0.9
ADAM_B2 = 0.999
ADAM_EPS = 1e-08
ADAM_WD = 0.01
ADAM_STEP = 10

R_IN, R_PROJ, R_DOWN = 1024, 128, 352
O_PC = R_IN
O_PA = O_PC + R_PROJ
O_OUT = O_PA + R_PROJ
O_DOWN = O_OUT + R_PROJ
R_PACK = O_DOWN + R_DOWN

NT = (((1,), (1,)), ((), ()))
NN = (((1,), (0,)), ((), ()))
TN = (((0,), (0,)), ((), ()))


def _params(sem):
    return pltpu.CompilerParams(dimension_semantics=sem, vmem_limit_bytes=VMEM_LIMIT)


def _dot(a, b, dn=NN):
    return lax.dot_general(a, b, dn, preferred_element_type=F32)


def _exchange(arrs, gather, name):
    n = len(arrs)
    out_shape = [jax.ShapeDtypeStruct((N_DEV,) + (a.shape if gather else a.shape[1:]), a.dtype) for a in arrs]

    def body(*refs):
        ins, outs = refs[:n], refs[n:2 * n]
        send_sems, recv_sems, loc_sems = refs[2 * n:]
        x, y, c = lax.axis_index("x"), lax.axis_index("y"), lax.axis_index("c")
        me = 4 * x + 2 * y + c
        copies = []
        for a in range(n):
            loc = pltpu.make_async_copy(ins[a] if gather else ins[a].at[me], outs[a].at[me], loc_sems.at[a])
            loc.start()
            copies.append(loc)
        for k in range(1, N_DEV):
            px = 1 - x if k & 4 else x
            py = 1 - y if k & 2 else y
            pc = 1 - c if k & 1 else c
            peer = 4 * px + 2 * py + pc
            for a in range(n):
                cp = pltpu.make_async_remote_copy(
                    src_ref=ins[a] if gather else ins[a].at[peer],
                    dst_ref=outs[a].at[me],
                    send_sem=send_sems.at[a * (N_DEV - 1) + k - 1],
                    recv_sem=recv_sems.at[a * (N_DEV - 1) + k - 1],
                    device_id=(px, py, pc),
                    device_id_type=pl.DeviceIdType.MESH,
                )
                cp.start()
                copies.append(cp)
        for cp in copies:
            cp.wait()

    return pl.pallas_call(
        body,
        name=name,
        out_shape=out_shape,
        in_specs=[pl.BlockSpec(memory_space=pl.ANY)] * n,
        out_specs=[pl.BlockSpec(memory_space=pl.ANY)] * n,
        scratch_shapes=[
            pltpu.SemaphoreType.DMA((n * (N_DEV - 1),)),
            pltpu.SemaphoreType.DMA((n * (N_DEV - 1),)),
            pltpu.SemaphoreType.DMA((n,)),
        ],
    )(*arrs)


def _mm(a, w, *, w_rows, trans_w, out_dtype, name):
    nb, _, wc = w.shape
    m, k = a.shape[-2:]
    n = w_rows if trans_w else wc
    dn = NT if trans_w else NN

    def body(a_ref, w_ref, o_ref):
        o_ref[...] = _dot(a_ref[...], w_ref[...], dn).astype(out_dtype)

    if a.ndim == 2:
        a_spec = pl.BlockSpec((TM, k), lambda j, i: (i, 0))
    else:
        a_spec = pl.BlockSpec((None, TM, k), lambda j, i: (j, i, 0))
    return pl.pallas_call(
        body,
        name=name,
        grid=(nb, m // TM),
        in_specs=[a_spec, pl.BlockSpec((None, w_rows, wc), lambda j, i: (j, 0, 0))],
        out_specs=pl.BlockSpec((None, TM, n), lambda j, i: (j, i, 0)),
        out_shape=jax.ShapeDtypeStruct((nb, m, n), out_dtype),
        compiler_params=_params(("parallel", "parallel")),
    )(a, w)


def _mm_sum(a, w, *, w_rows, trans_w, out_dtype, name):
    nb, m, k = a.shape
    wc = w.shape[2]
    n = w_rows if trans_w else wc
    dn = NT if trans_w else NN

    def body(a_ref, w_ref, o_ref, acc_ref):
        j = pl.program_id(1)

        @pl.when(j == 0)
        def _():
            acc_ref[...] = jnp.zeros_like(acc_ref)

        acc_ref[...] += _dot(a_ref[...], w_ref[...], dn)

        @pl.when(j == nb - 1)
        def _():
            o_ref[...] = acc_ref[...].astype(out_dtype)

    return pl.pallas_call(
        body,
        name=name,
        grid=(m // TM, nb),
        in_specs=[
            pl.BlockSpec((None, TM, k), lambda i, j: (j, i, 0)),
            pl.BlockSpec((None, w_rows, wc), lambda i, j: (j, 0, 0)),
        ],
        out_specs=pl.BlockSpec((TM, n), lambda i, j: (i, 0)),
        out_shape=jax.ShapeDtypeStruct((m, n), out_dtype),
        scratch_shapes=[pltpu.VMEM((TM, n), F32)],
        compiler_params=_params(("parallel", "arbitrary")),
    )(a, w)


def _mm_tn(a, b, *, nb, out_dtype, name):
    m, ka = a.shape[-2:]
    n = b.shape[-1]
    steps = m // TM

    def body(a_ref, b_ref, o_ref, acc_ref):
        i = pl.program_id(1)

        @pl.when(i == 0)
        def _():
            acc_ref[...] = jnp.zeros_like(acc_ref)

        acc_ref[...] += _dot(a_ref[...], b_ref[...], TN)

        @pl.when(i == steps - 1)
        def _():
            o_ref[...] = acc_ref[...].astype(out_dtype)

    def spec(arr, cols):
        if arr.ndim == 2:
            return pl.BlockSpec((TM, cols), lambda j, i: (i, 0))
        return pl.BlockSpec((None, TM, cols), lambda j, i: (j, i, 0))

    return pl.pallas_call(
        body,
        name=name,
        grid=(nb, steps),
        in_specs=[spec(a, ka), spec(b, n)],
        out_specs=pl.BlockSpec((None, ka, n), lambda j, i: (j, 0, 0)),
        out_shape=jax.ShapeDtypeStruct((nb, ka, n), out_dtype),
        scratch_shapes=[pltpu.VMEM((ka, n), F32)],
        compiler_params=_params(("parallel", "arbitrary")),
    )(a, b)


def _rstd(x):
    return lax.rsqrt(jnp.mean(x * x, axis=-1, keepdims=True) + RMS_EPS)


def _rms_bwd(x, g, dy):
    r = _rstd(x)
    u = dy * g
    dx = r * u - x * (r * r * r) * jnp.mean(u * x, axis=-1, keepdims=True)
    return dx, dy * x * r


def _row_spec(cols=D, tm=TM):
    return pl.BlockSpec((tm, cols), lambda i: (i, 0))


def _vec_spec(rows=1, cols=D):
    return pl.BlockSpec((rows, cols), lambda i: (0, 0))


def _rms_fwd(x, g, name):
    def body(x_ref, g_ref, o_ref):
        x = x_ref[...]
        o_ref[...] = (x * _rstd(x) * g_ref[...]).astype(BF16)

    return pl.pallas_call(
        body,
        name=name,
        grid=(LP // TM,),
        in_specs=[_row_spec(), _vec_spec()],
        out_specs=_row_spec(),
        out_shape=jax.ShapeDtypeStruct((LP, D), BF16),
        compiler_params=_params(("parallel",)),
    )(x, g)


def _resid_rms(h0, mix, g_post, g_next):
    def body(h0_ref, mix_ref, gp_ref, gn_ref, h1_ref, xn_ref):
        mix = mix_ref[...]
        h1 = h0_ref[...] + mix * _rstd(mix) * gp_ref[...]
        h1_ref[...] = h1
        xn_ref[...] = (h1 * _rstd(h1) * gn_ref[...]).astype(BF16)

    return pl.pallas_call(
        body,
        name="resid_rms",
        grid=(LP // TM,),
        in_specs=[_row_spec(), _row_spec(), _vec_spec(), _vec_spec()],
        out_specs=[_row_spec(), _row_spec()],
        out_shape=[jax.ShapeDtypeStruct((LP, D), F32), jax.ShapeDtypeStruct((LP, D), BF16)],
        compiler_params=_params(("parallel",)),
    )(h0, mix, g_post, g_next)


def _loss_head(h1, ffn, g_post, target):
    nblk = LP // QB

    def body(h1_ref, ffn_ref, g_ref, t_ref, dout_ref, dffn_ref, loss_ref, dg_ref):
        i = pl.program_id(0)

        @pl.when(i == 0)
        def _():
            loss_ref[...] = jnp.zeros_like(loss_ref)
            dg_ref[...] = jnp.zeros_like(dg_ref)

        ffn = ffn_ref[...]
        g = g_ref[...]
        out = h1_ref[...] + ffn * _rstd(ffn) * g
        err = jnp.where(i > 0, out - t_ref[...], 0.0)
        loss_ref[...] += 0.5 * jnp.sum(err * err) / D
        dout = err / D
        dout_ref[...] = dout
        dffn, dg = _rms_bwd(ffn, g, dout)
        dffn_ref[...] = dffn.astype(BF16)
        dg_ref[0:1, :] += jnp.sum(dg, axis=0, keepdims=True)

    return pl.pallas_call(
        body,
        name="loss_head",
        grid=(nblk,),
        in_specs=[
            _row_spec(tm=QB),
            _row_spec(tm=QB),
            _vec_spec(),
            pl.BlockSpec((QB, D), lambda i: (jnp.maximum(i - 1, 0), 0)),
        ],
        out_specs=[_row_spec(tm=QB), _row_spec(tm=QB), _vec_spec(8, 128), _vec_spec(8, D)],
        out_shape=[
            jax.ShapeDtypeStruct((LP, D), F32),
            jax.ShapeDtypeStruct((LP, D), BF16),
            jax.ShapeDtypeStruct((8, 128), F32),
            jax.ShapeDtypeStruct((8, D), F32),
        ],
        compiler_params=_params(("arbitrary",)),
    )(h1, ffn, g_post, target)


def _mid_bwd(dout, h1, dxn2, mix, g_post_mix, g_pre_ffn):
    def body(dout_ref, h1_ref, dxn_ref, mix_ref, gpm_ref, gpf_ref, dh1_ref, dmix_ref, dg_ref):
        i = pl.program_id(0)

        @pl.when(i == 0)
        def _():
            dg_ref[...] = jnp.zeros_like(dg_ref)

        dx, dg_ffn = _rms_bwd(h1_ref[...], gpf_ref[...], dxn_ref[...])
        dh1 = dout_ref[...] + dx
        dh1_ref[...] = dh1
        dmix, dg_mix = _rms_bwd(mix_ref[...], gpm_ref[...], dh1)
        dmix_ref[...] = dmix.astype(BF16)
        dg_ref[0:1, :] += jnp.sum(dg_mix, axis=0, keepdims=True)
        dg_ref[1:2, :] += jnp.sum(dg_ffn, axis=0, keepdims=True)

    return pl.pallas_call(
        body,
        name="mid_bwd",
        grid=(LP // TM,),
        in_specs=[_row_spec(), _row_spec(), _row_spec(), _row_spec(), _vec_spec(), _vec_spec()],
        out_specs=[_row_spec(), _row_spec(), _vec_spec(8, D)],
        out_shape=[
            jax.ShapeDtypeStruct((LP, D), F32),
            jax.ShapeDtypeStruct((LP, D), BF16),
            jax.ShapeDtypeStruct((8, D), F32),
        ],
        compiler_params=_params(("arbitrary",)),
    )(dout, h1, dxn2, mix, g_post_mix, g_pre_ffn)


def _first_bwd(dh1, h0, dxn1, g_pre_mix):
    def body(dh1_ref, h0_ref, dxn_ref, g_ref, dh0_ref, dg_ref):
        i = pl.program_id(0)

        @pl.when(i == 0)
        def _():
            dg_ref[...] = jnp.zeros_like(dg_ref)

        dx, dg = _rms_bwd(h0_ref[...], g_ref[...], dxn_ref[...])
        dh0_ref[...] = dh1_ref[...] + dx
        dg_ref[0:1, :] += jnp.sum(dg, axis=0, keepdims=True)

    return pl.pallas_call(
        body,
        name="first_bwd",
        grid=(LP // TM,),
        in_specs=[_row_spec(), _row_spec(), _row_spec(), _vec_spec()],
        out_specs=[_row_spec(), _vec_spec(8, D)],
        out_shape=[jax.ShapeDtypeStruct((LP, D), F32), jax.ShapeDtypeStruct((8, D), F32)],
        compiler_params=_params(("arbitrary",)),
    )(dh1, h0, dxn1, g_pre_mix)


def _prev_halo(i):
    return jnp.maximum(i * (TM // HALO) - 1, 0)


def _next_halo(i):
    return jnp.minimum((i + 1) * (TM // HALO), LP // HALO - 1)


def _down(x, s):
    return pltpu.roll(x, s, 0)


def _up(x, s):
    return pltpu.roll(x, x.shape[0] - s, 0)


def _conv_mix_fwd(hin, cw):
    def body(b_ref, c_ref, h_ref, cp_ref, hp_ref, w_ref, y_ref):
        i = pl.program_id(0)
        p = c_ref[...].astype(F32) * h_ref[...].astype(F32)
        pp = jnp.where(i > 0, cp_ref[...].astype(F32) * hp_ref[...].astype(F32), 0.0)
        ext = jnp.concatenate([pp, p], axis=0)
        w = [w_ref[t:t + 1, :] for t in range(3)]
        cv = w[2] * ext + w[1] * _down(ext, 1) + w[0] * _down(ext, 2)
        y_ref[...] = (b_ref[...].astype(F32) * cv[HALO:]).astype(BF16)

    def tile(s):
        return pl.BlockSpec((None, TM, D), lambda i: (s, i, 0))

    def prev(s):
        return pl.BlockSpec((None, HALO, D), lambda i: (s, _prev_halo(i), 0))

    return pl.pallas_call(
        body,
        name="conv_mix_fwd",
        grid=(LP // TM,),
        in_specs=[tile(0), tile(1), tile(2), prev(1), prev(2), _vec_spec(3, D)],
        out_specs=_row_spec(),
        out_shape=jax.ShapeDtypeStruct((LP, D), BF16),
        compiler_params=_params(("parallel",)),
    )(hin, hin, hin, hin, hin, cw)


def _conv_mix_bwd(hin, dy, cw):
    last = LP // TM - 1

    def body(b_ref, c_ref, h_ref, dy_ref, cp_ref, hp_ref, bn_ref, dyn_ref, w_ref, db_ref, dc_ref, dh_ref, dw_ref):
        i = pl.program_id(0)

        @pl.when(i == 0)
        def _():
            dw_ref[...] = jnp.zeros_like(dw_ref)

        b = b_ref[...].astype(F32)
        c = c_ref[...].astype(F32)
        h = h_ref[...].astype(F32)
        dy = dy_ref[...].astype(F32)
        w = [w_ref[t:t + 1, :] for t in range(3)]
        p = c * h
        pp = jnp.where(i > 0, cp_ref[...].astype(F32) * hp_ref[...].astype(F32), 0.0)
        ext = jnp.concatenate([pp, p], axis=0)
        p1 = _down(ext, 1)[HALO:]
        p2 = _down(ext, 2)[HALO:]
        cv = w[2] * p + w[1] * p1 + w[0] * p2
        db_ref[...] = (dy * cv).astype(BF16)
        dcv = dy * b
        dcvn = jnp.where(i < last, dyn_ref[...].astype(F32) * bn_ref[...].astype(F32), 0.0)
        dext = jnp.concatenate([dcv, dcvn], axis=0)
        dp = (w[2] * dext + w[1] * _up(dext, 1) + w[0] * _up(dext, 2))[:TM]
        dc_ref[...] = (dp * h).astype(BF16)
        dh_ref[...] = (dp * c).astype(BF16)
        dw_ref[0:1, :] += jnp.sum(dcv * p2, axis=0, keepdims=True)
        dw_ref[1:2, :] += jnp.sum(dcv * p1, axis=0, keepdims=True)
        dw_ref[2:3, :] += jnp.sum(dcv * p, axis=0, keepdims=True)

    def tile(s):
        return pl.BlockSpec((None, TM, D), lambda i: (s, i, 0))

    def prev(s):
        return pl.BlockSpec((None, HALO, D), lambda i: (s, _prev_halo(i), 0))

    return pl.pallas_call(
        body,
        name="conv_mix_bwd",
        grid=(LP // TM,),
        in_specs=[
            tile(0), tile(1), tile(2), _row_spec(),
            prev(1), prev(2),
            pl.BlockSpec((None, HALO, D), lambda i: (0, _next_halo(i), 0)),
            pl.BlockSpec((HALO, D), lambda i: (_next_halo(i), 0)),
            _vec_spec(3, D),
        ],
        out_specs=[_row_spec(), _row_spec(), _row_spec(), _vec_spec(8, D)],
        out_shape=[jax.ShapeDtypeStruct((LP, D), BF16)] * 3 + [jax.ShapeDtypeStruct((8, D), F32)],
        compiler_params=_params(("arbitrary",)),
    )(hin, hin, hin, dy, hin, hin, hin, dy, cw)


GELU_K = math.sqrt(2.0 / math.pi)
GELU_A = 0.044715


def _gelu_and_grad(x):
    x2 = x * x
    t = jnp.tanh(GELU_K * (x + GELU_A * x2 * x))
    gelu = 0.5 * x * (1.0 + t)
    grad = 0.5 * (1.0 + t) + 0.5 * x * (1.0 - t * t) * GELU_K * (1.0 + 3.0 * GELU_A * x2)
    return gelu, grad


def _ffn_act_fwd(ug, cw4):
    def body(u_ref, g_ref, up_ref, w_ref, o_ref):
        i = pl.program_id(1)
        u = u_ref[...].astype(F32)
        up = jnp.where(i > 0, up_ref[...].astype(F32), 0.0)
        ext = jnp.concatenate([up, u], axis=0)
        w = [w_ref[t:t + 1, :] for t in range(3)]
        uc = (w[2] * ext + w[1] * _down(ext, 1) + w[0] * _down(ext, 2))[HALO:]
        gelu, _ = _gelu_and_grad(uc)
        o_ref[...] = (gelu * g_ref[...].astype(F32)).astype(BF16)

    return pl.pallas_call(
        body,
        name="ffn_act_fwd",
        grid=(N_FB, LP // TM),
        in_specs=[
            pl.BlockSpec((None, TM, FB), lambda j, i: (j, i, 0)),
            pl.BlockSpec((None, TM, FB), lambda j, i: (j + N_FB, i, 0)),
            pl.BlockSpec((None, HALO, FB), lambda j, i: (j, _prev_halo(i), 0)),
            pl.BlockSpec((None, 3, FB), lambda j, i: (j, 0, 0)),
        ],
        out_specs=pl.BlockSpec((None, TM, FB), lambda j, i: (j, i, 0)),
        out_shape=jax.ShapeDtypeStruct((N_FB, LP, FB), BF16),
        compiler_params=_params(("parallel", "parallel")),
    )(ug, ug, ug, cw4)


def _ffn_act_bwd(ug, dhid, cw4):
    last = LP // TM - 1
    n = TM + 2 * HALO

    def body(u_ref, g_ref, dh_ref, up_ref, un_ref, gn_ref, dhn_ref, w_ref, dug_ref, dw_ref):
        i = pl.program_id(1)

        @pl.when(i == 0)
        def _():
            dw_ref[...] = jnp.zeros_like(dw_ref)

        w = [w_ref[t:t + 1, :] for t in range(3)]
        u = u_ref[...].astype(F32)
        up = jnp.where(i > 0, up_ref[...].astype(F32), 0.0)
        ext = jnp.concatenate([up, u, un_ref[...].astype(F32)], axis=0)
        u1 = _down(ext, 1)
        u2 = _down(ext, 2)
        uc = w[2] * ext + w[1] * u1 + w[0] * u2
        gelu, ggrad = _gelu_and_grad(uc)
        zeros = jnp.zeros((HALO, FB), F32)
        gext = jnp.concatenate([zeros, g_ref[...].astype(F32), gn_ref[...].astype(F32)], axis=0)
        dhn = jnp.where(i < last, dhn_ref[...].astype(F32), 0.0)
        dhext = jnp.concatenate([zeros, dh_ref[...].astype(F32), dhn], axis=0)
        dug_ref[1] = (dhext * gelu)[HALO:HALO + TM].astype(BF16)
        duc = dhext * gext * ggrad
        du = w[2] * duc + w[1] * _up(duc, 1) + w[0] * _up(duc, 2)
        dug_ref[0] = du[HALO:HALO + TM].astype(BF16)
        row = lax.broadcasted_iota(jnp.int32, (n, 1), 0)
        own = jnp.where((row >= HALO) & (row < HALO + TM), duc, 0.0)
        dw_ref[0:1, :] += jnp.sum(own * u2, axis=0, keepdims=True)
        dw_ref[1:2, :] += jnp.sum(own * u1, axis=0, keepdims=True)
        dw_ref[2:3, :] += jnp.sum(own * ext, axis=0, keepdims=True)

    def tile(off):
        return pl.BlockSpec((None, TM, FB), lambda j, i: (j + off, i, 0))

    def nxt(off):
        return pl.BlockSpec((None, HALO, FB), lambda j, i: (j + off, _next_halo(i), 0))

    return pl.pallas_call(
        body,
        name="ffn_act_bwd",
        grid=(N_FB, LP // TM),
        in_specs=[
            tile(0), tile(N_FB), tile(0),
            pl.BlockSpec((None, HALO, FB), lambda j, i: (j, _prev_halo(i), 0)),
            nxt(0), nxt(N_FB), nxt(0),
            pl.BlockSpec((None, 3, FB), lambda j, i: (j, 0, 0)),
        ],
        out_specs=[
            pl.BlockSpec((2, None, TM, FB), lambda j, i: (0, j, i, 0)),
            pl.BlockSpec((None, 8, FB), lambda j, i: (j, 0, 0)),
        ],
        out_shape=[jax.ShapeDtypeStruct((2, N_FB, LP, FB), BF16), jax.ShapeDtypeStruct((N_FB, 8, FB), F32)],
        compiler_params=_params(("parallel", "arbitrary")),
    )(ug, ug, dhid, ug, ug, ug, dhid, cw4)


def _gate_fwd(bc, ba, hin, bgate):
    def body(bc_ref, ba_ref, gc_ref, ga_ref, b_ref, o_ref):
        b = b_ref[...]
        sc = jax.nn.sigmoid(gc_ref[...].astype(F32) + b[0:1])
        sa = jax.nn.sigmoid(ga_ref[...].astype(F32) + b[1:2])
        o_ref[...] = (sc * bc_ref[...].astype(F32) + sa * ba_ref[...].astype(F32)).astype(BF16)

    def tile(s):
        return pl.BlockSpec((None, TM, D), lambda i: (s, i, 0))

    return pl.pallas_call(
        body,
        name="gate_fwd",
        grid=(LP // TM,),
        in_specs=[_row_spec(), _row_spec(), tile(6), tile(7), _vec_spec(2, D)],
        out_specs=_row_spec(),
        out_shape=jax.ShapeDtypeStruct((LP, D), BF16),
        compiler_params=_params(("parallel",)),
    )(bc, ba, hin, hin, bgate)


def _gate_bwd(dm, bc, ba, hin, bgate):
    def body(dm_ref, bc_ref, ba_ref, gc_ref, ga_ref, b_ref, dbc_ref, dba_ref, dgc_ref, dga_ref, db_ref):
        i = pl.program_id(0)

        @pl.when(i == 0)
        def _():
            db_ref[...] = jnp.zeros_like(db_ref)

        b = b_ref[...]
        dm = dm_ref[...].astype(F32)
        sc = jax.nn.sigmoid(gc_ref[...].astype(F32) + b[0:1])
        sa = jax.nn.sigmoid(ga_ref[...].astype(F32) + b[1:2])
        dbc_ref[...] = (dm * sc).astype(BF16)
        dba_ref[...] = (dm * sa).astype(BF16)
        dgc = dm * bc_ref[...].astype(F32) * sc * (1.0 - sc)
        dga = dm * ba_ref[...].astype(F32) * sa * (1.0 - sa)
        dgc_ref[...] = dgc.astype(BF16)
        dga_ref[...] = dga.astype(BF16)
        db_ref[0:1, :] += jnp.sum(dgc, axis=0, keepdims=True)
        db_ref[1:2, :] += jnp.sum(dga, axis=0, keepdims=True)

    def tile(s):
        return pl.BlockSpec((None, TM, D), lambda i: (s, i, 0))

    return pl.pallas_call(
        body,
        name="gate_bwd",
        grid=(LP // TM,),
        in_specs=[_row_spec(), _row_spec(), _row_spec(), tile(6), tile(7), _vec_spec(2, D)],
        out_specs=[_row_spec()] * 4 + [_vec_spec(8, D)],
        out_shape=[jax.ShapeDtypeStruct((LP, D), BF16)] * 4 + [jax.ShapeDtypeStruct((8, D), F32)],
        compiler_params=_params(("arbitrary",)),
    )(dm, bc, ba, hin, hin, bgate)


def _softplus(z):
    return jnp.maximum(z, 0.0) + jnp.log(1.0 + jnp.exp(-jnp.abs(z)))


def _cumsum_matrix(inclusive, reverse):
    r = lax.broadcasted_iota(jnp.int32, (QB, 2 * QB), 0)
    c = lax.broadcasted_iota(jnp.int32, (QB, 2 * QB), 1)
    if reverse:
        tri = r > c
    elif inclusive:
        tri = r <= c
    else:
        tri = r < c
    return jnp.where((c >= QB) | tri, 1.0, 0.0).astype(BF16)


def _split_dot(x, m):
    hi = x.astype(BF16)
    lo = (x - hi.astype(F32)).astype(BF16)
    return _dot(hi, m) + _dot(lo, m)


def _block_mask(i, j):
    row = lax.broadcasted_iota(jnp.int32, (QB, QB), 0) + i * QB
    col = lax.broadcasted_iota(jnp.int32, (QB, QB), 1) + j * QB
    return (col < row) & (col >= PAD)


def _attn_fwd(hin):
    nblk = LP // QB

    def body(q_ref, k_ref, v_ref, o_ref, lt_ref):
        i = pl.program_id(1)
        head_a = lax.broadcasted_iota(jnp.int32, (QB, QB), 1) < HEAD_LANES
        um = _cumsum_matrix(False, True)
        q = (q_ref[...].astype(F32) * SCALE).astype(BF16)
        zero_b = jnp.zeros_like(q)
        q_heads = (jnp.where(head_a, q, zero_b), jnp.where(head_a, zero_b, q))

        def step(j, st, masked):
            start = pl.multiple_of(j * QB, QB)
            k = k_ref[pl.ds(start, QB), :]
            v = v_ref[pl.ds(start, QB), :]
            mask = _block_mask(i, j) if masked else None
            new = []
            for hd in range(2):
                c, acc = st[2 * hd], st[2 * hd + 1]
                z = _dot(q_heads[hd], k, NT)
                sp = _softplus(z)
                spm = jnp.where(mask, sp, 0.0) if masked else sp
                r = _split_dot(spm, um)
                a = jnp.exp(z - sp - c - r[:, :QB])
                if masked:
                    a = jnp.where(mask, a, 0.0)
                new += [c + r[:, QB:], acc + _dot(a.astype(BF16), v)]
            return tuple(new)

        zero = jnp.zeros((QB, QB), F32)
        st = step(i, (zero, zero, zero, zero), True)
        st = lax.fori_loop(0, jnp.maximum(i - 1, 0), lambda t, s: step(i - 1 - t, s, False), st)
        st = lax.fori_loop(0, jnp.minimum(i, 1), lambda t, s: step(0, s, True), st)
        o_ref[...] = jnp.where(head_a, st[1], st[3]).astype(BF16)
        lt_ref[...] = jnp.where(head_a, st[0], st[2])

    def seq(s):
        return pl.BlockSpec((None, LP, QB), lambda p, i: (s, 0, p))

    return pl.pallas_call(
        body,
        name="attn_fwd",
        grid=(D // QB, nblk),
        in_specs=[pl.BlockSpec((None, QB, QB), lambda p, i: (3, i, p)), seq(4), seq(5)],
        out_specs=[pl.BlockSpec((QB, QB), lambda p, i: (i, p))] * 2,
        out_shape=[jax.ShapeDtypeStruct((LP, D), BF16), jax.ShapeDtypeStruct((LP, D), F32)],
        compiler_params=_params(("parallel", "parallel")),
    )(hin, hin, hin)


def _attn_bwd(hin, do, lt):
    nblk = LP // QB

    def body(q_ref, k_ref, v_ref, do_ref, lt_ref, dq_ref, dk_ref, dv_ref, dk_acc, dv_acc):
        i = pl.program_id(1)

        @pl.when(i == 0)
        def _():
            dk_acc[...] = jnp.zeros_like(dk_acc)
            dv_acc[...] = jnp.zeros_like(dv_acc)

        head_a = lax.broadcasted_iota(jnp.int32, (QB, QB), 1) < HEAD_LANES
        um_sp = _cumsum_matrix(True, False)
        um_g = _cumsum_matrix(False, False)
        q = (q_ref[...].astype(F32) * SCALE).astype(BF16)
        do = do_ref[...]
        zero_b = jnp.zeros_like(q)
        q_heads = (jnp.where(head_a, q, zero_b), jnp.where(head_a, zero_b, q))
        do_heads = (jnp.where(head_a, do, zero_b), jnp.where(head_a, zero_b, do))
        lt = lt_ref[...]
        lt_sw = pltpu.roll(lt, HEAD_LANES, 1)
        totals = (jnp.where(head_a, lt, lt_sw), jnp.where(head_a, lt_sw, lt))

        def step(j, st, masked):
            start = pl.multiple_of(j * QB, QB)
            k = k_ref[pl.ds(start, QB), :]
            v = v_ref[pl.ds(start, QB), :]
            mask = _block_mask(i, j) if masked else None
            new = []
            dk_add = jnp.zeros((QB, QB), F32)
            dv_add = jnp.zeros((QB, QB), F32)
            for hd in range(2):
                psp, pg, dq = st[3 * hd], st[3 * hd + 1], st[3 * hd + 2]
                z = _dot(q_heads[hd], k, NT)
                sp = _softplus(z)
                spm = jnp.where(mask, sp, 0.0) if masked else sp
                r = _split_dot(spm, um_sp)
                a = jnp.exp(z - sp - (totals[hd] - psp - r[:, :QB]))
                if masked:
                    a = jnp.where(mask, a, 0.0)
                g = a * _dot(do_heads[hd], v, NT)
                rg = _dot(g.astype(BF16), um_g)
                sig = jnp.exp(z - sp)
                dz = g - sig * (g + pg + rg[:, :QB])
                if masked:
                    dz = jnp.where(mask, dz, 0.0)
                dzb = dz.astype(BF16)
                dk_add += _dot(dzb, q_heads[hd], TN)
                dv_add += _dot(a.astype(BF16), do_heads[hd], TN)
                new += [psp + r[:, QB:], pg + rg[:, QB:], dq + _dot(dzb, k)]
            dk_acc[pl.ds(start, QB), :] += dk_add
            dv_acc[pl.ds(start, QB), :] += dv_add
            return tuple(new)

        zero = jnp.zeros((QB, QB), F32)
        st = (zero,) * 6
        st = lax.fori_loop(0, jnp.minimum(i, 1), lambda t, s: step(0, s, True), st)
        st = lax.fori_loop(0, jnp.maximum(i - 1, 0), lambda t, s: step(t + 1, s, False), st)
        st = step(i, st, True)
        dq_ref[...] = (jnp.where(head_a, st[2], st[5]) * SCALE).astype(BF16)

        @pl.when(i == nblk - 1)
        def _():
            dk_ref[...] = dk_acc[...].astype(BF16)
            dv_ref[...] = dv_acc[...].astype(BF16)

    def seq(s):
        return pl.BlockSpec((None, LP, QB), lambda p, i: (s, 0, p))

    blk = pl.BlockSpec((QB, QB), lambda p, i: (i, p))
    col = pl.BlockSpec((LP, QB), lambda p, i: (0, p))
    return pl.pallas_call(
        body,
        name="attn_bwd",
        grid=(D // QB, nblk),
        in_specs=[pl.BlockSpec((None, QB, QB), lambda p, i: (3, i, p)), seq(4), seq(5), blk, blk],
        out_specs=[blk, col, col],
        out_shape=[jax.ShapeDtypeStruct((LP, D), BF16)] * 3,
        scratch_shapes=[pltpu.VMEM((LP, QB), F32), pltpu.VMEM((LP, QB), F32)],
        compiler_params=_params(("parallel", "arbitrary")),
    )(hin, hin, hin, do, lt)


def _adamw(pieces, w, m, v, *, rows, row_off, tr, name):
    npieces, _, cols = pieces.shape
    ob = row_off // tr

    def body(p_ref, w_ref, m_ref, v_ref, g_ref, d_ref, nm_ref, nv_ref):
        g = p_ref[0].astype(F32)
        for s in range(1, npieces):
            g = g + p_ref[s].astype(F32)
        w_ = w_ref[...]
        m_new = ADAM_B1 * m_ref[...] + (1.0 - ADAM_B1) * g
        v_new = ADAM_B2 * v_ref[...] + (1.0 - ADAM_B2) * jnp.square(g)
        m_hat = m_new / (1.0 - ADAM_B1 ** ADAM_STEP)
        v_hat = v_new / (1.0 - ADAM_B2 ** ADAM_STEP)
        g_ref[...] = g
        d_ref[...] = -ADAM_LR * (m_hat / (jnp.sqrt(v_hat) + ADAM_EPS) + ADAM_WD * w_)
        nm_ref[...] = m_new
        nv_ref[...] = v_new

    spec = pl.BlockSpec((tr, cols), lambda i: (i, 0))
    return pl.pallas_call(
        body,
        name=name,
        grid=(rows // tr,),
        in_specs=[pl.BlockSpec((npieces, tr, cols), lambda i: (0, ob + i, 0)), spec, spec, spec],
        out_specs=[spec] * 4,
        out_shape=[jax.ShapeDtypeStruct((rows, cols), F32)] * 4,
        compiler_params=_params(("parallel",)),
    )(pieces, w, m, v)


def _sum_pieces(pieces, name):
    npieces, rows, cols = pieces.shape

    def body(p_ref, o_ref):
        acc = p_ref[0]
        for s in range(1, npieces):
            acc = acc + p_ref[s]
        o_ref[...] = acc

    return pl.pallas_call(
        body,
        name=name,
        in_specs=[pl.BlockSpec(memory_space=pltpu.VMEM)],
        out_specs=pl.BlockSpec(memory_space=pltpu.VMEM),
        out_shape=jax.ShapeDtypeStruct((rows, cols), pieces.dtype),
    )(pieces)


SMALL_ROWS = 32
CWF_PAD = 384
GRAD_ROWS = 40
SMALL_FLAT = 8192


def _local_step(x, target, meta, g_pre_mix, g_post_mix, g_pre_ffn, g_post_ffn, cw_mix, b_gate, cw_ffn,
                wpack, w_pc, w_pa, w_out, w_ug, w_dn):
    cw4 = cw_ffn.reshape(3, N_FB, FB).transpose(1, 0, 2)
    h0 = jnp.concatenate([jnp.zeros((PAD, D), F32), meta, x], axis=0)

    xn1 = _rms_fwd(h0, g_pre_mix, "rms_pre_mix")
    hin = _mm(xn1, wpack, w_rows=R_IN, trans_w=False, out_dtype=BF16, name="mm_in")
    y_conv = _conv_mix_fwd(hin, cw_mix)
    o, lt = _attn_fwd(hin)
    bc = _mm(y_conv, w_pc, w_rows=D, trans_w=False, out_dtype=BF16, name="mm_proj_conv")[0]
    ba = _mm(o, w_pa, w_rows=D, trans_w=False, out_dtype=BF16, name="mm_proj_attn")[0]
    merged = _gate_fwd(bc, ba, hin, b_gate)
    mix = _mm(merged, w_out, w_rows=D, trans_w=False, out_dtype=F32, name="mm_out")[0]
    h1, xn2 = _resid_rms(h0, mix, g_post_mix, g_pre_ffn)
    ug = _mm(xn2, w_ug, w_rows=D, trans_w=False, out_dtype=BF16, name="mm_up_gate")
    hid = _ffn_act_fwd(ug, cw4)
    ffn = _mm_sum(hid, w_dn, w_rows=FB, trans_w=False, out_dtype=F32, name="mm_down")
    dout, dffn, loss8, dg_post_ffn = _loss_head(h1, ffn, g_post_ffn, target)

    dhid = _mm(dffn, w_dn, w_rows=FB, trans_w=True, out_dtype=BF16, name="mm_down_dx")
    gw_dn = _mm_tn(hid, dffn, nb=N_FB, out_dtype=BF16, name="mm_down_dw")
    dug, dcw4 = _ffn_act_bwd(ug, dhid, cw4)
    dug = dug.reshape(2 * N_FB, LP, FB)
    dxn2 = _mm_sum(dug, w_ug, w_rows=D, trans_w=True, out_dtype=F32, name="mm_up_gate_dx")
    gw_ug = _mm_tn(xn2, dug, nb=N_DEV, out_dtype=BF16, name="mm_up_gate_dw")
    dh1, dmix, dg_mid = _mid_bwd(dout, h1, dxn2, mix, g_post_mix, g_pre_ffn)
    dmerged = _mm(dmix, w_out, w_rows=D, trans_w=True, out_dtype=BF16, name="mm_out_dx")[0]
    gw_out = _mm_tn(merged, dmix, nb=1, out_dtype=BF16, name="mm_out_dw")
    dbc, dba, dgc, dga, db_gate = _gate_bwd(dmerged, bc, ba, hin, b_gate)
    dy_conv = _mm(dbc, w_pc, w_rows=D, trans_w=True, out_dtype=BF16, name="mm_proj_conv_dx")[0]
    gw_pc = _mm_tn(y_conv, dbc, nb=1, out_dtype=BF16, name="mm_proj_conv_dw")
    do = _mm(dba, w_pa, w_rows=D, trans_w=True, out_dtype=BF16, name="mm_proj_attn_dx")[0]
    gw_pa = _mm_tn(o, dba, nb=1, out_dtype=BF16, name="mm_proj_attn_dw")
    db, dc, dh, dcw_mix = _conv_mix_bwd(hin, dy_conv, cw_mix)
    dq, dk, dv = _attn_bwd(hin, do, lt)
    dhin = jnp.stack([db, dc, dh, dq, dk, dv, dgc, dga])
    dxn1 = _mm_sum(dhin, wpack, w_rows=R_IN, trans_w=True, out_dtype=F32, name="mm_in_dx")
    gw_in = _mm_tn(xn1, dhin, nb=N_DEV, out_dtype=BF16, name="mm_in_dw")
    dh0, dg_pre_mix = _first_bwd(dh1, h0, dxn1, g_pre_mix)

    gpack = jnp.concatenate(
        [gw_in, gw_pc.reshape(N_DEV, R_PROJ, D), gw_pa.reshape(N_DEV, R_PROJ, D), gw_out.reshape(N_DEV, R_PROJ, D),
         gw_dn.reshape(N_DEV, R_DOWN, D)], axis=1)
    dcw_ffn = dcw4[:, :3].transpose(1, 0, 2).reshape(3, D_FF)
    small = jnp.concatenate(
        [dh0[PAD:OFF], dg_pre_mix[0:1], dg_mid[0:1], dg_mid[1:2], dg_post_ffn[0:1], dcw_mix[0:3], db_gate[0:2],
         jnp.pad(dcw_ffn.reshape(-1), (0, 9 * D - 3 * D_FF)).reshape(9, D),
         jnp.zeros((GRAD_ROWS - 34, D), F32)], axis=0)
    return loss8[0, 0], dh0[OFF:], gpack, gw_ug, small


def kernel(x, meta_tokens, g_pre_mix, w_in, conv_w_mix, w_proj_conv, w_proj_attn, b_gate, w_out, g_post_mix, g_pre_ffn, w_up_gate, conv_w_ffn, w_down, g_post_ffn, loss_target, m_meta_tokens, m_g_pre_mix, m_w_in, m_conv_w_mix, m_w_proj_conv, m_w_proj_attn, m_b_gate, m_w_out, m_g_post_mix, m_g_pre_ffn, m_w_up_gate, m_conv_w_ffn, m_w_down, m_g_post_ffn, v_meta_tokens, v_g_pre_mix, v_w_in, v_conv_w_mix, v_w_proj_conv, v_w_proj_attn, v_b_gate, v_w_out, v_g_post_mix, v_g_pre_ffn, v_w_up_gate, v_conv_w_ffn, v_w_down, v_g_post_ffn):
    me = 4 * lax.axis_index("x") + 2 * lax.axis_index("y") + lax.axis_index("c")

    small_shard = jnp.concatenate(
        [meta_tokens, conv_w_mix[0], b_gate[0],
         jnp.pad(conv_w_ffn[0], ((0, 0), (0, CWF_PAD - R_DOWN))).reshape(9, 128),
         jnp.zeros((SMALL_ROWS - 30, 128), F32)], axis=0)
    wshard = jnp.concatenate([w_in[0], w_proj_conv[0], w_proj_attn[0], w_out[0], w_down[0]], axis=0).astype(BF16)
    ugshard = w_up_gate[0].astype(BF16)
    (small_all,) = _exchange([small_shard], True, "gather_small")
    wpack, w_ug = _exchange([wshard, ugshard], True, "gather_weights")

    def unshard(rows):
        return rows.transpose(1, 0, 2).reshape(rows.shape[1], N_DEV * rows.shape[2])

    meta_full = unshard(small_all[:, 0:16])
    cw_mix_full = unshard(small_all[:, 16:19])
    b_gate_full = unshard(small_all[:, 19:21])
    cw_ffn_full = unshard(small_all[:, 21:30].reshape(N_DEV, 3, CWF_PAD)[:, :, :R_DOWN])
    w_pc = wpack[:, O_PC:O_PA].reshape(1, D, D)
    w_pa = wpack[:, O_PA:O_OUT].reshape(1, D, D)
    w_o = wpack[:, O_OUT:O_DOWN].reshape(1, D, D)
    w_dn = wpack[:, O_DOWN:].reshape(N_FB, FB, D)

    loss_local, grad_x, gpack, gw_ug, gsmall = _local_step(
        x[0], loss_target[0], meta_full, g_pre_mix, g_post_mix, g_pre_ffn, g_post_ffn, cw_mix_full, b_gate_full,
        cw_ffn_full, wpack, w_pc, w_pa, w_o, w_ug, w_dn)

    rpack, rug = _exchange([gpack, gw_ug], False, "scatter_grads")
    (rsmall,) = _exchange([gsmall], True, "gather_small_grads")
    gs = _sum_pieces(rsmall, "sum_small_grads")
    loss = lax.psum(loss_local, ("x", "y", "c"))

    def cols(a, width):
        return lax.dynamic_slice_in_dim(a, me * width, width, axis=1)

    g_meta = cols(gs[0:16], 128)
    g_gpm, g_gpo, g_gpf, g_gff = gs[16:17], gs[17:18], gs[18:19], gs[19:20]
    g_cwm = cols(gs[20:23], 128)[None]
    g_bg = cols(gs[23:25], 128)[None]
    g_cwf = cols(gs[25:34].reshape(-1)[:3 * D_FF].reshape(3, D_FF), R_DOWN)[None]

    def big(pieces, w, m, v, rows, row_off, tr, name):
        g, d, nm, nv = _adamw(pieces, w[0], m[0], v[0], rows=rows, row_off=row_off, tr=tr, name=name)
        return g[None], d[None], nm[None], nv[None]

    r_in = big(rpack, w_in, m_w_in, v_w_in, R_IN, 0, 256, "adamw_in")
    r_pc = big(rpack, w_proj_conv, m_w_proj_conv, v_w_proj_conv, R_PROJ, O_PC, R_PROJ, "adamw_proj_conv")
    r_pa = big(rpack, w_proj_attn, m_w_proj_attn, v_w_proj_attn, R_PROJ, O_PA, R_PROJ, "adamw_proj_attn")
    r_out = big(rpack, w_out, m_w_out, v_w_out, R_PROJ, O_OUT, R_PROJ, "adamw_out")
    r_dn = big(rpack, w_down, m_w_down, v_w_down, R_DOWN, O_DOWN, R_DOWN, "adamw_down")
    r_ug = big(rug, w_up_gate, m_w_up_gate, v_w_up_gate, D, 0, 256, "adamw_up_gate")

    small_w = [meta_tokens, g_pre_mix, conv_w_mix, b_gate, g_post_mix, g_pre_ffn, conv_w_ffn, g_post_ffn]
    small_g = [g_meta, g_gpm, g_cwm, g_bg, g_gpo, g_gpf, g_cwf, g_gff]
    small_m = [m_meta_tokens, m_g_pre_mix, m_conv_w_mix, m_b_gate, m_g_post_mix, m_g_pre_ffn, m_conv_w_ffn, m_g_post_ffn]
    small_v = [v_meta_tokens, v_g_pre_mix, v_conv_w_mix, v_b_gate, v_g_post_mix, v_g_pre_ffn, v_conv_w_ffn, v_g_post_ffn]

    def flat(arrs, fill):
        parts = [a.reshape(-1) for a in arrs]
        used = sum(p.shape[0] for p in parts)
        return jnp.concatenate(parts + [jnp.full((SMALL_FLAT - used,), fill, F32)]).reshape(SMALL_FLAT // 128, 128)

    sg, sd, sm, sv = _adamw(flat(small_g, 0.0)[None], flat(small_w, 0.0), flat(small_m, 0.0), flat(small_v, 1.0),
                            rows=SMALL_FLAT // 128, row_off=0, tr=SMALL_FLAT // 128, name="adamw_small")

    def unflat(packed):
        out, pos = [], 0
        p = packed.reshape(-1)
        for a in small_w:
            out.append(p[pos:pos + a.size].reshape(a.shape))
            pos += a.size
        return out

    s_g, s_d, s_m, s_v = unflat(sg), unflat(sd), unflat(sm), unflat(sv)

    def ordered(k, smalls):
        meta, gpm, cwm, bg, gpo, gpf, cwf, gff = smalls
        return [meta, gpm, r_in[k], cwm, r_pc[k], r_pa[k], bg, r_out[k], gpo, gpf, r_ug[k], cwf, r_dn[k], gff]

    return (loss, grad_x[None], *ordered(0, s_g), *ordered(1, s_d), *ordered(2, s_m), *ordered(3, s_v))
```

```python
import functools
import math

import jax
import jax.numpy as jnp
from jax import lax
from jax.experimental import pallas as pl
from jax.experimental.pallas import tpu as pltpu

F32 = jnp.float32
BF16 = jnp.bfloat16

D = 1024
SEQ = 4096
N_META = 16
PAD = 112
OFF = PAD + N_META
LP = OFF + SEQ
QB = 128
AQ = 384
KPQ = AQ // QB
TM = 384
HALO = 16
N_DEV = 8
D_FF = 2816
FB = 704
N_FB = D_FF // FB
RMS_EPS = 1e-6
SCALE = 0.125
HEAD_LANES = 64
VMEM_LIMIT = 56 * 1024 * 1024

ADAM_LR = 0.001
ADAM_B1 = 0.9
ADAM_B2 = 0.999
ADAM_EPS = 1e-08
ADAM_WD = 0.01
ADAM_STEP = 10

R_IN, R_PROJ, R_DOWN = 1024, 128, 352
O_PC = R_IN
O_PA = O_PC + R_PROJ
O_OUT = O_PA + R_PROJ
O_DOWN = O_OUT + R_PROJ
R_PACK = O_DOWN + R_DOWN

NT = (((1,), (1,)), ((), ()))
NN = (((1,), (0,)), ((), ()))
TN = (((0,), (0,)), ((), ()))


def _params(sem):
    return pltpu.CompilerParams(dimension_semantics=sem, vmem_limit_bytes=VMEM_LIMIT)


def _dot(a, b, dn=NN):
    return lax.dot_general(a, b, dn, preferred_element_type=F32)


def _exchange(arrs, gather, name):
    n = len(arrs)
    out_shape = [jax.ShapeDtypeStruct((N_DEV,) + (a.shape if gather else a.shape[1:]), a.dtype) for a in arrs]

    def body(*refs):
        ins, outs = refs[:n], refs[n:2 * n]
        send_sems, recv_sems, loc_sems = refs[2 * n:]
        x, y, c = lax.axis_index("x"), lax.axis_index("y"), lax.axis_index("c")
        me = 4 * x + 2 * y + c
        copies = []
        for a in range(n):
            loc = pltpu.make_async_copy(ins[a] if gather else ins[a].at[me], outs[a].at[me], loc_sems.at[a])
            loc.start()
            copies.append(loc)
        for k in range(1, N_DEV):
            px = 1 - x if k & 4 else x
            py = 1 - y if k & 2 else y
            pc = 1 - c if k & 1 else c
            peer = 4 * px + 2 * py + pc
            for a in range(n):
                cp = pltpu.make_async_remote_copy(
                    src_ref=ins[a] if gather else ins[a].at[peer],
                    dst_ref=outs[a].at[me],
                    send_sem=send_sems.at[a * (N_DEV - 1) + k - 1],
                    recv_sem=recv_sems.at[a * (N_DEV - 1) + k - 1],
                    device_id=(px, py, pc),
                    device_id_type=pl.DeviceIdType.MESH,
                )
                cp.start()
                copies.append(cp)
        for cp in copies:
            cp.wait()

    return pl.pallas_call(
        body,
        name=name,
        out_shape=out_shape,
        in_specs=[pl.BlockSpec(memory_space=pl.ANY)] * n,
        out_specs=[pl.BlockSpec(memory_space=pl.ANY)] * n,
        scratch_shapes=[
            pltpu.SemaphoreType.DMA((n * (N_DEV - 1),)),
            pltpu.SemaphoreType.DMA((n * (N_DEV - 1),)),
            pltpu.SemaphoreType.DMA((n,)),
        ],
    )(*arrs)


def _mm(a, w, *, w_rows, trans_w, out_dtype, name):
    nb, _, wc = w.shape
    m, k = a.shape[-2:]
    n = w_rows if trans_w else wc
    dn = NT if trans_w else NN

    def body(a_ref, w_ref, o_ref):
        o_ref[...] = _dot(a_ref[...], w_ref[...], dn).astype(out_dtype)

    if a.ndim == 2:
        a_spec = pl.BlockSpec((TM, k), lambda j, i: (i, 0))
    else:
        a_spec = pl.BlockSpec((None, TM, k), lambda j, i: (j, i, 0))
    return pl.pallas_call(
        body,
        name=name,
        grid=(nb, m // TM),
        in_specs=[a_spec, pl.BlockSpec((None, w_rows, wc), lambda j, i: (j, 0, 0))],
        out_specs=pl.BlockSpec((None, TM, n), lambda j, i: (j, i, 0)),
        out_shape=jax.ShapeDtypeStruct((nb, m, n), out_dtype),
        compiler_params=_params(("parallel", "parallel")),
    )(a, w)


def _mm_sum(a, w, *, w_rows, trans_w, out_dtype, name):
    nb, m, k = a.shape
    wc = w.shape[2]
    n = w_rows if trans_w else wc
    dn = NT if trans_w else NN

    def body(a_ref, w_ref, o_ref, acc_ref):
        j = pl.program_id(1)

        @pl.when(j == 0)
        def _():
            acc_ref[...] = jnp.zeros_like(acc_ref)

        acc_ref[...] += _dot(a_ref[...], w_ref[...], dn)

        @pl.when(j == nb - 1)
        def _():
            o_ref[...] = acc_ref[...].astype(out_dtype)

    return pl.pallas_call(
        body,
        name=name,
        grid=(m // TM, nb),
        in_specs=[
            pl.BlockSpec((None, TM, k), lambda i, j: (j, i, 0)),
            pl.BlockSpec((None, w_rows, wc), lambda i, j: (j, 0, 0)),
        ],
        out_specs=pl.BlockSpec((TM, n), lambda i, j: (i, 0)),
        out_shape=jax.ShapeDtypeStruct((m, n), out_dtype),
        scratch_shapes=[pltpu.VMEM((TM, n), F32)],
        compiler_params=_params(("parallel", "arbitrary")),
    )(a, w)


def _mm_tn(a, b, *, nb, out_dtype, name):
    m, ka = a.shape[-2:]
    n = b.shape[-1]
    steps = m // TM

    def body(a_ref, b_ref, o_ref, acc_ref):
        i = pl.program_id(1)

        @pl.when(i == 0)
        def _():
            acc_ref[...] = jnp.zeros_like(acc_ref)

        acc_ref[...] += _dot(a_ref[...], b_ref[...], TN)

        @pl.when(i == steps - 1)
        def _():
            o_ref[...] = acc_ref[...].astype(out_dtype)

    def spec(arr, cols):
        if arr.ndim == 2:
            return pl.BlockSpec((TM, cols), lambda j, i: (i, 0))
        return pl.BlockSpec((None, TM, cols), lambda j, i: (j, i, 0))

    return pl.pallas_call(
        body,
        name=name,
        grid=(nb, steps),
        in_specs=[spec(a, ka), spec(b, n)],
        out_specs=pl.BlockSpec((None, ka, n), lambda j, i: (j, 0, 0)),
        out_shape=jax.ShapeDtypeStruct((nb, ka, n), out_dtype),
        scratch_shapes=[pltpu.VMEM((ka, n), F32)],
        compiler_params=_params(("parallel", "arbitrary")),
    )(a, b)


def _rstd(x):
    return lax.rsqrt(jnp.mean(x * x, axis=-1, keepdims=True) + RMS_EPS)


def _rms_bwd(x, g, dy):
    r = _rstd(x)
    u = dy * g
    dx = r * u - x * (r * r * r) * jnp.mean(u * x, axis=-1, keepdims=True)
    return dx, dy * x * r


def _row_spec(cols=D, tm=TM):
    return pl.BlockSpec((tm, cols), lambda i: (i, 0))


def _vec_spec(rows=1, cols=D):
    return pl.BlockSpec((rows, cols), lambda i: (0, 0))


def _rms_fwd(x, g, name):
    def body(x_ref, g_ref, o_ref):
        x = x_ref[...]
        o_ref[...] = (x * _rstd(x) * g_ref[...]).astype(BF16)

    return pl.pallas_call(
        body,
        name=name,
        grid=(LP // TM,),
        in_specs=[_row_spec(), _vec_spec()],
        out_specs=_row_spec(),
        out_shape=jax.ShapeDtypeStruct((LP, D), BF16),
        compiler_params=_params(("parallel",)),
    )(x, g)


def _resid_rms(h0, mix, g_post, g_next):
    def body(h0_ref, mix_ref, gp_ref, gn_ref, h1_ref, xn_ref):
        mix = mix_ref[...]
        h1 = h0_ref[...] + mix * _rstd(mix) * gp_ref[...]
        h1_ref[...] = h1
        xn_ref[...] = (h1 * _rstd(h1) * gn_ref[...]).astype(BF16)

    return pl.pallas_call(
        body,
        name="resid_rms",
        grid=(LP // TM,),
        in_specs=[_row_spec(), _row_spec(), _vec_spec(), _vec_spec()],
        out_specs=[_row_spec(), _row_spec()],
        out_shape=[jax.ShapeDtypeStruct((LP, D), F32), jax.ShapeDtypeStruct((LP, D), BF16)],
        compiler_params=_params(("parallel",)),
    )(h0, mix, g_post, g_next)


def _loss_head(h1, ffn, g_post, target):
    nblk = LP // QB

    def body(h1_ref, ffn_ref, g_ref, t_ref, dout_ref, dffn_ref, loss_ref, dg_ref):
        i = pl.program_id(0)

        @pl.when(i == 0)
        def _():
            loss_ref[...] = jnp.zeros_like(loss_ref)
            dg_ref[...] = jnp.zeros_like(dg_ref)

        ffn = ffn_ref[...]
        g = g_ref[...]
        out = h1_ref[...] + ffn * _rstd(ffn) * g
        err = jnp.where(i > 0, out - t_ref[...], 0.0)
        loss_ref[...] += 0.5 * jnp.sum(err * err) / D
        dout = err / D
        dout_ref[...] = dout
        dffn, dg = _rms_bwd(ffn, g, dout)
        dffn_ref[...] = dffn.astype(BF16)
        dg_ref[0:1, :] += jnp.sum(dg, axis=0, keepdims=True)

    return pl.pallas_call(
        body,
        name="loss_head",
        grid=(nblk,),
        in_specs=[
            _row_spec(tm=QB),
            _row_spec(tm=QB),
            _vec_spec(),
            pl.BlockSpec((QB, D), lambda i: (jnp.maximum(i - 1, 0), 0)),
        ],
        out_specs=[_row_spec(tm=QB), _row_spec(tm=QB), _vec_spec(8, 128), _vec_spec(8, D)],
        out_shape=[
            jax.ShapeDtypeStruct((LP, D), F32),
            jax.ShapeDtypeStruct((LP, D), BF16),
            jax.ShapeDtypeStruct((8, 128), F32),
            jax.ShapeDtypeStruct((8, D), F32),
        ],
        compiler_params=_params(("arbitrary",)),
    )(h1, ffn, g_post, target)


def _mid_bwd(dout, h1, dxn2, mix, g_post_mix, g_pre_ffn):
    def body(dout_ref, h1_ref, dxn_ref, mix_ref, gpm_ref, gpf_ref, dh1_ref, dmix_ref, dg_ref):
        i = pl.program_id(0)

        @pl.when(i == 0)
        def _():
            dg_ref[...] = jnp.zeros_like(dg_ref)

        dx, dg_ffn = _rms_bwd(h1_ref[...], gpf_ref[...], dxn_ref[...])
        dh1 = dout_ref[...] + dx
        dh1_ref[...] = dh1
        dmix, dg_mix = _rms_bwd(mix_ref[...], gpm_ref[...], dh1)
        dmix_ref[...] = dmix.astype(BF16)
        dg_ref[0:1, :] += jnp.sum(dg_mix, axis=0, keepdims=True)
        dg_ref[1:2, :] += jnp.sum(dg_ffn, axis=0, keepdims=True)

    return pl.pallas_call(
        body,
        name="mid_bwd",
        grid=(LP // TM,),
        in_specs=[_row_spec(), _row_spec(), _row_spec(), _row_spec(), _vec_spec(), _vec_spec()],
        out_specs=[_row_spec(), _row_spec(), _vec_spec(8, D)],
        out_shape=[
            jax.ShapeDtypeStruct((LP, D), F32),
            jax.ShapeDtypeStruct((LP, D), BF16),
            jax.ShapeDtypeStruct((8, D), F32),
        ],
        compiler_params=_params(("arbitrary",)),
    )(dout, h1, dxn2, mix, g_post_mix, g_pre_ffn)


def _first_bwd(dh1, h0, dxn1, g_pre_mix):
    def body(dh1_ref, h0_ref, dxn_ref, g_ref, dh0_ref, dg_ref):
        i = pl.program_id(0)

        @pl.when(i == 0)
        def _():
            dg_ref[...] = jnp.zeros_like(dg_ref)

        dx, dg = _rms_bwd(h0_ref[...], g_ref[...], dxn_ref[...])
        dh0_ref[...] = dh1_ref[...] + dx
        dg_ref[0:1, :] += jnp.sum(dg, axis=0, keepdims=True)

    return pl.pallas_call(
        body,
        name="first_bwd",
        grid=(LP // TM,),
        in_specs=[_row_spec(), _row_spec(), _row_spec(), _vec_spec()],
        out_specs=[_row_spec(), _vec_spec(8, D)],
        out_shape=[jax.ShapeDtypeStruct((LP, D), F32), jax.ShapeDtypeStruct((8, D), F32)],
        compiler_params=_params(("arbitrary",)),
    )(dh1, h0, dxn1, g_pre_mix)


def _prev_halo(i):
    return jnp.maximum(i * (TM // HALO) - 1, 0)


def _next_halo(i):
    return jnp.minimum((i + 1) * (TM // HALO), LP // HALO - 1)


def _down(x, s):
    return pltpu.roll(x, s, 0)


def _up(x, s):
    return pltpu.roll(x, x.shape[0] - s, 0)


def _conv_mix_fwd(hin, cw):
    def body(b_ref, c_ref, h_ref, cp_ref, hp_ref, w_ref, y_ref):
        i = pl.program_id(0)
        p = c_ref[...].astype(F32) * h_ref[...].astype(F32)
        pp = jnp.where(i > 0, cp_ref[...].astype(F32) * hp_ref[...].astype(F32), 0.0)
        ext = jnp.concatenate([pp, p], axis=0)
        w = [w_ref[t:t + 1, :] for t in range(3)]
        cv = w[2] * ext + w[1] * _down(ext, 1) + w[0] * _down(ext, 2)
        y_ref[...] = (b_ref[...].astype(F32) * cv[HALO:]).astype(BF16)

    def tile(s):
        return pl.BlockSpec((None, TM, D), lambda i: (s, i, 0))

    def prev(s):
        return pl.BlockSpec((None, HALO, D), lambda i: (s, _prev_halo(i), 0))

    return pl.pallas_call(
        body,
        name="conv_mix_fwd",
        grid=(LP // TM,),
        in_specs=[tile(0), tile(1), tile(2), prev(1), prev(2), _vec_spec(3, D)],
        out_specs=_row_spec(),
        out_shape=jax.ShapeDtypeStruct((LP, D), BF16),
        compiler_params=_params(("parallel",)),
    )(hin, hin, hin, hin, hin, cw)


def _conv_mix_bwd(hin, dy, cw, dhin):
    last = LP // TM - 1

    def body(b_ref, c_ref, h_ref, dy_ref, cp_ref, hp_ref, bn_ref, dyn_ref, w_ref, _, out_ref, dw_ref):
        i = pl.program_id(0)

        @pl.when(i == 0)
        def _():
            dw_ref[...] = jnp.zeros_like(dw_ref)

        b = b_ref[...].astype(F32)
        c = c_ref[...].astype(F32)
        h = h_ref[...].astype(F32)
        dy = dy_ref[...].astype(F32)
        w = [w_ref[t:t + 1, :] for t in range(3)]
        p = c * h
        pp = jnp.where(i > 0, cp_ref[...].astype(F32) * hp_ref[...].astype(F32), 0.0)
        ext = jnp.concatenate([pp, p], axis=0)
        p1 = _down(ext, 1)[HALO:]
        p2 = _down(ext, 2)[HALO:]
        cv = w[2] * p + w[1] * p1 + w[0] * p2
        out_ref[0] = (dy * cv).astype(BF16)
        dcv = dy * b
        dcvn = jnp.where(i < last, dyn_ref[...].astype(F32) * bn_ref[...].astype(F32), 0.0)
        dext = jnp.concatenate([dcv, dcvn], axis=0)
        dp = (w[2] * dext + w[1] * _up(dext, 1) + w[0] * _up(dext, 2))[:TM]
        out_ref[1] = (dp * h).astype(BF16)
        out_ref[2] = (dp * c).astype(BF16)
        dw_ref[0:1, :] += jnp.sum(dcv * p2, axis=0, keepdims=True)
        dw_ref[1:2, :] += jnp.sum(dcv * p1, axis=0, keepdims=True)
        dw_ref[2:3, :] += jnp.sum(dcv * p, axis=0, keepdims=True)

    def tile(s):
        return pl.BlockSpec((None, TM, D), lambda i: (s, i, 0))

    def prev(s):
        return pl.BlockSpec((None, HALO, D), lambda i: (s, _prev_halo(i), 0))

    return pl.pallas_call(
        body,
        name="conv_mix_bwd",
        grid=(LP // TM,),
        in_specs=[
            tile(0), tile(1), tile(2), _row_spec(),
            prev(1), prev(2),
            pl.BlockSpec((None, HALO, D), lambda i: (0, _next_halo(i), 0)),
            pl.BlockSpec((HALO, D), lambda i: (_next_halo(i), 0)),
            _vec_spec(3, D),
            pl.BlockSpec(memory_space=pl.ANY),
        ],
        out_specs=[pl.BlockSpec((3, TM, D), lambda i: (0, i, 0)), _vec_spec(8, D)],
        out_shape=[jax.ShapeDtypeStruct((N_DEV, LP, D), BF16), jax.ShapeDtypeStruct((8, D), F32)],
        input_output_aliases={9: 0},
        compiler_params=_params(("arbitrary",)),
    )(hin, hin, hin, dy, hin, hin, hin, dy, cw, dhin)


GELU_K = math.sqrt(2.0 / math.pi)
GELU_A = 0.044715


def _gelu_and_grad(x):
    x2 = x * x
    t = jnp.tanh(GELU_K * (x + GELU_A * x2 * x))
    gelu = 0.5 * x * (1.0 + t)
    grad = 0.5 * (1.0 + t) + 0.5 * x * (1.0 - t * t) * GELU_K * (1.0 + 3.0 * GELU_A * x2)
    return gelu, grad


def _ffn_act_fwd(ug, cw4):
    def body(u_ref, g_ref, up_ref, w_ref, o_ref):
        i = pl.program_id(1)
        u = u_ref[...].astype(F32)
        up = jnp.where(i > 0, up_ref[...].astype(F32), 0.0)
        ext = jnp.concatenate([up, u], axis=0)
        w = [w_ref[t:t + 1, :] for t in range(3)]
        uc = (w[2] * ext + w[1] * _down(ext, 1) + w[0] * _down(ext, 2))[HALO:]
        gelu, _ = _gelu_and_grad(uc)
        o_ref[...] = (gelu * g_ref[...].astype(F32)).astype(BF16)

    return pl.pallas_call(
        body,
        name="ffn_act_fwd",
        grid=(N_FB, LP // TM),
        in_specs=[
            pl.BlockSpec((None, TM, FB), lambda j, i: (j, i, 0)),
            pl.BlockSpec((None, TM, FB), lambda j, i: (j + N_FB, i, 0)),
            pl.BlockSpec((None, HALO, FB), lambda j, i: (j, _prev_halo(i), 0)),
            pl.BlockSpec((None, 3, FB), lambda j, i: (j, 0, 0)),
        ],
        out_specs=pl.BlockSpec((None, TM, FB), lambda j, i: (j, i, 0)),
        out_shape=jax.ShapeDtypeStruct((N_FB, LP, FB), BF16),
        compiler_params=_params(("parallel", "parallel")),
    )(ug, ug, ug, cw4)


def _ffn_act_bwd(ug, dhid, cw4):
    last = LP // TM - 1
    n = TM + 2 * HALO

    def body(u_ref, g_ref, dh_ref, up_ref, un_ref, gn_ref, dhn_ref, w_ref, dug_ref, dw_ref):
        i = pl.program_id(1)

        @pl.when(i == 0)
        def _():
            dw_ref[...] = jnp.zeros_like(dw_ref)

        w = [w_ref[t:t + 1, :] for t in range(3)]
        u = u_ref[...].astype(F32)
        up = jnp.where(i > 0, up_ref[...].astype(F32), 0.0)
        ext = jnp.concatenate([up, u, un_ref[...].astype(F32)], axis=0)
        u1 = _down(ext, 1)
        u2 = _down(ext, 2)
        uc = w[2] * ext + w[1] * u1 + w[0] * u2
        gelu, ggrad = _gelu_and_grad(uc)
        zeros = jnp.zeros((HALO, FB), F32)
        gext = jnp.concatenate([zeros, g_ref[...].astype(F32), gn_ref[...].astype(F32)], axis=0)
        dhn = jnp.where(i < last, dhn_ref[...].astype(F32), 0.0)
        dhext = jnp.concatenate([zeros, dh_ref[...].astype(F32), dhn], axis=0)
        dug_ref[1] = (dhext * gelu)[HALO:HALO + TM].astype(BF16)
        duc = dhext * gext * ggrad
        du = w[2] * duc + w[1] * _up(duc, 1) + w[0] * _up(duc, 2)
        dug_ref[0] = du[HALO:HALO + TM].astype(BF16)
        row = lax.broadcasted_iota(jnp.int32, (n, 1), 0)
        own = jnp.where((row >= HALO) & (row < HALO + TM), duc, 0.0)
        dw_ref[0:1, :] += jnp.sum(own * u2, axis=0, keepdims=True)
        dw_ref[1:2, :] += jnp.sum(own * u1, axis=0, keepdims=True)
        dw_ref[2:3, :] += jnp.sum(own * ext, axis=0, keepdims=True)

    def tile(off):
        return pl.BlockSpec((None, TM, FB), lambda j, i: (j + off, i, 0))

    def nxt(off):
        return pl.BlockSpec((None, HALO, FB), lambda j, i: (j + off, _next_halo(i), 0))

    return pl.pallas_call(
        body,
        name="ffn_act_bwd",
        grid=(N_FB, LP // TM),
        in_specs=[
            tile(0), tile(N_FB), tile(0),
            pl.BlockSpec((None, HALO, FB), lambda j, i: (j, _prev_halo(i), 0)),
            nxt(0), nxt(N_FB), nxt(0),
            pl.BlockSpec((None, 3, FB), lambda j, i: (j, 0, 0)),
        ],
        out_specs=[
            pl.BlockSpec((2, None, TM, FB), lambda j, i: (0, j, i, 0)),
            pl.BlockSpec((None, 8, FB), lambda j, i: (j, 0, 0)),
        ],
        out_shape=[jax.ShapeDtypeStruct((2, N_FB, LP, FB), BF16), jax.ShapeDtypeStruct((N_FB, 8, FB), F32)],
        compiler_params=_params(("parallel", "arbitrary")),
    )(ug, ug, dhid, ug, ug, ug, dhid, cw4)


def _gate_fwd(bc, ba, hin, bgate):
    def body(bc_ref, ba_ref, gc_ref, ga_ref, b_ref, o_ref):
        b = b_ref[...]
        sc = jax.nn.sigmoid(gc_ref[...].astype(F32) + b[0:1])
        sa = jax.nn.sigmoid(ga_ref[...].astype(F32) + b[1:2])
        o_ref[...] = (sc * bc_ref[...].astype(F32) + sa * ba_ref[...].astype(F32)).astype(BF16)

    def tile(s):
        return pl.BlockSpec((None, TM, D), lambda i: (s, i, 0))

    return pl.pallas_call(
        body,
        name="gate_fwd",
        grid=(LP // TM,),
        in_specs=[_row_spec(), _row_spec(), tile(6), tile(7), _vec_spec(2, D)],
        out_specs=_row_spec(),
        out_shape=jax.ShapeDtypeStruct((LP, D), BF16),
        compiler_params=_params(("parallel",)),
    )(bc, ba, hin, hin, bgate)


def _gate_bwd(dm, bc, ba, hin, bgate):
    def body(dm_ref, bc_ref, ba_ref, gc_ref, ga_ref, b_ref, dbc_ref, dba_ref, dg_ref, db_ref):
        i = pl.program_id(0)

        @pl.when(i == 0)
        def _():
            db_ref[...] = jnp.zeros_like(db_ref)

        b = b_ref[...]
        dm = dm_ref[...].astype(F32)
        sc = jax.nn.sigmoid(gc_ref[...].astype(F32) + b[0:1])
        sa = jax.nn.sigmoid(ga_ref[...].astype(F32) + b[1:2])
        dbc_ref[...] = (dm * sc).astype(BF16)
        dba_ref[...] = (dm * sa).astype(BF16)
        dgc = dm * bc_ref[...].astype(F32) * sc * (1.0 - sc)
        dga = dm * ba_ref[...].astype(F32) * sa * (1.0 - sa)
        dg_ref[0] = dgc.astype(BF16)
        dg_ref[1] = dga.astype(BF16)
        db_ref[0:1, :] += jnp.sum(dgc, axis=0, keepdims=True)
        db_ref[1:2, :] += jnp.sum(dga, axis=0, keepdims=True)

    def tile(s):
        return pl.BlockSpec((None, TM, D), lambda i: (s, i, 0))

    return pl.pallas_call(
        body,
        name="gate_bwd",
        grid=(LP // TM,),
        in_specs=[_row_spec(), _row_spec(), _row_spec(), tile(6), tile(7), _vec_spec(2, D)],
        out_specs=[_row_spec(), _row_spec(), pl.BlockSpec((2, TM, D), lambda i: (3, i, 0)), _vec_spec(8, D)],
        out_shape=[jax.ShapeDtypeStruct((LP, D), BF16)] * 2
        + [jax.ShapeDtypeStruct((N_DEV, LP, D), BF16), jax.ShapeDtypeStruct((8, D), F32)],
        compiler_params=_params(("arbitrary",)),
    )(dm, bc, ba, hin, hin, bgate)


def _softplus(z):
    return jnp.maximum(z, 0.0) + jnp.log(1.0 + jnp.exp(-jnp.abs(z)))


def _cumsum_matrix(inclusive, reverse):
    r = lax.broadcasted_iota(jnp.int32, (QB, 2 * QB), 0)
    c = lax.broadcasted_iota(jnp.int32, (QB, 2 * QB), 1)
    if reverse:
        tri = r > c
    elif inclusive:
        tri = r <= c
    else:
        tri = r < c
    return jnp.where((c >= QB) | tri, 1.0, 0.0).astype(BF16)


def _split_dot(x, m2):
    hi = x.astype(BF16)
    lo = (x - hi.astype(F32)).astype(BF16)
    return _dot(jnp.concatenate([hi, lo], axis=1), m2)


def _stack_heads(x):
    return jnp.concatenate(_split_heads(x), axis=0)


def _block_mask(i, j):
    row = lax.broadcasted_iota(jnp.int32, (AQ, QB), 0) + i * AQ
    col = lax.broadcasted_iota(jnp.int32, (AQ, QB), 1) + j * QB
    return (col < row) & (col >= PAD)


def _key_block(ref, j):
    return ref[pl.ds(pl.multiple_of(j * QB, QB), QB), :]


def _split_heads(x):
    head_a = lax.broadcasted_iota(jnp.int32, x.shape, 1) < HEAD_LANES
    zero = jnp.zeros_like(x)
    return jnp.where(head_a, x, zero), jnp.where(head_a, zero, x)


def _attn_fwd(hin):
    def body(q_ref, k_ref, v_ref, o_ref, lt_ref, c_sc, acc_sc):
        i = pl.program_id(1)
        um = _cumsum_matrix(False, True)
        um2 = jnp.concatenate([um, um], axis=0)
        q = (q_ref[...].astype(F32) * SCALE).astype(BF16)
        c_sc[...] = jnp.zeros_like(c_sc)
        acc_sc[...] = jnp.zeros_like(acc_sc)

        def step(j, masked):
            z2 = _dot(q, _stack_heads(_key_block(k_ref, j)), NT)
            mask = _block_mask(i, j) if masked else None
            a2 = []
            for hd in range(2):
                z = z2[:, hd * QB:(hd + 1) * QB]
                sp = _softplus(z)
                r = _split_dot(jnp.where(mask, sp, 0.0) if masked else sp, um2)
                a = jnp.exp(z - sp - c_sc[hd] - r[:, :QB])
                if masked:
                    a = jnp.where(mask, a, 0.0)
                a2.append(a.astype(BF16))
                c_sc[hd] += r[:, QB:]
            acc_sc[...] += _dot(jnp.concatenate(a2, axis=1), _stack_heads(_key_block(v_ref, j)))

        last = KPQ * i + KPQ - 1

        @pl.loop(0, KPQ)
        def _(t):
            step(last - t, True)

        @pl.loop(0, jnp.maximum(KPQ * i - 1, 0))
        def _(t):
            step(KPQ * i - 1 - t, False)

        @pl.when(i > 0)
        def _():
            step(0, True)

        head_a = lax.broadcasted_iota(jnp.int32, (AQ, QB), 1) < HEAD_LANES
        o_ref[...] = acc_sc[...].astype(BF16)
        lt_ref[...] = jnp.where(head_a, c_sc[0], c_sc[1])

    def seq(s):
        return pl.BlockSpec((None, LP, QB), lambda p, i: (s, 0, p))

    return pl.pallas_call(
        body,
        name="attn_fwd",
        grid=(D // QB, LP // AQ),
        in_specs=[pl.BlockSpec((None, AQ, QB), lambda p, i: (3, i, p)), seq(4), seq(5)],
        out_specs=[pl.BlockSpec((AQ, QB), lambda p, i: (i, p))] * 2,
        out_shape=[jax.ShapeDtypeStruct((LP, D), BF16), jax.ShapeDtypeStruct((LP, D), F32)],
        scratch_shapes=[pltpu.VMEM((2, AQ, QB), F32), pltpu.VMEM((AQ, QB), F32)],
        compiler_params=_params(("parallel", "parallel")),
    )(hin, hin, hin)


def _attn_bwd(hin, do, lt, dhin):
    nq = LP // AQ

    def body(q_ref, k_ref, v_ref, do_ref, lt_ref, _, out_ref, psp_sc, pg_sc, dq_sc, dk_acc, dv_acc):
        i = pl.program_id(1)

        @pl.when(i == 0)
        def _():
            dk_acc[...] = jnp.zeros_like(dk_acc)
            dv_acc[...] = jnp.zeros_like(dv_acc)

        um_sp = _cumsum_matrix(True, False)
        um_sp2 = jnp.concatenate([um_sp, um_sp], axis=0)
        um_g = _cumsum_matrix(False, False)
        q = (q_ref[...].astype(F32) * SCALE).astype(BF16)
        do = do_ref[...]
        head_a = lax.broadcasted_iota(jnp.int32, (AQ, QB), 1) < HEAD_LANES
        key_head_a = lax.broadcasted_iota(jnp.int32, (QB, QB), 1) < HEAD_LANES
        lt = lt_ref[...]
        lt_sw = pltpu.roll(lt, HEAD_LANES, 1)
        totals = (jnp.where(head_a, lt, lt_sw), jnp.where(head_a, lt_sw, lt))
        psp_sc[...] = jnp.zeros_like(psp_sc)
        pg_sc[...] = jnp.zeros_like(pg_sc)
        dq_sc[...] = jnp.zeros_like(dq_sc)

        def step(j, masked):
            k2 = _stack_heads(_key_block(k_ref, j))
            z2 = _dot(q, k2, NT)
            da2 = _dot(do, _stack_heads(_key_block(v_ref, j)), NT)
            mask = _block_mask(i, j) if masked else None
            a2, dz2 = [], []
            for hd in range(2):
                z = z2[:, hd * QB:(hd + 1) * QB]
                sp = _softplus(z)
                r = _split_dot(jnp.where(mask, sp, 0.0) if masked else sp, um_sp2)
                a = jnp.exp(z - sp - (totals[hd] - psp_sc[hd] - r[:, :QB]))
                if masked:
                    a = jnp.where(mask, a, 0.0)
                g = a * da2[:, hd * QB:(hd + 1) * QB]
                rg = _dot(g.astype(BF16), um_g)
                dz = g - jnp.exp(z - sp) * (g + pg_sc[hd] + rg[:, :QB])
                if masked:
                    dz = jnp.where(mask, dz, 0.0)
                a2.append(a.astype(BF16))
                dz2.append(dz.astype(BF16))
                psp_sc[hd] += r[:, QB:]
                pg_sc[hd] += rg[:, QB:]
            dz2 = jnp.concatenate(dz2, axis=1)
            dq_sc[...] += _dot(dz2, k2)
            dk2 = _dot(dz2, q, TN)
            dv2 = _dot(jnp.concatenate(a2, axis=1), do, TN)
            rows = pl.ds(pl.multiple_of(j * QB, QB), QB)
            dk_acc[rows, :] += jnp.where(key_head_a, dk2[:QB], dk2[QB:])
            dv_acc[rows, :] += jnp.where(key_head_a, dv2[:QB], dv2[QB:])

        @pl.when(i > 0)
        def _():
            step(0, True)

        @pl.loop(0, jnp.maximum(KPQ * i - 1, 0))
        def _(t):
            step(t + 1, False)

        @pl.loop(0, KPQ)
        def _(t):
            step(KPQ * i + t, True)

        out_ref[0, pl.ds(pl.multiple_of(i * AQ, AQ), AQ), :] = (dq_sc[...] * SCALE).astype(BF16)

        @pl.when(i == nq - 1)
        def _():
            out_ref[1] = dk_acc[...].astype(BF16)
            out_ref[2] = dv_acc[...].astype(BF16)

    def seq(s):
        return pl.BlockSpec((None, LP, QB), lambda p, i: (s, 0, p))

    blk = pl.BlockSpec((AQ, QB), lambda p, i: (i, p))
    return pl.pallas_call(
        body,
        name="attn_bwd",
        grid=(D // QB, nq),
        in_specs=[pl.BlockSpec((None, AQ, QB), lambda p, i: (3, i, p)), seq(4), seq(5), blk, blk,
                  pl.BlockSpec(memory_space=pl.ANY)],
        out_specs=pl.BlockSpec((3, LP, QB), lambda p, i: (1, 0, p)),
        out_shape=jax.ShapeDtypeStruct((N_DEV, LP, D), BF16),
        input_output_aliases={5: 0},
        scratch_shapes=[pltpu.VMEM((2, AQ, QB), F32)] * 2 + [pltpu.VMEM((AQ, QB), F32)] + [pltpu.VMEM((LP, QB), F32)] * 2,
        compiler_params=_params(("parallel", "arbitrary")),
    )(hin, hin, hin, do, lt, dhin)


def _adamw(pieces, w, m, v, *, rows, row_off, tr, name):
    npieces, _, cols = pieces.shape
    ob = row_off // tr

    def body(p_ref, w_ref, m_ref, v_ref, g_ref, d_ref, nm_ref, nv_ref):
        g = p_ref[0].astype(F32)
        for s in range(1, npieces):
            g = g + p_ref[s].astype(F32)
        w_ = w_ref[...]
        m_new = ADAM_B1 * m_ref[...] + (1.0 - ADAM_B1) * g
        v_new = ADAM_B2 * v_ref[...] + (1.0 - ADAM_B2) * jnp.square(g)
        m_hat = m_new / (1.0 - ADAM_B1 ** ADAM_STEP)
        v_hat = v_new / (1.0 - ADAM_B2 ** ADAM_STEP)
        g_ref[...] = g
        d_ref[...] = -ADAM_LR * (m_hat / (jnp.sqrt(v_hat) + ADAM_EPS) + ADAM_WD * w_)
        nm_ref[...] = m_new
        nv_ref[...] = v_new

    spec = pl.BlockSpec((tr, cols), lambda i: (i, 0))
    return pl.pallas_call(
        body,
        name=name,
        grid=(rows // tr,),
        in_specs=[pl.BlockSpec((npieces, tr, cols), lambda i: (0, ob + i, 0)), spec, spec, spec],
        out_specs=[spec] * 4,
        out_shape=[jax.ShapeDtypeStruct((rows, cols), F32)] * 4,
        compiler_params=_params(("parallel",)),
    )(pieces, w, m, v)


def _sum_pieces(pieces, name):
    npieces, rows, cols = pieces.shape

    def body(p_ref, o_ref):
        acc = p_ref[0]
        for s in range(1, npieces):
            acc = acc + p_ref[s]
        o_ref[...] = acc

    return pl.pallas_call(
        body,
        name=name,
        in_specs=[pl.BlockSpec(memory_space=pltpu.VMEM)],
        out_specs=pl.BlockSpec(memory_space=pltpu.VMEM),
        out_shape=jax.ShapeDtypeStruct((rows, cols), pieces.dtype),
    )(pieces)


SMALL_ROWS = 48
CWF_PAD = 384
GRAD_ROWS = 72
G_META, G_PRE_MIX, G_MID, G_POST_FFN, G_CW_MIX, G_B_GATE, G_CW_FFN = 0, 16, 24, 32, 40, 48, 56
SMALL_FLAT = 8192


def _local_step(x, target, meta, g_pre_mix, g_post_mix, g_pre_ffn, g_post_ffn, cw_mix, b_gate, cw_ffn,
                wpack, w_pc, w_pa, w_out, w_ug, w_dn):
    cw4 = cw_ffn.reshape(3, N_FB, FB).transpose(1, 0, 2)
    h0 = jnp.concatenate([jnp.zeros((PAD, D), F32), meta, x], axis=0)

    xn1 = _rms_fwd(h0, g_pre_mix, "rms_pre_mix")
    hin = _mm(xn1, wpack, w_rows=R_IN, trans_w=False, out_dtype=BF16, name="mm_in")
    y_conv = _conv_mix_fwd(hin, cw_mix)
    o, lt = _attn_fwd(hin)
    bc = _mm(y_conv, w_pc, w_rows=D, trans_w=False, out_dtype=BF16, name="mm_proj_conv")[0]
    ba = _mm(o, w_pa, w_rows=D, trans_w=False, out_dtype=BF16, name="mm_proj_attn")[0]
    merged = _gate_fwd(bc, ba, hin, b_gate)
    mix = _mm(merged, w_out, w_rows=D, trans_w=False, out_dtype=F32, name="mm_out")[0]
    h1, xn2 = _resid_rms(h0, mix, g_post_mix, g_pre_ffn)
    ug = _mm(xn2, w_ug, w_rows=D, trans_w=False, out_dtype=BF16, name="mm_up_gate")
    hid = _ffn_act_fwd(ug, cw4)
    ffn = _mm_sum(hid, w_dn, w_rows=FB, trans_w=False, out_dtype=F32, name="mm_down")
    dout, dffn, loss8, dg_post_ffn = _loss_head(h1, ffn, g_post_ffn, target)

    dhid = _mm(dffn, w_dn, w_rows=FB, trans_w=True, out_dtype=BF16, name="mm_down_dx")
    gw_dn = _mm_tn(hid, dffn, nb=N_FB, out_dtype=BF16, name="mm_down_dw")
    dug, dcw4 = _ffn_act_bwd(ug, dhid, cw4)
    dug = dug.reshape(2 * N_FB, LP, FB)
    dxn2 = _mm_sum(dug, w_ug, w_rows=D, trans_w=True, out_dtype=F32, name="mm_up_gate_dx")
    gw_ug = _mm_tn(xn2, dug, nb=N_DEV, out_dtype=BF16, name="mm_up_gate_dw")
    dh1, dmix, dg_mid = _mid_bwd(dout, h1, dxn2, mix, g_post_mix, g_pre_ffn)
    dmerged = _mm(dmix, w_out, w_rows=D, trans_w=True, out_dtype=BF16, name="mm_out_dx")[0]
    gw_out = _mm_tn(merged, dmix, nb=1, out_dtype=BF16, name="mm_out_dw")
    dbc, dba, dhin, db_gate = _gate_bwd(dmerged, bc, ba, hin, b_gate)
    dy_conv = _mm(dbc, w_pc, w_rows=D, trans_w=True, out_dtype=BF16, name="mm_proj_conv_dx")[0]
    gw_pc = _mm_tn(y_conv, dbc, nb=1, out_dtype=BF16, name="mm_proj_conv_dw")
    do = _mm(dba, w_pa, w_rows=D, trans_w=True, out_dtype=BF16, name="mm_proj_attn_dx")[0]
    gw_pa = _mm_tn(o, dba, nb=1, out_dtype=BF16, name="mm_proj_attn_dw")
    dhin, dcw_mix = _conv_mix_bwd(hin, dy_conv, cw_mix, dhin)
    dhin = _attn_bwd(hin, do, lt, dhin)
    dxn1 = _mm_sum(dhin, wpack, w_rows=R_IN, trans_w=True, out_dtype=F32, name="mm_in_dx")
    gw_in = _mm_tn(xn1, dhin, nb=N_DEV, out_dtype=BF16, name="mm_in_dw")
    dh0, dg_pre_mix = _first_bwd(dh1, h0, dxn1, g_pre_mix)

    gpack = jnp.concatenate(
        [gw_in, gw_pc.reshape(N_DEV, R_PROJ, D), gw_pa.reshape(N_DEV, R_PROJ, D), gw_out.reshape(N_DEV, R_PROJ, D),
         gw_dn.reshape(N_DEV, R_DOWN, D)], axis=1)
    dcw_ffn = dcw4[:, :3].transpose(1, 0, 2).reshape(3, D_FF)
    small = jnp.concatenate(
        [dh0[PAD:OFF], dg_pre_mix, dg_mid, dg_post_ffn, dcw_mix, db_gate,
         jnp.pad(dcw_ffn.reshape(-1), (0, 16 * D - 3 * D_FF)).reshape(16, D)], axis=0)
    return loss8[0, 0], dh0[OFF:], gpack, gw_ug, small


def kernel(x, meta_tokens, g_pre_mix, w_in, conv_w_mix, w_proj_conv, w_proj_attn, b_gate, w_out, g_post_mix, g_pre_ffn, w_up_gate, conv_w_ffn, w_down, g_post_ffn, loss_target, m_meta_tokens, m_g_pre_mix, m_w_in, m_conv_w_mix, m_w_proj_conv, m_w_proj_attn, m_b_gate, m_w_out, m_g_post_mix, m_g_pre_ffn, m_w_up_gate, m_conv_w_ffn, m_w_down, m_g_post_ffn, v_meta_tokens, v_g_pre_mix, v_w_in, v_conv_w_mix, v_w_proj_conv, v_w_proj_attn, v_b_gate, v_w_out, v_g_post_mix, v_g_pre_ffn, v_w_up_gate, v_conv_w_ffn, v_w_down, v_g_post_ffn):
    me = 4 * lax.axis_index("x") + 2 * lax.axis_index("y") + lax.axis_index("c")

    def rows_to(a, n):
        return jnp.pad(a, ((0, n - a.shape[0]), (0, 0)))

    small_shard = jnp.concatenate(
        [meta_tokens, rows_to(conv_w_mix[0], 8), rows_to(b_gate[0], 8),
         rows_to(jnp.pad(conv_w_ffn[0], ((0, 0), (0, CWF_PAD - R_DOWN))).reshape(9, 128), 16)], axis=0)
    wshard = jnp.concatenate([w_in[0], w_proj_conv[0], w_proj_attn[0], w_out[0], w_down[0]], axis=0).astype(BF16)
    ugshard = w_up_gate[0].astype(BF16)
    (small_all,) = _exchange([small_shard], True, "gather_small")
    wpack, w_ug = _exchange([wshard, ugshard], True, "gather_weights")

    def unshard(rows):
        return rows.transpose(1, 0, 2).reshape(rows.shape[1], N_DEV * rows.shape[2])

    meta_full = unshard(small_all[:, 0:16])
    cw_mix_full = unshard(small_all[:, 16:19])
    b_gate_full = unshard(small_all[:, 24:26])
    cw_ffn_full = unshard(small_all[:, 32:41].reshape(N_DEV, 3, CWF_PAD)[:, :, :R_DOWN])
    w_pc = wpack[:, O_PC:O_PA].reshape(1, D, D)
    w_pa = wpack[:, O_PA:O_OUT].reshape(1, D, D)
    w_o = wpack[:, O_OUT:O_DOWN].reshape(1, D, D)
    w_dn = wpack[:, O_DOWN:].reshape(N_FB, FB, D)

    loss_local, grad_x, gpack, gw_ug, gsmall = _local_step(
        x[0], loss_target[0], meta_full, g_pre_mix, g_post_mix, g_pre_ffn, g_post_ffn, cw_mix_full, b_gate_full,
        cw_ffn_full, wpack, w_pc, w_pa, w_o, w_ug, w_dn)

    rpack, rug = _exchange([gpack, gw_ug], False, "scatter_grads")
    (rsmall,) = _exchange([gsmall], True, "gather_small_grads")
    gs = _sum_pieces(rsmall, "sum_small_grads")
    loss = lax.psum(loss_local, ("x", "y", "c"))

    def cols(a, width):
        return lax.dynamic_slice_in_dim(a, me * width, width, axis=1)

    g_meta = cols(gs[G_META:G_META + N_META], 128)
    g_gpm, g_gff = gs[G_PRE_MIX:G_PRE_MIX + 1], gs[G_POST_FFN:G_POST_FFN + 1]
    g_gpo, g_gpf = gs[G_MID:G_MID + 1], gs[G_MID + 1:G_MID + 2]
    g_cwm = cols(gs[G_CW_MIX:G_CW_MIX + 3], 128)[None]
    g_bg = cols(gs[G_B_GATE:G_B_GATE + 2], 128)[None]
    g_cwf = cols(gs[G_CW_FFN:G_CW_FFN + 9].reshape(-1)[:3 * D_FF].reshape(3, D_FF), R_DOWN)[None]

    def big(pieces, w, m, v, rows, row_off, tr, name):
        g, d, nm, nv = _adamw(pieces, w[0], m[0], v[0], rows=rows, row_off=row_off, tr=tr, name=name)
        return g[None], d[None], nm[None], nv[None]

    r_in = big(rpack, w_in, m_w_in, v_w_in, R_IN, 0, 256, "adamw_in")
    r_pc = big(rpack, w_proj_conv, m_w_proj_conv, v_w_proj_conv, R_PROJ, O_PC, R_PROJ, "adamw_proj_conv")
    r_pa = big(rpack, w_proj_attn, m_w_proj_attn, v_w_proj_attn, R_PROJ, O_PA, R_PROJ, "adamw_proj_attn")
    r_out = big(rpack, w_out, m_w_out, v_w_out, R_PROJ, O_OUT, R_PROJ, "adamw_out")
    r_dn = big(rpack, w_down, m_w_down, v_w_down, R_DOWN, O_DOWN, R_DOWN, "adamw_down")
    r_ug = big(rug, w_up_gate, m_w_up_gate, v_w_up_gate, D, 0, 256, "adamw_up_gate")

    small_w = [meta_tokens, g_pre_mix, conv_w_mix, b_gate, g_post_mix, g_pre_ffn, conv_w_ffn, g_post_ffn]
    small_g = [g_meta, g_gpm, g_cwm, g_bg, g_gpo, g_gpf, g_cwf, g_gff]
    small_m = [m_meta_tokens, m_g_pre_mix, m_conv_w_mix, m_b_gate, m_g_post_mix, m_g_pre_ffn, m_conv_w_ffn, m_g_post_ffn]
    small_v = [v_meta_tokens, v_g_pre_mix, v_conv_w_mix, v_b_gate, v_g_post_mix, v_g_pre_ffn, v_conv_w_ffn, v_g_post_ffn]

    def flat(arrs, fill):
        parts = [a.reshape(-1) for a in arrs]
        used = sum(p.shape[0] for p in parts)
        return jnp.concatenate(parts + [jnp.full((SMALL_FLAT - used,), fill, F32)]).reshape(SMALL_FLAT // 128, 128)

    sg, sd, sm, sv = _adamw(flat(small_g, 0.0)[None], flat(small_w, 0.0), flat(small_m, 0.0), flat(small_v, 1.0),
                            rows=SMALL_FLAT // 128, row_off=0, tr=SMALL_FLAT // 128, name="adamw_small")

    def unflat(packed):
        out, pos = [], 0
        p = packed.reshape(-1)
        for a in small_w:
            out.append(p[pos:pos + a.size].reshape(a.shape))
            pos += a.size
        return out

    s_g, s_d, s_m, s_v = unflat(sg), unflat(sd), unflat(sm), unflat(sv)

    def ordered(k, smalls):
        meta, gpm, cwm, bg, gpo, gpf, cwf, gff = smalls
        return [meta, gpm, r_in[k], cwm, r_pc[k], r_pa[k], bg, r_out[k], gpo, gpf, r_ug[k], cwf, r_dn[k], gff]

    return (loss, grad_x[None], *ordered(0, s_g), *ordered(1, s_d), *ordered(2, s_m), *ordered(3, s_v))
```

```python
import functools
import math

import jax
import jax.numpy as jnp
from jax import lax
from jax.experimental import pallas as pl
from jax.experimental.pallas import tpu as pltpu

F32 = jnp.float32
BF16 = jnp.bfloat16

D = 1024
SEQ = 4096
N_META = 16
PAD = 112
OFF = PAD + N_META
LP = OFF + SEQ
QB = 128
AQ = 384
KPQ = AQ // QB
TM = 384
HALO = 16
N_DEV = 8
D_FF = 2816
FB = 704
N_FB = D_FF // FB
RMS_EPS = 1e-6
SCALE = 0.125
HEAD_LANES = 64
VMEM_LIMIT = 56 * 1024 * 1024

ADAM_LR = 0.001
ADAM_B1 = 0.9
ADAM_B2 = 0.999
ADAM_EPS = 1e-08
ADAM_WD = 0.01
ADAM_STEP = 10

R_PROJ, R_DOWN = 128, 352
O_PC = 0
O_PA = O_PC + R_PROJ
O_OUT = O_PA + R_PROJ
O_DOWN = O_OUT + R_PROJ
R_PACK = O_DOWN + R_DOWN

NT = (((1,), (1,)), ((), ()))
NN = (((1,), (0,)), ((), ()))
TN = (((0,), (0,)), ((), ()))


def _params(sem):
    return pltpu.CompilerParams(dimension_semantics=sem, vmem_limit_bytes=VMEM_LIMIT)


def _dot(a, b, dn=NN):
    return lax.dot_general(a, b, dn, preferred_element_type=F32)


def _exchange_copies(ins, outs, gathers, send_sems, recv_sems, loc_sems):
    x, y, c = lax.axis_index("x"), lax.axis_index("y"), lax.axis_index("c")
    me = 4 * x + 2 * y + c
    copies = []
    for a, gather in enumerate(gathers):
        copies.append(pltpu.make_async_copy(ins[a] if gather else ins[a].at[me], outs[a].at[me], loc_sems.at[a]))
    for k in range(1, N_DEV):
        px = 1 - x if k & 4 else x
        py = 1 - y if k & 2 else y
        pc = 1 - c if k & 1 else c
        peer = 4 * px + 2 * py + pc
        for a, gather in enumerate(gathers):
            copies.append(pltpu.make_async_remote_copy(
                src_ref=ins[a] if gather else ins[a].at[peer],
                dst_ref=outs[a].at[me],
                send_sem=send_sems.at[a * (N_DEV - 1) + k - 1],
                recv_sem=recv_sems.at[a * (N_DEV - 1) + k - 1],
                device_id=(px, py, pc),
                device_id_type=pl.DeviceIdType.MESH,
            ))
    return copies


def _exchange_shapes(arrs, gathers):
    return [jax.ShapeDtypeStruct((N_DEV,) + (a.shape if g else a.shape[1:]), a.dtype) for a, g in zip(arrs, gathers)]


def _exchange_sems(n):
    return [pltpu.SemaphoreType.DMA((n * (N_DEV - 1),)), pltpu.SemaphoreType.DMA((n * (N_DEV - 1),)),
            pltpu.SemaphoreType.DMA((n,))]


ANY_SPEC = pl.BlockSpec(memory_space=pl.ANY)


def _exchange(arrs, gathers, name):
    n = len(arrs)

    def body(*refs):
        copies = _exchange_copies(refs[:n], refs[n:2 * n], gathers, *refs[2 * n:])
        for cp in copies:
            cp.start()
        for cp in copies:
            cp.wait()

    return pl.pallas_call(
        body,
        name=name,
        out_shape=_exchange_shapes(arrs, gathers),
        in_specs=[ANY_SPEC] * n,
        out_specs=[ANY_SPEC] * n,
        scratch_shapes=_exchange_sems(n),
    )(*arrs)


def _mm(a, w, *, w_rows, trans_w, out_dtype, name):
    nb, _, wc = w.shape
    m, k = a.shape[-2:]
    n = w_rows if trans_w else wc
    dn = NT if trans_w else NN

    def body(a_ref, w_ref, o_ref):
        o_ref[...] = _dot(a_ref[...], w_ref[...], dn).astype(out_dtype)

    if a.ndim == 2:
        a_spec = pl.BlockSpec((TM, k), lambda j, i: (i, 0))
    else:
        a_spec = pl.BlockSpec((None, TM, k), lambda j, i: (j, i, 0))
    return pl.pallas_call(
        body,
        name=name,
        grid=(nb, m // TM),
        in_specs=[a_spec, pl.BlockSpec((None, w_rows, wc), lambda j, i: (j, 0, 0))],
        out_specs=pl.BlockSpec((None, TM, n), lambda j, i: (j, i, 0)),
        out_shape=jax.ShapeDtypeStruct((nb, m, n), out_dtype),
        compiler_params=_params(("parallel", "parallel")),
    )(a, w)


def _mm_sum(a, w, *, w_rows, trans_w, out_dtype, name, exchange=((), ())):
    nb, m, k = a.shape
    wc = w.shape[2]
    n = w_rows if trans_w else wc
    dn = NT if trans_w else NN
    ex_arrs, gathers = exchange
    ne = len(ex_arrs)
    steps = m // TM

    def body(*refs):
        a_ref, w_ref, o_ref, acc_ref = refs[0], refs[1], refs[2 + ne], refs[3 + 2 * ne]
        i, j = pl.program_id(0), pl.program_id(1)

        def copies():
            return _exchange_copies(refs[2:2 + ne], refs[3 + ne:3 + 2 * ne], gathers, *refs[4 + 2 * ne:])

        if ne:
            @pl.when((i == 0) & (j == 0))
            def _():
                for cp in copies():
                    cp.start()

        @pl.when(j == 0)
        def _():
            acc_ref[...] = jnp.zeros_like(acc_ref)

        acc_ref[...] += _dot(a_ref[...], w_ref[...], dn)

        @pl.when(j == nb - 1)
        def _():
            o_ref[...] = acc_ref[...].astype(out_dtype)

        if ne:
            @pl.when((i == steps - 1) & (j == nb - 1))
            def _():
                for cp in copies():
                    cp.wait()

    res = pl.pallas_call(
        body,
        name=name,
        grid=(steps, nb),
        in_specs=[
            pl.BlockSpec((None, TM, k), lambda i, j: (j, i, 0)),
            pl.BlockSpec((None, w_rows, wc), lambda i, j: (j, 0, 0)),
        ] + [ANY_SPEC] * ne,
        out_specs=[pl.BlockSpec((TM, n), lambda i, j: (i, 0))] + [ANY_SPEC] * ne,
        out_shape=[jax.ShapeDtypeStruct((m, n), out_dtype)] + _exchange_shapes(ex_arrs, gathers),
        scratch_shapes=[pltpu.VMEM((TM, n), F32)] + (_exchange_sems(ne) if ne else []),
        compiler_params=_params(("arbitrary", "arbitrary")),
    )(a, w, *ex_arrs)
    return res if ne else res[0]


def _mm_tn(a, b, *, nb, out_dtype, name):
    m, ka = a.shape[-2:]
    n = b.shape[-1]
    steps = m // TM

    def body(a_ref, b_ref, o_ref, acc_ref):
        i = pl.program_id(1)

        @pl.when(i == 0)
        def _():
            acc_ref[...] = jnp.zeros_like(acc_ref)

        acc_ref[...] += _dot(a_ref[...], b_ref[...], TN)

        @pl.when(i == steps - 1)
        def _():
            o_ref[...] = acc_ref[...].astype(out_dtype)

    def spec(arr, cols):
        if arr.ndim == 2:
            return pl.BlockSpec((TM, cols), lambda j, i: (i, 0))
        return pl.BlockSpec((None, TM, cols), lambda j, i: (j, i, 0))

    return pl.pallas_call(
        body,
        name=name,
        grid=(nb, steps),
        in_specs=[spec(a, ka), spec(b, n)],
        out_specs=pl.BlockSpec((None, ka, n), lambda j, i: (j, 0, 0)),
        out_shape=jax.ShapeDtypeStruct((nb, ka, n), out_dtype),
        scratch_shapes=[pltpu.VMEM((ka, n), F32)],
        compiler_params=_params(("parallel", "arbitrary")),
    )(a, b)


def _rstd(x):
    return lax.rsqrt(jnp.mean(x * x, axis=-1, keepdims=True) + RMS_EPS)


def _rms_bwd(x, g, dy):
    r = _rstd(x)
    u = dy * g
    dx = r * u - x * (r * r * r) * jnp.mean(u * x, axis=-1, keepdims=True)
    return dx, dy * x * r


def _row_spec(cols=D, tm=TM):
    return pl.BlockSpec((tm, cols), lambda i: (i, 0))


def _vec_spec(rows=1, cols=D):
    return pl.BlockSpec((rows, cols), lambda i: (0, 0))


def _rms_fwd(x, g, name):
    def body(x_ref, g_ref, o_ref):
        x = x_ref[...]
        o_ref[...] = (x * _rstd(x) * g_ref[...]).astype(BF16)

    return pl.pallas_call(
        body,
        name=name,
        grid=(LP // TM,),
        in_specs=[_row_spec(), _vec_spec()],
        out_specs=_row_spec(),
        out_shape=jax.ShapeDtypeStruct((LP, D), BF16),
        compiler_params=_params(("parallel",)),
    )(x, g)


def _resid_rms(h0, mix, g_post, g_next):
    def body(h0_ref, mix_ref, gp_ref, gn_ref, h1_ref, xn_ref):
        mix = mix_ref[...]
        h1 = h0_ref[...] + mix * _rstd(mix) * gp_ref[...]
        h1_ref[...] = h1
        xn_ref[...] = (h1 * _rstd(h1) * gn_ref[...]).astype(BF16)

    return pl.pallas_call(
        body,
        name="resid_rms",
        grid=(LP // TM,),
        in_specs=[_row_spec(), _row_spec(), _vec_spec(), _vec_spec()],
        out_specs=[_row_spec(), _row_spec()],
        out_shape=[jax.ShapeDtypeStruct((LP, D), F32), jax.ShapeDtypeStruct((LP, D), BF16)],
        compiler_params=_params(("parallel",)),
    )(h0, mix, g_post, g_next)


def _loss_head(h1, ffn, g_post, target):
    nblk = LP // QB

    def body(h1_ref, ffn_ref, g_ref, t_ref, dout_ref, dffn_ref, loss_ref, dg_ref):
        i = pl.program_id(0)

        @pl.when(i == 0)
        def _():
            loss_ref[...] = jnp.zeros_like(loss_ref)
            dg_ref[...] = jnp.zeros_like(dg_ref)

        ffn = ffn_ref[...]
        g = g_ref[...]
        out = h1_ref[...] + ffn * _rstd(ffn) * g
        err = jnp.where(i > 0, out - t_ref[...], 0.0)
        loss_ref[...] += 0.5 * jnp.sum(err * err) / D
        dout = err / D
        dout_ref[...] = dout
        dffn, dg = _rms_bwd(ffn, g, dout)
        dffn_ref[...] = dffn.astype(BF16)
        dg_ref[0:1, :] += jnp.sum(dg, axis=0, keepdims=True)

    return pl.pallas_call(
        body,
        name="loss_head",
        grid=(nblk,),
        in_specs=[
            _row_spec(tm=QB),
            _row_spec(tm=QB),
            _vec_spec(),
            pl.BlockSpec((QB, D), lambda i: (jnp.maximum(i - 1, 0), 0)),
        ],
        out_specs=[_row_spec(tm=QB), _row_spec(tm=QB), _vec_spec(8, 128), _vec_spec(8, D)],
        out_shape=[
            jax.ShapeDtypeStruct((LP, D), F32),
            jax.ShapeDtypeStruct((LP, D), BF16),
            jax.ShapeDtypeStruct((8, 128), F32),
            jax.ShapeDtypeStruct((8, D), F32),
        ],
        compiler_params=_params(("arbitrary",)),
    )(h1, ffn, g_post, target)


def _mid_bwd(dout, h1, dxn2, mix, g_post_mix, g_pre_ffn):
    def body(dout_ref, h1_ref, dxn_ref, mix_ref, gpm_ref, gpf_ref, dh1_ref, dmix_ref, dg_ref):
        i = pl.program_id(0)

        @pl.when(i == 0)
        def _():
            dg_ref[...] = jnp.zeros_like(dg_ref)

        dx, dg_ffn = _rms_bwd(h1_ref[...], gpf_ref[...], dxn_ref[...])
        dh1 = dout_ref[...] + dx
        dh1_ref[...] = dh1
        dmix, dg_mix = _rms_bwd(mix_ref[...], gpm_ref[...], dh1)
        dmix_ref[...] = dmix.astype(BF16)
        dg_ref[0:1, :] += jnp.sum(dg_mix, axis=0, keepdims=True)
        dg_ref[1:2, :] += jnp.sum(dg_ffn, axis=0, keepdims=True)

    return pl.pallas_call(
        body,
        name="mid_bwd",
        grid=(LP // TM,),
        in_specs=[_row_spec(), _row_spec(), _row_spec(), _row_spec(), _vec_spec(), _vec_spec()],
        out_specs=[_row_spec(), _row_spec(), _vec_spec(8, D)],
        out_shape=[
            jax.ShapeDtypeStruct((LP, D), F32),
            jax.ShapeDtypeStruct((LP, D), BF16),
            jax.ShapeDtypeStruct((8, D), F32),
        ],
        compiler_params=_params(("arbitrary",)),
    )(dout, h1, dxn2, mix, g_post_mix, g_pre_ffn)


def _first_bwd(dh1, h0, dxn1, g_pre_mix):
    def body(dh1_ref, h0_ref, dxn_ref, g_ref, dh0_ref, dg_ref):
        i = pl.program_id(0)

        @pl.when(i == 0)
        def _():
            dg_ref[...] = jnp.zeros_like(dg_ref)

        dx, dg = _rms_bwd(h0_ref[...], g_ref[...], dxn_ref[...])
        dh0_ref[...] = dh1_ref[...] + dx
        dg_ref[0:1, :] += jnp.sum(dg, axis=0, keepdims=True)

    return pl.pallas_call(
        body,
        name="first_bwd",
        grid=(LP // TM,),
        in_specs=[_row_spec(), _row_spec(), _row_spec(), _vec_spec()],
        out_specs=[_row_spec(), _vec_spec(8, D)],
        out_shape=[jax.ShapeDtypeStruct((LP, D), F32), jax.ShapeDtypeStruct((8, D), F32)],
        compiler_params=_params(("arbitrary",)),
    )(dh1, h0, dxn1, g_pre_mix)


def _prev_halo(i):
    return jnp.maximum(i * (TM // HALO) - 1, 0)


def _next_halo(i):
    return jnp.minimum((i + 1) * (TM // HALO), LP // HALO - 1)


def _down(x, s):
    return pltpu.roll(x, s, 0)


def _up(x, s):
    return pltpu.roll(x, x.shape[0] - s, 0)


def _conv_mix_fwd(hin, cw):
    def body(b_ref, c_ref, h_ref, cp_ref, hp_ref, w_ref, y_ref):
        i = pl.program_id(0)
        p = c_ref[...].astype(F32) * h_ref[...].astype(F32)
        pp = jnp.where(i > 0, cp_ref[...].astype(F32) * hp_ref[...].astype(F32), 0.0)
        ext = jnp.concatenate([pp, p], axis=0)
        w = [w_ref[t:t + 1, :] for t in range(3)]
        cv = w[2] * ext + w[1] * _down(ext, 1) + w[0] * _down(ext, 2)
        y_ref[...] = (b_ref[...].astype(F32) * cv[HALO:]).astype(BF16)

    def tile(s):
        return pl.BlockSpec((None, TM, D), lambda i: (s, i, 0))

    def prev(s):
        return pl.BlockSpec((None, HALO, D), lambda i: (s, _prev_halo(i), 0))

    return pl.pallas_call(
        body,
        name="conv_mix_fwd",
        grid=(LP // TM,),
        in_specs=[tile(0), tile(1), tile(2), prev(1), prev(2), _vec_spec(3, D)],
        out_specs=_row_spec(),
        out_shape=jax.ShapeDtypeStruct((LP, D), BF16),
        compiler_params=_params(("parallel",)),
    )(hin, hin, hin, hin, hin, cw)


def _conv_mix_bwd(hin, dy, cw, dhin):
    last = LP // TM - 1

    def body(b_ref, c_ref, h_ref, dy_ref, cp_ref, hp_ref, bn_ref, dyn_ref, w_ref, _, out_ref, dw_ref):
        i = pl.program_id(0)

        @pl.when(i == 0)
        def _():
            dw_ref[...] = jnp.zeros_like(dw_ref)

        b = b_ref[...].astype(F32)
        c = c_ref[...].astype(F32)
        h = h_ref[...].astype(F32)
        dy = dy_ref[...].astype(F32)
        w = [w_ref[t:t + 1, :] for t in range(3)]
        p = c * h
        pp = jnp.where(i > 0, cp_ref[...].astype(F32) * hp_ref[...].astype(F32), 0.0)
        ext = jnp.concatenate([pp, p], axis=0)
        p1 = _down(ext, 1)[HALO:]
        p2 = _down(ext, 2)[HALO:]
        cv = w[2] * p + w[1] * p1 + w[0] * p2
        out_ref[0] = (dy * cv).astype(BF16)
        dcv = dy * b
        dcvn = jnp.where(i < last, dyn_ref[...].astype(F32) * bn_ref[...].astype(F32), 0.0)
        dext = jnp.concatenate([dcv, dcvn], axis=0)
        dp = (w[2] * dext + w[1] * _up(dext, 1) + w[0] * _up(dext, 2))[:TM]
        out_ref[1] = (dp * h).astype(BF16)
        out_ref[2] = (dp * c).astype(BF16)
        dw_ref[0:1, :] += jnp.sum(dcv * p2, axis=0, keepdims=True)
        dw_ref[1:2, :] += jnp.sum(dcv * p1, axis=0, keepdims=True)
        dw_ref[2:3, :] += jnp.sum(dcv * p, axis=0, keepdims=True)

    def tile(s):
        return pl.BlockSpec((None, TM, D), lambda i: (s, i, 0))

    def prev(s):
        return pl.BlockSpec((None, HALO, D), lambda i: (s, _prev_halo(i), 0))

    return pl.pallas_call(
        body,
        name="conv_mix_bwd",
        grid=(LP // TM,),
        in_specs=[
            tile(0), tile(1), tile(2), _row_spec(),
            prev(1), prev(2),
            pl.BlockSpec((None, HALO, D), lambda i: (0, _next_halo(i), 0)),
            pl.BlockSpec((HALO, D), lambda i: (_next_halo(i), 0)),
            _vec_spec(3, D),
            pl.BlockSpec(memory_space=pl.ANY),
        ],
        out_specs=[pl.BlockSpec((3, TM, D), lambda i: (0, i, 0)), _vec_spec(8, D)],
        out_shape=[jax.ShapeDtypeStruct((N_DEV, LP, D), BF16), jax.ShapeDtypeStruct((8, D), F32)],
        input_output_aliases={9: 0},
        compiler_params=_params(("arbitrary",)),
    )(hin, hin, hin, dy, hin, hin, hin, dy, cw, dhin)


GELU_K = math.sqrt(2.0 / math.pi)
GELU_A = 0.044715


def _gelu_and_grad(x):
    x2 = x * x
    t = jnp.tanh(GELU_K * (x + GELU_A * x2 * x))
    gelu = 0.5 * x * (1.0 + t)
    grad = 0.5 * (1.0 + t) + 0.5 * x * (1.0 - t * t) * GELU_K * (1.0 + 3.0 * GELU_A * x2)
    return gelu, grad


def _ffn_act_fwd(ug, cw4):
    def body(u_ref, g_ref, up_ref, w_ref, o_ref):
        i = pl.program_id(1)
        u = u_ref[...].astype(F32)
        up = jnp.where(i > 0, up_ref[...].astype(F32), 0.0)
        ext = jnp.concatenate([up, u], axis=0)
        w = [w_ref[t:t + 1, :] for t in range(3)]
        uc = (w[2] * ext + w[1] * _down(ext, 1) + w[0] * _down(ext, 2))[HALO:]
        gelu, _ = _gelu_and_grad(uc)
        o_ref[...] = (gelu * g_ref[...].astype(F32)).astype(BF16)

    return pl.pallas_call(
        body,
        name="ffn_act_fwd",
        grid=(N_FB, LP // TM),
        in_specs=[
            pl.BlockSpec((None, TM, FB), lambda j, i: (j, i, 0)),
            pl.BlockSpec((None, TM, FB), lambda j, i: (j + N_FB, i, 0)),
            pl.BlockSpec((None, HALO, FB), lambda j, i: (j, _prev_halo(i), 0)),
            pl.BlockSpec((None, 3, FB), lambda j, i: (j, 0, 0)),
        ],
        out_specs=pl.BlockSpec((None, TM, FB), lambda j, i: (j, i, 0)),
        out_shape=jax.ShapeDtypeStruct((N_FB, LP, FB), BF16),
        compiler_params=_params(("parallel", "parallel")),
    )(ug, ug, ug, cw4)


def _ffn_act_bwd(ug, dhid, cw4):
    last = LP // TM - 1
    n = TM + 2 * HALO

    def body(u_ref, g_ref, dh_ref, up_ref, un_ref, gn_ref, dhn_ref, w_ref, dug_ref, dw_ref):
        i = pl.program_id(1)

        @pl.when(i == 0)
        def _():
            dw_ref[...] = jnp.zeros_like(dw_ref)

        w = [w_ref[t:t + 1, :] for t in range(3)]
        u = u_ref[...].astype(F32)
        up = jnp.where(i > 0, up_ref[...].astype(F32), 0.0)
        ext = jnp.concatenate([up, u, un_ref[...].astype(F32)], axis=0)
        u1 = _down(ext, 1)
        u2 = _down(ext, 2)
        uc = w[2] * ext + w[1] * u1 + w[0] * u2
        gelu, ggrad = _gelu_and_grad(uc)
        zeros = jnp.zeros((HALO, FB), F32)
        gext = jnp.concatenate([zeros, g_ref[...].astype(F32), gn_ref[...].astype(F32)], axis=0)
        dhn = jnp.where(i < last, dhn_ref[...].astype(F32), 0.0)
        dhext = jnp.concatenate([zeros, dh_ref[...].astype(F32), dhn], axis=0)
        dug_ref[1] = (dhext * gelu)[HALO:HALO + TM].astype(BF16)
        duc = dhext * gext * ggrad
        du = w[2] * duc + w[1] * _up(duc, 1) + w[0] * _up(duc, 2)
        dug_ref[0] = du[HALO:HALO + TM].astype(BF16)
        row = lax.broadcasted_iota(jnp.int32, (n, 1), 0)
        own = jnp.where((row >= HALO) & (row < HALO + TM), duc, 0.0)
        dw_ref[0:1, :] += jnp.sum(own * u2, axis=0, keepdims=True)
        dw_ref[1:2, :] += jnp.sum(own * u1, axis=0, keepdims=True)
        dw_ref[2:3, :] += jnp.sum(own * ext, axis=0, keepdims=True)

    def tile(off):
        return pl.BlockSpec((None, TM, FB), lambda j, i: (j + off, i, 0))

    def nxt(off):
        return pl.BlockSpec((None, HALO, FB), lambda j, i: (j + off, _next_halo(i), 0))

    return pl.pallas_call(
        body,
        name="ffn_act_bwd",
        grid=(N_FB, LP // TM),
        in_specs=[
            tile(0), tile(N_FB), tile(0),
            pl.BlockSpec((None, HALO, FB), lambda j, i: (j, _prev_halo(i), 0)),
            nxt(0), nxt(N_FB), nxt(0),
            pl.BlockSpec((None, 3, FB), lambda j, i: (j, 0, 0)),
        ],
        out_specs=[
            pl.BlockSpec((2, None, TM, FB), lambda j, i: (0, j, i, 0)),
            pl.BlockSpec((None, 8, FB), lambda j, i: (j, 0, 0)),
        ],
        out_shape=[jax.ShapeDtypeStruct((2, N_FB, LP, FB), BF16), jax.ShapeDtypeStruct((N_FB, 8, FB), F32)],
        compiler_params=_params(("parallel", "arbitrary")),
    )(ug, ug, dhid, ug, ug, ug, dhid, cw4)


def _gate_fwd(bc, ba, hin, bgate):
    def body(bc_ref, ba_ref, gc_ref, ga_ref, b_ref, o_ref):
        b = b_ref[...]
        sc = jax.nn.sigmoid(gc_ref[...].astype(F32) + b[0:1])
        sa = jax.nn.sigmoid(ga_ref[...].astype(F32) + b[1:2])
        o_ref[...] = (sc * bc_ref[...].astype(F32) + sa * ba_ref[...].astype(F32)).astype(BF16)

    def tile(s):
        return pl.BlockSpec((None, TM, D), lambda i: (s, i, 0))

    return pl.pallas_call(
        body,
        name="gate_fwd",
        grid=(LP // TM,),
        in_specs=[_row_spec(), _row_spec(), tile(6), tile(7), _vec_spec(2, D)],
        out_specs=_row_spec(),
        out_shape=jax.ShapeDtypeStruct((LP, D), BF16),
        compiler_params=_params(("parallel",)),
    )(bc, ba, hin, hin, bgate)


def _gate_bwd(dm, bc, ba, hin, bgate):
    def body(dm_ref, bc_ref, ba_ref, gc_ref, ga_ref, b_ref, dbc_ref, dba_ref, dg_ref, db_ref):
        i = pl.program_id(0)

        @pl.when(i == 0)
        def _():
            db_ref[...] = jnp.zeros_like(db_ref)

        b = b_ref[...]
        dm = dm_ref[...].astype(F32)
        sc = jax.nn.sigmoid(gc_ref[...].astype(F32) + b[0:1])
        sa = jax.nn.sigmoid(ga_ref[...].astype(F32) + b[1:2])
        dbc_ref[...] = (dm * sc).astype(BF16)
        dba_ref[...] = (dm * sa).astype(BF16)
        dgc = dm * bc_ref[...].astype(F32) * sc * (1.0 - sc)
        dga = dm * ba_ref[...].astype(F32) * sa * (1.0 - sa)
        dg_ref[0] = dgc.astype(BF16)
        dg_ref[1] = dga.astype(BF16)
        db_ref[0:1, :] += jnp.sum(dgc, axis=0, keepdims=True)
        db_ref[1:2, :] += jnp.sum(dga, axis=0, keepdims=True)

    def tile(s):
        return pl.BlockSpec((None, TM, D), lambda i: (s, i, 0))

    return pl.pallas_call(
        body,
        name="gate_bwd",
        grid=(LP // TM,),
        in_specs=[_row_spec(), _row_spec(), _row_spec(), tile(6), tile(7), _vec_spec(2, D)],
        out_specs=[_row_spec(), _row_spec(), pl.BlockSpec((2, TM, D), lambda i: (3, i, 0)), _vec_spec(8, D)],
        out_shape=[jax.ShapeDtypeStruct((LP, D), BF16)] * 2
        + [jax.ShapeDtypeStruct((N_DEV, LP, D), BF16), jax.ShapeDtypeStruct((8, D), F32)],
        compiler_params=_params(("arbitrary",)),
    )(dm, bc, ba, hin, hin, bgate)


def _softplus(z):
    return jnp.maximum(z, 0.0) + jnp.log(1.0 + jnp.exp(-jnp.abs(z)))


def _cumsum_matrix(inclusive, reverse):
    r = lax.broadcasted_iota(jnp.int32, (QB, 2 * QB), 0)
    c = lax.broadcasted_iota(jnp.int32, (QB, 2 * QB), 1)
    if reverse:
        tri = r > c
    elif inclusive:
        tri = r <= c
    else:
        tri = r < c
    return jnp.where((c >= QB) | tri, 1.0, 0.0).astype(BF16)


def _split_dot(x, m2):
    hi = x.astype(BF16)
    lo = (x - hi.astype(F32)).astype(BF16)
    return _dot(jnp.concatenate([hi, lo], axis=1), m2)


def _stack_heads(x):
    return jnp.concatenate(_split_heads(x), axis=0)


def _block_mask(i, j):
    row = lax.broadcasted_iota(jnp.int32, (AQ, QB), 0) + i * AQ
    col = lax.broadcasted_iota(jnp.int32, (AQ, QB), 1) + j * QB
    return (col < row) & (col >= PAD)


def _key_block(ref, j):
    return ref[pl.ds(pl.multiple_of(j * QB, QB), QB), :]


def _split_heads(x):
    head_a = lax.broadcasted_iota(jnp.int32, x.shape, 1) < HEAD_LANES
    zero = jnp.zeros_like(x)
    return jnp.where(head_a, x, zero), jnp.where(head_a, zero, x)


def _attn_fwd(hin, ex_arrs, gathers):
    ne = len(ex_arrs)
    npair, nq = D // QB, LP // AQ

    def body(*refs):
        q_ref, k_ref, v_ref = refs[:3]
        o_ref, lt_ref = refs[3 + ne:5 + ne]
        c_sc, acc_sc = refs[5 + 2 * ne:7 + 2 * ne]
        p, i = pl.program_id(0), pl.program_id(1)

        def copies():
            return _exchange_copies(refs[3:3 + ne], refs[5 + ne:5 + 2 * ne], gathers, *refs[7 + 2 * ne:])

        @pl.when((p == 0) & (i == 0))
        def _():
            for cp in copies():
                cp.start()

        um = _cumsum_matrix(False, True)
        um2 = jnp.concatenate([um, um], axis=0)
        q = (q_ref[...].astype(F32) * SCALE).astype(BF16)
        c_sc[...] = jnp.zeros_like(c_sc)
        acc_sc[...] = jnp.zeros_like(acc_sc)

        def step(j, masked):
            z2 = _dot(q, _stack_heads(_key_block(k_ref, j)), NT)
            mask = _block_mask(i, j) if masked else None
            a2 = []
            for hd in range(2):
                z = z2[:, hd * QB:(hd + 1) * QB]
                sp = _softplus(z)
                r = _split_dot(jnp.where(mask, sp, 0.0) if masked else sp, um2)
                a = jnp.exp(z - sp - c_sc[hd] - r[:, :QB])
                if masked:
                    a = jnp.where(mask, a, 0.0)
                a2.append(a.astype(BF16))
                c_sc[hd] += r[:, QB:]
            acc_sc[...] += _dot(jnp.concatenate(a2, axis=1), _stack_heads(_key_block(v_ref, j)))

        last = KPQ * i + KPQ - 1

        @pl.loop(0, KPQ)
        def _(t):
            step(last - t, True)

        @pl.loop(0, jnp.maximum(KPQ * i - 1, 0))
        def _(t):
            step(KPQ * i - 1 - t, False)

        @pl.when(i > 0)
        def _():
            step(0, True)

        head_a = lax.broadcasted_iota(jnp.int32, (AQ, QB), 1) < HEAD_LANES
        o_ref[...] = acc_sc[...].astype(BF16)
        lt_ref[...] = jnp.where(head_a, c_sc[0], c_sc[1])

        @pl.when((p == npair - 1) & (i == nq - 1))
        def _():
            for cp in copies():
                cp.wait()

    def seq(s):
        return pl.BlockSpec((None, LP, QB), lambda p, i: (s, 0, p))

    return pl.pallas_call(
        body,
        name="attn_fwd",
        grid=(npair, nq),
        in_specs=[pl.BlockSpec((None, AQ, QB), lambda p, i: (3, i, p)), seq(4), seq(5)] + [ANY_SPEC] * ne,
        out_specs=[pl.BlockSpec((AQ, QB), lambda p, i: (i, p))] * 2 + [ANY_SPEC] * ne,
        out_shape=[jax.ShapeDtypeStruct((LP, D), BF16), jax.ShapeDtypeStruct((LP, D), F32)]
        + _exchange_shapes(ex_arrs, gathers),
        scratch_shapes=[pltpu.VMEM((2, AQ, QB), F32), pltpu.VMEM((AQ, QB), F32)] + _exchange_sems(ne),
        compiler_params=_params(("arbitrary", "arbitrary")),
    )(hin, hin, hin, *ex_arrs)


def _attn_bwd(hin, do, lt, dhin, ex_arrs, gathers):
    ne = len(ex_arrs)
    npair, nq = D // QB, LP // AQ

    def body(*refs):
        q_ref, k_ref, v_ref, do_ref, lt_ref = refs[:5]
        out_ref = refs[6 + ne]
        psp_sc, pg_sc, dq_sc, dk_acc, dv_acc = refs[7 + 2 * ne:12 + 2 * ne]
        p, i = pl.program_id(0), pl.program_id(1)

        def copies():
            return _exchange_copies(refs[6:6 + ne], refs[7 + ne:7 + 2 * ne], gathers, *refs[12 + 2 * ne:])

        @pl.when((p == 0) & (i == 0))
        def _():
            for cp in copies():
                cp.start()

        @pl.when(i == 0)
        def _():
            dk_acc[...] = jnp.zeros_like(dk_acc)
            dv_acc[...] = jnp.zeros_like(dv_acc)

        um_sp = _cumsum_matrix(True, False)
        um_sp2 = jnp.concatenate([um_sp, um_sp], axis=0)
        um_g = _cumsum_matrix(False, False)
        q = (q_ref[...].astype(F32) * SCALE).astype(BF16)
        do = do_ref[...]
        head_a = lax.broadcasted_iota(jnp.int32, (AQ, QB), 1) < HEAD_LANES
        key_head_a = lax.broadcasted_iota(jnp.int32, (QB, QB), 1) < HEAD_LANES
        lt = lt_ref[...]
        lt_sw = pltpu.roll(lt, HEAD_LANES, 1)
        totals = (jnp.where(head_a, lt, lt_sw), jnp.where(head_a, lt_sw, lt))
        psp_sc[...] = jnp.zeros_like(psp_sc)
        pg_sc[...] = jnp.zeros_like(pg_sc)
        dq_sc[...] = jnp.zeros_like(dq_sc)

        def step(j, masked):
            k2 = _stack_heads(_key_block(k_ref, j))
            z2 = _dot(q, k2, NT)
            da2 = _dot(do, _stack_heads(_key_block(v_ref, j)), NT)
            mask = _block_mask(i, j) if masked else None
            a2, dz2 = [], []
            for hd in range(2):
                z = z2[:, hd * QB:(hd + 1) * QB]
                sp = _softplus(z)
                r = _split_dot(jnp.where(mask, sp, 0.0) if masked else sp, um_sp2)
                a = jnp.exp(z - sp - (totals[hd] - psp_sc[hd] - r[:, :QB]))
                if masked:
                    a = jnp.where(mask, a, 0.0)
                g = a * da2[:, hd * QB:(hd + 1) * QB]
                rg = _dot(g.astype(BF16), um_g)
                dz = g - jnp.exp(z - sp) * (g + pg_sc[hd] + rg[:, :QB])
                if masked:
                    dz = jnp.where(mask, dz, 0.0)
                a2.append(a.astype(BF16))
                dz2.append(dz.astype(BF16))
                psp_sc[hd] += r[:, QB:]
                pg_sc[hd] += rg[:, QB:]
            dz2 = jnp.concatenate(dz2, axis=1)
            dq_sc[...] += _dot(dz2, k2)
            dk2 = _dot(dz2, q, TN)
            dv2 = _dot(jnp.concatenate(a2, axis=1), do, TN)
            rows = pl.ds(pl.multiple_of(j * QB, QB), QB)
            dk_acc[rows, :] += jnp.where(key_head_a, dk2[:QB], dk2[QB:])
            dv_acc[rows, :] += jnp.where(key_head_a, dv2[:QB], dv2[QB:])

        @pl.when(i > 0)
        def _():
            step(0, True)

        @pl.loop(0, jnp.maximum(KPQ * i - 1, 0))
        def _(t):
            step(t + 1, False)

        @pl.loop(0, KPQ)
        def _(t):
            step(KPQ * i + t, True)

        out_ref[0, pl.ds(pl.multiple_of(i * AQ, AQ), AQ), :] = (dq_sc[...] * SCALE).astype(BF16)

        @pl.when(i == nq - 1)
        def _():
            out_ref[1] = dk_acc[...].astype(BF16)
            out_ref[2] = dv_acc[...].astype(BF16)

        @pl.when((p == npair - 1) & (i == nq - 1))
        def _():
            for cp in copies():
                cp.wait()

    def seq(s):
        return pl.BlockSpec((None, LP, QB), lambda p, i: (s, 0, p))

    blk = pl.BlockSpec((AQ, QB), lambda p, i: (i, p))
    return pl.pallas_call(
        body,
        name="attn_bwd",
        grid=(npair, nq),
        in_specs=[pl.BlockSpec((None, AQ, QB), lambda p, i: (3, i, p)), seq(4), seq(5), blk, blk]
        + [ANY_SPEC] * (1 + ne),
        out_specs=[pl.BlockSpec((3, LP, QB), lambda p, i: (1, 0, p))] + [ANY_SPEC] * ne,
        out_shape=[jax.ShapeDtypeStruct((N_DEV, LP, D), BF16)] + _exchange_shapes(ex_arrs, gathers),
        input_output_aliases={5: 0},
        scratch_shapes=[pltpu.VMEM((2, AQ, QB), F32)] * 2 + [pltpu.VMEM((AQ, QB), F32)] + [pltpu.VMEM((LP, QB), F32)] * 2
        + _exchange_sems(ne),
        compiler_params=_params(("arbitrary", "arbitrary")),
    )(hin, hin, hin, do, lt, dhin, *ex_arrs)


def _adamw(pieces, w, m, v, *, rows, row_off, tr, name):
    npieces, _, cols = pieces.shape
    ob = row_off // tr

    def body(p_ref, w_ref, m_ref, v_ref, g_ref, d_ref, nm_ref, nv_ref):
        g = p_ref[0].astype(F32)
        for s in range(1, npieces):
            g = g + p_ref[s].astype(F32)
        w_ = w_ref[...]
        m_new = ADAM_B1 * m_ref[...] + (1.0 - ADAM_B1) * g
        v_new = ADAM_B2 * v_ref[...] + (1.0 - ADAM_B2) * jnp.square(g)
        m_hat = m_new / (1.0 - ADAM_B1 ** ADAM_STEP)
        v_hat = v_new / (1.0 - ADAM_B2 ** ADAM_STEP)
        g_ref[...] = g
        d_ref[...] = -ADAM_LR * (m_hat / (jnp.sqrt(v_hat) + ADAM_EPS) + ADAM_WD * w_)
        nm_ref[...] = m_new
        nv_ref[...] = v_new

    spec = pl.BlockSpec((tr, cols), lambda i: (i, 0))
    return pl.pallas_call(
        body,
        name=name,
        grid=(rows // tr,),
        in_specs=[pl.BlockSpec((npieces, tr, cols), lambda i: (0, ob + i, 0)), spec, spec, spec],
        out_specs=[spec] * 4,
        out_shape=[jax.ShapeDtypeStruct((rows, cols), F32)] * 4,
        compiler_params=_params(("parallel",)),
    )(pieces, w, m, v)


def _sum_pieces(pieces, name):
    npieces, rows, cols = pieces.shape

    def body(p_ref, o_ref):
        acc = p_ref[0]
        for s in range(1, npieces):
            acc = acc + p_ref[s]
        o_ref[...] = acc

    return pl.pallas_call(
        body,
        name=name,
        in_specs=[pl.BlockSpec(memory_space=pltpu.VMEM)],
        out_specs=pl.BlockSpec(memory_space=pltpu.VMEM),
        out_shape=jax.ShapeDtypeStruct((rows, cols), pieces.dtype),
    )(pieces)


SMALL_ROWS = 48
CWF_PAD = 384
GRAD_ROWS = 72
G_META, G_PRE_MIX, G_MID, G_POST_FFN, G_CW_MIX, G_B_GATE, G_CW_FFN = 0, 16, 24, 32, 40, 48, 56
SMALL_FLAT = 8192


def kernel(x, meta_tokens, g_pre_mix, w_in, conv_w_mix, w_proj_conv, w_proj_attn, b_gate, w_out, g_post_mix, g_pre_ffn, w_up_gate, conv_w_ffn, w_down, g_post_ffn, loss_target, m_meta_tokens, m_g_pre_mix, m_w_in, m_conv_w_mix, m_w_proj_conv, m_w_proj_attn, m_b_gate, m_w_out, m_g_post_mix, m_g_pre_ffn, m_w_up_gate, m_conv_w_ffn, m_w_down, m_g_post_ffn, v_meta_tokens, v_g_pre_mix, v_w_in, v_conv_w_mix, v_w_proj_conv, v_w_proj_attn, v_b_gate, v_w_out, v_g_post_mix, v_g_pre_ffn, v_w_up_gate, v_conv_w_ffn, v_w_down, v_g_post_ffn):
    me = 4 * lax.axis_index("x") + 2 * lax.axis_index("y") + lax.axis_index("c")

    def rows_to(a, n):
        return jnp.pad(a, ((0, n - a.shape[0]), (0, 0)))

    small_shard = jnp.concatenate(
        [meta_tokens, rows_to(conv_w_mix[0], 8), rows_to(b_gate[0], 8),
         rows_to(jnp.pad(conv_w_ffn[0], ((0, 0), (0, CWF_PAD - R_DOWN))).reshape(9, 128), 16)], axis=0)
    wshard = jnp.concatenate([w_proj_conv[0], w_proj_attn[0], w_out[0], w_down[0]], axis=0).astype(BF16)
    w_in_all, small_all = _exchange([w_in[0].astype(BF16), small_shard], (True, True), "gather_in")

    def unshard(rows):
        return rows.transpose(1, 0, 2).reshape(rows.shape[1], N_DEV * rows.shape[2])

    meta = unshard(small_all[:, 0:16])
    cw_mix = unshard(small_all[:, 16:19])
    bgate = unshard(small_all[:, 24:26])
    cw_ffn = unshard(small_all[:, 32:41].reshape(N_DEV, 3, CWF_PAD)[:, :, :R_DOWN])
    cw4 = cw_ffn.reshape(3, N_FB, FB).transpose(1, 0, 2)
    h0 = jnp.concatenate([jnp.zeros((PAD, D), F32), meta, x[0]], axis=0)

    xn1 = _rms_fwd(h0, g_pre_mix, "rms_pre_mix")
    hin = _mm(xn1, w_in_all, w_rows=D, trans_w=False, out_dtype=BF16, name="mm_in")
    y_conv = _conv_mix_fwd(hin, cw_mix)
    o, lt, wpack, w_ug = _attn_fwd(hin, [wshard, w_up_gate[0].astype(BF16)], (True, True))
    w_pc = wpack[:, O_PC:O_PA].reshape(1, D, D)
    w_pa = wpack[:, O_PA:O_OUT].reshape(1, D, D)
    w_o = wpack[:, O_OUT:O_DOWN].reshape(1, D, D)
    w_dn = wpack[:, O_DOWN:].reshape(N_FB, FB, D)
    bc = _mm(y_conv, w_pc, w_rows=D, trans_w=False, out_dtype=BF16, name="mm_proj_conv")[0]
    ba = _mm(o, w_pa, w_rows=D, trans_w=False, out_dtype=BF16, name="mm_proj_attn")[0]
    merged = _gate_fwd(bc, ba, hin, bgate)
    mix = _mm(merged, w_o, w_rows=D, trans_w=False, out_dtype=F32, name="mm_out")[0]
    h1, xn2 = _resid_rms(h0, mix, g_post_mix, g_pre_ffn)
    ug = _mm(xn2, w_ug, w_rows=D, trans_w=False, out_dtype=BF16, name="mm_up_gate")
    hid = _ffn_act_fwd(ug, cw4)
    ffn = _mm_sum(hid, w_dn, w_rows=FB, trans_w=False, out_dtype=F32, name="mm_down")
    dout, dffn, loss8, dg_post_ffn = _loss_head(h1, ffn, g_post_ffn, loss_target[0])

    dhid = _mm(dffn, w_dn, w_rows=FB, trans_w=True, out_dtype=BF16, name="mm_down_dx")
    gw_dn = _mm_tn(hid, dffn, nb=N_FB, out_dtype=BF16, name="mm_down_dw")
    dug, dcw4 = _ffn_act_bwd(ug, dhid, cw4)
    dug = dug.reshape(2 * N_FB, LP, FB)
    dxn2 = _mm_sum(dug, w_ug, w_rows=D, trans_w=True, out_dtype=F32, name="mm_up_gate_dx")
    gw_ug = _mm_tn(xn2, dug, nb=N_DEV, out_dtype=BF16, name="mm_up_gate_dw")
    dh1, dmix, dg_mid = _mid_bwd(dout, h1, dxn2, mix, g_post_mix, g_pre_ffn)
    dmerged = _mm(dmix, w_o, w_rows=D, trans_w=True, out_dtype=BF16, name="mm_out_dx")[0]
    gw_out = _mm_tn(merged, dmix, nb=1, out_dtype=BF16, name="mm_out_dw")
    dbc, dba, dhin, db_gate = _gate_bwd(dmerged, bc, ba, hin, bgate)
    dy_conv = _mm(dbc, w_pc, w_rows=D, trans_w=True, out_dtype=BF16, name="mm_proj_conv_dx")[0]
    gw_pc = _mm_tn(y_conv, dbc, nb=1, out_dtype=BF16, name="mm_proj_conv_dw")
    do = _mm(dba, w_pa, w_rows=D, trans_w=True, out_dtype=BF16, name="mm_proj_attn_dx")[0]
    gw_pa = _mm_tn(o, dba, nb=1, out_dtype=BF16, name="mm_proj_attn_dw")
    dhin, dcw_mix = _conv_mix_bwd(hin, dy_conv, cw_mix, dhin)
    gpack = jnp.concatenate(
        [gw_pc.reshape(N_DEV, R_PROJ, D), gw_pa.reshape(N_DEV, R_PROJ, D), gw_out.reshape(N_DEV, R_PROJ, D),
         gw_dn.reshape(N_DEV, R_DOWN, D)], axis=1)
    dhin, rpack, rug = _attn_bwd(hin, do, lt, dhin, [gpack, gw_ug], (False, False))
    gw_in = _mm_tn(xn1, dhin, nb=N_DEV, out_dtype=BF16, name="mm_in_dw")
    dcw_ffn = dcw4[:, :3].transpose(1, 0, 2).reshape(3, D_FF)
    small_a = jnp.concatenate(
        [dg_mid, dg_post_ffn, dcw_mix, db_gate,
         jnp.pad(dcw_ffn.reshape(-1), (0, 16 * D - 3 * D_FF)).reshape(16, D)], axis=0)
    dxn1, rin, rsmall_a = _mm_sum(dhin, w_in_all, w_rows=D, trans_w=True, out_dtype=F32, name="mm_in_dx",
                                  exchange=([gw_in, small_a], (False, True)))
    dh0, dg_pre_mix = _first_bwd(dh1, h0, dxn1, g_pre_mix)
    (rsmall_b,) = _exchange([jnp.concatenate([dh0[PAD:OFF], dg_pre_mix], axis=0)], (True,), "gather_last_grads")
    gs = _sum_pieces(jnp.concatenate([rsmall_b, rsmall_a], axis=1), "sum_small_grads")
    grad_x = dh0[OFF:]
    loss = lax.psum(loss8[0, 0], ("x", "y", "c"))

    def cols(a, width):
        return lax.dynamic_slice_in_dim(a, me * width, width, axis=1)

    g_meta = cols(gs[G_META:G_META + N_META], 128)
    g_gpm, g_gff = gs[G_PRE_MIX:G_PRE_MIX + 1], gs[G_POST_FFN:G_POST_FFN + 1]
    g_gpo, g_gpf = gs[G_MID:G_MID + 1], gs[G_MID + 1:G_MID + 2]
    g_cwm = cols(gs[G_CW_MIX:G_CW_MIX + 3], 128)[None]
    g_bg = cols(gs[G_B_GATE:G_B_GATE + 2], 128)[None]
    g_cwf = cols(gs[G_CW_FFN:G_CW_FFN + 9].reshape(-1)[:3 * D_FF].reshape(3, D_FF), R_DOWN)[None]

    def big(pieces, w, m, v, rows, row_off, tr, name):
        g, d, nm, nv = _adamw(pieces, w[0], m[0], v[0], rows=rows, row_off=row_off, tr=tr, name=name)
        return g[None], d[None], nm[None], nv[None]

    r_in = big(rin, w_in, m_w_in, v_w_in, D, 0, 256, "adamw_in")
    r_pc = big(rpack, w_proj_conv, m_w_proj_conv, v_w_proj_conv, R_PROJ, O_PC, R_PROJ, "adamw_proj_conv")
    r_pa = big(rpack, w_proj_attn, m_w_proj_attn, v_w_proj_attn, R_PROJ, O_PA, R_PROJ, "adamw_proj_attn")
    r_out = big(rpack, w_out, m_w_out, v_w_out, R_PROJ, O_OUT, R_PROJ, "adamw_out")
    r_dn = big(rpack, w_down, m_w_down, v_w_down, R_DOWN, O_DOWN, 32, "adamw_down")
    r_ug = big(rug, w_up_gate, m_w_up_gate, v_w_up_gate, D, 0, 256, "adamw_up_gate")

    small_w = [meta_tokens, g_pre_mix, conv_w_mix, b_gate, g_post_mix, g_pre_ffn, conv_w_ffn, g_post_ffn]
    small_g = [g_meta, g_gpm, g_cwm, g_bg, g_gpo, g_gpf, g_cwf, g_gff]
    small_m = [m_meta_tokens, m_g_pre_mix, m_conv_w_mix, m_b_gate, m_g_post_mix, m_g_pre_ffn, m_conv_w_ffn, m_g_post_ffn]
    small_v = [v_meta_tokens, v_g_pre_mix, v_conv_w_mix, v_b_gate, v_g_post_mix, v_g_pre_ffn, v_conv_w_ffn, v_g_post_ffn]

    def flat(arrs, fill):
        parts = [a.reshape(-1) for a in arrs]
        used = sum(p.shape[0] for p in parts)
        return jnp.concatenate(parts + [jnp.full((SMALL_FLAT - used,), fill, F32)]).reshape(SMALL_FLAT // 128, 128)

    sg, sd, sm, sv = _adamw(flat(small_g, 0.0)[None], flat(small_w, 0.0), flat(small_m, 0.0), flat(small_v, 1.0),
                            rows=SMALL_FLAT // 128, row_off=0, tr=SMALL_FLAT // 128, name="adamw_small")

    def unflat(packed):
        out, pos = [], 0
        p = packed.reshape(-1)
        for a in small_w:
            out.append(p[pos:pos + a.size].reshape(a.shape))
            pos += a.size
        return out

    s_g, s_d, s_m, s_v = unflat(sg), unflat(sd), unflat(sm), unflat(sv)

    def ordered(k, smalls):
        meta, gpm, cwm, bg, gpo, gpf, cwf, gff = smalls
        return [meta, gpm, r_in[k], cwm, r_pc[k], r_pa[k], bg, r_out[k], gpo, gpf, r_ug[k], cwf, r_dn[k], gff]

    return (loss, grad_x[None], *ordered(0, s_g), *ordered(1, s_d), *ordered(2, s_m), *ordered(3, s_v))
```

```python
import functools
import math

import jax
import jax.numpy as jnp
from jax import lax
from jax.experimental import pallas as pl
from jax.experimental.pallas import tpu as pltpu

F32 = jnp.float32
BF16 = jnp.bfloat16

D = 1024
SEQ = 4096
N_META = 16
PAD = 112
OFF = PAD + N_META
LP = OFF + SEQ
QB = 128
AQ = 384
KPQ = AQ // QB
TM = 384
MM_TM = 1408
HALO = 16
N_DEV = 8
D_FF = 2816
FB = 704
N_FB = D_FF // FB
RMS_EPS = 1e-6
SCALE = 0.125
HEAD_LANES = 64
VMEM_LIMIT = 56 * 1024 * 1024

ADAM_LR = 0.001
ADAM_B1 = 0.9
ADAM_B2 = 0.999
ADAM_EPS = 1e-08
ADAM_WD = 0.01
ADAM_STEP = 10

R_PROJ, R_DOWN = 128, 352
O_PC = 0
O_PA = O_PC + R_PROJ
O_OUT = O_PA + R_PROJ
O_DOWN = O_OUT + R_PROJ
R_PACK = O_DOWN + R_DOWN

NT = (((1,), (1,)), ((), ()))
NN = (((1,), (0,)), ((), ()))
TN = (((0,), (0,)), ((), ()))


def _params(sem):
    return pltpu.CompilerParams(dimension_semantics=sem, vmem_limit_bytes=VMEM_LIMIT)


def _dot(a, b, dn=NN):
    return lax.dot_general(a, b, dn, preferred_element_type=F32)


def _exchange_copies(ins, outs, gathers, send_sems, recv_sems, loc_sems):
    x, y, c = lax.axis_index("x"), lax.axis_index("y"), lax.axis_index("c")
    me = 4 * x + 2 * y + c
    copies = []
    for a, gather in enumerate(gathers):
        copies.append(pltpu.make_async_copy(ins[a] if gather else ins[a].at[me], outs[a].at[me], loc_sems.at[a]))
    for k in range(1, N_DEV):
        px = 1 - x if k & 4 else x
        py = 1 - y if k & 2 else y
        pc = 1 - c if k & 1 else c
        peer = 4 * px + 2 * py + pc
        for a, gather in enumerate(gathers):
            copies.append(pltpu.make_async_remote_copy(
                src_ref=ins[a] if gather else ins[a].at[peer],
                dst_ref=outs[a].at[me],
                send_sem=send_sems.at[a * (N_DEV - 1) + k - 1],
                recv_sem=recv_sems.at[a * (N_DEV - 1) + k - 1],
                device_id=(px, py, pc),
                device_id_type=pl.DeviceIdType.MESH,
            ))
    return copies


def _exchange_shapes(arrs, gathers):
    return [jax.ShapeDtypeStruct((N_DEV,) + (a.shape if g else a.shape[1:]), a.dtype) for a, g in zip(arrs, gathers)]


def _exchange_sems(n):
    return [pltpu.SemaphoreType.DMA((n * (N_DEV - 1),)), pltpu.SemaphoreType.DMA((n * (N_DEV - 1),)),
            pltpu.SemaphoreType.DMA((n,))]


ANY_SPEC = pl.BlockSpec(memory_space=pl.ANY)


def _exchange(arrs, gathers, name):
    n = len(arrs)

    def body(*refs):
        copies = _exchange_copies(refs[:n], refs[n:2 * n], gathers, *refs[2 * n:])
        for cp in copies:
            cp.start()
        for cp in copies:
            cp.wait()

    return pl.pallas_call(
        body,
        name=name,
        out_shape=_exchange_shapes(arrs, gathers),
        in_specs=[ANY_SPEC] * n,
        out_specs=[ANY_SPEC] * n,
        scratch_shapes=_exchange_sems(n),
    )(*arrs)


def _mm(a, w, *, w_rows, trans_w, out_dtype, name):
    nb, _, wc = w.shape
    m, k = a.shape[-2:]
    n = w_rows if trans_w else wc
    dn = NT if trans_w else NN

    def body(a_ref, w_ref, o_ref):
        o_ref[...] = _dot(a_ref[...], w_ref[...], dn).astype(out_dtype)

    if a.ndim == 2:
        a_spec = pl.BlockSpec((MM_TM, k), lambda j, i: (i, 0))
    else:
        a_spec = pl.BlockSpec((None, MM_TM, k), lambda j, i: (j, i, 0))
    return pl.pallas_call(
        body,
        name=name,
        grid=(nb, m // MM_TM),
        in_specs=[a_spec, pl.BlockSpec((None, w_rows, wc), lambda j, i: (j, 0, 0))],
        out_specs=pl.BlockSpec((None, MM_TM, n), lambda j, i: (j, i, 0)),
        out_shape=jax.ShapeDtypeStruct((nb, m, n), out_dtype),
        compiler_params=_params(("parallel", "parallel")),
    )(a, w)


def _mm_sum(a, w, *, w_rows, trans_w, out_dtype, name, exchange=((), ())):
    nb, m, k = a.shape
    wc = w.shape[2]
    n = w_rows if trans_w else wc
    dn = NT if trans_w else NN
    ex_arrs, gathers = exchange
    ne = len(ex_arrs)
    steps = m // MM_TM

    def body(*refs):
        a_ref, w_ref, o_ref, acc_ref = refs[0], refs[1], refs[2 + ne], refs[3 + 2 * ne]
        i, j = pl.program_id(0), pl.program_id(1)

        def copies():
            return _exchange_copies(refs[2:2 + ne], refs[3 + ne:3 + 2 * ne], gathers, *refs[4 + 2 * ne:])

        if ne:
            @pl.when((i == 0) & (j == 0))
            def _():
                for cp in copies():
                    cp.start()

        @pl.when(j == 0)
        def _():
            acc_ref[...] = jnp.zeros_like(acc_ref)

        acc_ref[...] += _dot(a_ref[...], w_ref[...], dn)

        @pl.when(j == nb - 1)
        def _():
            o_ref[...] = acc_ref[...].astype(out_dtype)

        if ne:
            @pl.when((i == steps - 1) & (j == nb - 1))
            def _():
                for cp in copies():
                    cp.wait()

    res = pl.pallas_call(
        body,
        name=name,
        grid=(steps, nb),
        in_specs=[
            pl.BlockSpec((None, MM_TM, k), lambda i, j: (j, i, 0)),
            pl.BlockSpec((None, w_rows, wc), lambda i, j: (j, 0, 0)),
        ] + [ANY_SPEC] * ne,
        out_specs=[pl.BlockSpec((MM_TM, n), lambda i, j: (i, 0))] + [ANY_SPEC] * ne,
        out_shape=[jax.ShapeDtypeStruct((m, n), out_dtype)] + _exchange_shapes(ex_arrs, gathers),
        scratch_shapes=[pltpu.VMEM((MM_TM, n), F32)] + (_exchange_sems(ne) if ne else []),
        compiler_params=_params(("arbitrary", "arbitrary")),
    )(a, w, *ex_arrs)
    return res if ne else res[0]


def _mm_tn(a, b, *, nb, out_dtype, name):
    m, ka = a.shape[-2:]
    n = b.shape[-1]
    steps = m // MM_TM

    def body(a_ref, b_ref, o_ref, acc_ref):
        i = pl.program_id(1)

        @pl.when(i == 0)
        def _():
            acc_ref[...] = jnp.zeros_like(acc_ref)

        acc_ref[...] += _dot(a_ref[...], b_ref[...], TN)

        @pl.when(i == steps - 1)
        def _():
            o_ref[...] = acc_ref[...].astype(out_dtype)

    def spec(arr, cols):
        if arr.ndim == 2:
            return pl.BlockSpec((MM_TM, cols), lambda j, i: (i, 0))
        return pl.BlockSpec((None, MM_TM, cols), lambda j, i: (j, i, 0))

    return pl.pallas_call(
        body,
        name=name,
        grid=(nb, steps),
        in_specs=[spec(a, ka), spec(b, n)],
        out_specs=pl.BlockSpec((None, ka, n), lambda j, i: (j, 0, 0)),
        out_shape=jax.ShapeDtypeStruct((nb, ka, n), out_dtype),
        scratch_shapes=[pltpu.VMEM((ka, n), F32)],
        compiler_params=_params(("parallel", "arbitrary")),
    )(a, b)


def _rstd(x):
    return lax.rsqrt(jnp.mean(x * x, axis=-1, keepdims=True) + RMS_EPS)


def _rms_bwd(x, g, dy):
    r = _rstd(x)
    u = dy * g
    dx = r * u - x * (r * r * r) * jnp.mean(u * x, axis=-1, keepdims=True)
    return dx, dy * x * r


def _row_spec(cols=D, tm=TM):
    return pl.BlockSpec((tm, cols), lambda i: (i, 0))


def _vec_spec(rows=1, cols=D):
    return pl.BlockSpec((rows, cols), lambda i: (0, 0))


def _rms_fwd(x, g, name):
    def body(x_ref, g_ref, o_ref):
        x = x_ref[...]
        o_ref[...] = (x * _rstd(x) * g_ref[...]).astype(BF16)

    return pl.pallas_call(
        body,
        name=name,
        grid=(LP // TM,),
        in_specs=[_row_spec(), _vec_spec()],
        out_specs=_row_spec(),
        out_shape=jax.ShapeDtypeStruct((LP, D), BF16),
        compiler_params=_params(("parallel",)),
    )(x, g)


def _resid_rms(h0, mix, g_post, g_next):
    def body(h0_ref, mix_ref, gp_ref, gn_ref, h1_ref, xn_ref):
        mix = mix_ref[...]
        h1 = h0_ref[...] + mix * _rstd(mix) * gp_ref[...]
        h1_ref[...] = h1
        xn_ref[...] = (h1 * _rstd(h1) * gn_ref[...]).astype(BF16)

    return pl.pallas_call(
        body,
        name="resid_rms",
        grid=(LP // TM,),
        in_specs=[_row_spec(), _row_spec(), _vec_spec(), _vec_spec()],
        out_specs=[_row_spec(), _row_spec()],
        out_shape=[jax.ShapeDtypeStruct((LP, D), F32), jax.ShapeDtypeStruct((LP, D), BF16)],
        compiler_params=_params(("parallel",)),
    )(h0, mix, g_post, g_next)


def _loss_head(h1, ffn, g_post, target):
    nblk = LP // QB

    def body(h1_ref, ffn_ref, g_ref, t_ref, dout_ref, dffn_ref, loss_ref, dg_ref):
        i = pl.program_id(0)

        @pl.when(i == 0)
        def _():
            loss_ref[...] = jnp.zeros_like(loss_ref)
            dg_ref[...] = jnp.zeros_like(dg_ref)

        ffn = ffn_ref[...]
        g = g_ref[...]
        out = h1_ref[...] + ffn * _rstd(ffn) * g
        err = jnp.where(i > 0, out - t_ref[...], 0.0)
        loss_ref[...] += 0.5 * jnp.sum(err * err) / D
        dout = err / D
        dout_ref[...] = dout
        dffn, dg = _rms_bwd(ffn, g, dout)
        dffn_ref[...] = dffn.astype(BF16)
        dg_ref[0:1, :] += jnp.sum(dg, axis=0, keepdims=True)

    return pl.pallas_call(
        body,
        name="loss_head",
        grid=(nblk,),
        in_specs=[
            _row_spec(tm=QB),
            _row_spec(tm=QB),
            _vec_spec(),
            pl.BlockSpec((QB, D), lambda i: (jnp.maximum(i - 1, 0), 0)),
        ],
        out_specs=[_row_spec(tm=QB), _row_spec(tm=QB), _vec_spec(8, 128), _vec_spec(8, D)],
        out_shape=[
            jax.ShapeDtypeStruct((LP, D), F32),
            jax.ShapeDtypeStruct((LP, D), BF16),
            jax.ShapeDtypeStruct((8, 128), F32),
            jax.ShapeDtypeStruct((8, D), F32),
        ],
        compiler_params=_params(("arbitrary",)),
    )(h1, ffn, g_post, target)


def _mid_bwd(dout, h1, dxn2, mix, g_post_mix, g_pre_ffn):
    def body(dout_ref, h1_ref, dxn_ref, mix_ref, gpm_ref, gpf_ref, dh1_ref, dmix_ref, dg_ref):
        i = pl.program_id(0)

        @pl.when(i == 0)
        def _():
            dg_ref[...] = jnp.zeros_like(dg_ref)

        dx, dg_ffn = _rms_bwd(h1_ref[...], gpf_ref[...], dxn_ref[...])
        dh1 = dout_ref[...] + dx
        dh1_ref[...] = dh1
        dmix, dg_mix = _rms_bwd(mix_ref[...], gpm_ref[...], dh1)
        dmix_ref[...] = dmix.astype(BF16)
        dg_ref[0:1, :] += jnp.sum(dg_mix, axis=0, keepdims=True)
        dg_ref[1:2, :] += jnp.sum(dg_ffn, axis=0, keepdims=True)

    return pl.pallas_call(
        body,
        name="mid_bwd",
        grid=(LP // TM,),
        in_specs=[_row_spec(), _row_spec(), _row_spec(), _row_spec(), _vec_spec(), _vec_spec()],
        out_specs=[_row_spec(), _row_spec(), _vec_spec(8, D)],
        out_shape=[
            jax.ShapeDtypeStruct((LP, D), F32),
            jax.ShapeDtypeStruct((LP, D), BF16),
            jax.ShapeDtypeStruct((8, D), F32),
        ],
        compiler_params=_params(("arbitrary",)),
    )(dout, h1, dxn2, mix, g_post_mix, g_pre_ffn)


def _first_bwd(dh1, h0, dxn1, g_pre_mix):
    def body(dh1_ref, h0_ref, dxn_ref, g_ref, dh0_ref, dg_ref):
        i = pl.program_id(0)

        @pl.when(i == 0)
        def _():
            dg_ref[...] = jnp.zeros_like(dg_ref)

        dx, dg = _rms_bwd(h0_ref[...], g_ref[...], dxn_ref[...])
        dh0_ref[...] = dh1_ref[...] + dx
        dg_ref[0:1, :] += jnp.sum(dg, axis=0, keepdims=True)

    return pl.pallas_call(
        body,
        name="first_bwd",
        grid=(LP // TM,),
        in_specs=[_row_spec(), _row_spec(), _row_spec(), _vec_spec()],
        out_specs=[_row_spec(), _vec_spec(8, D)],
        out_shape=[jax.ShapeDtypeStruct((LP, D), F32), jax.ShapeDtypeStruct((8, D), F32)],
        compiler_params=_params(("arbitrary",)),
    )(dh1, h0, dxn1, g_pre_mix)


def _prev_halo(i):
    return jnp.maximum(i * (TM // HALO) - 1, 0)


def _next_halo(i):
    return jnp.minimum((i + 1) * (TM // HALO), LP // HALO - 1)


def _down(x, s):
    return pltpu.roll(x, s, 0)


def _up(x, s):
    return pltpu.roll(x, x.shape[0] - s, 0)


def _conv_mix_fwd(hin, cw):
    def body(b_ref, c_ref, h_ref, cp_ref, hp_ref, w_ref, y_ref):
        i = pl.program_id(0)
        p = c_ref[...].astype(F32) * h_ref[...].astype(F32)
        pp = jnp.where(i > 0, cp_ref[...].astype(F32) * hp_ref[...].astype(F32), 0.0)
        ext = jnp.concatenate([pp, p], axis=0)
        w = [w_ref[t:t + 1, :] for t in range(3)]
        cv = w[2] * ext + w[1] * _down(ext, 1) + w[0] * _down(ext, 2)
        y_ref[...] = (b_ref[...].astype(F32) * cv[HALO:]).astype(BF16)

    def tile(s):
        return pl.BlockSpec((None, TM, D), lambda i: (s, i, 0))

    def prev(s):
        return pl.BlockSpec((None, HALO, D), lambda i: (s, _prev_halo(i), 0))

    return pl.pallas_call(
        body,
        name="conv_mix_fwd",
        grid=(LP // TM,),
        in_specs=[tile(0), tile(1), tile(2), prev(1), prev(2), _vec_spec(3, D)],
        out_specs=_row_spec(),
        out_shape=jax.ShapeDtypeStruct((LP, D), BF16),
        compiler_params=_params(("parallel",)),
    )(hin, hin, hin, hin, hin, cw)


def _conv_mix_bwd(hin, dy, cw, dhin):
    last = LP // TM - 1

    def body(b_ref, c_ref, h_ref, dy_ref, cp_ref, hp_ref, bn_ref, dyn_ref, w_ref, _, out_ref, dw_ref):
        i = pl.program_id(0)

        @pl.when(i == 0)
        def _():
            dw_ref[...] = jnp.zeros_like(dw_ref)

        b = b_ref[...].astype(F32)
        c = c_ref[...].astype(F32)
        h = h_ref[...].astype(F32)
        dy = dy_ref[...].astype(F32)
        w = [w_ref[t:t + 1, :] for t in range(3)]
        p = c * h
        pp = jnp.where(i > 0, cp_ref[...].astype(F32) * hp_ref[...].astype(F32), 0.0)
        ext = jnp.concatenate([pp, p], axis=0)
        p1 = _down(ext, 1)[HALO:]
        p2 = _down(ext, 2)[HALO:]
        cv = w[2] * p + w[1] * p1 + w[0] * p2
        out_ref[0] = (dy * cv).astype(BF16)
        dcv = dy * b
        dcvn = jnp.where(i < last, dyn_ref[...].astype(F32) * bn_ref[...].astype(F32), 0.0)
        dext = jnp.concatenate([dcv, dcvn], axis=0)
        dp = (w[2] * dext + w[1] * _up(dext, 1) + w[0] * _up(dext, 2))[:TM]
        out_ref[1] = (dp * h).astype(BF16)
        out_ref[2] = (dp * c).astype(BF16)
        dw_ref[0:1, :] += jnp.sum(dcv * p2, axis=0, keepdims=True)
        dw_ref[1:2, :] += jnp.sum(dcv * p1, axis=0, keepdims=True)
        dw_ref[2:3, :] += jnp.sum(dcv * p, axis=0, keepdims=True)

    def tile(s):
        return pl.BlockSpec((None, TM, D), lambda i: (s, i, 0))

    def prev(s):
        return pl.BlockSpec((None, HALO, D), lambda i: (s, _prev_halo(i), 0))

    return pl.pallas_call(
        body,
        name="conv_mix_bwd",
        grid=(LP // TM,),
        in_specs=[
            tile(0), tile(1), tile(2), _row_spec(),
            prev(1), prev(2),
            pl.BlockSpec((None, HALO, D), lambda i: (0, _next_halo(i), 0)),
            pl.BlockSpec((HALO, D), lambda i: (_next_halo(i), 0)),
            _vec_spec(3, D),
            pl.BlockSpec(memory_space=pl.ANY),
        ],
        out_specs=[pl.BlockSpec((3, TM, D), lambda i: (0, i, 0)), _vec_spec(8, D)],
        out_shape=[jax.ShapeDtypeStruct((N_DEV, LP, D), BF16), jax.ShapeDtypeStruct((8, D), F32)],
        input_output_aliases={9: 0},
        compiler_params=_params(("arbitrary",)),
    )(hin, hin, hin, dy, hin, hin, hin, dy, cw, dhin)


GELU_K = math.sqrt(2.0 / math.pi)
GELU_A = 0.044715


def _gelu_and_grad(x):
    x2 = x * x
    t = jnp.tanh(GELU_K * (x + GELU_A * x2 * x))
    gelu = 0.5 * x * (1.0 + t)
    grad = 0.5 * (1.0 + t) + 0.5 * x * (1.0 - t * t) * GELU_K * (1.0 + 3.0 * GELU_A * x2)
    return gelu, grad


def _ffn_act_fwd(ug, cw4):
    def body(u_ref, g_ref, up_ref, w_ref, o_ref):
        i = pl.program_id(1)
        u = u_ref[...].astype(F32)
        up = jnp.where(i > 0, up_ref[...].astype(F32), 0.0)
        ext = jnp.concatenate([up, u], axis=0)
        w = [w_ref[t:t + 1, :] for t in range(3)]
        uc = (w[2] * ext + w[1] * _down(ext, 1) + w[0] * _down(ext, 2))[HALO:]
        gelu, _ = _gelu_and_grad(uc)
        o_ref[...] = (gelu * g_ref[...].astype(F32)).astype(BF16)

    return pl.pallas_call(
        body,
        name="ffn_act_fwd",
        grid=(N_FB, LP // TM),
        in_specs=[
            pl.BlockSpec((None, TM, FB), lambda j, i: (j, i, 0)),
            pl.BlockSpec((None, TM, FB), lambda j, i: (j + N_FB, i, 0)),
            pl.BlockSpec((None, HALO, FB), lambda j, i: (j, _prev_halo(i), 0)),
            pl.BlockSpec((None, 3, FB), lambda j, i: (j, 0, 0)),
        ],
        out_specs=pl.BlockSpec((None, TM, FB), lambda j, i: (j, i, 0)),
        out_shape=jax.ShapeDtypeStruct((N_FB, LP, FB), BF16),
        compiler_params=_params(("parallel", "parallel")),
    )(ug, ug, ug, cw4)


def _ffn_act_bwd(ug, dhid, cw4):
    last = LP // TM - 1
    n = TM + 2 * HALO

    def body(u_ref, g_ref, dh_ref, up_ref, un_ref, gn_ref, dhn_ref, w_ref, dug_ref, dw_ref):
        i = pl.program_id(1)

        @pl.when(i == 0)
        def _():
            dw_ref[...] = jnp.zeros_like(dw_ref)

        w = [w_ref[t:t + 1, :] for t in range(3)]
        u = u_ref[...].astype(F32)
        up = jnp.where(i > 0, up_ref[...].astype(F32), 0.0)
        ext = jnp.concatenate([up, u, un_ref[...].astype(F32)], axis=0)
        u1 = _down(ext, 1)
        u2 = _down(ext, 2)
        uc = w[2] * ext + w[1] * u1 + w[0] * u2
        gelu, ggrad = _gelu_and_grad(uc)
        zeros = jnp.zeros((HALO, FB), F32)
        gext = jnp.concatenate([zeros, g_ref[...].astype(F32), gn_ref[...].astype(F32)], axis=0)
        dhn = jnp.where(i < last, dhn_ref[...].astype(F32), 0.0)
        dhext = jnp.concatenate([zeros, dh_ref[...].astype(F32), dhn], axis=0)
        dug_ref[1] = (dhext * gelu)[HALO:HALO + TM].astype(BF16)
        duc = dhext * gext * ggrad
        du = w[2] * duc + w[1] * _up(duc, 1) + w[0] * _up(duc, 2)
        dug_ref[0] = du[HALO:HALO + TM].astype(BF16)
        row = lax.broadcasted_iota(jnp.int32, (n, 1), 0)
        own = jnp.where((row >= HALO) & (row < HALO + TM), duc, 0.0)
        dw_ref[0:1, :] += jnp.sum(own * u2, axis=0, keepdims=True)
        dw_ref[1:2, :] += jnp.sum(own * u1, axis=0, keepdims=True)
        dw_ref[2:3, :] += jnp.sum(own * ext, axis=0, keepdims=True)

    def tile(off):
        return pl.BlockSpec((None, TM, FB), lambda j, i: (j + off, i, 0))

    def nxt(off):
        return pl.BlockSpec((None, HALO, FB), lambda j, i: (j + off, _next_halo(i), 0))

    return pl.pallas_call(
        body,
        name="ffn_act_bwd",
        grid=(N_FB, LP // TM),
        in_specs=[
            tile(0), tile(N_FB), tile(0),
            pl.BlockSpec((None, HALO, FB), lambda j, i: (j, _prev_halo(i), 0)),
            nxt(0), nxt(N_FB), nxt(0),
            pl.BlockSpec((None, 3, FB), lambda j, i: (j, 0, 0)),
        ],
        out_specs=[
            pl.BlockSpec((2, None, TM, FB), lambda j, i: (0, j, i, 0)),
            pl.BlockSpec((None, 8, FB), lambda j, i: (j, 0, 0)),
        ],
        out_shape=[jax.ShapeDtypeStruct((2, N_FB, LP, FB), BF16), jax.ShapeDtypeStruct((N_FB, 8, FB), F32)],
        compiler_params=_params(("parallel", "arbitrary")),
    )(ug, ug, dhid, ug, ug, ug, dhid, cw4)


def _gate_fwd(bc, ba, hin, bgate):
    def body(bc_ref, ba_ref, gc_ref, ga_ref, b_ref, o_ref):
        b = b_ref[...]
        sc = jax.nn.sigmoid(gc_ref[...].astype(F32) + b[0:1])
        sa = jax.nn.sigmoid(ga_ref[...].astype(F32) + b[1:2])
        o_ref[...] = (sc * bc_ref[...].astype(F32) + sa * ba_ref[...].astype(F32)).astype(BF16)

    def tile(s):
        return pl.BlockSpec((None, TM, D), lambda i: (s, i, 0))

    return pl.pallas_call(
        body,
        name="gate_fwd",
        grid=(LP // TM,),
        in_specs=[_row_spec(), _row_spec(), tile(6), tile(7), _vec_spec(2, D)],
        out_specs=_row_spec(),
        out_shape=jax.ShapeDtypeStruct((LP, D), BF16),
        compiler_params=_params(("parallel",)),
    )(bc, ba, hin, hin, bgate)


def _gate_bwd(dm, bc, ba, hin, bgate):
    def body(dm_ref, bc_ref, ba_ref, gc_ref, ga_ref, b_ref, dbc_ref, dba_ref, dg_ref, db_ref):
        i = pl.program_id(0)

        @pl.when(i == 0)
        def _():
            db_ref[...] = jnp.zeros_like(db_ref)

        b = b_ref[...]
        dm = dm_ref[...].astype(F32)
        sc = jax.nn.sigmoid(gc_ref[...].astype(F32) + b[0:1])
        sa = jax.nn.sigmoid(ga_ref[...].astype(F32) + b[1:2])
        dbc_ref[...] = (dm * sc).astype(BF16)
        dba_ref[...] = (dm * sa).astype(BF16)
        dgc = dm * bc_ref[...].astype(F32) * sc * (1.0 - sc)
        dga = dm * ba_ref[...].astype(F32) * sa * (1.0 - sa)
        dg_ref[0] = dgc.astype(BF16)
        dg_ref[1] = dga.astype(BF16)
        db_ref[0:1, :] += jnp.sum(dgc, axis=0, keepdims=True)
        db_ref[1:2, :] += jnp.sum(dga, axis=0, keepdims=True)

    def tile(s):
        return pl.BlockSpec((None, TM, D), lambda i: (s, i, 0))

    return pl.pallas_call(
        body,
        name="gate_bwd",
        grid=(LP // TM,),
        in_specs=[_row_spec(), _row_spec(), _row_spec(), tile(6), tile(7), _vec_spec(2, D)],
        out_specs=[_row_spec(), _row_spec(), pl.BlockSpec((2, TM, D), lambda i: (3, i, 0)), _vec_spec(8, D)],
        out_shape=[jax.ShapeDtypeStruct((LP, D), BF16)] * 2
        + [jax.ShapeDtypeStruct((N_DEV, LP, D), BF16), jax.ShapeDtypeStruct((8, D), F32)],
        compiler_params=_params(("arbitrary",)),
    )(dm, bc, ba, hin, hin, bgate)


Z_LINEAR = 30.0


def _softplus(z):
    return jnp.maximum(z, jnp.log(1.0 + jnp.exp(jnp.minimum(z, Z_LINEAR))))


def _cumsum_matrix(inclusive, reverse):
    r = lax.broadcasted_iota(jnp.int32, (QB, 2 * QB), 0)
    c = lax.broadcasted_iota(jnp.int32, (QB, 2 * QB), 1)
    if reverse:
        tri = r > c
    elif inclusive:
        tri = r <= c
    else:
        tri = r < c
    return jnp.where((c >= QB) | tri, 1.0, 0.0).astype(BF16)


def _split_dot(x, m2):
    bits = lax.bitcast_convert_type(x, jnp.uint32) & jnp.uint32(0xFFFF0000)
    hi = lax.bitcast_convert_type(bits, F32)
    return _dot(jnp.concatenate([hi.astype(BF16), (x - hi).astype(BF16)], axis=1), m2)


def _stack_heads(x):
    return jnp.concatenate(_split_heads(x), axis=0)


def _block_mask(i, j):
    row = lax.broadcasted_iota(jnp.int32, (AQ, QB), 0) + i * AQ
    col = lax.broadcasted_iota(jnp.int32, (AQ, QB), 1) + j * QB
    return (col < row) & (col >= PAD)


def _key_block(ref, j):
    return ref[pl.ds(pl.multiple_of(j * QB, QB), QB), :]


def _split_heads(x):
    head_a = lax.broadcasted_iota(jnp.int32, x.shape, 1) < HEAD_LANES
    zero = jnp.zeros_like(x)
    return jnp.where(head_a, x, zero), jnp.where(head_a, zero, x)


def _attn_fwd(hin, ex_arrs, gathers):
    ne = len(ex_arrs)
    npair, nq = D // QB, LP // AQ

    def body(*refs):
        q_ref, k_ref, v_ref = refs[:3]
        o_ref, lt_ref = refs[3 + ne:5 + ne]
        c_sc, acc_sc = refs[5 + 2 * ne:7 + 2 * ne]
        p, i = pl.program_id(0), pl.program_id(1)

        def copies():
            return _exchange_copies(refs[3:3 + ne], refs[5 + ne:5 + 2 * ne], gathers, *refs[7 + 2 * ne:])

        @pl.when((p == 0) & (i == 0))
        def _():
            for cp in copies():
                cp.start()

        um = _cumsum_matrix(False, True)
        um2 = jnp.concatenate([um, um], axis=0)
        q = (q_ref[...].astype(F32) * SCALE).astype(BF16)
        c_sc[...] = jnp.zeros_like(c_sc)
        acc_sc[...] = jnp.zeros_like(acc_sc)

        def step(j, masked):
            z2 = _dot(q, _stack_heads(_key_block(k_ref, j)), NT)
            mask = _block_mask(i, j) if masked else None
            a2 = []
            for hd in range(2):
                z = z2[:, hd * QB:(hd + 1) * QB]
                sp = _softplus(z)
                r = _split_dot(jnp.where(mask, sp, 0.0) if masked else sp, um2)
                a = jnp.exp(z - sp - c_sc[hd] - r[:, :QB])
                if masked:
                    a = jnp.where(mask, a, 0.0)
                a2.append(a.astype(BF16))
                c_sc[hd] += r[:, QB:]
            acc_sc[...] += _dot(jnp.concatenate(a2, axis=1), _stack_heads(_key_block(v_ref, j)))

        last = KPQ * i + KPQ - 1

        @pl.loop(0, KPQ)
        def _(t):
            step(last - t, True)

        inner = jnp.maximum(KPQ * i - 1, 0)

        @pl.loop(0, inner // 2)
        def _(t):
            step(KPQ * i - 1 - 2 * t, False)
            step(KPQ * i - 2 - 2 * t, False)

        @pl.when(inner % 2 == 1)
        def _():
            step(1, False)

        @pl.when(i > 0)
        def _():
            step(0, True)

        head_a = lax.broadcasted_iota(jnp.int32, (AQ, QB), 1) < HEAD_LANES
        o_ref[...] = acc_sc[...].astype(BF16)
        lt_ref[...] = jnp.where(head_a, c_sc[0], c_sc[1])

        @pl.when((p == npair - 1) & (i == nq - 1))
        def _():
            for cp in copies():
                cp.wait()

    def seq(s):
        return pl.BlockSpec((None, LP, QB), lambda p, i: (s, 0, p))

    return pl.pallas_call(
        body,
        name="attn_fwd",
        grid=(npair, nq),
        in_specs=[pl.BlockSpec((None, AQ, QB), lambda p, i: (3, i, p)), seq(4), seq(5)] + [ANY_SPEC] * ne,
        out_specs=[pl.BlockSpec((AQ, QB), lambda p, i: (i, p))] * 2 + [ANY_SPEC] * ne,
        out_shape=[jax.ShapeDtypeStruct((LP, D), BF16), jax.ShapeDtypeStruct((LP, D), F32)]
        + _exchange_shapes(ex_arrs, gathers),
        scratch_shapes=[pltpu.VMEM((2, AQ, QB), F32), pltpu.VMEM((AQ, QB), F32)] + _exchange_sems(ne),
        compiler_params=_params(("arbitrary", "arbitrary")),
    )(hin, hin, hin, *ex_arrs)


def _attn_bwd(hin, do, lt, dhin, ex_arrs, gathers):
    ne = len(ex_arrs)
    npair, nq = D // QB, LP // AQ

    def body(*refs):
        q_ref, k_ref, v_ref, do_ref, lt_ref = refs[:5]
        out_ref = refs[6 + ne]
        psp_sc, pg_sc, dq_sc, dk_acc, dv_acc = refs[7 + 2 * ne:12 + 2 * ne]
        p, i = pl.program_id(0), pl.program_id(1)

        def copies():
            return _exchange_copies(refs[6:6 + ne], refs[7 + ne:7 + 2 * ne], gathers, *refs[12 + 2 * ne:])

        @pl.when((p == 0) & (i == 0))
        def _():
            for cp in copies():
                cp.start()

        @pl.when(i == 0)
        def _():
            dk_acc[...] = jnp.zeros_like(dk_acc)
            dv_acc[...] = jnp.zeros_like(dv_acc)

        um_sp = _cumsum_matrix(True, False)
        um_sp2 = jnp.concatenate([um_sp, um_sp], axis=0)
        um_g = _cumsum_matrix(False, False)
        q = (q_ref[...].astype(F32) * SCALE).astype(BF16)
        do = do_ref[...]
        head_a = lax.broadcasted_iota(jnp.int32, (AQ, QB), 1) < HEAD_LANES
        key_head_a = lax.broadcasted_iota(jnp.int32, (QB, QB), 1) < HEAD_LANES
        lt = lt_ref[...]
        lt_sw = pltpu.roll(lt, HEAD_LANES, 1)
        totals = (jnp.where(head_a, lt, lt_sw), jnp.where(head_a, lt_sw, lt))
        psp_sc[...] = jnp.zeros_like(psp_sc)
        pg_sc[...] = jnp.zeros_like(pg_sc)
        dq_sc[...] = jnp.zeros_like(dq_sc)

        def step(j, masked):
            k2 = _stack_heads(_key_block(k_ref, j))
            z2 = _dot(q, k2, NT)
            da2 = _dot(do, _stack_heads(_key_block(v_ref, j)), NT)
            mask = _block_mask(i, j) if masked else None
            a2, dz2 = [], []
            for hd in range(2):
                z = z2[:, hd * QB:(hd + 1) * QB]
                sp = _softplus(z)
                r = _split_dot(jnp.where(mask, sp, 0.0) if masked else sp, um_sp2)
                a = jnp.exp(z - sp - (totals[hd] - psp_sc[hd] - r[:, :QB]))
                if masked:
                    a = jnp.where(mask, a, 0.0)
                g = a * da2[:, hd * QB:(hd + 1) * QB]
                rg = _dot(g.astype(BF16), um_g)
                dz = g - jnp.exp(z - sp) * (g + pg_sc[hd] + rg[:, :QB])
                if masked:
                    dz = jnp.where(mask, dz, 0.0)
                a2.append(a.astype(BF16))
                dz2.append(dz.astype(BF16))
                psp_sc[hd] += r[:, QB:]
                pg_sc[hd] += rg[:, QB:]
            dz2 = jnp.concatenate(dz2, axis=1)
            dq_sc[...] += _dot(dz2, k2)
            dk2 = _dot(dz2, q, TN)
            dv2 = _dot(jnp.concatenate(a2, axis=1), do, TN)
            rows = pl.ds(pl.multiple_of(j * QB, QB), QB)
            dk_acc[rows, :] += jnp.where(key_head_a, dk2[:QB], dk2[QB:])
            dv_acc[rows, :] += jnp.where(key_head_a, dv2[:QB], dv2[QB:])

        @pl.when(i > 0)
        def _():
            step(0, True)

        inner = jnp.maximum(KPQ * i - 1, 0)

        @pl.loop(0, inner // 2)
        def _(t):
            step(2 * t + 1, False)
            step(2 * t + 2, False)

        @pl.when(inner % 2 == 1)
        def _():
            step(inner, False)

        @pl.loop(0, KPQ)
        def _(t):
            step(KPQ * i + t, True)

        out_ref[0, pl.ds(pl.multiple_of(i * AQ, AQ), AQ), :] = (dq_sc[...] * SCALE).astype(BF16)

        @pl.when(i == nq - 1)
        def _():
            out_ref[1] = dk_acc[...].astype(BF16)
            out_ref[2] = dv_acc[...].astype(BF16)

        @pl.when((p == npair - 1) & (i == nq - 1))
        def _():
            for cp in copies():
                cp.wait()

    def seq(s):
        return pl.BlockSpec((None, LP, QB), lambda p, i: (s, 0, p))

    blk = pl.BlockSpec((AQ, QB), lambda p, i: (i, p))
    return pl.pallas_call(
        body,
        name="attn_bwd",
        grid=(npair, nq),
        in_specs=[pl.BlockSpec((None, AQ, QB), lambda p, i: (3, i, p)), seq(4), seq(5), blk, blk]
        + [ANY_SPEC] * (1 + ne),
        out_specs=[pl.BlockSpec((3, LP, QB), lambda p, i: (1, 0, p))] + [ANY_SPEC] * ne,
        out_shape=[jax.ShapeDtypeStruct((N_DEV, LP, D), BF16)] + _exchange_shapes(ex_arrs, gathers),
        input_output_aliases={5: 0},
        scratch_shapes=[pltpu.VMEM((2, AQ, QB), F32)] * 2 + [pltpu.VMEM((AQ, QB), F32)] + [pltpu.VMEM((LP, QB), F32)] * 2
        + _exchange_sems(ne),
        compiler_params=_params(("arbitrary", "arbitrary")),
    )(hin, hin, hin, do, lt, dhin, *ex_arrs)


def _adamw_math(w, g, m, v):
    m_new = ADAM_B1 * m + (1.0 - ADAM_B1) * g
    v_new = ADAM_B2 * v + (1.0 - ADAM_B2) * jnp.square(g)
    m_hat = m_new / (1.0 - ADAM_B1 ** ADAM_STEP)
    v_hat = v_new / (1.0 - ADAM_B2 ** ADAM_STEP)
    return -ADAM_LR * (m_hat / (jnp.sqrt(v_hat) + ADAM_EPS) + ADAM_WD * w), m_new, v_new


def _adamw_small(ws, gs, ms, vs):
    n = len(ws)

    def body(*refs):
        for t in range(n):
            w_ref, g_ref, m_ref, v_ref = (refs[s * n + t] for s in range(4))
            d_ref, nm_ref, nv_ref = (refs[(4 + s) * n + t] for s in range(3))
            d_ref[...], nm_ref[...], nv_ref[...] = _adamw_math(w_ref[...], g_ref[...], m_ref[...], v_ref[...])

    vmem = pl.BlockSpec(memory_space=pltpu.VMEM)
    res = pl.pallas_call(
        body,
        name="adamw_small",
        in_specs=[vmem] * (4 * n),
        out_specs=[vmem] * (3 * n),
        out_shape=[jax.ShapeDtypeStruct(w.shape, F32) for w in ws] * 3,
    )(*ws, *gs, *ms, *vs)
    return res[:n], res[n:2 * n], res[2 * n:]


def _adamw(pieces, w, m, v, *, rows, row_off, tr, name):
    npieces, _, cols = pieces.shape
    ob = row_off // tr

    def body(p_ref, w_ref, m_ref, v_ref, g_ref, d_ref, nm_ref, nv_ref):
        g = p_ref[0].astype(F32)
        for s in range(1, npieces):
            g = g + p_ref[s].astype(F32)
        g_ref[...] = g
        d_ref[...], nm_ref[...], nv_ref[...] = _adamw_math(w_ref[...], g, m_ref[...], v_ref[...])

    spec = pl.BlockSpec((tr, cols), lambda i: (i, 0))
    return pl.pallas_call(
        body,
        name=name,
        grid=(rows // tr,),
        in_specs=[pl.BlockSpec((npieces, tr, cols), lambda i: (0, ob + i, 0)), spec, spec, spec],
        out_specs=[spec] * 4,
        out_shape=[jax.ShapeDtypeStruct((rows, cols), F32)] * 4,
        compiler_params=_params(("parallel",)),
    )(pieces, w, m, v)


def _sum_pieces(pieces, name):
    npieces, rows, cols = pieces.shape

    def body(p_ref, o_ref):
        acc = p_ref[0]
        for s in range(1, npieces):
            acc = acc + p_ref[s]
        o_ref[...] = acc

    return pl.pallas_call(
        body,
        name=name,
        in_specs=[pl.BlockSpec(memory_space=pltpu.VMEM)],
        out_specs=pl.BlockSpec(memory_space=pltpu.VMEM),
        out_shape=jax.ShapeDtypeStruct((rows, cols), pieces.dtype),
    )(pieces)


SMALL_ROWS = 48
CWF_PAD = 384
GRAD_ROWS = 72
G_META, G_PRE_MIX, G_MID, G_POST_FFN, G_CW_MIX, G_B_GATE, G_CW_FFN = 0, 16, 24, 32, 40, 48, 56


def kernel(x, meta_tokens, g_pre_mix, w_in, conv_w_mix, w_proj_conv, w_proj_attn, b_gate, w_out, g_post_mix, g_pre_ffn, w_up_gate, conv_w_ffn, w_down, g_post_ffn, loss_target, m_meta_tokens, m_g_pre_mix, m_w_in, m_conv_w_mix, m_w_proj_conv, m_w_proj_attn, m_b_gate, m_w_out, m_g_post_mix, m_g_pre_ffn, m_w_up_gate, m_conv_w_ffn, m_w_down, m_g_post_ffn, v_meta_tokens, v_g_pre_mix, v_w_in, v_conv_w_mix, v_w_proj_conv, v_w_proj_attn, v_b_gate, v_w_out, v_g_post_mix, v_g_pre_ffn, v_w_up_gate, v_conv_w_ffn, v_w_down, v_g_post_ffn):
    me = 4 * lax.axis_index("x") + 2 * lax.axis_index("y") + lax.axis_index("c")

    def rows_to(a, n):
        return jnp.pad(a, ((0, n - a.shape[0]), (0, 0)))

    small_shard = jnp.concatenate(
        [meta_tokens, rows_to(conv_w_mix[0], 8), rows_to(b_gate[0], 8),
         rows_to(jnp.pad(conv_w_ffn[0], ((0, 0), (0, CWF_PAD - R_DOWN))).reshape(9, 128), 16)], axis=0)
    wshard = jnp.concatenate([w_proj_conv[0], w_proj_attn[0], w_out[0], w_down[0]], axis=0).astype(BF16)
    w_in_all, small_all = _exchange([w_in[0].astype(BF16), small_shard], (True, True), "gather_in")

    def unshard(rows):
        return rows.transpose(1, 0, 2).reshape(rows.shape[1], N_DEV * rows.shape[2])

    meta = unshard(small_all[:, 0:16])
    cw_mix = unshard(small_all[:, 16:19])
    bgate = unshard(small_all[:, 24:26])
    cw_ffn = unshard(small_all[:, 32:41].reshape(N_DEV, 3, CWF_PAD)[:, :, :R_DOWN])
    cw4 = cw_ffn.reshape(3, N_FB, FB).transpose(1, 0, 2)
    h0 = jnp.concatenate([jnp.zeros((PAD, D), F32), meta, x[0]], axis=0)

    xn1 = _rms_fwd(h0, g_pre_mix, "rms_pre_mix")
    hin = _mm(xn1, w_in_all, w_rows=D, trans_w=False, out_dtype=BF16, name="mm_in")
    y_conv = _conv_mix_fwd(hin, cw_mix)
    o, lt, wpack, w_ug = _attn_fwd(hin, [wshard, w_up_gate[0].astype(BF16)], (True, True))
    w_pc = wpack[:, O_PC:O_PA].reshape(1, D, D)
    w_pa = wpack[:, O_PA:O_OUT].reshape(1, D, D)
    w_o = wpack[:, O_OUT:O_DOWN].reshape(1, D, D)
    w_dn = wpack[:, O_DOWN:].reshape(N_FB, FB, D)
    bc = _mm(y_conv, w_pc, w_rows=D, trans_w=False, out_dtype=BF16, name="mm_proj_conv")[0]
    ba = _mm(o, w_pa, w_rows=D, trans_w=False, out_dtype=BF16, name="mm_proj_attn")[0]
    merged = _gate_fwd(bc, ba, hin, bgate)
    mix = _mm(merged, w_o, w_rows=D, trans_w=False, out_dtype=F32, name="mm_out")[0]
    h1, xn2 = _resid_rms(h0, mix, g_post_mix, g_pre_ffn)
    ug = _mm(xn2, w_ug, w_rows=D, trans_w=False, out_dtype=BF16, name="mm_up_gate")
    hid = _ffn_act_fwd(ug, cw4)
    ffn = _mm_sum(hid, w_dn, w_rows=FB, trans_w=False, out_dtype=F32, name="mm_down")
    dout, dffn, loss8, dg_post_ffn = _loss_head(h1, ffn, g_post_ffn, loss_target[0])

    dhid = _mm(dffn, w_dn, w_rows=FB, trans_w=True, out_dtype=BF16, name="mm_down_dx")
    gw_dn = _mm_tn(hid, dffn, nb=N_FB, out_dtype=BF16, name="mm_down_dw")
    dug, dcw4 = _ffn_act_bwd(ug, dhid, cw4)
    dug = dug.reshape(2 * N_FB, LP, FB)
    dxn2 = _mm_sum(dug, w_ug, w_rows=D, trans_w=True, out_dtype=F32, name="mm_up_gate_dx")
    gw_ug = _mm_tn(xn2, dug, nb=N_DEV, out_dtype=BF16, name="mm_up_gate_dw")
    dh1, dmix, dg_mid = _mid_bwd(dout, h1, dxn2, mix, g_post_mix, g_pre_ffn)
    dmerged = _mm(dmix, w_o, w_rows=D, trans_w=True, out_dtype=BF16, name="mm_out_dx")[0]
    gw_out = _mm_tn(merged, dmix, nb=1, out_dtype=BF16, name="mm_out_dw")
    dbc, dba, dhin, db_gate = _gate_bwd(dmerged, bc, ba, hin, bgate)
    dy_conv = _mm(dbc, w_pc, w_rows=D, trans_w=True, out_dtype=BF16, name="mm_proj_conv_dx")[0]
    gw_pc = _mm_tn(y_conv, dbc, nb=1, out_dtype=BF16, name="mm_proj_conv_dw")
    do = _mm(dba, w_pa, w_rows=D, trans_w=True, out_dtype=BF16, name="mm_proj_attn_dx")[0]
    gw_pa = _mm_tn(o, dba, nb=1, out_dtype=BF16, name="mm_proj_attn_dw")
    dhin, dcw_mix = _conv_mix_bwd(hin, dy_conv, cw_mix, dhin)
    gpack = jnp.concatenate(
        [gw_pc.reshape(N_DEV, R_PROJ, D), gw_pa.reshape(N_DEV, R_PROJ, D), gw_out.reshape(N_DEV, R_PROJ, D),
         gw_dn.reshape(N_DEV, R_DOWN, D)], axis=1)
    dhin, rpack, rug = _attn_bwd(hin, do, lt, dhin, [gpack, gw_ug], (False, False))
    gw_in = _mm_tn(xn1, dhin, nb=N_DEV, out_dtype=BF16, name="mm_in_dw")
    dcw_ffn = dcw4[:, :3].transpose(1, 0, 2).reshape(3, D_FF)
    small_a = jnp.concatenate(
        [dg_mid, dg_post_ffn, dcw_mix, db_gate,
         jnp.pad(dcw_ffn.reshape(-1), (0, 16 * D - 3 * D_FF)).reshape(16, D)], axis=0)
    dxn1, rin, rsmall_a = _mm_sum(dhin, w_in_all, w_rows=D, trans_w=True, out_dtype=F32, name="mm_in_dx",
                                  exchange=([gw_in, small_a], (False, True)))
    dh0, dg_pre_mix = _first_bwd(dh1, h0, dxn1, g_pre_mix)
    (rsmall_b,) = _exchange([jnp.concatenate([dh0[PAD:OFF], dg_pre_mix], axis=0)], (True,), "gather_last_grads")
    gs = _sum_pieces(jnp.concatenate([rsmall_b, rsmall_a], axis=1), "sum_small_grads")
    grad_x = dh0[OFF:]
    loss = lax.psum(loss8[0, 0], ("x", "y", "c"))

    def cols(a, width):
        return lax.dynamic_slice_in_dim(a, me * width, width, axis=1)

    g_meta = cols(gs[G_META:G_META + N_META], 128)
    g_gpm, g_gff = gs[G_PRE_MIX:G_PRE_MIX + 1], gs[G_POST_FFN:G_POST_FFN + 1]
    g_gpo, g_gpf = gs[G_MID:G_MID + 1], gs[G_MID + 1:G_MID + 2]
    g_cwm = cols(gs[G_CW_MIX:G_CW_MIX + 3], 128)[None]
    g_bg = cols(gs[G_B_GATE:G_B_GATE + 2], 128)[None]
    g_cwf = cols(gs[G_CW_FFN:G_CW_FFN + 9].reshape(-1)[:3 * D_FF].reshape(3, D_FF), R_DOWN)[None]

    def big(pieces, w, m, v, rows, row_off, tr, name):
        g, d, nm, nv = _adamw(pieces, w[0], m[0], v[0], rows=rows, row_off=row_off, tr=tr, name=name)
        return g[None], d[None], nm[None], nv[None]

    r_in = big(rin, w_in, m_w_in, v_w_in, D, 0, 256, "adamw_in")
    r_pc = big(rpack, w_proj_conv, m_w_proj_conv, v_w_proj_conv, R_PROJ, O_PC, R_PROJ, "adamw_proj_conv")
    r_pa = big(rpack, w_proj_attn, m_w_proj_attn, v_w_proj_attn, R_PROJ, O_PA, R_PROJ, "adamw_proj_attn")
    r_out = big(rpack, w_out, m_w_out, v_w_out, R_PROJ, O_OUT, R_PROJ, "adamw_out")
    r_dn = big(rpack, w_down, m_w_down, v_w_down, R_DOWN, O_DOWN, 32, "adamw_down")
    r_ug = big(rug, w_up_gate, m_w_up_gate, v_w_up_gate, D, 0, 256, "adamw_up_gate")

    small_w = [meta_tokens, g_pre_mix, conv_w_mix, b_gate, g_post_mix, g_pre_ffn, conv_w_ffn, g_post_ffn]
    small_g = [g_meta, g_gpm, g_cwm, g_bg, g_gpo, g_gpf, g_cwf, g_gff]
    small_m = [m_meta_tokens, m_g_pre_mix, m_conv_w_mix, m_b_gate, m_g_post_mix, m_g_pre_ffn, m_conv_w_ffn, m_g_post_ffn]
    small_v = [v_meta_tokens, v_g_pre_mix, v_conv_w_mix, v_b_gate, v_g_post_mix, v_g_pre_ffn, v_conv_w_ffn, v_g_post_ffn]

    s_g = small_g
    s_d, s_m, s_v = _adamw_small(small_w, small_g, small_m, small_v)

    def ordered(k, smalls):
        meta, gpm, cwm, bg, gpo, gpf, cwf, gff = smalls
        return [meta, gpm, r_in[k], cwm, r_pc[k], r_pa[k], bg, r_out[k], gpo, gpf, r_ug[k], cwf, r_dn[k], gff]

    return (loss, grad_x[None], *ordered(0, s_g), *ordered(1, s_d), *ordered(2, s_m), *ordered(3, s_v))
```

```python
import functools
import math

import jax
import jax.numpy as jnp
from jax import lax
from jax.experimental import pallas as pl
from jax.experimental.pallas import tpu as pltpu

F32 = jnp.float32
BF16 = jnp.bfloat16

D = 1024
SEQ = 4096
N_META = 16
PAD = 112
OFF = PAD + N_META
LP = OFF + SEQ
QB = 128
AQ = 384
KPQ = AQ // QB
TM = 384
MM_TM = 1408
HALO = 16
N_DEV = 8
D_FF = 2816
FB = 704
N_FB = D_FF // FB
RMS_EPS = 1e-6
SCALE = 0.125
HEAD_LANES = 64
VMEM_LIMIT = 56 * 1024 * 1024

ADAM_LR = 0.001
ADAM_B1 = 0.9
ADAM_B2 = 0.999
ADAM_EPS = 1e-08
ADAM_WD = 0.01
ADAM_STEP = 10

R_PROJ, R_DOWN = 128, 352
O_PC = 0
O_PA = O_PC + R_PROJ
O_OUT = O_PA + R_PROJ
O_DOWN = O_OUT + R_PROJ
R_PACK = O_DOWN + R_DOWN

NT = (((1,), (1,)), ((), ()))
NN = (((1,), (0,)), ((), ()))
TN = (((0,), (0,)), ((), ()))


def _params(sem):
    return pltpu.CompilerParams(dimension_semantics=sem, vmem_limit_bytes=VMEM_LIMIT)


def _dot(a, b, dn=NN):
    return lax.dot_general(a, b, dn, preferred_element_type=F32)


def _exchange_copies(ins, outs, gathers, send_sems, recv_sems, loc_sems):
    x, y, c = lax.axis_index("x"), lax.axis_index("y"), lax.axis_index("c")
    me = 4 * x + 2 * y + c
    copies = []
    for a, gather in enumerate(gathers):
        copies.append(pltpu.make_async_copy(ins[a] if gather else ins[a].at[me], outs[a].at[me], loc_sems.at[a]))
    for k in range(1, N_DEV):
        px = 1 - x if k & 4 else x
        py = 1 - y if k & 2 else y
        pc = 1 - c if k & 1 else c
        peer = 4 * px + 2 * py + pc
        for a, gather in enumerate(gathers):
            copies.append(pltpu.make_async_remote_copy(
                src_ref=ins[a] if gather else ins[a].at[peer],
                dst_ref=outs[a].at[me],
                send_sem=send_sems.at[a * (N_DEV - 1) + k - 1],
                recv_sem=recv_sems.at[a * (N_DEV - 1) + k - 1],
                device_id=(px, py, pc),
                device_id_type=pl.DeviceIdType.MESH,
            ))
    return copies


def _exchange_shapes(arrs, gathers):
    return [jax.ShapeDtypeStruct((N_DEV,) + (a.shape if g else a.shape[1:]), a.dtype) for a, g in zip(arrs, gathers)]


def _exchange_sems(n):
    return [pltpu.SemaphoreType.DMA((n * (N_DEV - 1),)), pltpu.SemaphoreType.DMA((n * (N_DEV - 1),)),
            pltpu.SemaphoreType.DMA((n,))]


ANY_SPEC = pl.BlockSpec(memory_space=pl.ANY)


def _gather_two_level(arrs, name):
    n = len(arrs)
    per = 7

    def body(*refs):
        ins, outs = refs[:n], refs[n:2 * n]
        send_sems, recv_sems, loc_sems = refs[2 * n:]
        x, y, c = lax.axis_index("x"), lax.axis_index("y"), lax.axis_index("c")
        me, sibling = (x, y, c), (x, y, 1 - c)
        chips = [(1 - x, y), (x, 1 - y), (1 - x, 1 - y)]

        def copy(a, k, block, to, src=None):
            place = outs[a].at[4 * block[0] + 2 * block[1] + block[2]]
            return pltpu.make_async_remote_copy(
                src_ref=place if src is None else src, dst_ref=place,
                send_sem=send_sems.at[a * per + k], recv_sem=recv_sems.at[a * per + k],
                device_id=to, device_id_type=pl.DeviceIdType.MESH)

        mine = [pltpu.make_async_copy(ins[a], outs[a].at[4 * x + 2 * y + c], loc_sems.at[a]) for a in range(n)]
        first = [copy(a, 0, me, sibling, src=ins[a]) for a in range(n)]
        first += [copy(a, 1 + j, me, (*chip, c), src=ins[a]) for j, chip in enumerate(chips) for a in range(n)]
        for cp in mine + first:
            cp.start()
        passed = []
        for j, chip in enumerate(chips):
            for a in range(n):
                copy(a, 1 + j, (*chip, c), me).wait_recv()
                passed.append(copy(a, 4 + j, (*chip, c), sibling))
                passed[-1].start()
        for a in range(n):
            copy(a, 0, sibling, me).wait_recv()
        for j, chip in enumerate(chips):
            for a in range(n):
                copy(a, 4 + j, (*chip, 1 - c), me).wait_recv()
        for cp in first + passed:
            cp.wait_send()
        for cp in mine:
            cp.wait()

    return pl.pallas_call(
        body,
        name=name,
        out_shape=_exchange_shapes(arrs, (True,) * n),
        in_specs=[ANY_SPEC] * n,
        out_specs=[ANY_SPEC] * n,
        scratch_shapes=_exchange_sems(n),
    )(*arrs)


def _exchange(arrs, gathers, name):
    n = len(arrs)

    def body(*refs):
        copies = _exchange_copies(refs[:n], refs[n:2 * n], gathers, *refs[2 * n:])
        for cp in copies:
            cp.start()
        for cp in copies:
            cp.wait()

    return pl.pallas_call(
        body,
        name=name,
        out_shape=_exchange_shapes(arrs, gathers),
        in_specs=[ANY_SPEC] * n,
        out_specs=[ANY_SPEC] * n,
        scratch_shapes=_exchange_sems(n),
    )(*arrs)


def _mm(a, w, *, w_rows, trans_w, out_dtype, name):
    nb, _, wc = w.shape
    m, k = a.shape[-2:]
    n = w_rows if trans_w else wc
    dn = NT if trans_w else NN

    def body(a_ref, w_ref, o_ref):
        o_ref[...] = _dot(a_ref[...], w_ref[...], dn).astype(out_dtype)

    if a.ndim == 2:
        a_spec = pl.BlockSpec((MM_TM, k), lambda j, i: (i, 0))
    else:
        a_spec = pl.BlockSpec((None, MM_TM, k), lambda j, i: (j, i, 0))
    return pl.pallas_call(
        body,
        name=name,
        grid=(nb, m // MM_TM),
        in_specs=[a_spec, pl.BlockSpec((None, w_rows, wc), lambda j, i: (j, 0, 0))],
        out_specs=pl.BlockSpec((None, MM_TM, n), lambda j, i: (j, i, 0)),
        out_shape=jax.ShapeDtypeStruct((nb, m, n), out_dtype),
        compiler_params=_params(("parallel", "parallel")),
    )(a, w)


def _mm_sum(a, w, *, w_rows, trans_w, out_dtype, name, exchange=((), ())):
    nb, m, k = a.shape
    wc = w.shape[2]
    n = w_rows if trans_w else wc
    dn = NT if trans_w else NN
    ex_arrs, gathers = exchange
    ne = len(ex_arrs)
    steps = m // MM_TM

    def body(*refs):
        a_ref, w_ref, o_ref, acc_ref = refs[0], refs[1], refs[2 + ne], refs[3 + 2 * ne]
        i, j = pl.program_id(0), pl.program_id(1)

        def copies():
            return _exchange_copies(refs[2:2 + ne], refs[3 + ne:3 + 2 * ne], gathers, *refs[4 + 2 * ne:])

        if ne:
            @pl.when((i == 0) & (j == 0))
            def _():
                for cp in copies():
                    cp.start()

        @pl.when(j == 0)
        def _():
            acc_ref[...] = jnp.zeros_like(acc_ref)

        acc_ref[...] += _dot(a_ref[...], w_ref[...], dn)

        @pl.when(j == nb - 1)
        def _():
            o_ref[...] = acc_ref[...].astype(out_dtype)

        if ne:
            @pl.when((i == steps - 1) & (j == nb - 1))
            def _():
                for cp in copies():
                    cp.wait()

    res = pl.pallas_call(
        body,
        name=name,
        grid=(steps, nb),
        in_specs=[
            pl.BlockSpec((None, MM_TM, k), lambda i, j: (j, i, 0)),
            pl.BlockSpec((None, w_rows, wc), lambda i, j: (j, 0, 0)),
        ] + [ANY_SPEC] * ne,
        out_specs=[pl.BlockSpec((MM_TM, n), lambda i, j: (i, 0))] + [ANY_SPEC] * ne,
        out_shape=[jax.ShapeDtypeStruct((m, n), out_dtype)] + _exchange_shapes(ex_arrs, gathers),
        scratch_shapes=[pltpu.VMEM((MM_TM, n), F32)] + (_exchange_sems(ne) if ne else []),
        compiler_params=_params(("arbitrary", "arbitrary")),
    )(a, w, *ex_arrs)
    return res if ne else res[0]


def _mm_tn(a, b, *, nb, out_dtype, name):
    m, ka = a.shape[-2:]
    n = b.shape[-1]
    steps = m // MM_TM

    def body(a_ref, b_ref, o_ref, acc_ref):
        i = pl.program_id(1)

        @pl.when(i == 0)
        def _():
            acc_ref[...] = jnp.zeros_like(acc_ref)

        acc_ref[...] += _dot(a_ref[...], b_ref[...], TN)

        @pl.when(i == steps - 1)
        def _():
            o_ref[...] = acc_ref[...].astype(out_dtype)

    def spec(arr, cols):
        if arr.ndim == 2:
            return pl.BlockSpec((MM_TM, cols), lambda j, i: (i, 0))
        return pl.BlockSpec((None, MM_TM, cols), lambda j, i: (j, i, 0))

    return pl.pallas_call(
        body,
        name=name,
        grid=(nb, steps),
        in_specs=[spec(a, ka), spec(b, n)],
        out_specs=pl.BlockSpec((None, ka, n), lambda j, i: (j, 0, 0)),
        out_shape=jax.ShapeDtypeStruct((nb, ka, n), out_dtype),
        scratch_shapes=[pltpu.VMEM((ka, n), F32)],
        compiler_params=_params(("parallel", "arbitrary")),
    )(a, b)


def _rstd(x):
    return lax.rsqrt(jnp.mean(x * x, axis=-1, keepdims=True) + RMS_EPS)


def _rms_bwd(x, g, dy):
    r = _rstd(x)
    u = dy * g
    dx = r * u - x * (r * r * r) * jnp.mean(u * x, axis=-1, keepdims=True)
    return dx, dy * x * r


def _row_spec(cols=D, tm=TM):
    return pl.BlockSpec((tm, cols), lambda i: (i, 0))


def _vec_spec(rows=1, cols=D):
    return pl.BlockSpec((rows, cols), lambda i: (0, 0))


def _rms_fwd(x, g, name):
    def body(x_ref, g_ref, o_ref):
        x = x_ref[...]
        o_ref[...] = (x * _rstd(x) * g_ref[...]).astype(BF16)

    return pl.pallas_call(
        body,
        name=name,
        grid=(LP // TM,),
        in_specs=[_row_spec(), _vec_spec()],
        out_specs=_row_spec(),
        out_shape=jax.ShapeDtypeStruct((LP, D), BF16),
        compiler_params=_params(("parallel",)),
    )(x, g)


def _resid_rms(h0, mix, g_post, g_next):
    def body(h0_ref, mix_ref, gp_ref, gn_ref, h1_ref, xn_ref):
        mix = mix_ref[...]
        h1 = h0_ref[...] + mix * _rstd(mix) * gp_ref[...]
        h1_ref[...] = h1
        xn_ref[...] = (h1 * _rstd(h1) * gn_ref[...]).astype(BF16)

    return pl.pallas_call(
        body,
        name="resid_rms",
        grid=(LP // TM,),
        in_specs=[_row_spec(), _row_spec(), _vec_spec(), _vec_spec()],
        out_specs=[_row_spec(), _row_spec()],
        out_shape=[jax.ShapeDtypeStruct((LP, D), F32), jax.ShapeDtypeStruct((LP, D), BF16)],
        compiler_params=_params(("parallel",)),
    )(h0, mix, g_post, g_next)


def _loss_head(h1, ffn, g_post, target):
    nblk = LP // QB

    def body(h1_ref, ffn_ref, g_ref, t_ref, dout_ref, dffn_ref, loss_ref, dg_ref):
        i = pl.program_id(0)

        @pl.when(i == 0)
        def _():
            loss_ref[...] = jnp.zeros_like(loss_ref)
            dg_ref[...] = jnp.zeros_like(dg_ref)

        ffn = ffn_ref[...]
        g = g_ref[...]
        out = h1_ref[...] + ffn * _rstd(ffn) * g
        err = jnp.where(i > 0, out - t_ref[...], 0.0)
        loss_ref[...] += 0.5 * jnp.sum(err * err) / D
        dout = err / D
        dout_ref[...] = dout
        dffn, dg = _rms_bwd(ffn, g, dout)
        dffn_ref[...] = dffn.astype(BF16)
        dg_ref[0:1, :] += jnp.sum(dg, axis=0, keepdims=True)

    return pl.pallas_call(
        body,
        name="loss_head",
        grid=(nblk,),
        in_specs=[
            _row_spec(tm=QB),
            _row_spec(tm=QB),
            _vec_spec(),
            pl.BlockSpec((QB, D), lambda i: (jnp.maximum(i - 1, 0), 0)),
        ],
        out_specs=[_row_spec(tm=QB), _row_spec(tm=QB), _vec_spec(8, 128), _vec_spec(8, D)],
        out_shape=[
            jax.ShapeDtypeStruct((LP, D), F32),
            jax.ShapeDtypeStruct((LP, D), BF16),
            jax.ShapeDtypeStruct((8, 128), F32),
            jax.ShapeDtypeStruct((8, D), F32),
        ],
        compiler_params=_params(("arbitrary",)),
    )(h1, ffn, g_post, target)


def _mid_bwd(dout, h1, dxn2, mix, g_post_mix, g_pre_ffn):
    def body(dout_ref, h1_ref, dxn_ref, mix_ref, gpm_ref, gpf_ref, dh1_ref, dmix_ref, dg_ref):
        i = pl.program_id(0)

        @pl.when(i == 0)
        def _():
            dg_ref[...] = jnp.zeros_like(dg_ref)

        dx, dg_ffn = _rms_bwd(h1_ref[...], gpf_ref[...], dxn_ref[...])
        dh1 = dout_ref[...] + dx
        dh1_ref[...] = dh1
        dmix, dg_mix = _rms_bwd(mix_ref[...], gpm_ref[...], dh1)
        dmix_ref[...] = dmix.astype(BF16)
        dg_ref[0:1, :] += jnp.sum(dg_mix, axis=0, keepdims=True)
        dg_ref[1:2, :] += jnp.sum(dg_ffn, axis=0, keepdims=True)

    return pl.pallas_call(
        body,
        name="mid_bwd",
        grid=(LP // TM,),
        in_specs=[_row_spec(), _row_spec(), _row_spec(), _row_spec(), _vec_spec(), _vec_spec()],
        out_specs=[_row_spec(), _row_spec(), _vec_spec(8, D)],
        out_shape=[
            jax.ShapeDtypeStruct((LP, D), F32),
            jax.ShapeDtypeStruct((LP, D), BF16),
            jax.ShapeDtypeStruct((8, D), F32),
        ],
        compiler_params=_params(("arbitrary",)),
    )(dout, h1, dxn2, mix, g_post_mix, g_pre_ffn)


def _first_bwd(dh1, h0, dxn1, g_pre_mix):
    def body(dh1_ref, h0_ref, dxn_ref, g_ref, dh0_ref, dg_ref):
        i = pl.program_id(0)

        @pl.when(i == 0)
        def _():
            dg_ref[...] = jnp.zeros_like(dg_ref)

        dx, dg = _rms_bwd(h0_ref[...], g_ref[...], dxn_ref[...])
        dh0_ref[...] = dh1_ref[...] + dx
        dg_ref[0:1, :] += jnp.sum(dg, axis=0, keepdims=True)

    return pl.pallas_call(
        body,
        name="first_bwd",
        grid=(LP // TM,),
        in_specs=[_row_spec(), _row_spec(), _row_spec(), _vec_spec()],
        out_specs=[_row_spec(), _vec_spec(8, D)],
        out_shape=[jax.ShapeDtypeStruct((LP, D), F32), jax.ShapeDtypeStruct((8, D), F32)],
        compiler_params=_params(("arbitrary",)),
    )(dh1, h0, dxn1, g_pre_mix)


def _prev_halo(i):
    return jnp.maximum(i * (TM // HALO) - 1, 0)


def _next_halo(i):
    return jnp.minimum((i + 1) * (TM // HALO), LP // HALO - 1)


def _down(x, s):
    return pltpu.roll(x, s, 0)


def _up(x, s):
    return pltpu.roll(x, x.shape[0] - s, 0)


def _conv_mix_fwd(hin, cw):
    def body(b_ref, c_ref, h_ref, cp_ref, hp_ref, w_ref, y_ref):
        i = pl.program_id(0)
        p = c_ref[...].astype(F32) * h_ref[...].astype(F32)
        pp = jnp.where(i > 0, cp_ref[...].astype(F32) * hp_ref[...].astype(F32), 0.0)
        ext = jnp.concatenate([pp, p], axis=0)
        w = [w_ref[t:t + 1, :] for t in range(3)]
        cv = w[2] * ext + w[1] * _down(ext, 1) + w[0] * _down(ext, 2)
        y_ref[...] = (b_ref[...].astype(F32) * cv[HALO:]).astype(BF16)

    def tile(s):
        return pl.BlockSpec((None, TM, D), lambda i: (s, i, 0))

    def prev(s):
        return pl.BlockSpec((None, HALO, D), lambda i: (s, _prev_halo(i), 0))

    return pl.pallas_call(
        body,
        name="conv_mix_fwd",
        grid=(LP // TM,),
        in_specs=[tile(0), tile(1), tile(2), prev(1), prev(2), _vec_spec(3, D)],
        out_specs=_row_spec(),
        out_shape=jax.ShapeDtypeStruct((LP, D), BF16),
        compiler_params=_params(("parallel",)),
    )(hin, hin, hin, hin, hin, cw)


def _conv_mix_bwd(hin, dy, cw, dhin):
    last = LP // TM - 1

    def body(b_ref, c_ref, h_ref, dy_ref, cp_ref, hp_ref, bn_ref, dyn_ref, w_ref, _, out_ref, dw_ref):
        i = pl.program_id(0)

        @pl.when(i == 0)
        def _():
            dw_ref[...] = jnp.zeros_like(dw_ref)

        b = b_ref[...].astype(F32)
        c = c_ref[...].astype(F32)
        h = h_ref[...].astype(F32)
        dy = dy_ref[...].astype(F32)
        w = [w_ref[t:t + 1, :] for t in range(3)]
        p = c * h
        pp = jnp.where(i > 0, cp_ref[...].astype(F32) * hp_ref[...].astype(F32), 0.0)
        ext = jnp.concatenate([pp, p], axis=0)
        p1 = _down(ext, 1)[HALO:]
        p2 = _down(ext, 2)[HALO:]
        cv = w[2] * p + w[1] * p1 + w[0] * p2
        out_ref[0] = (dy * cv).astype(BF16)
        dcv = dy * b
        dcvn = jnp.where(i < last, dyn_ref[...].astype(F32) * bn_ref[...].astype(F32), 0.0)
        dext = jnp.concatenate([dcv, dcvn], axis=0)
        dp = (w[2] * dext + w[1] * _up(dext, 1) + w[0] * _up(dext, 2))[:TM]
        out_ref[1] = (dp * h).astype(BF16)
        out_ref[2] = (dp * c).astype(BF16)
        dw_ref[0:1, :] += jnp.sum(dcv * p2, axis=0, keepdims=True)
        dw_ref[1:2, :] += jnp.sum(dcv * p1, axis=0, keepdims=True)
        dw_ref[2:3, :] += jnp.sum(dcv * p, axis=0, keepdims=True)

    def tile(s):
        return pl.BlockSpec((None, TM, D), lambda i: (s, i, 0))

    def prev(s):
        return pl.BlockSpec((None, HALO, D), lambda i: (s, _prev_halo(i), 0))

    return pl.pallas_call(
        body,
        name="conv_mix_bwd",
        grid=(LP // TM,),
        in_specs=[
            tile(0), tile(1), tile(2), _row_spec(),
            prev(1), prev(2),
            pl.BlockSpec((None, HALO, D), lambda i: (0, _next_halo(i), 0)),
            pl.BlockSpec((HALO, D), lambda i: (_next_halo(i), 0)),
            _vec_spec(3, D),
            pl.BlockSpec(memory_space=pl.ANY),
        ],
        out_specs=[pl.BlockSpec((3, TM, D), lambda i: (0, i, 0)), _vec_spec(8, D)],
        out_shape=[jax.ShapeDtypeStruct((N_DEV, LP, D), BF16), jax.ShapeDtypeStruct((8, D), F32)],
        input_output_aliases={9: 0},
        compiler_params=_params(("arbitrary",)),
    )(hin, hin, hin, dy, hin, hin, hin, dy, cw, dhin)


GELU_K = math.sqrt(2.0 / math.pi)
GELU_A = 0.044715


def _gelu_and_grad(x):
    x2 = x * x
    t = jnp.tanh(GELU_K * (x + GELU_A * x2 * x))
    gelu = 0.5 * x * (1.0 + t)
    grad = 0.5 * (1.0 + t) + 0.5 * x * (1.0 - t * t) * GELU_K * (1.0 + 3.0 * GELU_A * x2)
    return gelu, grad


def _ffn_act_fwd(ug, cw4):
    def body(u_ref, g_ref, up_ref, w_ref, o_ref):
        i = pl.program_id(1)
        u = u_ref[...].astype(F32)
        up = jnp.where(i > 0, up_ref[...].astype(F32), 0.0)
        ext = jnp.concatenate([up, u], axis=0)
        w = [w_ref[t:t + 1, :] for t in range(3)]
        uc = (w[2] * ext + w[1] * _down(ext, 1) + w[0] * _down(ext, 2))[HALO:]
        gelu, _ = _gelu_and_grad(uc)
        o_ref[...] = (gelu * g_ref[...].astype(F32)).astype(BF16)

    return pl.pallas_call(
        body,
        name="ffn_act_fwd",
        grid=(N_FB, LP // TM),
        in_specs=[
            pl.BlockSpec((None, TM, FB), lambda j, i: (j, i, 0)),
            pl.BlockSpec((None, TM, FB), lambda j, i: (j + N_FB, i, 0)),
            pl.BlockSpec((None, HALO, FB), lambda j, i: (j, _prev_halo(i), 0)),
            pl.BlockSpec((None, 3, FB), lambda j, i: (j, 0, 0)),
        ],
        out_specs=pl.BlockSpec((None, TM, FB), lambda j, i: (j, i, 0)),
        out_shape=jax.ShapeDtypeStruct((N_FB, LP, FB), BF16),
        compiler_params=_params(("parallel", "parallel")),
    )(ug, ug, ug, cw4)


def _ffn_act_bwd(ug, dhid, cw4):
    last = LP // TM - 1
    n = TM + 2 * HALO

    def body(u_ref, g_ref, dh_ref, up_ref, un_ref, gn_ref, dhn_ref, w_ref, dug_ref, dw_ref):
        i = pl.program_id(1)

        @pl.when(i == 0)
        def _():
            dw_ref[...] = jnp.zeros_like(dw_ref)

        w = [w_ref[t:t + 1, :] for t in range(3)]
        u = u_ref[...].astype(F32)
        up = jnp.where(i > 0, up_ref[...].astype(F32), 0.0)
        ext = jnp.concatenate([up, u, un_ref[...].astype(F32)], axis=0)
        u1 = _down(ext, 1)
        u2 = _down(ext, 2)
        uc = w[2] * ext + w[1] * u1 + w[0] * u2
        gelu, ggrad = _gelu_and_grad(uc)
        zeros = jnp.zeros((HALO, FB), F32)
        gext = jnp.concatenate([zeros, g_ref[...].astype(F32), gn_ref[...].astype(F32)], axis=0)
        dhn = jnp.where(i < last, dhn_ref[...].astype(F32), 0.0)
        dhext = jnp.concatenate([zeros, dh_ref[...].astype(F32), dhn], axis=0)
        dug_ref[1] = (dhext * gelu)[HALO:HALO + TM].astype(BF16)
        duc = dhext * gext * ggrad
        du = w[2] * duc + w[1] * _up(duc, 1) + w[0] * _up(duc, 2)
        dug_ref[0] = du[HALO:HALO + TM].astype(BF16)
        row = lax.broadcasted_iota(jnp.int32, (n, 1), 0)
        own = jnp.where((row >= HALO) & (row < HALO + TM), duc, 0.0)
        dw_ref[0:1, :] += jnp.sum(own * u2, axis=0, keepdims=True)
        dw_ref[1:2, :] += jnp.sum(own * u1, axis=0, keepdims=True)
        dw_ref[2:3, :] += jnp.sum(own * ext, axis=0, keepdims=True)

    def tile(off):
        return pl.BlockSpec((None, TM, FB), lambda j, i: (j + off, i, 0))

    def nxt(off):
        return pl.BlockSpec((None, HALO, FB), lambda j, i: (j + off, _next_halo(i), 0))

    return pl.pallas_call(
        body,
        name="ffn_act_bwd",
        grid=(N_FB, LP // TM),
        in_specs=[
            tile(0), tile(N_FB), tile(0),
            pl.BlockSpec((None, HALO, FB), lambda j, i: (j, _prev_halo(i), 0)),
            nxt(0), nxt(N_FB), nxt(0),
            pl.BlockSpec((None, 3, FB), lambda j, i: (j, 0, 0)),
        ],
        out_specs=[
            pl.BlockSpec((2, None, TM, FB), lambda j, i: (0, j, i, 0)),
            pl.BlockSpec((None, 8, FB), lambda j, i: (j, 0, 0)),
        ],
        out_shape=[jax.ShapeDtypeStruct((2, N_FB, LP, FB), BF16), jax.ShapeDtypeStruct((N_FB, 8, FB), F32)],
        compiler_params=_params(("parallel", "arbitrary")),
    )(ug, ug, dhid, ug, ug, ug, dhid, cw4)


def _gate_fwd(bc, ba, hin, bgate):
    def body(bc_ref, ba_ref, gc_ref, ga_ref, b_ref, o_ref):
        b = b_ref[...]
        sc = jax.nn.sigmoid(gc_ref[...].astype(F32) + b[0:1])
        sa = jax.nn.sigmoid(ga_ref[...].astype(F32) + b[1:2])
        o_ref[...] = (sc * bc_ref[...].astype(F32) + sa * ba_ref[...].astype(F32)).astype(BF16)

    def tile(s):
        return pl.BlockSpec((None, TM, D), lambda i: (s, i, 0))

    return pl.pallas_call(
        body,
        name="gate_fwd",
        grid=(LP // TM,),
        in_specs=[_row_spec(), _row_spec(), tile(6), tile(7), _vec_spec(2, D)],
        out_specs=_row_spec(),
        out_shape=jax.ShapeDtypeStruct((LP, D), BF16),
        compiler_params=_params(("parallel",)),
    )(bc, ba, hin, hin, bgate)


def _gate_bwd(dm, bc, ba, hin, bgate):
    def body(dm_ref, bc_ref, ba_ref, gc_ref, ga_ref, b_ref, dbc_ref, dba_ref, dg_ref, db_ref):
        i = pl.program_id(0)

        @pl.when(i == 0)
        def _():
            db_ref[...] = jnp.zeros_like(db_ref)

        b = b_ref[...]
        dm = dm_ref[...].astype(F32)
        sc = jax.nn.sigmoid(gc_ref[...].astype(F32) + b[0:1])
        sa = jax.nn.sigmoid(ga_ref[...].astype(F32) + b[1:2])
        dbc_ref[...] = (dm * sc).astype(BF16)
        dba_ref[...] = (dm * sa).astype(BF16)
        dgc = dm * bc_ref[...].astype(F32) * sc * (1.0 - sc)
        dga = dm * ba_ref[...].astype(F32) * sa * (1.0 - sa)
        dg_ref[0] = dgc.astype(BF16)
        dg_ref[1] = dga.astype(BF16)
        db_ref[0:1, :] += jnp.sum(dgc, axis=0, keepdims=True)
        db_ref[1:2, :] += jnp.sum(dga, axis=0, keepdims=True)

    def tile(s):
        return pl.BlockSpec((None, TM, D), lambda i: (s, i, 0))

    return pl.pallas_call(
        body,
        name="gate_bwd",
        grid=(LP // TM,),
        in_specs=[_row_spec(), _row_spec(), _row_spec(), tile(6), tile(7), _vec_spec(2, D)],
        out_specs=[_row_spec(), _row_spec(), pl.BlockSpec((2, TM, D), lambda i: (3, i, 0)), _vec_spec(8, D)],
        out_shape=[jax.ShapeDtypeStruct((LP, D), BF16)] * 2
        + [jax.ShapeDtypeStruct((N_DEV, LP, D), BF16), jax.ShapeDtypeStruct((8, D), F32)],
        compiler_params=_params(("arbitrary",)),
    )(dm, bc, ba, hin, hin, bgate)


Z_LINEAR = 30.0


def _softplus(z):
    return jnp.maximum(z, jnp.log(1.0 + jnp.exp(jnp.minimum(z, Z_LINEAR))))


def _cumsum_matrix(inclusive, reverse):
    r = lax.broadcasted_iota(jnp.int32, (QB, 2 * QB), 0)
    c = lax.broadcasted_iota(jnp.int32, (QB, 2 * QB), 1)
    if reverse:
        tri = r > c
    elif inclusive:
        tri = r <= c
    else:
        tri = r < c
    return jnp.where((c >= QB) | tri, 1.0, 0.0).astype(BF16)


def _split_dot(x, m2):
    bits = lax.bitcast_convert_type(x, jnp.uint32) & jnp.uint32(0xFFFF0000)
    hi = lax.bitcast_convert_type(bits, F32)
    return _dot(jnp.concatenate([hi.astype(BF16), (x - hi).astype(BF16)], axis=1), m2)


def _stack_heads(x):
    return jnp.concatenate(_split_heads(x), axis=0)


def _block_mask(i, j, row0):
    row = lax.broadcasted_iota(jnp.int32, (AQ - row0, QB), 0) + (i * AQ + row0)
    col = lax.broadcasted_iota(jnp.int32, (AQ - row0, QB), 1) + j * QB
    return (col < row) & (col >= PAD)


def _key_block(ref, j):
    return ref[pl.ds(pl.multiple_of(j * QB, QB), QB), :]


def _split_heads(x):
    head_a = lax.broadcasted_iota(jnp.int32, x.shape, 1) < HEAD_LANES
    zero = jnp.zeros_like(x)
    return jnp.where(head_a, x, zero), jnp.where(head_a, zero, x)


def _attn_fwd(hin, ex_arrs, gathers):
    ne = len(ex_arrs)
    npair, nq = D // QB, LP // AQ

    def body(*refs):
        q_ref, k_ref, v_ref = refs[:3]
        o_ref, lt_ref = refs[3 + ne:5 + ne]
        c_sc, acc_sc = refs[5 + 2 * ne:7 + 2 * ne]
        p, i = pl.program_id(0), pl.program_id(1)

        def copies():
            return _exchange_copies(refs[3:3 + ne], refs[5 + ne:5 + 2 * ne], gathers, *refs[7 + 2 * ne:])

        @pl.when((p == 0) & (i == 0))
        def _():
            for cp in copies():
                cp.start()

        um = _cumsum_matrix(False, True)
        um2 = jnp.concatenate([um, um], axis=0)
        q = (q_ref[...].astype(F32) * SCALE).astype(BF16)
        c_sc[...] = jnp.zeros_like(c_sc)
        acc_sc[...] = jnp.zeros_like(acc_sc)

        def step(j, masked, row0=0):
            rows = slice(row0, AQ)
            z2 = _dot(q[rows], _stack_heads(_key_block(k_ref, j)), NT)
            mask = _block_mask(i, j, row0) if masked else None
            a2 = []
            for hd in range(2):
                z = z2[:, hd * QB:(hd + 1) * QB]
                sp = _softplus(z)
                r = _split_dot(jnp.where(mask, sp, 0.0) if masked else sp, um2)
                a = jnp.exp(z - sp - c_sc[hd, rows] - r[:, :QB])
                if masked:
                    a = jnp.where(mask, a, 0.0)
                a2.append(a.astype(BF16))
                c_sc[hd, rows] += r[:, QB:]
            acc_sc[rows] += _dot(jnp.concatenate(a2, axis=1), _stack_heads(_key_block(v_ref, j)))

        for t in reversed(range(KPQ)):
            step(KPQ * i + t, True, t * QB)

        inner = jnp.maximum(KPQ * i - 1, 0)

        @pl.loop(0, inner // 2)
        def _(t):
            step(KPQ * i - 1 - 2 * t, False)
            step(KPQ * i - 2 - 2 * t, False)

        @pl.when(inner % 2 == 1)
        def _():
            step(1, False)

        @pl.when(i > 0)
        def _():
            step(0, True)

        head_a = lax.broadcasted_iota(jnp.int32, (AQ, QB), 1) < HEAD_LANES
        o_ref[...] = acc_sc[...].astype(BF16)
        lt_ref[...] = jnp.where(head_a, c_sc[0], c_sc[1])

        @pl.when((p == npair - 1) & (i == nq - 1))
        def _():
            for cp in copies():
                cp.wait()

    def seq(s):
        return pl.BlockSpec((None, LP, QB), lambda p, i: (s, 0, p))

    return pl.pallas_call(
        body,
        name="attn_fwd",
        grid=(npair, nq),
        in_specs=[pl.BlockSpec((None, AQ, QB), lambda p, i: (3, i, p)), seq(4), seq(5)] + [ANY_SPEC] * ne,
        out_specs=[pl.BlockSpec((AQ, QB), lambda p, i: (i, p))] * 2 + [ANY_SPEC] * ne,
        out_shape=[jax.ShapeDtypeStruct((LP, D), BF16), jax.ShapeDtypeStruct((LP, D), F32)]
        + _exchange_shapes(ex_arrs, gathers),
        scratch_shapes=[pltpu.VMEM((2, AQ, QB), F32), pltpu.VMEM((AQ, QB), F32)] + _exchange_sems(ne),
        compiler_params=_params(("arbitrary", "arbitrary")),
    )(hin, hin, hin, *ex_arrs)


def _attn_bwd(hin, do, lt, dhin, ex_arrs, gathers):
    ne = len(ex_arrs)
    npair, nq = D // QB, LP // AQ

    def body(*refs):
        q_ref, k_ref, v_ref, do_ref, lt_ref = refs[:5]
        out_ref = refs[6 + ne]
        psp_sc, pg_sc, dq_sc, dk_acc, dv_acc = refs[7 + 2 * ne:12 + 2 * ne]
        p, i = pl.program_id(0), pl.program_id(1)

        def copies():
            return _exchange_copies(refs[6:6 + ne], refs[7 + ne:7 + 2 * ne], gathers, *refs[12 + 2 * ne:])

        @pl.when((p == 0) & (i == 0))
        def _():
            for cp in copies():
                cp.start()

        @pl.when(i == 0)
        def _():
            dk_acc[...] = jnp.zeros_like(dk_acc)
            dv_acc[...] = jnp.zeros_like(dv_acc)

        um_sp = _cumsum_matrix(True, False)
        um_sp2 = jnp.concatenate([um_sp, um_sp], axis=0)
        um_g = _cumsum_matrix(False, False)
        q = (q_ref[...].astype(F32) * SCALE).astype(BF16)
        do = do_ref[...]
        head_a = lax.broadcasted_iota(jnp.int32, (AQ, QB), 1) < HEAD_LANES
        key_head_a = lax.broadcasted_iota(jnp.int32, (QB, QB), 1) < HEAD_LANES
        lt = lt_ref[...]
        lt_sw = pltpu.roll(lt, HEAD_LANES, 1)
        totals = (jnp.where(head_a, lt, lt_sw), jnp.where(head_a, lt_sw, lt))
        psp_sc[...] = jnp.zeros_like(psp_sc)
        pg_sc[...] = jnp.zeros_like(pg_sc)
        dq_sc[...] = jnp.zeros_like(dq_sc)

        def step(j, masked, row0=0):
            rows = slice(row0, AQ)
            k2 = _stack_heads(_key_block(k_ref, j))
            z2 = _dot(q[rows], k2, NT)
            da2 = _dot(do[rows], _stack_heads(_key_block(v_ref, j)), NT)
            mask = _block_mask(i, j, row0) if masked else None
            a2, dz2 = [], []
            for hd in range(2):
                z = z2[:, hd * QB:(hd + 1) * QB]
                sp = _softplus(z)
                r = _split_dot(jnp.where(mask, sp, 0.0) if masked else sp, um_sp2)
                a = jnp.exp(z - sp - (totals[hd][rows] - psp_sc[hd, rows] - r[:, :QB]))
                if masked:
                    a = jnp.where(mask, a, 0.0)
                g = a * da2[:, hd * QB:(hd + 1) * QB]
                rg = _dot(g.astype(BF16), um_g)
                dz = g - jnp.exp(z - sp) * (g + pg_sc[hd, rows] + rg[:, :QB])
                if masked:
                    dz = jnp.where(mask, dz, 0.0)
                a2.append(a.astype(BF16))
                dz2.append(dz.astype(BF16))
                psp_sc[hd, rows] += r[:, QB:]
                pg_sc[hd, rows] += rg[:, QB:]
            dz2 = jnp.concatenate(dz2, axis=1)
            dq_sc[rows] += _dot(dz2, k2)
            dk2 = _dot(dz2, q[rows], TN)
            dv2 = _dot(jnp.concatenate(a2, axis=1), do[rows], TN)
            keys = pl.ds(pl.multiple_of(j * QB, QB), QB)
            dk_acc[keys, :] += jnp.where(key_head_a, dk2[:QB], dk2[QB:])
            dv_acc[keys, :] += jnp.where(key_head_a, dv2[:QB], dv2[QB:])

        @pl.when(i > 0)
        def _():
            step(0, True)

        inner = jnp.maximum(KPQ * i - 1, 0)

        @pl.loop(0, inner // 2)
        def _(t):
            step(2 * t + 1, False)
            step(2 * t + 2, False)

        @pl.when(inner % 2 == 1)
        def _():
            step(inner, False)

        for t in range(KPQ):
            step(KPQ * i + t, True, t * QB)

        out_ref[0, pl.ds(pl.multiple_of(i * AQ, AQ), AQ), :] = (dq_sc[...] * SCALE).astype(BF16)

        @pl.when(i == nq - 1)
        def _():
            out_ref[1] = dk_acc[...].astype(BF16)
            out_ref[2] = dv_acc[...].astype(BF16)

        @pl.when((p == npair - 1) & (i == nq - 1))
        def _():
            for cp in copies():
                cp.wait()

    def seq(s):
        return pl.BlockSpec((None, LP, QB), lambda p, i: (s, 0, p))

    blk = pl.BlockSpec((AQ, QB), lambda p, i: (i, p))
    return pl.pallas_call(
        body,
        name="attn_bwd",
        grid=(npair, nq),
        in_specs=[pl.BlockSpec((None, AQ, QB), lambda p, i: (3, i, p)), seq(4), seq(5), blk, blk]
        + [ANY_SPEC] * (1 + ne),
        out_specs=[pl.BlockSpec((3, LP, QB), lambda p, i: (1, 0, p))] + [ANY_SPEC] * ne,
        out_shape=[jax.ShapeDtypeStruct((N_DEV, LP, D), BF16)] + _exchange_shapes(ex_arrs, gathers),
        input_output_aliases={5: 0},
        scratch_shapes=[pltpu.VMEM((2, AQ, QB), F32)] * 2 + [pltpu.VMEM((AQ, QB), F32)] + [pltpu.VMEM((LP, QB), F32)] * 2
        + _exchange_sems(ne),
        compiler_params=_params(("arbitrary", "arbitrary")),
    )(hin, hin, hin, do, lt, dhin, *ex_arrs)


def _adamw_math(w, g, m, v):
    m_new = ADAM_B1 * m + (1.0 - ADAM_B1) * g
    v_new = ADAM_B2 * v + (1.0 - ADAM_B2) * jnp.square(g)
    m_hat = m_new / (1.0 - ADAM_B1 ** ADAM_STEP)
    v_hat = v_new / (1.0 - ADAM_B2 ** ADAM_STEP)
    return -ADAM_LR * (m_hat / (jnp.sqrt(v_hat) + ADAM_EPS) + ADAM_WD * w), m_new, v_new


def _adamw_small(ws, gs, ms, vs):
    n = len(ws)

    def body(*refs):
        for t in range(n):
            w_ref, g_ref, m_ref, v_ref = (refs[s * n + t] for s in range(4))
            d_ref, nm_ref, nv_ref = (refs[(4 + s) * n + t] for s in range(3))
            d_ref[...], nm_ref[...], nv_ref[...] = _adamw_math(w_ref[...], g_ref[...], m_ref[...], v_ref[...])

    vmem = pl.BlockSpec(memory_space=pltpu.VMEM)
    res = pl.pallas_call(
        body,
        name="adamw_small",
        in_specs=[vmem] * (4 * n),
        out_specs=[vmem] * (3 * n),
        out_shape=[jax.ShapeDtypeStruct(w.shape, F32) for w in ws] * 3,
    )(*ws, *gs, *ms, *vs)
    return res[:n], res[n:2 * n], res[2 * n:]


def _adamw(pieces, w, m, v, *, rows, row_off, tr, name):
    npieces, _, cols = pieces.shape
    ob = row_off // tr

    def body(p_ref, w_ref, m_ref, v_ref, g_ref, d_ref, nm_ref, nv_ref):
        g = p_ref[0].astype(F32)
        for s in range(1, npieces):
            g = g + p_ref[s].astype(F32)
        g_ref[...] = g
        d_ref[...], nm_ref[...], nv_ref[...] = _adamw_math(w_ref[...], g, m_ref[...], v_ref[...])

    spec = pl.BlockSpec((tr, cols), lambda i: (i, 0))
    return pl.pallas_call(
        body,
        name=name,
        grid=(rows // tr,),
        in_specs=[pl.BlockSpec((npieces, tr, cols), lambda i: (0, ob + i, 0)), spec, spec, spec],
        out_specs=[spec] * 4,
        out_shape=[jax.ShapeDtypeStruct((rows, cols), F32)] * 4,
        compiler_params=_params(("parallel",)),
    )(pieces, w, m, v)


def _sum_pieces(pieces, name):
    npieces, rows, cols = pieces.shape

    def body(p_ref, o_ref):
        acc = p_ref[0]
        for s in range(1, npieces):
            acc = acc + p_ref[s]
        o_ref[...] = acc

    return pl.pallas_call(
        body,
        name=name,
        in_specs=[pl.BlockSpec(memory_space=pltpu.VMEM)],
        out_specs=pl.BlockSpec(memory_space=pltpu.VMEM),
        out_shape=jax.ShapeDtypeStruct((rows, cols), pieces.dtype),
    )(pieces)


SMALL_ROWS = 48
CWF_PAD = 384
GRAD_ROWS = 72
G_META, G_PRE_MIX, G_MID, G_POST_FFN, G_CW_MIX, G_B_GATE, G_CW_FFN = 0, 16, 24, 32, 40, 48, 56


def kernel(x, meta_tokens, g_pre_mix, w_in, conv_w_mix, w_proj_conv, w_proj_attn, b_gate, w_out, g_post_mix, g_pre_ffn, w_up_gate, conv_w_ffn, w_down, g_post_ffn, loss_target, m_meta_tokens, m_g_pre_mix, m_w_in, m_conv_w_mix, m_w_proj_conv, m_w_proj_attn, m_b_gate, m_w_out, m_g_post_mix, m_g_pre_ffn, m_w_up_gate, m_conv_w_ffn, m_w_down, m_g_post_ffn, v_meta_tokens, v_g_pre_mix, v_w_in, v_conv_w_mix, v_w_proj_conv, v_w_proj_attn, v_b_gate, v_w_out, v_g_post_mix, v_g_pre_ffn, v_w_up_gate, v_conv_w_ffn, v_w_down, v_g_post_ffn):
    me = 4 * lax.axis_index("x") + 2 * lax.axis_index("y") + lax.axis_index("c")

    def rows_to(a, n):
        return jnp.pad(a, ((0, n - a.shape[0]), (0, 0)))

    small_shard = jnp.concatenate(
        [meta_tokens, rows_to(conv_w_mix[0], 8), rows_to(b_gate[0], 8),
         rows_to(jnp.pad(conv_w_ffn[0], ((0, 0), (0, CWF_PAD - R_DOWN))).reshape(9, 128), 16)], axis=0)
    wshard = jnp.concatenate([w_proj_conv[0], w_proj_attn[0], w_out[0], w_down[0]], axis=0).astype(BF16)
    w_in_all, small_all = _gather_two_level([w_in[0].astype(BF16), small_shard], "gather_in")

    def unshard(rows):
        return rows.transpose(1, 0, 2).reshape(rows.shape[1], N_DEV * rows.shape[2])

    meta = unshard(small_all[:, 0:16])
    cw_mix = unshard(small_all[:, 16:19])
    bgate = unshard(small_all[:, 24:26])
    cw_ffn = unshard(small_all[:, 32:41].reshape(N_DEV, 3, CWF_PAD)[:, :, :R_DOWN])
    cw4 = cw_ffn.reshape(3, N_FB, FB).transpose(1, 0, 2)
    h0 = jnp.concatenate([jnp.zeros((PAD, D), F32), meta, x[0]], axis=0)

    xn1 = _rms_fwd(h0, g_pre_mix, "rms_pre_mix")
    hin = _mm(xn1, w_in_all, w_rows=D, trans_w=False, out_dtype=BF16, name="mm_in")
    y_conv = _conv_mix_fwd(hin, cw_mix)
    o, lt, wpack, w_ug = _attn_fwd(hin, [wshard, w_up_gate[0].astype(BF16)], (True, True))
    w_pc = wpack[:, O_PC:O_PA].reshape(1, D, D)
    w_pa = wpack[:, O_PA:O_OUT].reshape(1, D, D)
    w_o = wpack[:, O_OUT:O_DOWN].reshape(1, D, D)
    w_dn = wpack[:, O_DOWN:].reshape(N_FB, FB, D)
    bc = _mm(y_conv, w_pc, w_rows=D, trans_w=False, out_dtype=BF16, name="mm_proj_conv")[0]
    ba = _mm(o, w_pa, w_rows=D, trans_w=False, out_dtype=BF16, name="mm_proj_attn")[0]
    merged = _gate_fwd(bc, ba, hin, bgate)
    mix = _mm(merged, w_o, w_rows=D, trans_w=False, out_dtype=F32, name="mm_out")[0]
    h1, xn2 = _resid_rms(h0, mix, g_post_mix, g_pre_ffn)
    ug = _mm(xn2, w_ug, w_rows=D, trans_w=False, out_dtype=BF16, name="mm_up_gate")
    hid = _ffn_act_fwd(ug, cw4)
    ffn = _mm_sum(hid, w_dn, w_rows=FB, trans_w=False, out_dtype=F32, name="mm_down")
    dout, dffn, loss8, dg_post_ffn = _loss_head(h1, ffn, g_post_ffn, loss_target[0])

    dhid = _mm(dffn, w_dn, w_rows=FB, trans_w=True, out_dtype=BF16, name="mm_down_dx")
    gw_dn = _mm_tn(hid, dffn, nb=N_FB, out_dtype=BF16, name="mm_down_dw")
    dug, dcw4 = _ffn_act_bwd(ug, dhid, cw4)
    dug = dug.reshape(2 * N_FB, LP, FB)
    dxn2 = _mm_sum(dug, w_ug, w_rows=D, trans_w=True, out_dtype=F32, name="mm_up_gate_dx")
    gw_ug = _mm_tn(xn2, dug, nb=N_DEV, out_dtype=BF16, name="mm_up_gate_dw")
    dh1, dmix, dg_mid = _mid_bwd(dout, h1, dxn2, mix, g_post_mix, g_pre_ffn)
    dmerged = _mm(dmix, w_o, w_rows=D, trans_w=True, out_dtype=BF16, name="mm_out_dx")[0]
    gw_out = _mm_tn(merged, dmix, nb=1, out_dtype=BF16, name="mm_out_dw")
    dbc, dba, dhin, db_gate = _gate_bwd(dmerged, bc, ba, hin, bgate)
    dy_conv = _mm(dbc, w_pc, w_rows=D, trans_w=True, out_dtype=BF16, name="mm_proj_conv_dx")[0]
    gw_pc = _mm_tn(y_conv, dbc, nb=1, out_dtype=BF16, name="mm_proj_conv_dw")
    do = _mm(dba, w_pa, w_rows=D, trans_w=True, out_dtype=BF16, name="mm_proj_attn_dx")[0]
    gw_pa = _mm_tn(o, dba, nb=1, out_dtype=BF16, name="mm_proj_attn_dw")
    dhin, dcw_mix = _conv_mix_bwd(hin, dy_conv, cw_mix, dhin)
    gpack = jnp.concatenate(
        [gw_pc.reshape(N_DEV, R_PROJ, D), gw_pa.reshape(N_DEV, R_PROJ, D), gw_out.reshape(N_DEV, R_PROJ, D),
         gw_dn.reshape(N_DEV, R_DOWN, D)], axis=1)
    dhin, rpack, rug = _attn_bwd(hin, do, lt, dhin, [gpack, gw_ug], (False, False))
    gw_in = _mm_tn(xn1, dhin, nb=N_DEV, out_dtype=BF16, name="mm_in_dw")
    dcw_ffn = dcw4[:, :3].transpose(1, 0, 2).reshape(3, D_FF)
    small_a = jnp.concatenate(
        [dg_mid, dg_post_ffn, dcw_mix, db_gate,
         jnp.pad(dcw_ffn.reshape(-1), (0, 16 * D - 3 * D_FF)).reshape(16, D)], axis=0)
    dxn1, rin, rsmall_a = _mm_sum(dhin, w_in_all, w_rows=D, trans_w=True, out_dtype=F32, name="mm_in_dx",
                                  exchange=([gw_in, small_a], (False, True)))
    dh0, dg_pre_mix = _first_bwd(dh1, h0, dxn1, g_pre_mix)
    (rsmall_b,) = _exchange([jnp.concatenate([dh0[PAD:OFF], dg_pre_mix], axis=0)], (True,), "gather_last_grads")
    gs = _sum_pieces(jnp.concatenate([rsmall_b, rsmall_a], axis=1), "sum_small_grads")
    grad_x = dh0[OFF:]
    loss = lax.psum(loss8[0, 0], ("x", "y", "c"))

    def cols(a, width):
        return lax.dynamic_slice_in_dim(a, me * width, width, axis=1)

    g_meta = cols(gs[G_META:G_META + N_META], 128)
    g_gpm, g_gff = gs[G_PRE_MIX:G_PRE_MIX + 1], gs[G_POST_FFN:G_POST_FFN + 1]
    g_gpo, g_gpf = gs[G_MID:G_MID + 1], gs[G_MID + 1:G_MID + 2]
    g_cwm = cols(gs[G_CW_MIX:G_CW_MIX + 3], 128)[None]
    g_bg = cols(gs[G_B_GATE:G_B_GATE + 2], 128)[None]
    g_cwf = cols(gs[G_CW_FFN:G_CW_FFN + 9].reshape(-1)[:3 * D_FF].reshape(3, D_FF), R_DOWN)[None]

    def big(pieces, w, m, v, rows, row_off, tr, name):
        g, d, nm, nv = _adamw(pieces, w[0], m[0], v[0], rows=rows, row_off=row_off, tr=tr, name=name)
        return g[None], d[None], nm[None], nv[None]

    r_in = big(rin, w_in, m_w_in, v_w_in, D, 0, 256, "adamw_in")
    r_pc = big(rpack, w_proj_conv, m_w_proj_conv, v_w_proj_conv, R_PROJ, O_PC, R_PROJ, "adamw_proj_conv")
    r_pa = big(rpack, w_proj_attn, m_w_proj_attn, v_w_proj_attn, R_PROJ, O_PA, R_PROJ, "adamw_proj_attn")
    r_out = big(rpack, w_out, m_w_out, v_w_out, R_PROJ, O_OUT, R_PROJ, "adamw_out")
    r_dn = big(rpack, w_down, m_w_down, v_w_down, R_DOWN, O_DOWN, 32, "adamw_down")
    r_ug = big(rug, w_up_gate, m_w_up_gate, v_w_up_gate, D, 0, 256, "adamw_up_gate")

    small_w = [meta_tokens, g_pre_mix, conv_w_mix, b_gate, g_post_mix, g_pre_ffn, conv_w_ffn, g_post_ffn]
    small_g = [g_meta, g_gpm, g_cwm, g_bg, g_gpo, g_gpf, g_cwf, g_gff]
    small_m = [m_meta_tokens, m_g_pre_mix, m_conv_w_mix, m_b_gate, m_g_post_mix, m_g_pre_ffn, m_conv_w_ffn, m_g_post_ffn]
    small_v = [v_meta_tokens, v_g_pre_mix, v_conv_w_mix, v_b_gate, v_g_post_mix, v_g_pre_ffn, v_conv_w_ffn, v_g_post_ffn]

    s_g = small_g
    s_d, s_m, s_v = _adamw_small(small_w, small_g, small_m, small_v)

    def ordered(k, smalls):
        meta, gpm, cwm, bg, gpo, gpf, cwf, gff = smalls
        return [meta, gpm, r_in[k], cwm, r_pc[k], r_pa[k], bg, r_out[k], gpo, gpf, r_ug[k], cwf, r_dn[k], gff]

    return (loss, grad_x[None], *ordered(0, s_g), *ordered(1, s_d), *ordered(2, s_m), *ordered(3, s_v))
```

```python
import functools
import math

import jax
import jax.numpy as jnp
from jax import lax
from jax.experimental import pallas as pl
from jax.experimental.pallas import tpu as pltpu

F32 = jnp.float32
BF16 = jnp.bfloat16

D = 1024
SEQ = 4096
N_META = 16
PAD = 112
OFF = PAD + N_META
LP = OFF + SEQ
QB = 128
AQ = 384
KPQ = AQ // QB
TM = 384
MM_TM = 1408
HALO = 16
N_DEV = 8
D_FF = 2816
FB = 704
N_FB = D_FF // FB
RMS_EPS = 1e-6
SCALE = 0.125
HEAD_LANES = 64
VMEM_LIMIT = 56 * 1024 * 1024

ADAM_LR = 0.001
ADAM_B1 = 0.9
ADAM_B2 = 0.999
ADAM_EPS = 1e-08
ADAM_WD = 0.01
ADAM_STEP = 10

R_PROJ, R_DOWN = 128, 352
O_PC = 0
O_PA = O_PC + R_PROJ
O_OUT = O_PA + R_PROJ
O_DOWN = O_OUT + R_PROJ
R_PACK = O_DOWN + R_DOWN

NT = (((1,), (1,)), ((), ()))
NN = (((1,), (0,)), ((), ()))
TN = (((0,), (0,)), ((), ()))


def _params(sem):
    return pltpu.CompilerParams(dimension_semantics=sem, vmem_limit_bytes=VMEM_LIMIT)


def _dot(a, b, dn=NN):
    return lax.dot_general(a, b, dn, preferred_element_type=F32)


def _exchange_copies(ins, outs, gathers, send_sems, recv_sems, loc_sems):
    x, y, c = lax.axis_index("x"), lax.axis_index("y"), lax.axis_index("c")
    me = 4 * x + 2 * y + c
    copies = []
    for a, gather in enumerate(gathers):
        copies.append(pltpu.make_async_copy(ins[a] if gather else ins[a].at[me], outs[a].at[me], loc_sems.at[a]))
    for k in range(1, N_DEV):
        px = 1 - x if k & 4 else x
        py = 1 - y if k & 2 else y
        pc = 1 - c if k & 1 else c
        peer = 4 * px + 2 * py + pc
        for a, gather in enumerate(gathers):
            copies.append(pltpu.make_async_remote_copy(
                src_ref=ins[a] if gather else ins[a].at[peer],
                dst_ref=outs[a].at[me],
                send_sem=send_sems.at[a * (N_DEV - 1) + k - 1],
                recv_sem=recv_sems.at[a * (N_DEV - 1) + k - 1],
                device_id=(px, py, pc),
                device_id_type=pl.DeviceIdType.MESH,
            ))
    return copies


def _exchange_shapes(arrs, gathers):
    return [jax.ShapeDtypeStruct((N_DEV,) + (a.shape if g else a.shape[1:]), a.dtype) for a, g in zip(arrs, gathers)]


def _exchange_sems(n):
    return [pltpu.SemaphoreType.DMA((n * (N_DEV - 1),)), pltpu.SemaphoreType.DMA((n * (N_DEV - 1),)),
            pltpu.SemaphoreType.DMA((n,))]


ANY_SPEC = pl.BlockSpec(memory_space=pl.ANY)


def _gather_two_level(arrs, name):
    n = len(arrs)
    per = 7

    def body(*refs):
        ins, outs = refs[:n], refs[n:2 * n]
        send_sems, recv_sems, loc_sems = refs[2 * n:]
        x, y, c = lax.axis_index("x"), lax.axis_index("y"), lax.axis_index("c")
        me, sibling = (x, y, c), (x, y, 1 - c)
        chips = [(1 - x, y), (x, 1 - y), (1 - x, 1 - y)]

        def copy(a, k, block, to, src=None):
            place = outs[a].at[4 * block[0] + 2 * block[1] + block[2]]
            return pltpu.make_async_remote_copy(
                src_ref=place if src is None else src, dst_ref=place,
                send_sem=send_sems.at[a * per + k], recv_sem=recv_sems.at[a * per + k],
                device_id=to, device_id_type=pl.DeviceIdType.MESH)

        mine = [pltpu.make_async_copy(ins[a], outs[a].at[4 * x + 2 * y + c], loc_sems.at[a]) for a in range(n)]
        first = [copy(a, 0, me, sibling, src=ins[a]) for a in range(n)]
        first += [copy(a, 1 + j, me, (*chip, c), src=ins[a]) for j, chip in enumerate(chips) for a in range(n)]
        for cp in mine + first:
            cp.start()
        passed = []
        for j, chip in enumerate(chips):
            for a in range(n):
                copy(a, 1 + j, (*chip, c), me).wait_recv()
                passed.append(copy(a, 4 + j, (*chip, c), sibling))
                passed[-1].start()
        for a in range(n):
            copy(a, 0, sibling, me).wait_recv()
        for j, chip in enumerate(chips):
            for a in range(n):
                copy(a, 4 + j, (*chip, 1 - c), me).wait_recv()
        for cp in first + passed:
            cp.wait_send()
        for cp in mine:
            cp.wait()

    return pl.pallas_call(
        body,
        name=name,
        out_shape=_exchange_shapes(arrs, (True,) * n),
        in_specs=[ANY_SPEC] * n,
        out_specs=[ANY_SPEC] * n,
        scratch_shapes=_exchange_sems(n),
    )(*arrs)


HBM_SPEC = pl.BlockSpec(memory_space=pltpu.HBM)
SEM_SPEC = pl.BlockSpec(memory_space=pltpu.SEMAPHORE)
DATAFLOW = pltpu.SideEffectType.DATAFLOW_SIDE_EFFECTING


def _scatter_copies(g_ref, land_ref, send_sems, recv_sems):
    x, y, c = lax.axis_index("x"), lax.axis_index("y"), lax.axis_index("c")
    me = 4 * x + 2 * y + c
    copies = []
    for k in range(1, N_DEV):
        px = 1 - x if k & 4 else x
        py = 1 - y if k & 2 else y
        pc = 1 - c if k & 1 else c
        copies.append(pltpu.make_async_remote_copy(
            src_ref=g_ref.at[4 * px + 2 * py + pc], dst_ref=land_ref.at[me],
            send_sem=send_sems.at[k - 1], recv_sem=recv_sems.at[k - 1],
            device_id=(px, py, pc), device_id_type=pl.DeviceIdType.MESH))
    return copies


def _scatter_start(g):
    def body(g_ref, land_ref, send_sems, recv_sems, g_thru, land_thru, token):
        for cp in _scatter_copies(g_ref, land_ref, send_sems, recv_sems):
            cp.start()
        token[...] = jnp.zeros_like(token)

    return pl.pallas_call(
        body,
        name="scatter_in_start",
        out_shape=(pltpu.SemaphoreType.DMA((N_DEV - 1,)), pltpu.SemaphoreType.DMA((N_DEV - 1,)),
                   pltpu.HBM(g.shape, g.dtype), pltpu.HBM(g.shape, g.dtype), jax.ShapeDtypeStruct((8, 128), F32)),
        in_specs=(HBM_SPEC, HBM_SPEC),
        out_specs=(SEM_SPEC, SEM_SPEC, HBM_SPEC, HBM_SPEC, pl.BlockSpec(memory_space=pltpu.VMEM)),
        input_output_aliases={0: 2, 1: 3},
        compiler_params=pltpu.CompilerParams(has_side_effects=DATAFLOW),
    )(pltpu.with_memory_space_constraint(g, pltpu.HBM),
      pltpu.with_memory_space_constraint(lax.empty(g.shape, g.dtype), pltpu.HBM))


def _scatter_wait(send_sems, recv_sems, g_thru, land_thru, after):
    def body(g_ref, land_ref, send_sems, recv_sems, *_):
        for cp in _scatter_copies(g_ref, land_ref, send_sems, recv_sems):
            cp.wait_send()
            cp.wait_recv()

    return pl.pallas_call(
        body,
        name="scatter_in_wait",
        out_shape=(pltpu.HBM(g_thru.shape, g_thru.dtype), pltpu.HBM(g_thru.shape, g_thru.dtype)),
        in_specs=(HBM_SPEC, HBM_SPEC, SEM_SPEC, SEM_SPEC) + (ANY_SPEC,) * len(after),
        out_specs=(HBM_SPEC, HBM_SPEC),
        input_output_aliases={0: 0, 1: 1},
        compiler_params=pltpu.CompilerParams(has_side_effects=DATAFLOW),
    )(g_thru, land_thru, send_sems, recv_sems, *after)


def _exchange(arrs, gathers, name):
    n = len(arrs)

    def body(*refs):
        copies = _exchange_copies(refs[:n], refs[n:2 * n], gathers, *refs[2 * n:])
        for cp in copies:
            cp.start()
        for cp in copies:
            cp.wait()

    return pl.pallas_call(
        body,
        name=name,
        out_shape=_exchange_shapes(arrs, gathers),
        in_specs=[ANY_SPEC] * n,
        out_specs=[ANY_SPEC] * n,
        scratch_shapes=_exchange_sems(n),
    )(*arrs)


def _mm(a, w, *, w_rows, trans_w, out_dtype, name):
    nb, _, wc = w.shape
    m, k = a.shape[-2:]
    n = w_rows if trans_w else wc
    dn = NT if trans_w else NN

    def body(a_ref, w_ref, o_ref):
        o_ref[...] = _dot(a_ref[...], w_ref[...], dn).astype(out_dtype)

    if a.ndim == 2:
        a_spec = pl.BlockSpec((MM_TM, k), lambda j, i: (i, 0))
    else:
        a_spec = pl.BlockSpec((None, MM_TM, k), lambda j, i: (j, i, 0))
    return pl.pallas_call(
        body,
        name=name,
        grid=(nb, m // MM_TM),
        in_specs=[a_spec, pl.BlockSpec((None, w_rows, wc), lambda j, i: (j, 0, 0))],
        out_specs=pl.BlockSpec((None, MM_TM, n), lambda j, i: (j, i, 0)),
        out_shape=jax.ShapeDtypeStruct((nb, m, n), out_dtype),
        compiler_params=_params(("parallel", "parallel")),
    )(a, w)


def _mm_sum(a, w, *, w_rows, trans_w, out_dtype, name, after=()):
    nb, m, k = a.shape
    wc = w.shape[2]
    n = w_rows if trans_w else wc
    dn = NT if trans_w else NN
    after = tuple(after) if isinstance(after, (tuple, list)) else (after,)
    na = len(after)

    def body(*refs):
        a_ref, w_ref, o_ref, acc_ref = refs[0], refs[1], refs[2 + na], refs[3 + na]
        j = pl.program_id(1)

        @pl.when(j == 0)
        def _():
            acc_ref[...] = jnp.zeros_like(acc_ref)

        acc_ref[...] += _dot(a_ref[...], w_ref[...], dn)

        @pl.when(j == nb - 1)
        def _():
            o_ref[...] = acc_ref[...].astype(out_dtype)

    return pl.pallas_call(
        body,
        name=name,
        grid=(m // MM_TM, nb),
        in_specs=[
            pl.BlockSpec((None, MM_TM, k), lambda i, j: (j, i, 0)),
            pl.BlockSpec((None, w_rows, wc), lambda i, j: (j, 0, 0)),
        ] + [ANY_SPEC] * na,
        out_specs=pl.BlockSpec((MM_TM, n), lambda i, j: (i, 0)),
        out_shape=jax.ShapeDtypeStruct((m, n), out_dtype),
        scratch_shapes=[pltpu.VMEM((MM_TM, n), F32)],
        compiler_params=_params(("parallel", "arbitrary")),
    )(a, w, *after)


def _mm_tn(a, b, *, nb, out_dtype, name):
    m, ka = a.shape[-2:]
    n = b.shape[-1]
    steps = m // MM_TM

    def body(a_ref, b_ref, o_ref, acc_ref):
        i = pl.program_id(1)

        @pl.when(i == 0)
        def _():
            acc_ref[...] = jnp.zeros_like(acc_ref)

        acc_ref[...] += _dot(a_ref[...], b_ref[...], TN)

        @pl.when(i == steps - 1)
        def _():
            o_ref[...] = acc_ref[...].astype(out_dtype)

    def spec(arr, cols):
        if arr.ndim == 2:
            return pl.BlockSpec((MM_TM, cols), lambda j, i: (i, 0))
        return pl.BlockSpec((None, MM_TM, cols), lambda j, i: (j, i, 0))

    return pl.pallas_call(
        body,
        name=name,
        grid=(nb, steps),
        in_specs=[spec(a, ka), spec(b, n)],
        out_specs=pl.BlockSpec((None, ka, n), lambda j, i: (j, 0, 0)),
        out_shape=jax.ShapeDtypeStruct((nb, ka, n), out_dtype),
        scratch_shapes=[pltpu.VMEM((ka, n), F32)],
        compiler_params=_params(("parallel", "arbitrary")),
    )(a, b)


def _rstd(x):
    return lax.rsqrt(jnp.mean(x * x, axis=-1, keepdims=True) + RMS_EPS)


def _rms_bwd(x, g, dy):
    r = _rstd(x)
    u = dy * g
    dx = r * u - x * (r * r * r) * jnp.mean(u * x, axis=-1, keepdims=True)
    return dx, dy * x * r


def _row_spec(cols=D, tm=TM):
    return pl.BlockSpec((tm, cols), lambda i: (i, 0))


def _vec_spec(rows=1, cols=D):
    return pl.BlockSpec((rows, cols), lambda i: (0, 0))


def _rms_fwd(x, g, name):
    def body(x_ref, g_ref, o_ref):
        x = x_ref[...]
        o_ref[...] = (x * _rstd(x) * g_ref[...]).astype(BF16)

    return pl.pallas_call(
        body,
        name=name,
        grid=(LP // TM,),
        in_specs=[_row_spec(), _vec_spec()],
        out_specs=_row_spec(),
        out_shape=jax.ShapeDtypeStruct((LP, D), BF16),
        compiler_params=_params(("parallel",)),
    )(x, g)


def _resid_rms(h0, mix, g_post, g_next):
    def body(h0_ref, mix_ref, gp_ref, gn_ref, h1_ref, xn_ref):
        mix = mix_ref[...]
        h1 = h0_ref[...] + mix * _rstd(mix) * gp_ref[...]
        h1_ref[...] = h1
        xn_ref[...] = (h1 * _rstd(h1) * gn_ref[...]).astype(BF16)

    return pl.pallas_call(
        body,
        name="resid_rms",
        grid=(LP // TM,),
        in_specs=[_row_spec(), _row_spec(), _vec_spec(), _vec_spec()],
        out_specs=[_row_spec(), _row_spec()],
        out_shape=[jax.ShapeDtypeStruct((LP, D), F32), jax.ShapeDtypeStruct((LP, D), BF16)],
        compiler_params=_params(("parallel",)),
    )(h0, mix, g_post, g_next)


def _loss_head(h1, ffn, g_post, target):
    nblk = LP // QB

    def body(h1_ref, ffn_ref, g_ref, t_ref, dout_ref, dffn_ref, loss_ref, dg_ref):
        i = pl.program_id(0)

        @pl.when(i == 0)
        def _():
            loss_ref[...] = jnp.zeros_like(loss_ref)
            dg_ref[...] = jnp.zeros_like(dg_ref)

        ffn = ffn_ref[...]
        g = g_ref[...]
        out = h1_ref[...] + ffn * _rstd(ffn) * g
        err = jnp.where(i > 0, out - t_ref[...], 0.0)
        loss_ref[...] += 0.5 * jnp.sum(err * err) / D
        dout = err / D
        dout_ref[...] = dout
        dffn, dg = _rms_bwd(ffn, g, dout)
        dffn_ref[...] = dffn.astype(BF16)
        dg_ref[0:1, :] += jnp.sum(dg, axis=0, keepdims=True)

    return pl.pallas_call(
        body,
        name="loss_head",
        grid=(nblk,),
        in_specs=[
            _row_spec(tm=QB),
            _row_spec(tm=QB),
            _vec_spec(),
            pl.BlockSpec((QB, D), lambda i: (jnp.maximum(i - 1, 0), 0)),
        ],
        out_specs=[_row_spec(tm=QB), _row_spec(tm=QB), _vec_spec(8, 128), _vec_spec(8, D)],
        out_shape=[
            jax.ShapeDtypeStruct((LP, D), F32),
            jax.ShapeDtypeStruct((LP, D), BF16),
            jax.ShapeDtypeStruct((8, 128), F32),
            jax.ShapeDtypeStruct((8, D), F32),
        ],
        compiler_params=_params(("arbitrary",)),
    )(h1, ffn, g_post, target)


def _mid_bwd(dout, h1, dxn2, mix, g_post_mix, g_pre_ffn):
    def body(dout_ref, h1_ref, dxn_ref, mix_ref, gpm_ref, gpf_ref, dh1_ref, dmix_ref, dg_ref):
        i = pl.program_id(0)

        @pl.when(i == 0)
        def _():
            dg_ref[...] = jnp.zeros_like(dg_ref)

        dx, dg_ffn = _rms_bwd(h1_ref[...], gpf_ref[...], dxn_ref[...])
        dh1 = dout_ref[...] + dx
        dh1_ref[...] = dh1
        dmix, dg_mix = _rms_bwd(mix_ref[...], gpm_ref[...], dh1)
        dmix_ref[...] = dmix.astype(BF16)
        dg_ref[0:1, :] += jnp.sum(dg_mix, axis=0, keepdims=True)
        dg_ref[1:2, :] += jnp.sum(dg_ffn, axis=0, keepdims=True)

    return pl.pallas_call(
        body,
        name="mid_bwd",
        grid=(LP // TM,),
        in_specs=[_row_spec(), _row_spec(), _row_spec(), _row_spec(), _vec_spec(), _vec_spec()],
        out_specs=[_row_spec(), _row_spec(), _vec_spec(8, D)],
        out_shape=[
            jax.ShapeDtypeStruct((LP, D), F32),
            jax.ShapeDtypeStruct((LP, D), BF16),
            jax.ShapeDtypeStruct((8, D), F32),
        ],
        compiler_params=_params(("arbitrary",)),
    )(dout, h1, dxn2, mix, g_post_mix, g_pre_ffn)


def _first_bwd(dh1, h0, dxn1, g_pre_mix):
    def body(dh1_ref, h0_ref, dxn_ref, g_ref, dh0_ref, dg_ref):
        i = pl.program_id(0)

        @pl.when(i == 0)
        def _():
            dg_ref[...] = jnp.zeros_like(dg_ref)

        dx, dg = _rms_bwd(h0_ref[...], g_ref[...], dxn_ref[...])
        dh0_ref[...] = dh1_ref[...] + dx
        dg_ref[0:1, :] += jnp.sum(dg, axis=0, keepdims=True)

    return pl.pallas_call(
        body,
        name="first_bwd",
        grid=(LP // TM,),
        in_specs=[_row_spec(), _row_spec(), _row_spec(), _vec_spec()],
        out_specs=[_row_spec(), _vec_spec(8, D)],
        out_shape=[jax.ShapeDtypeStruct((LP, D), F32), jax.ShapeDtypeStruct((8, D), F32)],
        compiler_params=_params(("arbitrary",)),
    )(dh1, h0, dxn1, g_pre_mix)


def _prev_halo(i):
    return jnp.maximum(i * (TM // HALO) - 1, 0)


def _next_halo(i):
    return jnp.minimum((i + 1) * (TM // HALO), LP // HALO - 1)


def _down(x, s):
    return pltpu.roll(x, s, 0)


def _up(x, s):
    return pltpu.roll(x, x.shape[0] - s, 0)


def _conv_mix_fwd(hin, cw):
    def body(b_ref, c_ref, h_ref, cp_ref, hp_ref, w_ref, y_ref):
        i = pl.program_id(0)
        p = c_ref[...].astype(F32) * h_ref[...].astype(F32)
        pp = jnp.where(i > 0, cp_ref[...].astype(F32) * hp_ref[...].astype(F32), 0.0)
        ext = jnp.concatenate([pp, p], axis=0)
        w = [w_ref[t:t + 1, :] for t in range(3)]
        cv = w[2] * ext + w[1] * _down(ext, 1) + w[0] * _down(ext, 2)
        y_ref[...] = (b_ref[...].astype(F32) * cv[HALO:]).astype(BF16)

    def tile(s):
        return pl.BlockSpec((None, TM, D), lambda i: (s, i, 0))

    def prev(s):
        return pl.BlockSpec((None, HALO, D), lambda i: (s, _prev_halo(i), 0))

    return pl.pallas_call(
        body,
        name="conv_mix_fwd",
        grid=(LP // TM,),
        in_specs=[tile(0), tile(1), tile(2), prev(1), prev(2), _vec_spec(3, D)],
        out_specs=_row_spec(),
        out_shape=jax.ShapeDtypeStruct((LP, D), BF16),
        compiler_params=_params(("parallel",)),
    )(hin, hin, hin, hin, hin, cw)


def _conv_mix_bwd(hin, dy, cw, dhin):
    last = LP // TM - 1

    def body(b_ref, c_ref, h_ref, dy_ref, cp_ref, hp_ref, bn_ref, dyn_ref, w_ref, _, out_ref, dw_ref):
        i = pl.program_id(0)

        @pl.when(i == 0)
        def _():
            dw_ref[...] = jnp.zeros_like(dw_ref)

        b = b_ref[...].astype(F32)
        c = c_ref[...].astype(F32)
        h = h_ref[...].astype(F32)
        dy = dy_ref[...].astype(F32)
        w = [w_ref[t:t + 1, :] for t in range(3)]
        p = c * h
        pp = jnp.where(i > 0, cp_ref[...].astype(F32) * hp_ref[...].astype(F32), 0.0)
        ext = jnp.concatenate([pp, p], axis=0)
        p1 = _down(ext, 1)[HALO:]
        p2 = _down(ext, 2)[HALO:]
        cv = w[2] * p + w[1] * p1 + w[0] * p2
        out_ref[0] = (dy * cv).astype(BF16)
        dcv = dy * b
        dcvn = jnp.where(i < last, dyn_ref[...].astype(F32) * bn_ref[...].astype(F32), 0.0)
        dext = jnp.concatenate([dcv, dcvn], axis=0)
        dp = (w[2] * dext + w[1] * _up(dext, 1) + w[0] * _up(dext, 2))[:TM]
        out_ref[1] = (dp * h).astype(BF16)
        out_ref[2] = (dp * c).astype(BF16)
        dw_ref[0:1, :] += jnp.sum(dcv * p2, axis=0, keepdims=True)
        dw_ref[1:2, :] += jnp.sum(dcv * p1, axis=0, keepdims=True)
        dw_ref[2:3, :] += jnp.sum(dcv * p, axis=0, keepdims=True)

    def tile(s):
        return pl.BlockSpec((None, TM, D), lambda i: (s, i, 0))

    def prev(s):
        return pl.BlockSpec((None, HALO, D), lambda i: (s, _prev_halo(i), 0))

    return pl.pallas_call(
        body,
        name="conv_mix_bwd",
        grid=(LP // TM,),
        in_specs=[
            tile(0), tile(1), tile(2), _row_spec(),
            prev(1), prev(2),
            pl.BlockSpec((None, HALO, D), lambda i: (0, _next_halo(i), 0)),
            pl.BlockSpec((HALO, D), lambda i: (_next_halo(i), 0)),
            _vec_spec(3, D),
            pl.BlockSpec(memory_space=pl.ANY),
        ],
        out_specs=[pl.BlockSpec((3, TM, D), lambda i: (0, i, 0)), _vec_spec(8, D)],
        out_shape=[jax.ShapeDtypeStruct((N_DEV, LP, D), BF16), jax.ShapeDtypeStruct((8, D), F32)],
        input_output_aliases={9: 0},
        compiler_params=_params(("arbitrary",)),
    )(hin, hin, hin, dy, hin, hin, hin, dy, cw, dhin)


GELU_K = math.sqrt(2.0 / math.pi)
GELU_A = 0.044715


def _gelu_and_grad(x):
    x2 = x * x
    t = jnp.tanh(GELU_K * (x + GELU_A * x2 * x))
    gelu = 0.5 * x * (1.0 + t)
    grad = 0.5 * (1.0 + t) + 0.5 * x * (1.0 - t * t) * GELU_K * (1.0 + 3.0 * GELU_A * x2)
    return gelu, grad


def _ffn_act_fwd(ug, cw4):
    def body(u_ref, g_ref, up_ref, w_ref, o_ref):
        i = pl.program_id(1)
        u = u_ref[...].astype(F32)
        up = jnp.where(i > 0, up_ref[...].astype(F32), 0.0)
        ext = jnp.concatenate([up, u], axis=0)
        w = [w_ref[t:t + 1, :] for t in range(3)]
        uc = (w[2] * ext + w[1] * _down(ext, 1) + w[0] * _down(ext, 2))[HALO:]
        gelu, _ = _gelu_and_grad(uc)
        o_ref[...] = (gelu * g_ref[...].astype(F32)).astype(BF16)

    return pl.pallas_call(
        body,
        name="ffn_act_fwd",
        grid=(N_FB, LP // TM),
        in_specs=[
            pl.BlockSpec((None, TM, FB), lambda j, i: (j, i, 0)),
            pl.BlockSpec((None, TM, FB), lambda j, i: (j + N_FB, i, 0)),
            pl.BlockSpec((None, HALO, FB), lambda j, i: (j, _prev_halo(i), 0)),
            pl.BlockSpec((None, 3, FB), lambda j, i: (j, 0, 0)),
        ],
        out_specs=pl.BlockSpec((None, TM, FB), lambda j, i: (j, i, 0)),
        out_shape=jax.ShapeDtypeStruct((N_FB, LP, FB), BF16),
        compiler_params=_params(("parallel", "parallel")),
    )(ug, ug, ug, cw4)


def _ffn_act_bwd(ug, dhid, cw4):
    last = LP // TM - 1
    n = TM + 2 * HALO

    def body(u_ref, g_ref, dh_ref, up_ref, un_ref, gn_ref, dhn_ref, w_ref, dug_ref, dw_ref):
        i = pl.program_id(1)

        @pl.when(i == 0)
        def _():
            dw_ref[...] = jnp.zeros_like(dw_ref)

        w = [w_ref[t:t + 1, :] for t in range(3)]
        u = u_ref[...].astype(F32)
        up = jnp.where(i > 0, up_ref[...].astype(F32), 0.0)
        ext = jnp.concatenate([up, u, un_ref[...].astype(F32)], axis=0)
        u1 = _down(ext, 1)
        u2 = _down(ext, 2)
        uc = w[2] * ext + w[1] * u1 + w[0] * u2
        gelu, ggrad = _gelu_and_grad(uc)
        zeros = jnp.zeros((HALO, FB), F32)
        gext = jnp.concatenate([zeros, g_ref[...].astype(F32), gn_ref[...].astype(F32)], axis=0)
        dhn = jnp.where(i < last, dhn_ref[...].astype(F32), 0.0)
        dhext = jnp.concatenate([zeros, dh_ref[...].astype(F32), dhn], axis=0)
        dug_ref[1] = (dhext * gelu)[HALO:HALO + TM].astype(BF16)
        duc = dhext * gext * ggrad
        du = w[2] * duc + w[1] * _up(duc, 1) + w[0] * _up(duc, 2)
        dug_ref[0] = du[HALO:HALO + TM].astype(BF16)
        row = lax.broadcasted_iota(jnp.int32, (n, 1), 0)
        own = jnp.where((row >= HALO) & (row < HALO + TM), duc, 0.0)
        dw_ref[0:1, :] += jnp.sum(own * u2, axis=0, keepdims=True)
        dw_ref[1:2, :] += jnp.sum(own * u1, axis=0, keepdims=True)
        dw_ref[2:3, :] += jnp.sum(own * ext, axis=0, keepdims=True)

    def tile(off):
        return pl.BlockSpec((None, TM, FB), lambda j, i: (j + off, i, 0))

    def nxt(off):
        return pl.BlockSpec((None, HALO, FB), lambda j, i: (j + off, _next_halo(i), 0))

    return pl.pallas_call(
        body,
        name="ffn_act_bwd",
        grid=(N_FB, LP // TM),
        in_specs=[
            tile(0), tile(N_FB), tile(0),
            pl.BlockSpec((None, HALO, FB), lambda j, i: (j, _prev_halo(i), 0)),
            nxt(0), nxt(N_FB), nxt(0),
            pl.BlockSpec((None, 3, FB), lambda j, i: (j, 0, 0)),
        ],
        out_specs=[
            pl.BlockSpec((2, None, TM, FB), lambda j, i: (0, j, i, 0)),
            pl.BlockSpec((None, 8, FB), lambda j, i: (j, 0, 0)),
        ],
        out_shape=[jax.ShapeDtypeStruct((2, N_FB, LP, FB), BF16), jax.ShapeDtypeStruct((N_FB, 8, FB), F32)],
        compiler_params=_params(("parallel", "arbitrary")),
    )(ug, ug, dhid, ug, ug, ug, dhid, cw4)


def _gate_fwd(bc, ba, hin, bgate):
    def body(bc_ref, ba_ref, gc_ref, ga_ref, b_ref, o_ref):
        b = b_ref[...]
        sc = jax.nn.sigmoid(gc_ref[...].astype(F32) + b[0:1])
        sa = jax.nn.sigmoid(ga_ref[...].astype(F32) + b[1:2])
        o_ref[...] = (sc * bc_ref[...].astype(F32) + sa * ba_ref[...].astype(F32)).astype(BF16)

    def tile(s):
        return pl.BlockSpec((None, TM, D), lambda i: (s, i, 0))

    return pl.pallas_call(
        body,
        name="gate_fwd",
        grid=(LP // TM,),
        in_specs=[_row_spec(), _row_spec(), tile(6), tile(7), _vec_spec(2, D)],
        out_specs=_row_spec(),
        out_shape=jax.ShapeDtypeStruct((LP, D), BF16),
        compiler_params=_params(("parallel",)),
    )(bc, ba, hin, hin, bgate)


def _gate_bwd(dm, bc, ba, hin, bgate):
    def body(dm_ref, bc_ref, ba_ref, gc_ref, ga_ref, b_ref, dbc_ref, dba_ref, dg_ref, db_ref):
        i = pl.program_id(0)

        @pl.when(i == 0)
        def _():
            db_ref[...] = jnp.zeros_like(db_ref)

        b = b_ref[...]
        dm = dm_ref[...].astype(F32)
        sc = jax.nn.sigmoid(gc_ref[...].astype(F32) + b[0:1])
        sa = jax.nn.sigmoid(ga_ref[...].astype(F32) + b[1:2])
        dbc_ref[...] = (dm * sc).astype(BF16)
        dba_ref[...] = (dm * sa).astype(BF16)
        dgc = dm * bc_ref[...].astype(F32) * sc * (1.0 - sc)
        dga = dm * ba_ref[...].astype(F32) * sa * (1.0 - sa)
        dg_ref[0] = dgc.astype(BF16)
        dg_ref[1] = dga.astype(BF16)
        db_ref[0:1, :] += jnp.sum(dgc, axis=0, keepdims=True)
        db_ref[1:2, :] += jnp.sum(dga, axis=0, keepdims=True)

    def tile(s):
        return pl.BlockSpec((None, TM, D), lambda i: (s, i, 0))

    return pl.pallas_call(
        body,
        name="gate_bwd",
        grid=(LP // TM,),
        in_specs=[_row_spec(), _row_spec(), _row_spec(), tile(6), tile(7), _vec_spec(2, D)],
        out_specs=[_row_spec(), _row_spec(), pl.BlockSpec((2, TM, D), lambda i: (3, i, 0)), _vec_spec(8, D)],
        out_shape=[jax.ShapeDtypeStruct((LP, D), BF16)] * 2
        + [jax.ShapeDtypeStruct((N_DEV, LP, D), BF16), jax.ShapeDtypeStruct((8, D), F32)],
        compiler_params=_params(("arbitrary",)),
    )(dm, bc, ba, hin, hin, bgate)


Z_LINEAR = 30.0


def _softplus(z):
    return jnp.maximum(z, jnp.log(1.0 + jnp.exp(jnp.minimum(z, Z_LINEAR))))


def _cumsum_matrix(inclusive, reverse):
    r = lax.broadcasted_iota(jnp.int32, (QB, 2 * QB), 0)
    c = lax.broadcasted_iota(jnp.int32, (QB, 2 * QB), 1)
    if reverse:
        tri = r > c
    elif inclusive:
        tri = r <= c
    else:
        tri = r < c
    return jnp.where((c >= QB) | tri, 1.0, 0.0).astype(BF16)


def _split_dot(x, m2):
    bits = lax.bitcast_convert_type(x, jnp.uint32) & jnp.uint32(0xFFFF0000)
    hi = lax.bitcast_convert_type(bits, F32)
    return _dot(jnp.concatenate([hi.astype(BF16), (x - hi).astype(BF16)], axis=1), m2)


def _stack_heads(x):
    return jnp.concatenate(_split_heads(x), axis=0)


def _block_mask(i, j, row0):
    row = lax.broadcasted_iota(jnp.int32, (AQ - row0, QB), 0) + (i * AQ + row0)
    col = lax.broadcasted_iota(jnp.int32, (AQ - row0, QB), 1) + j * QB
    return (col < row) & (col >= PAD)


def _key_block(ref, j):
    return ref[pl.ds(pl.multiple_of(j * QB, QB), QB), :]


def _split_heads(x):
    head_a = lax.broadcasted_iota(jnp.int32, x.shape, 1) < HEAD_LANES
    zero = jnp.zeros_like(x)
    return jnp.where(head_a, x, zero), jnp.where(head_a, zero, x)


def _attn_fwd(hin, ex_arrs, gathers):
    ne = len(ex_arrs)
    npair, nq = D // QB, LP // AQ

    def body(*refs):
        q_ref, k_ref, v_ref = refs[:3]
        o_ref, lt_ref = refs[3 + ne:5 + ne]
        c_sc, acc_sc = refs[5 + 2 * ne:7 + 2 * ne]
        p, i = pl.program_id(0), pl.program_id(1)

        def copies():
            return _exchange_copies(refs[3:3 + ne], refs[5 + ne:5 + 2 * ne], gathers, *refs[7 + 2 * ne:])

        @pl.when((p == 0) & (i == 0))
        def _():
            for cp in copies():
                cp.start()

        um = _cumsum_matrix(False, True)
        um2 = jnp.concatenate([um, um], axis=0)
        q = (q_ref[...].astype(F32) * SCALE).astype(BF16)
        c_sc[...] = jnp.zeros_like(c_sc)
        acc_sc[...] = jnp.zeros_like(acc_sc)

        def step(j, masked, row0=0):
            rows = slice(row0, AQ)
            z2 = _dot(q[rows], _stack_heads(_key_block(k_ref, j)), NT)
            mask = _block_mask(i, j, row0) if masked else None
            a2 = []
            for hd in range(2):
                z = z2[:, hd * QB:(hd + 1) * QB]
                sp = _softplus(z)
                r = _split_dot(jnp.where(mask, sp, 0.0) if masked else sp, um2)
                a = jnp.exp(z - sp - c_sc[hd, rows] - r[:, :QB])
                if masked:
                    a = jnp.where(mask, a, 0.0)
                a2.append(a.astype(BF16))
                c_sc[hd, rows] += r[:, QB:]
            acc_sc[rows] += _dot(jnp.concatenate(a2, axis=1), _stack_heads(_key_block(v_ref, j)))

        for t in reversed(range(KPQ)):
            step(KPQ * i + t, True, t * QB)

        inner = jnp.maximum(KPQ * i - 1, 0)

        @pl.loop(0, inner // 2)
        def _(t):
            step(KPQ * i - 1 - 2 * t, False)
            step(KPQ * i - 2 - 2 * t, False)

        @pl.when(inner % 2 == 1)
        def _():
            step(1, False)

        @pl.when(i > 0)
        def _():
            step(0, True)

        head_a = lax.broadcasted_iota(jnp.int32, (AQ, QB), 1) < HEAD_LANES
        o_ref[...] = acc_sc[...].astype(BF16)
        lt_ref[...] = jnp.where(head_a, c_sc[0], c_sc[1])

        @pl.when((p == npair - 1) & (i == nq - 1))
        def _():
            for cp in copies():
                cp.wait()

    def seq(s):
        return pl.BlockSpec((None, LP, QB), lambda p, i: (s, 0, p))

    return pl.pallas_call(
        body,
        name="attn_fwd",
        grid=(npair, nq),
        in_specs=[pl.BlockSpec((None, AQ, QB), lambda p, i: (3, i, p)), seq(4), seq(5)] + [ANY_SPEC] * ne,
        out_specs=[pl.BlockSpec((AQ, QB), lambda p, i: (i, p))] * 2 + [ANY_SPEC] * ne,
        out_shape=[jax.ShapeDtypeStruct((LP, D), BF16), jax.ShapeDtypeStruct((LP, D), F32)]
        + _exchange_shapes(ex_arrs, gathers),
        scratch_shapes=[pltpu.VMEM((2, AQ, QB), F32), pltpu.VMEM((AQ, QB), F32)] + _exchange_sems(ne),
        compiler_params=_params(("arbitrary", "arbitrary")),
    )(hin, hin, hin, *ex_arrs)


def _attn_bwd(hin, do, lt, dhin, ex_arrs, gathers):
    ne = len(ex_arrs)
    npair, nq = D // QB, LP // AQ

    def body(*refs):
        q_ref, k_ref, v_ref, do_ref, lt_ref = refs[:5]
        out_ref = refs[6 + ne]
        psp_sc, pg_sc, dq_sc, dk_acc, dv_acc = refs[7 + 2 * ne:12 + 2 * ne]
        p, i = pl.program_id(0), pl.program_id(1)

        def copies():
            return _exchange_copies(refs[6:6 + ne], refs[7 + ne:7 + 2 * ne], gathers, *refs[12 + 2 * ne:])

        @pl.when((p == 0) & (i == 0))
        def _():
            for cp in copies():
                cp.start()

        @pl.when(i == 0)
        def _():
            dk_acc[...] = jnp.zeros_like(dk_acc)
            dv_acc[...] = jnp.zeros_like(dv_acc)

        um_sp = _cumsum_matrix(True, False)
        um_sp2 = jnp.concatenate([um_sp, um_sp], axis=0)
        um_g = _cumsum_matrix(False, False)
        q = (q_ref[...].astype(F32) * SCALE).astype(BF16)
        do = do_ref[...]
        head_a = lax.broadcasted_iota(jnp.int32, (AQ, QB), 1) < HEAD_LANES
        key_head_a = lax.broadcasted_iota(jnp.int32, (QB, QB), 1) < HEAD_LANES
        lt = lt_ref[...]
        lt_sw = pltpu.roll(lt, HEAD_LANES, 1)
        totals = (jnp.where(head_a, lt, lt_sw), jnp.where(head_a, lt_sw, lt))
        psp_sc[...] = jnp.zeros_like(psp_sc)
        pg_sc[...] = jnp.zeros_like(pg_sc)
        dq_sc[...] = jnp.zeros_like(dq_sc)

        def step(j, masked, row0=0):
            rows = slice(row0, AQ)
            k2 = _stack_heads(_key_block(k_ref, j))
            z2 = _dot(q[rows], k2, NT)
            da2 = _dot(do[rows], _stack_heads(_key_block(v_ref, j)), NT)
            mask = _block_mask(i, j, row0) if masked else None
            a2, dz2 = [], []
            for hd in range(2):
                z = z2[:, hd * QB:(hd + 1) * QB]
                sp = _softplus(z)
                r = _split_dot(jnp.where(mask, sp, 0.0) if masked else sp, um_sp2)
                a = jnp.exp(z - sp - (totals[hd][rows] - psp_sc[hd, rows] - r[:, :QB]))
                if masked:
                    a = jnp.where(mask, a, 0.0)
                g = a * da2[:, hd * QB:(hd + 1) * QB]
                rg = _dot(g.astype(BF16), um_g)
                dz = g - jnp.exp(z - sp) * (g + pg_sc[hd, rows] + rg[:, :QB])
                if masked:
                    dz = jnp.where(mask, dz, 0.0)
                a2.append(a.astype(BF16))
                dz2.append(dz.astype(BF16))
                psp_sc[hd, rows] += r[:, QB:]
                pg_sc[hd, rows] += rg[:, QB:]
            dz2 = jnp.concatenate(dz2, axis=1)
            dq_sc[rows] += _dot(dz2, k2)
            dk2 = _dot(dz2, q[rows], TN)
            dv2 = _dot(jnp.concatenate(a2, axis=1), do[rows], TN)
            keys = pl.ds(pl.multiple_of(j * QB, QB), QB)
            dk_acc[keys, :] += jnp.where(key_head_a, dk2[:QB], dk2[QB:])
            dv_acc[keys, :] += jnp.where(key_head_a, dv2[:QB], dv2[QB:])

        @pl.when(i > 0)
        def _():
            step(0, True)

        inner = jnp.maximum(KPQ * i - 1, 0)

        @pl.loop(0, inner // 2)
        def _(t):
            step(2 * t + 1, False)
            step(2 * t + 2, False)

        @pl.when(inner % 2 == 1)
        def _():
            step(inner, False)

        for t in range(KPQ):
            step(KPQ * i + t, True, t * QB)

        out_ref[0, pl.ds(pl.multiple_of(i * AQ, AQ), AQ), :] = (dq_sc[...] * SCALE).astype(BF16)

        @pl.when(i == nq - 1)
        def _():
            out_ref[1] = dk_acc[...].astype(BF16)
            out_ref[2] = dv_acc[...].astype(BF16)

        @pl.when((p == npair - 1) & (i == nq - 1))
        def _():
            for cp in copies():
                cp.wait()

    def seq(s):
        return pl.BlockSpec((None, LP, QB), lambda p, i: (s, 0, p))

    blk = pl.BlockSpec((AQ, QB), lambda p, i: (i, p))
    return pl.pallas_call(
        body,
        name="attn_bwd",
        grid=(npair, nq),
        in_specs=[pl.BlockSpec((None, AQ, QB), lambda p, i: (3, i, p)), seq(4), seq(5), blk, blk]
        + [ANY_SPEC] * (1 + ne),
        out_specs=[pl.BlockSpec((3, LP, QB), lambda p, i: (1, 0, p))] + [ANY_SPEC] * ne,
        out_shape=[jax.ShapeDtypeStruct((N_DEV, LP, D), BF16)] + _exchange_shapes(ex_arrs, gathers),
        input_output_aliases={5: 0},
        scratch_shapes=[pltpu.VMEM((2, AQ, QB), F32)] * 2 + [pltpu.VMEM((AQ, QB), F32)] + [pltpu.VMEM((LP, QB), F32)] * 2
        + _exchange_sems(ne),
        compiler_params=_params(("arbitrary", "arbitrary")),
    )(hin, hin, hin, do, lt, dhin, *ex_arrs)


def _adamw_math(w, g, m, v):
    m_new = ADAM_B1 * m + (1.0 - ADAM_B1) * g
    v_new = ADAM_B2 * v + (1.0 - ADAM_B2) * jnp.square(g)
    m_hat = m_new / (1.0 - ADAM_B1 ** ADAM_STEP)
    v_hat = v_new / (1.0 - ADAM_B2 ** ADAM_STEP)
    return -ADAM_LR * (m_hat / (jnp.sqrt(v_hat) + ADAM_EPS) + ADAM_WD * w), m_new, v_new


def _adamw_small(ws, gs, ms, vs):
    n = len(ws)

    def body(*refs):
        for t in range(n):
            w_ref, g_ref, m_ref, v_ref = (refs[s * n + t] for s in range(4))
            d_ref, nm_ref, nv_ref = (refs[(4 + s) * n + t] for s in range(3))
            d_ref[...], nm_ref[...], nv_ref[...] = _adamw_math(w_ref[...], g_ref[...], m_ref[...], v_ref[...])

    vmem = pl.BlockSpec(memory_space=pltpu.VMEM)
    res = pl.pallas_call(
        body,
        name="adamw_small",
        in_specs=[vmem] * (4 * n),
        out_specs=[vmem] * (3 * n),
        out_shape=[jax.ShapeDtypeStruct(w.shape, F32) for w in ws] * 3,
    )(*ws, *gs, *ms, *vs)
    return res[:n], res[n:2 * n], res[2 * n:]


def _adamw(pieces, w, m, v, *, rows, row_off, tr, name):
    npieces, _, cols = pieces.shape
    ob = row_off // tr

    def body(p_ref, w_ref, m_ref, v_ref, g_ref, d_ref, nm_ref, nv_ref):
        g = p_ref[0].astype(F32)
        for s in range(1, npieces):
            g = g + p_ref[s].astype(F32)
        g_ref[...] = g
        d_ref[...], nm_ref[...], nv_ref[...] = _adamw_math(w_ref[...], g, m_ref[...], v_ref[...])

    spec = pl.BlockSpec((tr, cols), lambda i: (i, 0))
    return pl.pallas_call(
        body,
        name=name,
        grid=(rows // tr,),
        in_specs=[pl.BlockSpec((npieces, tr, cols), lambda i: (0, ob + i, 0)), spec, spec, spec],
        out_specs=[spec] * 4,
        out_shape=[jax.ShapeDtypeStruct((rows, cols), F32)] * 4,
        compiler_params=_params(("parallel",)),
    )(pieces, w, m, v)


def _sum_pieces(pieces, name):
    npieces, rows, cols = pieces.shape

    def body(p_ref, o_ref):
        acc = p_ref[0]
        for s in range(1, npieces):
            acc = acc + p_ref[s]
        o_ref[...] = acc

    return pl.pallas_call(
        body,
        name=name,
        in_specs=[pl.BlockSpec(memory_space=pltpu.VMEM)],
        out_specs=pl.BlockSpec(memory_space=pltpu.VMEM),
        out_shape=jax.ShapeDtypeStruct((rows, cols), pieces.dtype),
    )(pieces)


SMALL_ROWS = 48
CWF_PAD = 384
GRAD_ROWS = 80
G_META, G_PRE_MIX, G_MID, G_POST_FFN, G_CW_MIX, G_B_GATE, G_CW_FFN, G_LOSS = 0, 16, 24, 32, 40, 48, 56, 72


def kernel(x, meta_tokens, g_pre_mix, w_in, conv_w_mix, w_proj_conv, w_proj_attn, b_gate, w_out, g_post_mix, g_pre_ffn, w_up_gate, conv_w_ffn, w_down, g_post_ffn, loss_target, m_meta_tokens, m_g_pre_mix, m_w_in, m_conv_w_mix, m_w_proj_conv, m_w_proj_attn, m_b_gate, m_w_out, m_g_post_mix, m_g_pre_ffn, m_w_up_gate, m_conv_w_ffn, m_w_down, m_g_post_ffn, v_meta_tokens, v_g_pre_mix, v_w_in, v_conv_w_mix, v_w_proj_conv, v_w_proj_attn, v_b_gate, v_w_out, v_g_post_mix, v_g_pre_ffn, v_w_up_gate, v_conv_w_ffn, v_w_down, v_g_post_ffn):
    me = 4 * lax.axis_index("x") + 2 * lax.axis_index("y") + lax.axis_index("c")

    def rows_to(a, n):
        return jnp.pad(a, ((0, n - a.shape[0]), (0, 0)))

    small_shard = jnp.concatenate(
        [meta_tokens, rows_to(conv_w_mix[0], 8), rows_to(b_gate[0], 8),
         rows_to(jnp.pad(conv_w_ffn[0], ((0, 0), (0, CWF_PAD - R_DOWN))).reshape(9, 128), 16)], axis=0)
    wshard = jnp.concatenate([w_proj_conv[0], w_proj_attn[0], w_out[0], w_down[0]], axis=0).astype(BF16)
    w_in_all, small_all = _gather_two_level([w_in[0].astype(BF16), small_shard], "gather_in")

    def unshard(rows):
        return rows.transpose(1, 0, 2).reshape(rows.shape[1], N_DEV * rows.shape[2])

    meta = unshard(small_all[:, 0:16])
    cw_mix = unshard(small_all[:, 16:19])
    bgate = unshard(small_all[:, 24:26])
    cw_ffn = unshard(small_all[:, 32:41].reshape(N_DEV, 3, CWF_PAD)[:, :, :R_DOWN])
    cw4 = cw_ffn.reshape(3, N_FB, FB).transpose(1, 0, 2)
    h0 = jnp.concatenate([jnp.zeros((PAD, D), F32), meta, x[0]], axis=0)

    xn1 = _rms_fwd(h0, g_pre_mix, "rms_pre_mix")
    hin = _mm(xn1, w_in_all, w_rows=D, trans_w=False, out_dtype=BF16, name="mm_in")
    y_conv = _conv_mix_fwd(hin, cw_mix)
    o, lt, wpack, w_ug = _attn_fwd(hin, [wshard, w_up_gate[0].astype(BF16)], (True, True))
    w_pc = wpack[:, O_PC:O_PA].reshape(1, D, D)
    w_pa = wpack[:, O_PA:O_OUT].reshape(1, D, D)
    w_o = wpack[:, O_OUT:O_DOWN].reshape(1, D, D)
    w_dn = wpack[:, O_DOWN:].reshape(N_FB, FB, D)
    bc = _mm(y_conv, w_pc, w_rows=D, trans_w=False, out_dtype=BF16, name="mm_proj_conv")[0]
    ba = _mm(o, w_pa, w_rows=D, trans_w=False, out_dtype=BF16, name="mm_proj_attn")[0]
    merged = _gate_fwd(bc, ba, hin, bgate)
    mix = _mm(merged, w_o, w_rows=D, trans_w=False, out_dtype=F32, name="mm_out")[0]
    h1, xn2 = _resid_rms(h0, mix, g_post_mix, g_pre_ffn)
    ug = _mm(xn2, w_ug, w_rows=D, trans_w=False, out_dtype=BF16, name="mm_up_gate")
    hid = _ffn_act_fwd(ug, cw4)
    ffn = _mm_sum(hid, w_dn, w_rows=FB, trans_w=False, out_dtype=F32, name="mm_down")
    dout, dffn, loss8, dg_post_ffn = _loss_head(h1, ffn, g_post_ffn, loss_target[0])

    dhid = _mm(dffn, w_dn, w_rows=FB, trans_w=True, out_dtype=BF16, name="mm_down_dx")
    gw_dn = _mm_tn(hid, dffn, nb=N_FB, out_dtype=BF16, name="mm_down_dw")
    dug, dcw4 = _ffn_act_bwd(ug, dhid, cw4)
    dug = dug.reshape(2 * N_FB, LP, FB)
    dxn2 = _mm_sum(dug, w_ug, w_rows=D, trans_w=True, out_dtype=F32, name="mm_up_gate_dx")
    gw_ug = _mm_tn(xn2, dug, nb=N_DEV, out_dtype=BF16, name="mm_up_gate_dw")
    dh1, dmix, dg_mid = _mid_bwd(dout, h1, dxn2, mix, g_post_mix, g_pre_ffn)
    dmerged = _mm(dmix, w_o, w_rows=D, trans_w=True, out_dtype=BF16, name="mm_out_dx")[0]
    gw_out = _mm_tn(merged, dmix, nb=1, out_dtype=BF16, name="mm_out_dw")
    dbc, dba, dhin, db_gate = _gate_bwd(dmerged, bc, ba, hin, bgate)
    dy_conv = _mm(dbc, w_pc, w_rows=D, trans_w=True, out_dtype=BF16, name="mm_proj_conv_dx")[0]
    gw_pc = _mm_tn(y_conv, dbc, nb=1, out_dtype=BF16, name="mm_proj_conv_dw")
    do = _mm(dba, w_pa, w_rows=D, trans_w=True, out_dtype=BF16, name="mm_proj_attn_dx")[0]
    gw_pa = _mm_tn(o, dba, nb=1, out_dtype=BF16, name="mm_proj_attn_dw")
    dhin, dcw_mix = _conv_mix_bwd(hin, dy_conv, cw_mix, dhin)
    gpack = jnp.concatenate(
        [gw_pc.reshape(N_DEV, R_PROJ, D), gw_pa.reshape(N_DEV, R_PROJ, D), gw_out.reshape(N_DEV, R_PROJ, D),
         gw_dn.reshape(N_DEV, R_DOWN, D)], axis=1)
    dhin, rpack, rug = _attn_bwd(hin, do, lt, dhin, [gpack, gw_ug], (False, False))
    gw_in = _mm_tn(xn1, dhin, nb=N_DEV, out_dtype=BF16, name="mm_in_dw")
    send_sems, recv_sems, gw_thru, land_thru, token = _scatter_start(gw_in)
    dxn1 = _mm_sum(dhin, w_in_all, w_rows=D, trans_w=True, out_dtype=F32, name="mm_in_dx", after=token)
    dh0, dg_pre_mix = _first_bwd(dh1, h0, dxn1, g_pre_mix)
    dcw_ffn = dcw4[:, :3].transpose(1, 0, 2).reshape(3, D_FF)
    small = jnp.concatenate(
        [dh0[PAD:OFF], dg_pre_mix, dg_mid, dg_post_ffn, dcw_mix, db_gate,
         jnp.pad(dcw_ffn.reshape(-1), (0, 16 * D - 3 * D_FF)).reshape(16, D),
         jnp.pad(loss8, ((0, 0), (0, D - 128)))], axis=0)
    (rsmall,) = _exchange([small], (True,), "gather_small_grads")
    gs = _sum_pieces(rsmall, "sum_small_grads")
    grad_x = dh0[OFF:]
    loss = gs[G_LOSS, 0]

    def cols(a, width):
        return lax.dynamic_slice_in_dim(a, me * width, width, axis=1)

    g_meta = cols(gs[G_META:G_META + N_META], 128)
    g_gpm, g_gff = gs[G_PRE_MIX:G_PRE_MIX + 1], gs[G_POST_FFN:G_POST_FFN + 1]
    g_gpo, g_gpf = gs[G_MID:G_MID + 1], gs[G_MID + 1:G_MID + 2]
    g_cwm = cols(gs[G_CW_MIX:G_CW_MIX + 3], 128)[None]
    g_bg = cols(gs[G_B_GATE:G_B_GATE + 2], 128)[None]
    g_cwf = cols(gs[G_CW_FFN:G_CW_FFN + 9].reshape(-1)[:3 * D_FF].reshape(3, D_FF), R_DOWN)[None]

    def big(pieces, w, m, v, rows, row_off, tr, name):
        g, d, nm, nv = _adamw(pieces, w[0], m[0], v[0], rows=rows, row_off=row_off, tr=tr, name=name)
        return g[None], d[None], nm[None], nv[None]

    r_pc = big(rpack, w_proj_conv, m_w_proj_conv, v_w_proj_conv, R_PROJ, O_PC, R_PROJ, "adamw_proj_conv")
    r_pa = big(rpack, w_proj_attn, m_w_proj_attn, v_w_proj_attn, R_PROJ, O_PA, R_PROJ, "adamw_proj_attn")
    r_out = big(rpack, w_out, m_w_out, v_w_out, R_PROJ, O_OUT, R_PROJ, "adamw_out")
    r_dn = big(rpack, w_down, m_w_down, v_w_down, R_DOWN, O_DOWN, 32, "adamw_down")
    r_ug = big(rug, w_up_gate, m_w_up_gate, v_w_up_gate, D, 0, 256, "adamw_up_gate")

    small_w = [meta_tokens, g_pre_mix, conv_w_mix, b_gate, g_post_mix, g_pre_ffn, conv_w_ffn, g_post_ffn]
    small_g = [g_meta, g_gpm, g_cwm, g_bg, g_gpo, g_gpf, g_cwf, g_gff]
    small_m = [m_meta_tokens, m_g_pre_mix, m_conv_w_mix, m_b_gate, m_g_post_mix, m_g_pre_ffn, m_conv_w_ffn, m_g_post_ffn]
    small_v = [v_meta_tokens, v_g_pre_mix, v_conv_w_mix, v_b_gate, v_g_post_mix, v_g_pre_ffn, v_conv_w_ffn, v_g_post_ffn]

    s_g = small_g
    s_d, s_m, s_v = _adamw_small(small_w, small_g, small_m, small_v)

    gw_done, landed = _scatter_wait(send_sems, recv_sems, gw_thru, land_thru, (r_ug[1], r_dn[1], s_d[0]))
    rin = lax.dynamic_update_index_in_dim(landed, lax.dynamic_index_in_dim(gw_done, me, 0, keepdims=False), me, 0)
    r_in = big(rin, w_in, m_w_in, v_w_in, D, 0, 256, "adamw_in")

    def ordered(k, smalls):
        meta, gpm, cwm, bg, gpo, gpf, cwf, gff = smalls
        return [meta, gpm, r_in[k], cwm, r_pc[k], r_pa[k], bg, r_out[k], gpo, gpf, r_ug[k], cwf, r_dn[k], gff]

    return (loss, grad_x[None], *ordered(0, s_g), *ordered(1, s_d), *ordered(2, s_m), *ordered(3, s_v))
```

```python
import functools
import math

import jax
import jax.numpy as jnp
from jax import lax
from jax.experimental import pallas as pl
from jax.experimental.pallas import tpu as pltpu

F32 = jnp.float32
BF16 = jnp.bfloat16

D = 1024
SEQ = 4096
N_META = 16
PAD = 112
OFF = PAD + N_META
LP = OFF + SEQ
QB = 128
AQ = 384
KPQ = AQ // QB
TM = 384
MM_TM = 1408
HALO = 16
N_DEV = 8
D_FF = 2816
FB = 704
N_FB = D_FF // FB
RMS_EPS = 1e-6
SCALE = 0.125
HEAD_LANES = 64
VMEM_LIMIT = 56 * 1024 * 1024

ADAM_LR = 0.001
ADAM_B1 = 0.9
ADAM_B2 = 0.999
ADAM_EPS = 1e-08
ADAM_WD = 0.01
ADAM_STEP = 10

R_PROJ, R_DOWN = 128, 352
O_PC = 0
O_PA = O_PC + R_PROJ
O_OUT = O_PA + R_PROJ
O_DOWN = O_OUT + R_PROJ
R_PACK = O_DOWN + R_DOWN

NT = (((1,), (1,)), ((), ()))
NN = (((1,), (0,)), ((), ()))
TN = (((0,), (0,)), ((), ()))


def _params(sem):
    return pltpu.CompilerParams(dimension_semantics=sem, vmem_limit_bytes=VMEM_LIMIT)


def _dot(a, b, dn=NN):
    return lax.dot_general(a, b, dn, preferred_element_type=F32)


def _exchange_copies(ins, outs, gathers, send_sems, recv_sems, loc_sems):
    x, y, c = lax.axis_index("x"), lax.axis_index("y"), lax.axis_index("c")
    me = 4 * x + 2 * y + c
    copies = []
    for a, gather in enumerate(gathers):
        copies.append(pltpu.make_async_copy(ins[a] if gather else ins[a].at[me], outs[a].at[me], loc_sems.at[a]))
    for k in range(1, N_DEV):
        px = 1 - x if k & 4 else x
        py = 1 - y if k & 2 else y
        pc = 1 - c if k & 1 else c
        peer = 4 * px + 2 * py + pc
        for a, gather in enumerate(gathers):
            copies.append(pltpu.make_async_remote_copy(
                src_ref=ins[a] if gather else ins[a].at[peer],
                dst_ref=outs[a].at[me],
                send_sem=send_sems.at[a * (N_DEV - 1) + k - 1],
                recv_sem=recv_sems.at[a * (N_DEV - 1) + k - 1],
                device_id=(px, py, pc),
                device_id_type=pl.DeviceIdType.MESH,
            ))
    return copies


def _exchange_shapes(arrs, gathers):
    return [jax.ShapeDtypeStruct((N_DEV,) + (a.shape if g else a.shape[1:]), a.dtype) for a, g in zip(arrs, gathers)]


def _exchange_sems(n):
    return [pltpu.SemaphoreType.DMA((n * (N_DEV - 1),)), pltpu.SemaphoreType.DMA((n * (N_DEV - 1),)),
            pltpu.SemaphoreType.DMA((n,))]


ANY_SPEC = pl.BlockSpec(memory_space=pl.ANY)


def _gather_two_level(arrs, name):
    n = len(arrs)
    per = 7

    def body(*refs):
        ins, outs = refs[:n], refs[n:2 * n]
        send_sems, recv_sems, loc_sems = refs[2 * n:]
        x, y, c = lax.axis_index("x"), lax.axis_index("y"), lax.axis_index("c")
        me, sibling = (x, y, c), (x, y, 1 - c)
        chips = [(1 - x, y), (x, 1 - y), (1 - x, 1 - y)]

        def copy(a, k, block, to, src=None):
            place = outs[a].at[4 * block[0] + 2 * block[1] + block[2]]
            return pltpu.make_async_remote_copy(
                src_ref=place if src is None else src, dst_ref=place,
                send_sem=send_sems.at[a * per + k], recv_sem=recv_sems.at[a * per + k],
                device_id=to, device_id_type=pl.DeviceIdType.MESH)

        mine = [pltpu.make_async_copy(ins[a], outs[a].at[4 * x + 2 * y + c], loc_sems.at[a]) for a in range(n)]
        first = [copy(a, 0, me, sibling, src=ins[a]) for a in range(n)]
        first += [copy(a, 1 + j, me, (*chip, c), src=ins[a]) for j, chip in enumerate(chips) for a in range(n)]
        for cp in mine + first:
            cp.start()
        passed = []
        for j, chip in enumerate(chips):
            for a in range(n):
                copy(a, 1 + j, (*chip, c), me).wait_recv()
                passed.append(copy(a, 4 + j, (*chip, c), sibling))
                passed[-1].start()
        for a in range(n):
            copy(a, 0, sibling, me).wait_recv()
        for j, chip in enumerate(chips):
            for a in range(n):
                copy(a, 4 + j, (*chip, 1 - c), me).wait_recv()
        for cp in first + passed:
            cp.wait_send()
        for cp in mine:
            cp.wait()

    return pl.pallas_call(
        body,
        name=name,
        out_shape=_exchange_shapes(arrs, (True,) * n),
        in_specs=[ANY_SPEC] * n,
        out_specs=[ANY_SPEC] * n,
        scratch_shapes=_exchange_sems(n),
    )(*arrs)


HBM_SPEC = pl.BlockSpec(memory_space=pltpu.HBM)
SEM_SPEC = pl.BlockSpec(memory_space=pltpu.SEMAPHORE)
DATAFLOW = pltpu.SideEffectType.DATAFLOW_SIDE_EFFECTING


def _scatter_copies(g_ref, land_ref, send_sems, recv_sems):
    x, y, c = lax.axis_index("x"), lax.axis_index("y"), lax.axis_index("c")
    me = 4 * x + 2 * y + c
    copies = []
    for k in range(1, N_DEV):
        px = 1 - x if k & 4 else x
        py = 1 - y if k & 2 else y
        pc = 1 - c if k & 1 else c
        copies.append(pltpu.make_async_remote_copy(
            src_ref=g_ref.at[4 * px + 2 * py + pc], dst_ref=land_ref.at[me],
            send_sem=send_sems.at[k - 1], recv_sem=recv_sems.at[k - 1],
            device_id=(px, py, pc), device_id_type=pl.DeviceIdType.MESH))
    return copies


def _scatter_start(g):
    def body(g_ref, land_ref, send_sems, recv_sems, g_thru, land_thru, token):
        for cp in _scatter_copies(g_ref, land_ref, send_sems, recv_sems):
            cp.start()
        token[...] = jnp.zeros_like(token)

    return pl.pallas_call(
        body,
        name="scatter_in_start",
        out_shape=(pltpu.SemaphoreType.DMA((N_DEV - 1,)), pltpu.SemaphoreType.DMA((N_DEV - 1,)),
                   pltpu.HBM(g.shape, g.dtype), pltpu.HBM(g.shape, g.dtype), jax.ShapeDtypeStruct((8, 128), F32)),
        in_specs=(HBM_SPEC, HBM_SPEC),
        out_specs=(SEM_SPEC, SEM_SPEC, HBM_SPEC, HBM_SPEC, pl.BlockSpec(memory_space=pltpu.VMEM)),
        input_output_aliases={0: 2, 1: 3},
        compiler_params=pltpu.CompilerParams(has_side_effects=DATAFLOW),
    )(pltpu.with_memory_space_constraint(g, pltpu.HBM),
      pltpu.with_memory_space_constraint(lax.empty(g.shape, g.dtype), pltpu.HBM))


def _scatter_wait(send_sems, recv_sems, g_thru, land_thru, after):
    def body(g_ref, land_ref, send_sems, recv_sems, *_):
        for cp in _scatter_copies(g_ref, land_ref, send_sems, recv_sems):
            cp.wait_send()
            cp.wait_recv()

    return pl.pallas_call(
        body,
        name="scatter_in_wait",
        out_shape=(pltpu.HBM(g_thru.shape, g_thru.dtype), pltpu.HBM(g_thru.shape, g_thru.dtype)),
        in_specs=(HBM_SPEC, HBM_SPEC, SEM_SPEC, SEM_SPEC) + (ANY_SPEC,) * len(after),
        out_specs=(HBM_SPEC, HBM_SPEC),
        input_output_aliases={0: 0, 1: 1},
        compiler_params=pltpu.CompilerParams(has_side_effects=DATAFLOW),
    )(g_thru, land_thru, send_sems, recv_sems, *after)


def _exchange(arrs, gathers, name, after=()):
    n, na = len(arrs), len(after)

    def body(*refs):
        copies = _exchange_copies(refs[:n], refs[n + na:2 * n + na], gathers, *refs[2 * n + na:])
        for cp in copies:
            cp.start()
        for cp in copies:
            cp.wait()

    return pl.pallas_call(
        body,
        name=name,
        out_shape=_exchange_shapes(arrs, gathers),
        in_specs=[ANY_SPEC] * (n + na),
        out_specs=[ANY_SPEC] * n,
        scratch_shapes=_exchange_sems(n),
    )(*arrs, *after)


def _mm(a, w, *, w_rows, trans_w, out_dtype, name):
    nb, _, wc = w.shape
    m, k = a.shape[-2:]
    n = w_rows if trans_w else wc
    dn = NT if trans_w else NN

    def body(a_ref, w_ref, o_ref):
        o_ref[...] = _dot(a_ref[...], w_ref[...], dn).astype(out_dtype)

    if a.ndim == 2:
        a_spec = pl.BlockSpec((MM_TM, k), lambda j, i: (i, 0))
    else:
        a_spec = pl.BlockSpec((None, MM_TM, k), lambda j, i: (j, i, 0))
    return pl.pallas_call(
        body,
        name=name,
        grid=(nb, m // MM_TM),
        in_specs=[a_spec, pl.BlockSpec((None, w_rows, wc), lambda j, i: (j, 0, 0))],
        out_specs=pl.BlockSpec((None, MM_TM, n), lambda j, i: (j, i, 0)),
        out_shape=jax.ShapeDtypeStruct((nb, m, n), out_dtype),
        compiler_params=_params(("parallel", "parallel")),
    )(a, w)


def _mm_sum(a, w, *, w_rows, trans_w, out_dtype, name, after=()):
    nb, m, k = a.shape
    wc = w.shape[2]
    n = w_rows if trans_w else wc
    dn = NT if trans_w else NN
    after = tuple(after) if isinstance(after, (tuple, list)) else (after,)
    na = len(after)

    def body(*refs):
        a_ref, w_ref, o_ref, acc_ref = refs[0], refs[1], refs[2 + na], refs[3 + na]
        j = pl.program_id(1)

        @pl.when(j == 0)
        def _():
            acc_ref[...] = jnp.zeros_like(acc_ref)

        acc_ref[...] += _dot(a_ref[...], w_ref[...], dn)

        @pl.when(j == nb - 1)
        def _():
            o_ref[...] = acc_ref[...].astype(out_dtype)

    return pl.pallas_call(
        body,
        name=name,
        grid=(m // MM_TM, nb),
        in_specs=[
            pl.BlockSpec((None, MM_TM, k), lambda i, j: (j, i, 0)),
            pl.BlockSpec((None, w_rows, wc), lambda i, j: (j, 0, 0)),
        ] + [ANY_SPEC] * na,
        out_specs=pl.BlockSpec((MM_TM, n), lambda i, j: (i, 0)),
        out_shape=jax.ShapeDtypeStruct((m, n), out_dtype),
        scratch_shapes=[pltpu.VMEM((MM_TM, n), F32)],
        compiler_params=_params(("parallel", "arbitrary")),
    )(a, w, *after)


def _mm_tn(a, b, *, nb, out_dtype, name):
    m, ka = a.shape[-2:]
    n = b.shape[-1]
    steps = m // MM_TM

    def body(a_ref, b_ref, o_ref, acc_ref):
        i = pl.program_id(1)

        @pl.when(i == 0)
        def _():
            acc_ref[...] = jnp.zeros_like(acc_ref)

        acc_ref[...] += _dot(a_ref[...], b_ref[...], TN)

        @pl.when(i == steps - 1)
        def _():
            o_ref[...] = acc_ref[...].astype(out_dtype)

    def spec(arr, cols):
        if arr.ndim == 2:
            return pl.BlockSpec((MM_TM, cols), lambda j, i: (i, 0))
        return pl.BlockSpec((None, MM_TM, cols), lambda j, i: (j, i, 0))

    return pl.pallas_call(
        body,
        name=name,
        grid=(nb, steps),
        in_specs=[spec(a, ka), spec(b, n)],
        out_specs=pl.BlockSpec((None, ka, n), lambda j, i: (j, 0, 0)),
        out_shape=jax.ShapeDtypeStruct((nb, ka, n), out_dtype),
        scratch_shapes=[pltpu.VMEM((ka, n), F32)],
        compiler_params=_params(("parallel", "arbitrary")),
    )(a, b)


def _rstd(x):
    return lax.rsqrt(jnp.mean(x * x, axis=-1, keepdims=True) + RMS_EPS)


def _rms_bwd(x, g, dy):
    r = _rstd(x)
    u = dy * g
    dx = r * u - x * (r * r * r) * jnp.mean(u * x, axis=-1, keepdims=True)
    return dx, dy * x * r


def _row_spec(cols=D, tm=TM):
    return pl.BlockSpec((tm, cols), lambda i: (i, 0))


def _vec_spec(rows=1, cols=D):
    return pl.BlockSpec((rows, cols), lambda i: (0, 0))


def _rms_fwd(x, g, name):
    def body(x_ref, g_ref, o_ref):
        x = x_ref[...]
        o_ref[...] = (x * _rstd(x) * g_ref[...]).astype(BF16)

    return pl.pallas_call(
        body,
        name=name,
        grid=(LP // TM,),
        in_specs=[_row_spec(), _vec_spec()],
        out_specs=_row_spec(),
        out_shape=jax.ShapeDtypeStruct((LP, D), BF16),
        compiler_params=_params(("parallel",)),
    )(x, g)


def _resid_rms(h0, mix, g_post, g_next):
    def body(h0_ref, mix_ref, gp_ref, gn_ref, h1_ref, xn_ref):
        mix = mix_ref[...]
        h1 = h0_ref[...] + mix * _rstd(mix) * gp_ref[...]
        h1_ref[...] = h1
        xn_ref[...] = (h1 * _rstd(h1) * gn_ref[...]).astype(BF16)

    return pl.pallas_call(
        body,
        name="resid_rms",
        grid=(LP // TM,),
        in_specs=[_row_spec(), _row_spec(), _vec_spec(), _vec_spec()],
        out_specs=[_row_spec(), _row_spec()],
        out_shape=[jax.ShapeDtypeStruct((LP, D), F32), jax.ShapeDtypeStruct((LP, D), BF16)],
        compiler_params=_params(("parallel",)),
    )(h0, mix, g_post, g_next)


def _loss_head(h1, ffn, g_post, target):
    nblk = LP // QB

    def body(h1_ref, ffn_ref, g_ref, t_ref, dout_ref, dffn_ref, loss_ref, dg_ref):
        i = pl.program_id(0)

        @pl.when(i == 0)
        def _():
            loss_ref[...] = jnp.zeros_like(loss_ref)
            dg_ref[...] = jnp.zeros_like(dg_ref)

        ffn = ffn_ref[...]
        g = g_ref[...]
        out = h1_ref[...] + ffn * _rstd(ffn) * g
        err = jnp.where(i > 0, out - t_ref[...], 0.0)
        loss_ref[...] += 0.5 * jnp.sum(err * err) / D
        dout = err / D
        dout_ref[...] = dout
        dffn, dg = _rms_bwd(ffn, g, dout)
        dffn_ref[...] = dffn.astype(BF16)
        dg_ref[0:1, :] += jnp.sum(dg, axis=0, keepdims=True)

    return pl.pallas_call(
        body,
        name="loss_head",
        grid=(nblk,),
        in_specs=[
            _row_spec(tm=QB),
            _row_spec(tm=QB),
            _vec_spec(),
            pl.BlockSpec((QB, D), lambda i: (jnp.maximum(i - 1, 0), 0)),
        ],
        out_specs=[_row_spec(tm=QB), _row_spec(tm=QB), _vec_spec(8, 128), _vec_spec(8, D)],
        out_shape=[
            jax.ShapeDtypeStruct((LP, D), F32),
            jax.ShapeDtypeStruct((LP, D), BF16),
            jax.ShapeDtypeStruct((8, 128), F32),
            jax.ShapeDtypeStruct((8, D), F32),
        ],
        compiler_params=_params(("arbitrary",)),
    )(h1, ffn, g_post, target)


def _mid_bwd(dout, h1, dxn2, mix, g_post_mix, g_pre_ffn):
    def body(dout_ref, h1_ref, dxn_ref, mix_ref, gpm_ref, gpf_ref, dh1_ref, dmix_ref, dg_ref):
        i = pl.program_id(0)

        @pl.when(i == 0)
        def _():
            dg_ref[...] = jnp.zeros_like(dg_ref)

        dx, dg_ffn = _rms_bwd(h1_ref[...], gpf_ref[...], dxn_ref[...])
        dh1 = dout_ref[...] + dx
        dh1_ref[...] = dh1
        dmix, dg_mix = _rms_bwd(mix_ref[...], gpm_ref[...], dh1)
        dmix_ref[...] = dmix.astype(BF16)
        dg_ref[0:1, :] += jnp.sum(dg_mix, axis=0, keepdims=True)
        dg_ref[1:2, :] += jnp.sum(dg_ffn, axis=0, keepdims=True)

    return pl.pallas_call(
        body,
        name="mid_bwd",
        grid=(LP // TM,),
        in_specs=[_row_spec(), _row_spec(), _row_spec(), _row_spec(), _vec_spec(), _vec_spec()],
        out_specs=[_row_spec(), _row_spec(), _vec_spec(8, D)],
        out_shape=[
            jax.ShapeDtypeStruct((LP, D), F32),
            jax.ShapeDtypeStruct((LP, D), BF16),
            jax.ShapeDtypeStruct((8, D), F32),
        ],
        compiler_params=_params(("arbitrary",)),
    )(dout, h1, dxn2, mix, g_post_mix, g_pre_ffn)


def _first_bwd(dh1, h0, dxn1, g_pre_mix):
    def body(dh1_ref, h0_ref, dxn_ref, g_ref, dh0_ref, dg_ref):
        i = pl.program_id(0)

        @pl.when(i == 0)
        def _():
            dg_ref[...] = jnp.zeros_like(dg_ref)

        dx, dg = _rms_bwd(h0_ref[...], g_ref[...], dxn_ref[...])
        dh0_ref[...] = dh1_ref[...] + dx
        dg_ref[0:1, :] += jnp.sum(dg, axis=0, keepdims=True)

    return pl.pallas_call(
        body,
        name="first_bwd",
        grid=(LP // TM,),
        in_specs=[_row_spec(), _row_spec(), _row_spec(), _vec_spec()],
        out_specs=[_row_spec(), _vec_spec(8, D)],
        out_shape=[jax.ShapeDtypeStruct((LP, D), F32), jax.ShapeDtypeStruct((8, D), F32)],
        compiler_params=_params(("arbitrary",)),
    )(dh1, h0, dxn1, g_pre_mix)


def _prev_halo(i):
    return jnp.maximum(i * (TM // HALO) - 1, 0)


def _next_halo(i):
    return jnp.minimum((i + 1) * (TM // HALO), LP // HALO - 1)


def _down(x, s):
    return pltpu.roll(x, s, 0)


def _up(x, s):
    return pltpu.roll(x, x.shape[0] - s, 0)


def _conv_mix_fwd(hin, cw):
    def body(b_ref, c_ref, h_ref, cp_ref, hp_ref, w_ref, y_ref):
        i = pl.program_id(0)
        p = c_ref[...].astype(F32) * h_ref[...].astype(F32)
        pp = jnp.where(i > 0, cp_ref[...].astype(F32) * hp_ref[...].astype(F32), 0.0)
        ext = jnp.concatenate([pp, p], axis=0)
        w = [w_ref[t:t + 1, :] for t in range(3)]
        cv = w[2] * ext + w[1] * _down(ext, 1) + w[0] * _down(ext, 2)
        y_ref[...] = (b_ref[...].astype(F32) * cv[HALO:]).astype(BF16)

    def tile(s):
        return pl.BlockSpec((None, TM, D), lambda i: (s, i, 0))

    def prev(s):
        return pl.BlockSpec((None, HALO, D), lambda i: (s, _prev_halo(i), 0))

    return pl.pallas_call(
        body,
        name="conv_mix_fwd",
        grid=(LP // TM,),
        in_specs=[tile(0), tile(1), tile(2), prev(1), prev(2), _vec_spec(3, D)],
        out_specs=_row_spec(),
        out_shape=jax.ShapeDtypeStruct((LP, D), BF16),
        compiler_params=_params(("parallel",)),
    )(hin, hin, hin, hin, hin, cw)


def _conv_mix_bwd(hin, dy, cw, dhin):
    last = LP // TM - 1

    def body(b_ref, c_ref, h_ref, dy_ref, cp_ref, hp_ref, bn_ref, dyn_ref, w_ref, _, out_ref, dw_ref):
        i = pl.program_id(0)

        @pl.when(i == 0)
        def _():
            dw_ref[...] = jnp.zeros_like(dw_ref)

        b = b_ref[...].astype(F32)
        c = c_ref[...].astype(F32)
        h = h_ref[...].astype(F32)
        dy = dy_ref[...].astype(F32)
        w = [w_ref[t:t + 1, :] for t in range(3)]
        p = c * h
        pp = jnp.where(i > 0, cp_ref[...].astype(F32) * hp_ref[...].astype(F32), 0.0)
        ext = jnp.concatenate([pp, p], axis=0)
        p1 = _down(ext, 1)[HALO:]
        p2 = _down(ext, 2)[HALO:]
        cv = w[2] * p + w[1] * p1 + w[0] * p2
        out_ref[0] = (dy * cv).astype(BF16)
        dcv = dy * b
        dcvn = jnp.where(i < last, dyn_ref[...].astype(F32) * bn_ref[...].astype(F32), 0.0)
        dext = jnp.concatenate([dcv, dcvn], axis=0)
        dp = (w[2] * dext + w[1] * _up(dext, 1) + w[0] * _up(dext, 2))[:TM]
        out_ref[1] = (dp * h).astype(BF16)
        out_ref[2] = (dp * c).astype(BF16)
        dw_ref[0:1, :] += jnp.sum(dcv * p2, axis=0, keepdims=True)
        dw_ref[1:2, :] += jnp.sum(dcv * p1, axis=0, keepdims=True)
        dw_ref[2:3, :] += jnp.sum(dcv * p, axis=0, keepdims=True)

    def tile(s):
        return pl.BlockSpec((None, TM, D), lambda i: (s, i, 0))

    def prev(s):
        return pl.BlockSpec((None, HALO, D), lambda i: (s, _prev_halo(i), 0))

    return pl.pallas_call(
        body,
        name="conv_mix_bwd",
        grid=(LP // TM,),
        in_specs=[
            tile(0), tile(1), tile(2), _row_spec(),
            prev(1), prev(2),
            pl.BlockSpec((None, HALO, D), lambda i: (0, _next_halo(i), 0)),
            pl.BlockSpec((HALO, D), lambda i: (_next_halo(i), 0)),
            _vec_spec(3, D),
            pl.BlockSpec(memory_space=pl.ANY),
        ],
        out_specs=[pl.BlockSpec((3, TM, D), lambda i: (0, i, 0)), _vec_spec(8, D)],
        out_shape=[jax.ShapeDtypeStruct((N_DEV, LP, D), BF16), jax.ShapeDtypeStruct((8, D), F32)],
        input_output_aliases={9: 0},
        compiler_params=_params(("arbitrary",)),
    )(hin, hin, hin, dy, hin, hin, hin, dy, cw, dhin)


GELU_K = math.sqrt(2.0 / math.pi)
GELU_A = 0.044715


def _gelu_and_grad(x):
    x2 = x * x
    t = jnp.tanh(GELU_K * (x + GELU_A * x2 * x))
    gelu = 0.5 * x * (1.0 + t)
    grad = 0.5 * (1.0 + t) + 0.5 * x * (1.0 - t * t) * GELU_K * (1.0 + 3.0 * GELU_A * x2)
    return gelu, grad


def _ffn_act_fwd(ug, cw4):
    def body(u_ref, g_ref, up_ref, w_ref, o_ref):
        i = pl.program_id(1)
        u = u_ref[...].astype(F32)
        up = jnp.where(i > 0, up_ref[...].astype(F32), 0.0)
        ext = jnp.concatenate([up, u], axis=0)
        w = [w_ref[t:t + 1, :] for t in range(3)]
        uc = (w[2] * ext + w[1] * _down(ext, 1) + w[0] * _down(ext, 2))[HALO:]
        gelu, _ = _gelu_and_grad(uc)
        o_ref[...] = (gelu * g_ref[...].astype(F32)).astype(BF16)

    return pl.pallas_call(
        body,
        name="ffn_act_fwd",
        grid=(N_FB, LP // TM),
        in_specs=[
            pl.BlockSpec((None, TM, FB), lambda j, i: (j, i, 0)),
            pl.BlockSpec((None, TM, FB), lambda j, i: (j + N_FB, i, 0)),
            pl.BlockSpec((None, HALO, FB), lambda j, i: (j, _prev_halo(i), 0)),
            pl.BlockSpec((None, 3, FB), lambda j, i: (j, 0, 0)),
        ],
        out_specs=pl.BlockSpec((None, TM, FB), lambda j, i: (j, i, 0)),
        out_shape=jax.ShapeDtypeStruct((N_FB, LP, FB), BF16),
        compiler_params=_params(("parallel", "parallel")),
    )(ug, ug, ug, cw4)


def _ffn_act_bwd(ug, dhid, cw4):
    last = LP // TM - 1
    n = TM + 2 * HALO

    def body(u_ref, g_ref, dh_ref, up_ref, un_ref, gn_ref, dhn_ref, w_ref, dug_ref, dw_ref):
        i = pl.program_id(1)

        @pl.when(i == 0)
        def _():
            dw_ref[...] = jnp.zeros_like(dw_ref)

        w = [w_ref[t:t + 1, :] for t in range(3)]
        u = u_ref[...].astype(F32)
        up = jnp.where(i > 0, up_ref[...].astype(F32), 0.0)
        ext = jnp.concatenate([up, u, un_ref[...].astype(F32)], axis=0)
        u1 = _down(ext, 1)
        u2 = _down(ext, 2)
        uc = w[2] * ext + w[1] * u1 + w[0] * u2
        gelu, ggrad = _gelu_and_grad(uc)
        zeros = jnp.zeros((HALO, FB), F32)
        gext = jnp.concatenate([zeros, g_ref[...].astype(F32), gn_ref[...].astype(F32)], axis=0)
        dhn = jnp.where(i < last, dhn_ref[...].astype(F32), 0.0)
        dhext = jnp.concatenate([zeros, dh_ref[...].astype(F32), dhn], axis=0)
        dug_ref[1] = (dhext * gelu)[HALO:HALO + TM].astype(BF16)
        duc = dhext * gext * ggrad
        du = w[2] * duc + w[1] * _up(duc, 1) + w[0] * _up(duc, 2)
        dug_ref[0] = du[HALO:HALO + TM].astype(BF16)
        row = lax.broadcasted_iota(jnp.int32, (n, 1), 0)
        own = jnp.where((row >= HALO) & (row < HALO + TM), duc, 0.0)
        dw_ref[0:1, :] += jnp.sum(own * u2, axis=0, keepdims=True)
        dw_ref[1:2, :] += jnp.sum(own * u1, axis=0, keepdims=True)
        dw_ref[2:3, :] += jnp.sum(own * ext, axis=0, keepdims=True)

    def tile(off):
        return pl.BlockSpec((None, TM, FB), lambda j, i: (j + off, i, 0))

    def nxt(off):
        return pl.BlockSpec((None, HALO, FB), lambda j, i: (j + off, _next_halo(i), 0))

    return pl.pallas_call(
        body,
        name="ffn_act_bwd",
        grid=(N_FB, LP // TM),
        in_specs=[
            tile(0), tile(N_FB), tile(0),
            pl.BlockSpec((None, HALO, FB), lambda j, i: (j, _prev_halo(i), 0)),
            nxt(0), nxt(N_FB), nxt(0),
            pl.BlockSpec((None, 3, FB), lambda j, i: (j, 0, 0)),
        ],
        out_specs=[
            pl.BlockSpec((2, None, TM, FB), lambda j, i: (0, j, i, 0)),
            pl.BlockSpec((None, 8, FB), lambda j, i: (j, 0, 0)),
        ],
        out_shape=[jax.ShapeDtypeStruct((2, N_FB, LP, FB), BF16), jax.ShapeDtypeStruct((N_FB, 8, FB), F32)],
        compiler_params=_params(("parallel", "arbitrary")),
    )(ug, ug, dhid, ug, ug, ug, dhid, cw4)


def _gate_fwd(bc, ba, hin, bgate):
    def body(bc_ref, ba_ref, gc_ref, ga_ref, b_ref, o_ref):
        b = b_ref[...]
        sc = jax.nn.sigmoid(gc_ref[...].astype(F32) + b[0:1])
        sa = jax.nn.sigmoid(ga_ref[...].astype(F32) + b[1:2])
        o_ref[...] = (sc * bc_ref[...].astype(F32) + sa * ba_ref[...].astype(F32)).astype(BF16)

    def tile(s):
        return pl.BlockSpec((None, TM, D), lambda i: (s, i, 0))

    return pl.pallas_call(
        body,
        name="gate_fwd",
        grid=(LP // TM,),
        in_specs=[_row_spec(), _row_spec(), tile(6), tile(7), _vec_spec(2, D)],
        out_specs=_row_spec(),
        out_shape=jax.ShapeDtypeStruct((LP, D), BF16),
        compiler_params=_params(("parallel",)),
    )(bc, ba, hin, hin, bgate)


def _gate_bwd(dm, bc, ba, hin, bgate):
    def body(dm_ref, bc_ref, ba_ref, gc_ref, ga_ref, b_ref, dbc_ref, dba_ref, dg_ref, db_ref):
        i = pl.program_id(0)

        @pl.when(i == 0)
        def _():
            db_ref[...] = jnp.zeros_like(db_ref)

        b = b_ref[...]
        dm = dm_ref[...].astype(F32)
        sc = jax.nn.sigmoid(gc_ref[...].astype(F32) + b[0:1])
        sa = jax.nn.sigmoid(ga_ref[...].astype(F32) + b[1:2])
        dbc_ref[...] = (dm * sc).astype(BF16)
        dba_ref[...] = (dm * sa).astype(BF16)
        dgc = dm * bc_ref[...].astype(F32) * sc * (1.0 - sc)
        dga = dm * ba_ref[...].astype(F32) * sa * (1.0 - sa)
        dg_ref[0] = dgc.astype(BF16)
        dg_ref[1] = dga.astype(BF16)
        db_ref[0:1, :] += jnp.sum(dgc, axis=0, keepdims=True)
        db_ref[1:2, :] += jnp.sum(dga, axis=0, keepdims=True)

    def tile(s):
        return pl.BlockSpec((None, TM, D), lambda i: (s, i, 0))

    return pl.pallas_call(
        body,
        name="gate_bwd",
        grid=(LP // TM,),
        in_specs=[_row_spec(), _row_spec(), _row_spec(), tile(6), tile(7), _vec_spec(2, D)],
        out_specs=[_row_spec(), _row_spec(), pl.BlockSpec((2, TM, D), lambda i: (3, i, 0)), _vec_spec(8, D)],
        out_shape=[jax.ShapeDtypeStruct((LP, D), BF16)] * 2
        + [jax.ShapeDtypeStruct((N_DEV, LP, D), BF16), jax.ShapeDtypeStruct((8, D), F32)],
        compiler_params=_params(("arbitrary",)),
    )(dm, bc, ba, hin, hin, bgate)


Z_LINEAR = 30.0


def _softplus(z):
    return jnp.maximum(z, jnp.log(1.0 + jnp.exp(jnp.minimum(z, Z_LINEAR))))


def _cumsum_matrix(inclusive, reverse):
    r = lax.broadcasted_iota(jnp.int32, (QB, 2 * QB), 0)
    c = lax.broadcasted_iota(jnp.int32, (QB, 2 * QB), 1)
    if reverse:
        tri = r > c
    elif inclusive:
        tri = r <= c
    else:
        tri = r < c
    return jnp.where((c >= QB) | tri, 1.0, 0.0).astype(BF16)


def _split_dot(x, m2):
    bits = lax.bitcast_convert_type(x, jnp.uint32) & jnp.uint32(0xFFFF0000)
    hi = lax.bitcast_convert_type(bits, F32)
    return _dot(jnp.concatenate([hi.astype(BF16), (x - hi).astype(BF16)], axis=1), m2)


def _stack_heads(x):
    return jnp.concatenate(_split_heads(x), axis=0)


def _block_mask(i, j, row0):
    row = lax.broadcasted_iota(jnp.int32, (AQ - row0, QB), 0) + (i * AQ + row0)
    col = lax.broadcasted_iota(jnp.int32, (AQ - row0, QB), 1) + j * QB
    return (col < row) & (col >= PAD)


def _key_block(ref, j):
    return ref[pl.ds(pl.multiple_of(j * QB, QB), QB), :]


def _split_heads(x):
    head_a = lax.broadcasted_iota(jnp.int32, x.shape, 1) < HEAD_LANES
    zero = jnp.zeros_like(x)
    return jnp.where(head_a, x, zero), jnp.where(head_a, zero, x)


def _attn_fwd(hin, ex_arrs, gathers):
    ne = len(ex_arrs)
    npair, nq = D // QB, LP // AQ

    def body(*refs):
        q_ref, k_ref, v_ref = refs[:3]
        o_ref, lt_ref = refs[3 + ne:5 + ne]
        c_sc, acc_sc = refs[5 + 2 * ne:7 + 2 * ne]
        p, i = pl.program_id(0), pl.program_id(1)

        def copies():
            return _exchange_copies(refs[3:3 + ne], refs[5 + ne:5 + 2 * ne], gathers, *refs[7 + 2 * ne:])

        @pl.when((p == 0) & (i == 0))
        def _():
            for cp in copies():
                cp.start()

        um = _cumsum_matrix(False, True)
        um2 = jnp.concatenate([um, um], axis=0)
        q = (q_ref[...].astype(F32) * SCALE).astype(BF16)
        c_sc[...] = jnp.zeros_like(c_sc)
        acc_sc[...] = jnp.zeros_like(acc_sc)

        def step(j, masked, row0=0):
            rows = slice(row0, AQ)
            z2 = _dot(q[rows], _stack_heads(_key_block(k_ref, j)), NT)
            mask = _block_mask(i, j, row0) if masked else None
            a2 = []
            for hd in range(2):
                z = z2[:, hd * QB:(hd + 1) * QB]
                sp = _softplus(z)
                r = _split_dot(jnp.where(mask, sp, 0.0) if masked else sp, um2)
                a = jnp.exp(z - sp - c_sc[hd, rows] - r[:, :QB])
                if masked:
                    a = jnp.where(mask, a, 0.0)
                a2.append(a.astype(BF16))
                c_sc[hd, rows] += r[:, QB:]
            acc_sc[rows] += _dot(jnp.concatenate(a2, axis=1), _stack_heads(_key_block(v_ref, j)))

        for t in reversed(range(KPQ)):
            step(KPQ * i + t, True, t * QB)

        inner = jnp.maximum(KPQ * i - 1, 0)

        @pl.loop(0, inner // 2)
        def _(t):
            step(KPQ * i - 1 - 2 * t, False)
            step(KPQ * i - 2 - 2 * t, False)

        @pl.when(inner % 2 == 1)
        def _():
            step(1, False)

        @pl.when(i > 0)
        def _():
            step(0, True)

        head_a = lax.broadcasted_iota(jnp.int32, (AQ, QB), 1) < HEAD_LANES
        o_ref[...] = acc_sc[...].astype(BF16)
        lt_ref[...] = jnp.where(head_a, c_sc[0], c_sc[1])

        @pl.when((p == npair - 1) & (i == nq - 1))
        def _():
            for cp in copies():
                cp.wait()

    def seq(s):
        return pl.BlockSpec((None, LP, QB), lambda p, i: (s, 0, p))

    return pl.pallas_call(
        body,
        name="attn_fwd",
        grid=(npair, nq),
        in_specs=[pl.BlockSpec((None, AQ, QB), lambda p, i: (3, i, p)), seq(4), seq(5)] + [ANY_SPEC] * ne,
        out_specs=[pl.BlockSpec((AQ, QB), lambda p, i: (i, p))] * 2 + [ANY_SPEC] * ne,
        out_shape=[jax.ShapeDtypeStruct((LP, D), BF16), jax.ShapeDtypeStruct((LP, D), F32)]
        + _exchange_shapes(ex_arrs, gathers),
        scratch_shapes=[pltpu.VMEM((2, AQ, QB), F32), pltpu.VMEM((AQ, QB), F32)] + _exchange_sems(ne),
        compiler_params=_params(("arbitrary", "arbitrary")),
    )(hin, hin, hin, *ex_arrs)


def _attn_bwd(hin, do, lt, dhin, ex_arrs, gathers):
    ne = len(ex_arrs)
    npair, nq = D // QB, LP // AQ

    def body(*refs):
        q_ref, k_ref, v_ref, do_ref, lt_ref = refs[:5]
        out_ref = refs[6 + ne]
        psp_sc, pg_sc, dq_sc, dk_acc, dv_acc = refs[7 + 2 * ne:12 + 2 * ne]
        p, i = pl.program_id(0), pl.program_id(1)

        def copies():
            return _exchange_copies(refs[6:6 + ne], refs[7 + ne:7 + 2 * ne], gathers, *refs[12 + 2 * ne:])

        @pl.when((p == 0) & (i == 0))
        def _():
            for cp in copies():
                cp.start()

        @pl.when(i == 0)
        def _():
            dk_acc[...] = jnp.zeros_like(dk_acc)
            dv_acc[...] = jnp.zeros_like(dv_acc)

        um_sp = _cumsum_matrix(True, False)
        um_sp2 = jnp.concatenate([um_sp, um_sp], axis=0)
        um_g = _cumsum_matrix(False, False)
        q = (q_ref[...].astype(F32) * SCALE).astype(BF16)
        do = do_ref[...]
        q_t, do_t = q.T, do.T
        head_a = lax.broadcasted_iota(jnp.int32, (AQ, QB), 1) < HEAD_LANES
        dim_head_a = lax.broadcasted_iota(jnp.int32, (QB, QB), 0) < HEAD_LANES
        lt = lt_ref[...]
        lt_sw = pltpu.roll(lt, HEAD_LANES, 1)
        totals = (jnp.where(head_a, lt, lt_sw), jnp.where(head_a, lt_sw, lt))
        psp_sc[...] = jnp.zeros_like(psp_sc)
        pg_sc[...] = jnp.zeros_like(pg_sc)
        dq_sc[...] = jnp.zeros_like(dq_sc)

        def step(j, masked, row0=0):
            rows = slice(row0, AQ)
            k2 = _stack_heads(_key_block(k_ref, j))
            z2 = _dot(q[rows], k2, NT)
            da2 = _dot(do[rows], _stack_heads(_key_block(v_ref, j)), NT)
            mask = _block_mask(i, j, row0) if masked else None
            a2, dz2 = [], []
            for hd in range(2):
                z = z2[:, hd * QB:(hd + 1) * QB]
                sp = _softplus(z)
                r = _split_dot(jnp.where(mask, sp, 0.0) if masked else sp, um_sp2)
                a = jnp.exp(z - sp - (totals[hd][rows] - psp_sc[hd, rows] - r[:, :QB]))
                if masked:
                    a = jnp.where(mask, a, 0.0)
                g = a * da2[:, hd * QB:(hd + 1) * QB]
                rg = _dot(g.astype(BF16), um_g)
                dz = g - jnp.exp(z - sp) * (g + pg_sc[hd, rows] + rg[:, :QB])
                if masked:
                    dz = jnp.where(mask, dz, 0.0)
                a2.append(a.astype(BF16))
                dz2.append(dz.astype(BF16))
                psp_sc[hd, rows] += r[:, QB:]
                pg_sc[hd, rows] += rg[:, QB:]
            dz2 = jnp.concatenate(dz2, axis=1)
            dq_sc[rows] += _dot(dz2, k2)
            dk2 = _dot(q_t[:, rows], dz2)
            dv2 = _dot(do_t[:, rows], jnp.concatenate(a2, axis=1))
            dk_acc[j] += jnp.where(dim_head_a, dk2[:, :QB], dk2[:, QB:])
            dv_acc[j] += jnp.where(dim_head_a, dv2[:, :QB], dv2[:, QB:])

        @pl.when(i > 0)
        def _():
            step(0, True)

        inner = jnp.maximum(KPQ * i - 1, 0)

        @pl.loop(0, inner // 2)
        def _(t):
            step(2 * t + 1, False)
            step(2 * t + 2, False)

        @pl.when(inner % 2 == 1)
        def _():
            step(inner, False)

        for t in range(KPQ):
            step(KPQ * i + t, True, t * QB)

        out_ref[0, pl.ds(pl.multiple_of(i * AQ, AQ), AQ), :] = (dq_sc[...] * SCALE).astype(BF16)

        @pl.when(i == nq - 1)
        def _():
            @pl.loop(0, LP // QB)
            def _(b):
                keys = pl.ds(pl.multiple_of(b * QB, QB), QB)
                out_ref[1, keys, :] = dk_acc[b].T.astype(BF16)
                out_ref[2, keys, :] = dv_acc[b].T.astype(BF16)

        @pl.when((p == npair - 1) & (i == nq - 1))
        def _():
            for cp in copies():
                cp.wait()

    def seq(s):
        return pl.BlockSpec((None, LP, QB), lambda p, i: (s, 0, p))

    blk = pl.BlockSpec((AQ, QB), lambda p, i: (i, p))
    return pl.pallas_call(
        body,
        name="attn_bwd",
        grid=(npair, nq),
        in_specs=[pl.BlockSpec((None, AQ, QB), lambda p, i: (3, i, p)), seq(4), seq(5), blk, blk]
        + [ANY_SPEC] * (1 + ne),
        out_specs=[pl.BlockSpec((3, LP, QB), lambda p, i: (1, 0, p))] + [ANY_SPEC] * ne,
        out_shape=[jax.ShapeDtypeStruct((N_DEV, LP, D), BF16)] + _exchange_shapes(ex_arrs, gathers),
        input_output_aliases={5: 0},
        scratch_shapes=[pltpu.VMEM((2, AQ, QB), F32)] * 2 + [pltpu.VMEM((AQ, QB), F32)]
        + [pltpu.VMEM((LP // QB, QB, QB), F32)] * 2 + _exchange_sems(ne),
        compiler_params=_params(("arbitrary", "arbitrary")),
    )(hin, hin, hin, do, lt, dhin, *ex_arrs)


def _adamw_math(w, g, m, v):
    m_new = ADAM_B1 * m + (1.0 - ADAM_B1) * g
    v_new = ADAM_B2 * v + (1.0 - ADAM_B2) * jnp.square(g)
    m_hat = m_new / (1.0 - ADAM_B1 ** ADAM_STEP)
    v_hat = v_new / (1.0 - ADAM_B2 ** ADAM_STEP)
    return -ADAM_LR * (m_hat / (jnp.sqrt(v_hat) + ADAM_EPS) + ADAM_WD * w), m_new, v_new


def _adamw_small(ws, gs, ms, vs):
    n = len(ws)

    def body(*refs):
        for t in range(n):
            w_ref, g_ref, m_ref, v_ref = (refs[s * n + t] for s in range(4))
            d_ref, nm_ref, nv_ref = (refs[(4 + s) * n + t] for s in range(3))
            d_ref[...], nm_ref[...], nv_ref[...] = _adamw_math(w_ref[...], g_ref[...], m_ref[...], v_ref[...])

    vmem = pl.BlockSpec(memory_space=pltpu.VMEM)
    res = pl.pallas_call(
        body,
        name="adamw_small",
        in_specs=[vmem] * (4 * n),
        out_specs=[vmem] * (3 * n),
        out_shape=[jax.ShapeDtypeStruct(w.shape, F32) for w in ws] * 3,
    )(*ws, *gs, *ms, *vs)
    return res[:n], res[n:2 * n], res[2 * n:]


def _adamw(pieces, w, m, v, *, rows, row_off, tr, name):
    npieces, _, cols = pieces.shape
    ob = row_off // tr

    def body(p_ref, w_ref, m_ref, v_ref, g_ref, d_ref, nm_ref, nv_ref):
        g = p_ref[0].astype(F32)
        for s in range(1, npieces):
            g = g + p_ref[s].astype(F32)
        g_ref[...] = g
        d_ref[...], nm_ref[...], nv_ref[...] = _adamw_math(w_ref[...], g, m_ref[...], v_ref[...])

    spec = pl.BlockSpec((tr, cols), lambda i: (i, 0))
    return pl.pallas_call(
        body,
        name=name,
        grid=(rows // tr,),
        in_specs=[pl.BlockSpec((npieces, tr, cols), lambda i: (0, ob + i, 0)), spec, spec, spec],
        out_specs=[spec] * 4,
        out_shape=[jax.ShapeDtypeStruct((rows, cols), F32)] * 4,
        compiler_params=_params(("parallel",)),
    )(pieces, w, m, v)


def _sum_pieces(pieces, name):
    npieces, rows, cols = pieces.shape

    def body(p_ref, o_ref):
        acc = p_ref[0]
        for s in range(1, npieces):
            acc = acc + p_ref[s]
        o_ref[...] = acc

    return pl.pallas_call(
        body,
        name=name,
        in_specs=[pl.BlockSpec(memory_space=pltpu.VMEM)],
        out_specs=pl.BlockSpec(memory_space=pltpu.VMEM),
        out_shape=jax.ShapeDtypeStruct((rows, cols), pieces.dtype),
    )(pieces)


SMALL_ROWS = 48
CWF_PAD = 384
GRAD_ROWS = 80
G_META, G_PRE_MIX, G_MID, G_POST_FFN, G_CW_MIX, G_B_GATE, G_CW_FFN, G_LOSS = 0, 16, 24, 32, 40, 48, 56, 72


def kernel(x, meta_tokens, g_pre_mix, w_in, conv_w_mix, w_proj_conv, w_proj_attn, b_gate, w_out, g_post_mix, g_pre_ffn, w_up_gate, conv_w_ffn, w_down, g_post_ffn, loss_target, m_meta_tokens, m_g_pre_mix, m_w_in, m_conv_w_mix, m_w_proj_conv, m_w_proj_attn, m_b_gate, m_w_out, m_g_post_mix, m_g_pre_ffn, m_w_up_gate, m_conv_w_ffn, m_w_down, m_g_post_ffn, v_meta_tokens, v_g_pre_mix, v_w_in, v_conv_w_mix, v_w_proj_conv, v_w_proj_attn, v_b_gate, v_w_out, v_g_post_mix, v_g_pre_ffn, v_w_up_gate, v_conv_w_ffn, v_w_down, v_g_post_ffn):
    me = 4 * lax.axis_index("x") + 2 * lax.axis_index("y") + lax.axis_index("c")

    def rows_to(a, n):
        return jnp.pad(a, ((0, n - a.shape[0]), (0, 0)))

    small_shard = jnp.concatenate(
        [meta_tokens, rows_to(conv_w_mix[0], 8), rows_to(b_gate[0], 8),
         rows_to(jnp.pad(conv_w_ffn[0], ((0, 0), (0, CWF_PAD - R_DOWN))).reshape(9, 128), 16)], axis=0)
    wshard = jnp.concatenate([w_proj_conv[0], w_proj_attn[0], w_out[0], w_down[0]], axis=0).astype(BF16)
    w_in_all, small_all = _gather_two_level([w_in[0].astype(BF16), small_shard], "gather_in")

    def unshard(rows):
        return rows.transpose(1, 0, 2).reshape(rows.shape[1], N_DEV * rows.shape[2])

    meta = unshard(small_all[:, 0:16])
    cw_mix = unshard(small_all[:, 16:19])
    bgate = unshard(small_all[:, 24:26])
    cw_ffn = unshard(small_all[:, 32:41].reshape(N_DEV, 3, CWF_PAD)[:, :, :R_DOWN])
    cw4 = cw_ffn.reshape(3, N_FB, FB).transpose(1, 0, 2)
    h0 = jnp.concatenate([jnp.zeros((PAD, D), F32), meta, x[0]], axis=0)

    xn1 = _rms_fwd(h0, g_pre_mix, "rms_pre_mix")
    hin = _mm(xn1, w_in_all, w_rows=D, trans_w=False, out_dtype=BF16, name="mm_in")
    y_conv = _conv_mix_fwd(hin, cw_mix)
    o, lt, wpack, w_ug = _attn_fwd(hin, [wshard, w_up_gate[0].astype(BF16)], (True, True))
    w_pc = wpack[:, O_PC:O_PA].reshape(1, D, D)
    w_pa = wpack[:, O_PA:O_OUT].reshape(1, D, D)
    w_o = wpack[:, O_OUT:O_DOWN].reshape(1, D, D)
    w_dn = wpack[:, O_DOWN:].reshape(N_FB, FB, D)
    bc = _mm(y_conv, w_pc, w_rows=D, trans_w=False, out_dtype=BF16, name="mm_proj_conv")[0]
    ba = _mm(o, w_pa, w_rows=D, trans_w=False, out_dtype=BF16, name="mm_proj_attn")[0]
    merged = _gate_fwd(bc, ba, hin, bgate)
    mix = _mm(merged, w_o, w_rows=D, trans_w=False, out_dtype=F32, name="mm_out")[0]
    h1, xn2 = _resid_rms(h0, mix, g_post_mix, g_pre_ffn)
    ug = _mm(xn2, w_ug, w_rows=D, trans_w=False, out_dtype=BF16, name="mm_up_gate")
    hid = _ffn_act_fwd(ug, cw4)
    ffn = _mm_sum(hid, w_dn, w_rows=FB, trans_w=False, out_dtype=F32, name="mm_down")
    dout, dffn, loss8, dg_post_ffn = _loss_head(h1, ffn, g_post_ffn, loss_target[0])

    dhid = _mm(dffn, w_dn, w_rows=FB, trans_w=True, out_dtype=BF16, name="mm_down_dx")
    gw_dn = _mm_tn(hid, dffn, nb=N_FB, out_dtype=BF16, name="mm_down_dw")
    dug, dcw4 = _ffn_act_bwd(ug, dhid, cw4)
    dug = dug.reshape(2 * N_FB, LP, FB)
    dxn2 = _mm_sum(dug, w_ug, w_rows=D, trans_w=True, out_dtype=F32, name="mm_up_gate_dx")
    gw_ug = _mm_tn(xn2, dug, nb=N_DEV, out_dtype=BF16, name="mm_up_gate_dw")
    dh1, dmix, dg_mid = _mid_bwd(dout, h1, dxn2, mix, g_post_mix, g_pre_ffn)
    dmerged = _mm(dmix, w_o, w_rows=D, trans_w=True, out_dtype=BF16, name="mm_out_dx")[0]
    gw_out = _mm_tn(merged, dmix, nb=1, out_dtype=BF16, name="mm_out_dw")
    dbc, dba, dhin, db_gate = _gate_bwd(dmerged, bc, ba, hin, bgate)
    dy_conv = _mm(dbc, w_pc, w_rows=D, trans_w=True, out_dtype=BF16, name="mm_proj_conv_dx")[0]
    gw_pc = _mm_tn(y_conv, dbc, nb=1, out_dtype=BF16, name="mm_proj_conv_dw")
    do = _mm(dba, w_pa, w_rows=D, trans_w=True, out_dtype=BF16, name="mm_proj_attn_dx")[0]
    gw_pa = _mm_tn(o, dba, nb=1, out_dtype=BF16, name="mm_proj_attn_dw")
    dhin, dcw_mix = _conv_mix_bwd(hin, dy_conv, cw_mix, dhin)
    gpack = jnp.concatenate(
        [gw_pc.reshape(N_DEV, R_PROJ, D), gw_pa.reshape(N_DEV, R_PROJ, D), gw_out.reshape(N_DEV, R_PROJ, D),
         gw_dn.reshape(N_DEV, R_DOWN, D)], axis=1)
    dhin, rpack, rug = _attn_bwd(hin, do, lt, dhin, [gpack, gw_ug], (False, False))
    gw_in = _mm_tn(xn1, dhin, nb=N_DEV, out_dtype=BF16, name="mm_in_dw")
    send_sems, recv_sems, gw_thru, land_thru, token = _scatter_start(gw_in)
    dxn1 = _mm_sum(dhin, w_in_all, w_rows=D, trans_w=True, out_dtype=F32, name="mm_in_dx", after=token)
    dh0, dg_pre_mix = _first_bwd(dh1, h0, dxn1, g_pre_mix)
    dcw_ffn = dcw4[:, :3].transpose(1, 0, 2).reshape(3, D_FF)
    small = jnp.concatenate(
        [dh0[PAD:OFF], dg_pre_mix, dg_mid, dg_post_ffn, dcw_mix, db_gate,
         jnp.pad(dcw_ffn.reshape(-1), (0, 16 * D - 3 * D_FF)).reshape(16, D),
         jnp.pad(loss8, ((0, 0), (0, D - 128)))], axis=0)

    def big(pieces, w, m, v, rows, row_off, tr, name):
        g, d, nm, nv = _adamw(pieces, w[0], m[0], v[0], rows=rows, row_off=row_off, tr=tr, name=name)
        return g[None], d[None], nm[None], nv[None]

    r_pc = big(rpack, w_proj_conv, m_w_proj_conv, v_w_proj_conv, R_PROJ, O_PC, R_PROJ, "adamw_proj_conv")
    r_pa = big(rpack, w_proj_attn, m_w_proj_attn, v_w_proj_attn, R_PROJ, O_PA, R_PROJ, "adamw_proj_attn")
    r_out = big(rpack, w_out, m_w_out, v_w_out, R_PROJ, O_OUT, R_PROJ, "adamw_out")
    r_dn = big(rpack, w_down, m_w_down, v_w_down, R_DOWN, O_DOWN, 32, "adamw_down")
    r_ug = big(rug, w_up_gate, m_w_up_gate, v_w_up_gate, D, 0, 256, "adamw_up_gate")
    (rsmall,) = _exchange([small], (True,), "gather_small_grads",
                          after=(r_pc[1], r_pa[1], r_out[1], r_dn[1], r_ug[1]))
    gs = _sum_pieces(rsmall, "sum_small_grads")
    grad_x = dh0[OFF:]
    loss = gs[G_LOSS, 0]

    def cols(a, width):
        return lax.dynamic_slice_in_dim(a, me * width, width, axis=1)

    g_meta = cols(gs[G_META:G_META + N_META], 128)
    g_gpm, g_gff = gs[G_PRE_MIX:G_PRE_MIX + 1], gs[G_POST_FFN:G_POST_FFN + 1]
    g_gpo, g_gpf = gs[G_MID:G_MID + 1], gs[G_MID + 1:G_MID + 2]
    g_cwm = cols(gs[G_CW_MIX:G_CW_MIX + 3], 128)[None]
    g_bg = cols(gs[G_B_GATE:G_B_GATE + 2], 128)[None]
    g_cwf = cols(gs[G_CW_FFN:G_CW_FFN + 9].reshape(-1)[:3 * D_FF].reshape(3, D_FF), R_DOWN)[None]

    small_w = [meta_tokens, g_pre_mix, conv_w_mix, b_gate, g_post_mix, g_pre_ffn, conv_w_ffn, g_post_ffn]
    small_g = [g_meta, g_gpm, g_cwm, g_bg, g_gpo, g_gpf, g_cwf, g_gff]
    small_m = [m_meta_tokens, m_g_pre_mix, m_conv_w_mix, m_b_gate, m_g_post_mix, m_g_pre_ffn, m_conv_w_ffn, m_g_post_ffn]
    small_v = [v_meta_tokens, v_g_pre_mix, v_conv_w_mix, v_b_gate, v_g_post_mix, v_g_pre_ffn, v_conv_w_ffn, v_g_post_ffn]

    s_g = small_g
    s_d, s_m, s_v = _adamw_small(small_w, small_g, small_m, small_v)

    gw_done, landed = _scatter_wait(send_sems, recv_sems, gw_thru, land_thru, (s_d[0],))
    rin = lax.dynamic_update_index_in_dim(landed, lax.dynamic_index_in_dim(gw_done, me, 0, keepdims=False), me, 0)
    r_in = big(rin, w_in, m_w_in, v_w_in, D, 0, 256, "adamw_in")

    def ordered(k, smalls):
        meta, gpm, cwm, bg, gpo, gpf, cwf, gff = smalls
        return [meta, gpm, r_in[k], cwm, r_pc[k], r_pa[k], bg, r_out[k], gpo, gpf, r_ug[k], cwf, r_dn[k], gff]

    return (loss, grad_x[None], *ordered(0, s_g), *ordered(1, s_d), *ordered(2, s_m), *ordered(3, s_v))
```

```python
import functools
import math

import jax
import jax.numpy as jnp
from jax import lax
from jax.experimental import pallas as pl
from jax.experimental.pallas import tpu as pltpu

F32 = jnp.float32
BF16 = jnp.bfloat16

D = 1024
SEQ = 4096
N_META = 16
PAD = 112
OFF = PAD + N_META
LP = OFF + SEQ
QB = 128
AQ = 384
KPQ = AQ // QB
TM = 384
MM_TM = 1408
HALO = 16
N_DEV = 8
D_FF = 2816
FB = 704
N_FB = D_FF // FB
RMS_EPS = 1e-6
SCALE = 0.125
HEAD_LANES = 64
VMEM_LIMIT = 56 * 1024 * 1024

ADAM_LR = 0.001
ADAM_B1 = 0.9
ADAM_B2 = 0.999
ADAM_EPS = 1e-08
ADAM_WD = 0.01
ADAM_STEP = 10

R_PROJ, R_DOWN = 128, 352
O_PC = 0
O_PA = O_PC + R_PROJ
O_OUT = O_PA + R_PROJ
O_DOWN = O_OUT + R_PROJ
R_PACK = O_DOWN + R_DOWN

NT = (((1,), (1,)), ((), ()))
NN = (((1,), (0,)), ((), ()))
TN = (((0,), (0,)), ((), ()))


def _params(sem):
    return pltpu.CompilerParams(dimension_semantics=sem, vmem_limit_bytes=VMEM_LIMIT)


def _dot(a, b, dn=NN):
    return lax.dot_general(a, b, dn, preferred_element_type=F32)


def _exchange_copies(ins, outs, gathers, send_sems, recv_sems, loc_sems):
    x, y, c = lax.axis_index("x"), lax.axis_index("y"), lax.axis_index("c")
    me = 4 * x + 2 * y + c
    copies = []
    for a, gather in enumerate(gathers):
        copies.append(pltpu.make_async_copy(ins[a] if gather else ins[a].at[me], outs[a].at[me], loc_sems.at[a]))
    for k in range(1, N_DEV):
        px = 1 - x if k & 4 else x
        py = 1 - y if k & 2 else y
        pc = 1 - c if k & 1 else c
        peer = 4 * px + 2 * py + pc
        for a, gather in enumerate(gathers):
            copies.append(pltpu.make_async_remote_copy(
                src_ref=ins[a] if gather else ins[a].at[peer],
                dst_ref=outs[a].at[me],
                send_sem=send_sems.at[a * (N_DEV - 1) + k - 1],
                recv_sem=recv_sems.at[a * (N_DEV - 1) + k - 1],
                device_id=(px, py, pc),
                device_id_type=pl.DeviceIdType.MESH,
            ))
    return copies


def _exchange_shapes(arrs, gathers):
    return [jax.ShapeDtypeStruct((N_DEV,) + (a.shape if g else a.shape[1:]), a.dtype) for a, g in zip(arrs, gathers)]


def _exchange_sems(n):
    return [pltpu.SemaphoreType.DMA((n * (N_DEV - 1),)), pltpu.SemaphoreType.DMA((n * (N_DEV - 1),)),
            pltpu.SemaphoreType.DMA((n,))]


ANY_SPEC = pl.BlockSpec(memory_space=pl.ANY)


def _gather_two_level(arrs, name):
    n = len(arrs)
    per = 7

    def body(*refs):
        ins, outs = refs[:n], refs[n:2 * n]
        send_sems, recv_sems, loc_sems = refs[2 * n:]
        x, y, c = lax.axis_index("x"), lax.axis_index("y"), lax.axis_index("c")
        me, sibling = (x, y, c), (x, y, 1 - c)
        chips = [(1 - x, y), (x, 1 - y), (1 - x, 1 - y)]

        def copy(a, k, block, to, src=None):
            place = outs[a].at[4 * block[0] + 2 * block[1] + block[2]]
            return pltpu.make_async_remote_copy(
                src_ref=place if src is None else src, dst_ref=place,
                send_sem=send_sems.at[a * per + k], recv_sem=recv_sems.at[a * per + k],
                device_id=to, device_id_type=pl.DeviceIdType.MESH)

        mine = [pltpu.make_async_copy(ins[a], outs[a].at[4 * x + 2 * y + c], loc_sems.at[a]) for a in range(n)]
        first = [copy(a, 0, me, sibling, src=ins[a]) for a in range(n)]
        first += [copy(a, 1 + j, me, (*chip, c), src=ins[a]) for j, chip in enumerate(chips) for a in range(n)]
        for cp in mine + first:
            cp.start()
        passed = []
        for j, chip in enumerate(chips):
            for a in range(n):
                copy(a, 1 + j, (*chip, c), me).wait_recv()
                passed.append(copy(a, 4 + j, (*chip, c), sibling))
                passed[-1].start()
        for a in range(n):
            copy(a, 0, sibling, me).wait_recv()
        for j, chip in enumerate(chips):
            for a in range(n):
                copy(a, 4 + j, (*chip, 1 - c), me).wait_recv()
        for cp in first + passed:
            cp.wait_send()
        for cp in mine:
            cp.wait()

    return pl.pallas_call(
        body,
        name=name,
        out_shape=_exchange_shapes(arrs, (True,) * n),
        in_specs=[ANY_SPEC] * n,
        out_specs=[ANY_SPEC] * n,
        scratch_shapes=_exchange_sems(n),
    )(*arrs)


HBM_SPEC = pl.BlockSpec(memory_space=pltpu.HBM)
SEM_SPEC = pl.BlockSpec(memory_space=pltpu.SEMAPHORE)
DATAFLOW = pltpu.SideEffectType.DATAFLOW_SIDE_EFFECTING


def _scatter_copies(g_ref, land_ref, send_sems, recv_sems):
    x, y, c = lax.axis_index("x"), lax.axis_index("y"), lax.axis_index("c")
    me = 4 * x + 2 * y + c
    copies = []
    for k in range(1, N_DEV):
        px = 1 - x if k & 4 else x
        py = 1 - y if k & 2 else y
        pc = 1 - c if k & 1 else c
        copies.append(pltpu.make_async_remote_copy(
            src_ref=g_ref.at[4 * px + 2 * py + pc], dst_ref=land_ref.at[me],
            send_sem=send_sems.at[k - 1], recv_sem=recv_sems.at[k - 1],
            device_id=(px, py, pc), device_id_type=pl.DeviceIdType.MESH))
    return copies


def _scatter_start(g):
    def body(g_ref, land_ref, send_sems, recv_sems, g_thru, land_thru, token):
        for cp in _scatter_copies(g_ref, land_ref, send_sems, recv_sems):
            cp.start()
        token[...] = jnp.zeros_like(token)

    return pl.pallas_call(
        body,
        name="scatter_in_start",
        out_shape=(pltpu.SemaphoreType.DMA((N_DEV - 1,)), pltpu.SemaphoreType.DMA((N_DEV - 1,)),
                   pltpu.HBM(g.shape, g.dtype), pltpu.HBM(g.shape, g.dtype), jax.ShapeDtypeStruct((8, 128), F32)),
        in_specs=(HBM_SPEC, HBM_SPEC),
        out_specs=(SEM_SPEC, SEM_SPEC, HBM_SPEC, HBM_SPEC, pl.BlockSpec(memory_space=pltpu.VMEM)),
        input_output_aliases={0: 2, 1: 3},
        compiler_params=pltpu.CompilerParams(has_side_effects=DATAFLOW),
    )(pltpu.with_memory_space_constraint(g, pltpu.HBM),
      pltpu.with_memory_space_constraint(lax.empty(g.shape, g.dtype), pltpu.HBM))


def _scatter_wait(send_sems, recv_sems, g_thru, land_thru, after):
    def body(g_ref, land_ref, send_sems, recv_sems, *_):
        for cp in _scatter_copies(g_ref, land_ref, send_sems, recv_sems):
            cp.wait_send()
            cp.wait_recv()

    return pl.pallas_call(
        body,
        name="scatter_in_wait",
        out_shape=(pltpu.HBM(g_thru.shape, g_thru.dtype), pltpu.HBM(g_thru.shape, g_thru.dtype)),
        in_specs=(HBM_SPEC, HBM_SPEC, SEM_SPEC, SEM_SPEC) + (ANY_SPEC,) * len(after),
        out_specs=(HBM_SPEC, HBM_SPEC),
        input_output_aliases={0: 0, 1: 1},
        compiler_params=pltpu.CompilerParams(has_side_effects=DATAFLOW),
    )(g_thru, land_thru, send_sems, recv_sems, *after)


def _exchange(arrs, gathers, name, after=()):
    n, na = len(arrs), len(after)

    def body(*refs):
        copies = _exchange_copies(refs[:n], refs[n + na:2 * n + na], gathers, *refs[2 * n + na:])
        for cp in copies:
            cp.start()
        for cp in copies:
            cp.wait()

    return pl.pallas_call(
        body,
        name=name,
        out_shape=_exchange_shapes(arrs, gathers),
        in_specs=[ANY_SPEC] * (n + na),
        out_specs=[ANY_SPEC] * n,
        scratch_shapes=_exchange_sems(n),
    )(*arrs, *after)


def _mm(a, w, *, w_rows, trans_w, out_dtype, name):
    nb, _, wc = w.shape
    m, k = a.shape[-2:]
    n = w_rows if trans_w else wc
    dn = NT if trans_w else NN

    def body(a_ref, w_ref, o_ref):
        o_ref[...] = _dot(a_ref[...], w_ref[...], dn).astype(out_dtype)

    if a.ndim == 2:
        a_spec = pl.BlockSpec((MM_TM, k), lambda j, i: (i, 0))
    else:
        a_spec = pl.BlockSpec((None, MM_TM, k), lambda j, i: (j, i, 0))
    return pl.pallas_call(
        body,
        name=name,
        grid=(nb, m // MM_TM),
        in_specs=[a_spec, pl.BlockSpec((None, w_rows, wc), lambda j, i: (j, 0, 0))],
        out_specs=pl.BlockSpec((None, MM_TM, n), lambda j, i: (j, i, 0)),
        out_shape=jax.ShapeDtypeStruct((nb, m, n), out_dtype),
        compiler_params=_params(("parallel", "parallel")),
    )(a, w)


def _mm_sum(a, w, *, w_rows, trans_w, out_dtype, name, after=()):
    nb, m, k = a.shape
    wc = w.shape[2]
    n = w_rows if trans_w else wc
    dn = NT if trans_w else NN
    after = tuple(after) if isinstance(after, (tuple, list)) else (after,)
    na = len(after)

    def body(*refs):
        a_ref, w_ref, o_ref, acc_ref = refs[0], refs[1], refs[2 + na], refs[3 + na]
        j = pl.program_id(1)

        @pl.when(j == 0)
        def _():
            acc_ref[...] = jnp.zeros_like(acc_ref)

        acc_ref[...] += _dot(a_ref[...], w_ref[...], dn)

        @pl.when(j == nb - 1)
        def _():
            o_ref[...] = acc_ref[...].astype(out_dtype)

    return pl.pallas_call(
        body,
        name=name,
        grid=(m // MM_TM, nb),
        in_specs=[
            pl.BlockSpec((None, MM_TM, k), lambda i, j: (j, i, 0)),
            pl.BlockSpec((None, w_rows, wc), lambda i, j: (j, 0, 0)),
        ] + [ANY_SPEC] * na,
        out_specs=pl.BlockSpec((MM_TM, n), lambda i, j: (i, 0)),
        out_shape=jax.ShapeDtypeStruct((m, n), out_dtype),
        scratch_shapes=[pltpu.VMEM((MM_TM, n), F32)],
        compiler_params=_params(("parallel", "arbitrary")),
    )(a, w, *after)


def _mm_tn(a, b, *, nb, out_dtype, name):
    m, ka = a.shape[-2:]
    n = b.shape[-1]
    steps = m // MM_TM

    def body(a_ref, b_ref, o_ref, acc_ref):
        i = pl.program_id(1)

        @pl.when(i == 0)
        def _():
            acc_ref[...] = jnp.zeros_like(acc_ref)

        acc_ref[...] += _dot(a_ref[...], b_ref[...], TN)

        @pl.when(i == steps - 1)
        def _():
            o_ref[...] = acc_ref[...].astype(out_dtype)

    def spec(arr, cols):
        if arr.ndim == 2:
            return pl.BlockSpec((MM_TM, cols), lambda j, i: (i, 0))
        return pl.BlockSpec((None, MM_TM, cols), lambda j, i: (j, i, 0))

    return pl.pallas_call(
        body,
        name=name,
        grid=(nb, steps),
        in_specs=[spec(a, ka), spec(b, n)],
        out_specs=pl.BlockSpec((None, ka, n), lambda j, i: (j, 0, 0)),
        out_shape=jax.ShapeDtypeStruct((nb, ka, n), out_dtype),
        scratch_shapes=[pltpu.VMEM((ka, n), F32)],
        compiler_params=_params(("parallel", "arbitrary")),
    )(a, b)


def _rstd(x):
    return lax.rsqrt(jnp.mean(x * x, axis=-1, keepdims=True) + RMS_EPS)


def _rms_bwd(x, g, dy):
    r = _rstd(x)
    u = dy * g
    dx = r * u - x * (r * r * r) * jnp.mean(u * x, axis=-1, keepdims=True)
    return dx, dy * x * r


def _row_spec(cols=D, tm=TM):
    return pl.BlockSpec((tm, cols), lambda i: (i, 0))


def _vec_spec(rows=1, cols=D):
    return pl.BlockSpec((rows, cols), lambda i: (0, 0))


def _rms_fwd(x, g, name):
    def body(x_ref, g_ref, o_ref):
        x = x_ref[...]
        o_ref[...] = (x * _rstd(x) * g_ref[...]).astype(BF16)

    return pl.pallas_call(
        body,
        name=name,
        grid=(LP // TM,),
        in_specs=[_row_spec(), _vec_spec()],
        out_specs=_row_spec(),
        out_shape=jax.ShapeDtypeStruct((LP, D), BF16),
        compiler_params=_params(("parallel",)),
    )(x, g)


def _resid_rms(h0, mix, g_post, g_next):
    def body(h0_ref, mix_ref, gp_ref, gn_ref, h1_ref, xn_ref):
        mix = mix_ref[...]
        h1 = h0_ref[...] + mix * _rstd(mix) * gp_ref[...]
        h1_ref[...] = h1
        xn_ref[...] = (h1 * _rstd(h1) * gn_ref[...]).astype(BF16)

    return pl.pallas_call(
        body,
        name="resid_rms",
        grid=(LP // TM,),
        in_specs=[_row_spec(), _row_spec(), _vec_spec(), _vec_spec()],
        out_specs=[_row_spec(), _row_spec()],
        out_shape=[jax.ShapeDtypeStruct((LP, D), F32), jax.ShapeDtypeStruct((LP, D), BF16)],
        compiler_params=_params(("parallel",)),
    )(h0, mix, g_post, g_next)


def _loss_head(h1, ffn, g_post, target):
    nblk = LP // QB

    def body(h1_ref, ffn_ref, g_ref, t_ref, dout_ref, dffn_ref, loss_ref, dg_ref):
        i = pl.program_id(0)

        @pl.when(i == 0)
        def _():
            loss_ref[...] = jnp.zeros_like(loss_ref)
            dg_ref[...] = jnp.zeros_like(dg_ref)

        ffn = ffn_ref[...]
        g = g_ref[...]
        out = h1_ref[...] + ffn * _rstd(ffn) * g
        err = jnp.where(i > 0, out - t_ref[...], 0.0)
        loss_ref[...] += 0.5 * jnp.sum(err * err) / D
        dout = err / D
        dout_ref[...] = dout
        dffn, dg = _rms_bwd(ffn, g, dout)
        dffn_ref[...] = dffn.astype(BF16)
        dg_ref[0:1, :] += jnp.sum(dg, axis=0, keepdims=True)

    return pl.pallas_call(
        body,
        name="loss_head",
        grid=(nblk,),
        in_specs=[
            _row_spec(tm=QB),
            _row_spec(tm=QB),
            _vec_spec(),
            pl.BlockSpec((QB, D), lambda i: (jnp.maximum(i - 1, 0), 0)),
        ],
        out_specs=[_row_spec(tm=QB), _row_spec(tm=QB), _vec_spec(8, 128), _vec_spec(8, D)],
        out_shape=[
            jax.ShapeDtypeStruct((LP, D), F32),
            jax.ShapeDtypeStruct((LP, D), BF16),
            jax.ShapeDtypeStruct((8, 128), F32),
            jax.ShapeDtypeStruct((8, D), F32),
        ],
        compiler_params=_params(("arbitrary",)),
    )(h1, ffn, g_post, target)


def _mid_bwd(dout, h1, dxn2, mix, g_post_mix, g_pre_ffn):
    def body(dout_ref, h1_ref, dxn_ref, mix_ref, gpm_ref, gpf_ref, dh1_ref, dmix_ref, dg_ref):
        i = pl.program_id(0)

        @pl.when(i == 0)
        def _():
            dg_ref[...] = jnp.zeros_like(dg_ref)

        dx, dg_ffn = _rms_bwd(h1_ref[...], gpf_ref[...], dxn_ref[...])
        dh1 = dout_ref[...] + dx
        dh1_ref[...] = dh1
        dmix, dg_mix = _rms_bwd(mix_ref[...], gpm_ref[...], dh1)
        dmix_ref[...] = dmix.astype(BF16)
        dg_ref[0:1, :] += jnp.sum(dg_mix, axis=0, keepdims=True)
        dg_ref[1:2, :] += jnp.sum(dg_ffn, axis=0, keepdims=True)

    return pl.pallas_call(
        body,
        name="mid_bwd",
        grid=(LP // TM,),
        in_specs=[_row_spec(), _row_spec(), _row_spec(), _row_spec(), _vec_spec(), _vec_spec()],
        out_specs=[_row_spec(), _row_spec(), _vec_spec(8, D)],
        out_shape=[
            jax.ShapeDtypeStruct((LP, D), F32),
            jax.ShapeDtypeStruct((LP, D), BF16),
            jax.ShapeDtypeStruct((8, D), F32),
        ],
        compiler_params=_params(("arbitrary",)),
    )(dout, h1, dxn2, mix, g_post_mix, g_pre_ffn)


def _first_bwd(dh1, h0, dxn1, g_pre_mix):
    def body(dh1_ref, h0_ref, dxn_ref, g_ref, dh0_ref, dg_ref):
        i = pl.program_id(0)

        @pl.when(i == 0)
        def _():
            dg_ref[...] = jnp.zeros_like(dg_ref)

        dx, dg = _rms_bwd(h0_ref[...], g_ref[...], dxn_ref[...])
        dh0_ref[...] = dh1_ref[...] + dx
        dg_ref[0:1, :] += jnp.sum(dg, axis=0, keepdims=True)

    return pl.pallas_call(
        body,
        name="first_bwd",
        grid=(LP // TM,),
        in_specs=[_row_spec(), _row_spec(), _row_spec(), _vec_spec()],
        out_specs=[_row_spec(), _vec_spec(8, D)],
        out_shape=[jax.ShapeDtypeStruct((LP, D), F32), jax.ShapeDtypeStruct((8, D), F32)],
        compiler_params=_params(("arbitrary",)),
    )(dh1, h0, dxn1, g_pre_mix)


def _prev_halo(i):
    return jnp.maximum(i * (TM // HALO) - 1, 0)


def _next_halo(i):
    return jnp.minimum((i + 1) * (TM // HALO), LP // HALO - 1)


def _down(x, s):
    return pltpu.roll(x, s, 0)


def _up(x, s):
    return pltpu.roll(x, x.shape[0] - s, 0)


def _conv_mix_fwd(hin, cw):
    def body(b_ref, c_ref, h_ref, cp_ref, hp_ref, w_ref, y_ref):
        i = pl.program_id(0)
        p = c_ref[...].astype(F32) * h_ref[...].astype(F32)
        pp = jnp.where(i > 0, cp_ref[...].astype(F32) * hp_ref[...].astype(F32), 0.0)
        ext = jnp.concatenate([pp, p], axis=0)
        w = [w_ref[t:t + 1, :] for t in range(3)]
        cv = w[2] * ext + w[1] * _down(ext, 1) + w[0] * _down(ext, 2)
        y_ref[...] = (b_ref[...].astype(F32) * cv[HALO:]).astype(BF16)

    def tile(s):
        return pl.BlockSpec((None, TM, D), lambda i: (s, i, 0))

    def prev(s):
        return pl.BlockSpec((None, HALO, D), lambda i: (s, _prev_halo(i), 0))

    return pl.pallas_call(
        body,
        name="conv_mix_fwd",
        grid=(LP // TM,),
        in_specs=[tile(0), tile(1), tile(2), prev(1), prev(2), _vec_spec(3, D)],
        out_specs=_row_spec(),
        out_shape=jax.ShapeDtypeStruct((LP, D), BF16),
        compiler_params=_params(("parallel",)),
    )(hin, hin, hin, hin, hin, cw)


def _conv_mix_bwd(hin, dy, cw, dhin):
    last = LP // TM - 1

    def body(b_ref, c_ref, h_ref, dy_ref, cp_ref, hp_ref, bn_ref, dyn_ref, w_ref, _, out_ref, dw_ref):
        i = pl.program_id(0)

        @pl.when(i == 0)
        def _():
            dw_ref[...] = jnp.zeros_like(dw_ref)

        b = b_ref[...].astype(F32)
        c = c_ref[...].astype(F32)
        h = h_ref[...].astype(F32)
        dy = dy_ref[...].astype(F32)
        w = [w_ref[t:t + 1, :] for t in range(3)]
        p = c * h
        pp = jnp.where(i > 0, cp_ref[...].astype(F32) * hp_ref[...].astype(F32), 0.0)
        ext = jnp.concatenate([pp, p], axis=0)
        p1 = _down(ext, 1)[HALO:]
        p2 = _down(ext, 2)[HALO:]
        cv = w[2] * p + w[1] * p1 + w[0] * p2
        out_ref[0] = (dy * cv).astype(BF16)
        dcv = dy * b
        dcvn = jnp.where(i < last, dyn_ref[...].astype(F32) * bn_ref[...].astype(F32), 0.0)
        dext = jnp.concatenate([dcv, dcvn], axis=0)
        dp = (w[2] * dext + w[1] * _up(dext, 1) + w[0] * _up(dext, 2))[:TM]
        out_ref[1] = (dp * h).astype(BF16)
        out_ref[2] = (dp * c).astype(BF16)
        dw_ref[0:1, :] += jnp.sum(dcv * p2, axis=0, keepdims=True)
        dw_ref[1:2, :] += jnp.sum(dcv * p1, axis=0, keepdims=True)
        dw_ref[2:3, :] += jnp.sum(dcv * p, axis=0, keepdims=True)

    def tile(s):
        return pl.BlockSpec((None, TM, D), lambda i: (s, i, 0))

    def prev(s):
        return pl.BlockSpec((None, HALO, D), lambda i: (s, _prev_halo(i), 0))

    return pl.pallas_call(
        body,
        name="conv_mix_bwd",
        grid=(LP // TM,),
        in_specs=[
            tile(0), tile(1), tile(2), _row_spec(),
            prev(1), prev(2),
            pl.BlockSpec((None, HALO, D), lambda i: (0, _next_halo(i), 0)),
            pl.BlockSpec((HALO, D), lambda i: (_next_halo(i), 0)),
            _vec_spec(3, D),
            pl.BlockSpec(memory_space=pl.ANY),
        ],
        out_specs=[pl.BlockSpec((3, TM, D), lambda i: (0, i, 0)), _vec_spec(8, D)],
        out_shape=[jax.ShapeDtypeStruct((N_DEV, LP, D), BF16), jax.ShapeDtypeStruct((8, D), F32)],
        input_output_aliases={9: 0},
        compiler_params=_params(("arbitrary",)),
    )(hin, hin, hin, dy, hin, hin, hin, dy, cw, dhin)


GELU_K = math.sqrt(2.0 / math.pi)
GELU_A = 0.044715


def _gelu_and_grad(x):
    x2 = x * x
    t = jnp.tanh(x * (GELU_K + (GELU_K * GELU_A) * x2))
    s = 0.5 + 0.5 * t
    grad = s * (1.0 + x * (1.0 - t) * (GELU_K + (3.0 * GELU_K * GELU_A) * x2))
    return x * s, grad


def _ffn_act_fwd(ug, cw4):
    def body(u_ref, g_ref, up_ref, w_ref, o_ref):
        i = pl.program_id(1)
        u = u_ref[...].astype(F32)
        up = jnp.where(i > 0, up_ref[...].astype(F32), 0.0)
        ext = jnp.concatenate([up, u], axis=0)
        w = [w_ref[t:t + 1, :] for t in range(3)]
        uc = (w[2] * ext + w[1] * _down(ext, 1) + w[0] * _down(ext, 2))[HALO:]
        gelu, _ = _gelu_and_grad(uc)
        o_ref[...] = (gelu * g_ref[...].astype(F32)).astype(BF16)

    return pl.pallas_call(
        body,
        name="ffn_act_fwd",
        grid=(N_FB, LP // TM),
        in_specs=[
            pl.BlockSpec((None, TM, FB), lambda j, i: (j, i, 0)),
            pl.BlockSpec((None, TM, FB), lambda j, i: (j + N_FB, i, 0)),
            pl.BlockSpec((None, HALO, FB), lambda j, i: (j, _prev_halo(i), 0)),
            pl.BlockSpec((None, 3, FB), lambda j, i: (j, 0, 0)),
        ],
        out_specs=pl.BlockSpec((None, TM, FB), lambda j, i: (j, i, 0)),
        out_shape=jax.ShapeDtypeStruct((N_FB, LP, FB), BF16),
        compiler_params=_params(("parallel", "parallel")),
    )(ug, ug, ug, cw4)


def _ffn_act_bwd(ug, dhid, cw4):
    last = LP // TM - 1
    n = TM + 2 * HALO

    def body(u_ref, g_ref, dh_ref, up_ref, un_ref, gn_ref, dhn_ref, w_ref, dug_ref, dw_ref):
        i = pl.program_id(1)

        @pl.when(i == 0)
        def _():
            dw_ref[...] = jnp.zeros_like(dw_ref)

        w = [w_ref[t:t + 1, :] for t in range(3)]
        u = u_ref[...].astype(F32)
        up = jnp.where(i > 0, up_ref[...].astype(F32), 0.0)
        ext = jnp.concatenate([up, u, un_ref[...].astype(F32)], axis=0)
        u1 = _down(ext, 1)
        u2 = _down(ext, 2)
        uc = w[2] * ext + w[1] * u1 + w[0] * u2
        gelu, ggrad = _gelu_and_grad(uc)
        zeros = jnp.zeros((HALO, FB), F32)
        gext = jnp.concatenate([zeros, g_ref[...].astype(F32), gn_ref[...].astype(F32)], axis=0)
        dhn = jnp.where(i < last, dhn_ref[...].astype(F32), 0.0)
        dhext = jnp.concatenate([zeros, dh_ref[...].astype(F32), dhn], axis=0)
        dug_ref[1] = (dhext * gelu)[HALO:HALO + TM].astype(BF16)
        duc = dhext * gext * ggrad
        du = w[2] * duc + w[1] * _up(duc, 1) + w[0] * _up(duc, 2)
        dug_ref[0] = du[HALO:HALO + TM].astype(BF16)
        row = lax.broadcasted_iota(jnp.int32, (n, 1), 0)
        own = jnp.where((row >= HALO) & (row < HALO + TM), duc, 0.0)
        dw_ref[0:1, :] += jnp.sum(own * u2, axis=0, keepdims=True)
        dw_ref[1:2, :] += jnp.sum(own * u1, axis=0, keepdims=True)
        dw_ref[2:3, :] += jnp.sum(own * ext, axis=0, keepdims=True)

    def tile(off):
        return pl.BlockSpec((None, TM, FB), lambda j, i: (j + off, i, 0))

    def nxt(off):
        return pl.BlockSpec((None, HALO, FB), lambda j, i: (j + off, _next_halo(i), 0))

    return pl.pallas_call(
        body,
        name="ffn_act_bwd",
        grid=(N_FB, LP // TM),
        in_specs=[
            tile(0), tile(N_FB), tile(0),
            pl.BlockSpec((None, HALO, FB), lambda j, i: (j, _prev_halo(i), 0)),
            nxt(0), nxt(N_FB), nxt(0),
            pl.BlockSpec((None, 3, FB), lambda j, i: (j, 0, 0)),
        ],
        out_specs=[
            pl.BlockSpec((2, None, TM, FB), lambda j, i: (0, j, i, 0)),
            pl.BlockSpec((None, 8, FB), lambda j, i: (j, 0, 0)),
        ],
        out_shape=[jax.ShapeDtypeStruct((2, N_FB, LP, FB), BF16), jax.ShapeDtypeStruct((N_FB, 8, FB), F32)],
        compiler_params=_params(("parallel", "arbitrary")),
    )(ug, ug, dhid, ug, ug, ug, dhid, cw4)


def _gate_fwd(bc, ba, hin, bgate):
    def body(bc_ref, ba_ref, gc_ref, ga_ref, b_ref, o_ref):
        b = b_ref[...]
        sc = jax.nn.sigmoid(gc_ref[...].astype(F32) + b[0:1])
        sa = jax.nn.sigmoid(ga_ref[...].astype(F32) + b[1:2])
        o_ref[...] = (sc * bc_ref[...].astype(F32) + sa * ba_ref[...].astype(F32)).astype(BF16)

    def tile(s):
        return pl.BlockSpec((None, TM, D), lambda i: (s, i, 0))

    return pl.pallas_call(
        body,
        name="gate_fwd",
        grid=(LP // TM,),
        in_specs=[_row_spec(), _row_spec(), tile(6), tile(7), _vec_spec(2, D)],
        out_specs=_row_spec(),
        out_shape=jax.ShapeDtypeStruct((LP, D), BF16),
        compiler_params=_params(("parallel",)),
    )(bc, ba, hin, hin, bgate)


def _gate_bwd(dm, bc, ba, hin, bgate):
    def body(dm_ref, bc_ref, ba_ref, gc_ref, ga_ref, b_ref, dbc_ref, dba_ref, dg_ref, db_ref):
        i = pl.program_id(0)

        @pl.when(i == 0)
        def _():
            db_ref[...] = jnp.zeros_like(db_ref)

        b = b_ref[...]
        dm = dm_ref[...].astype(F32)
        sc = jax.nn.sigmoid(gc_ref[...].astype(F32) + b[0:1])
        sa = jax.nn.sigmoid(ga_ref[...].astype(F32) + b[1:2])
        dbc_ref[...] = (dm * sc).astype(BF16)
        dba_ref[...] = (dm * sa).astype(BF16)
        dgc = dm * bc_ref[...].astype(F32) * sc * (1.0 - sc)
        dga = dm * ba_ref[...].astype(F32) * sa * (1.0 - sa)
        dg_ref[0] = dgc.astype(BF16)
        dg_ref[1] = dga.astype(BF16)
        db_ref[0:1, :] += jnp.sum(dgc, axis=0, keepdims=True)
        db_ref[1:2, :] += jnp.sum(dga, axis=0, keepdims=True)

    def tile(s):
        return pl.BlockSpec((None, TM, D), lambda i: (s, i, 0))

    return pl.pallas_call(
        body,
        name="gate_bwd",
        grid=(LP // TM,),
        in_specs=[_row_spec(), _row_spec(), _row_spec(), tile(6), tile(7), _vec_spec(2, D)],
        out_specs=[_row_spec(), _row_spec(), pl.BlockSpec((2, TM, D), lambda i: (3, i, 0)), _vec_spec(8, D)],
        out_shape=[jax.ShapeDtypeStruct((LP, D), BF16)] * 2
        + [jax.ShapeDtypeStruct((N_DEV, LP, D), BF16), jax.ShapeDtypeStruct((8, D), F32)],
        compiler_params=_params(("arbitrary",)),
    )(dm, bc, ba, hin, hin, bgate)


Z_LINEAR = 30.0


def _softplus(z):
    return jnp.maximum(z, jnp.log(1.0 + jnp.exp(jnp.minimum(z, Z_LINEAR))))


def _cumsum_matrix(inclusive, reverse):
    r = lax.broadcasted_iota(jnp.int32, (QB, 2 * QB), 0)
    c = lax.broadcasted_iota(jnp.int32, (QB, 2 * QB), 1)
    if reverse:
        tri = r > c
    elif inclusive:
        tri = r <= c
    else:
        tri = r < c
    return jnp.where((c >= QB) | tri, 1.0, 0.0).astype(BF16)


def _split_dot(x, m2):
    bits = lax.bitcast_convert_type(x, jnp.uint32) & jnp.uint32(0xFFFF0000)
    hi = lax.bitcast_convert_type(bits, F32)
    return _dot(jnp.concatenate([hi.astype(BF16), (x - hi).astype(BF16)], axis=1), m2)


def _stack_heads(x):
    return jnp.concatenate(_split_heads(x), axis=0)


def _block_mask(i, j, row0):
    row = lax.broadcasted_iota(jnp.int32, (AQ - row0, QB), 0) + (i * AQ + row0)
    col = lax.broadcasted_iota(jnp.int32, (AQ - row0, QB), 1) + j * QB
    return (col < row) & (col >= PAD)


def _key_block(ref, j):
    return ref[pl.ds(pl.multiple_of(j * QB, QB), QB), :]


def _split_heads(x):
    head_a = lax.broadcasted_iota(jnp.int32, x.shape, 1) < HEAD_LANES
    zero = jnp.zeros_like(x)
    return jnp.where(head_a, x, zero), jnp.where(head_a, zero, x)


def _attn_fwd(hin, ex_arrs, gathers):
    ne = len(ex_arrs)
    npair, nq = D // QB, LP // AQ

    def body(*refs):
        q_ref, k_ref, v_ref = refs[:3]
        o_ref, lt_ref = refs[3 + ne:5 + ne]
        c_sc, acc_sc = refs[5 + 2 * ne:7 + 2 * ne]
        p, i = pl.program_id(0), pl.program_id(1)

        def copies():
            return _exchange_copies(refs[3:3 + ne], refs[5 + ne:5 + 2 * ne], gathers, *refs[7 + 2 * ne:])

        @pl.when((p == 0) & (i == 0))
        def _():
            for cp in copies():
                cp.start()

        um = _cumsum_matrix(False, True)
        um2 = jnp.concatenate([um, um], axis=0)
        q = (q_ref[...].astype(F32) * SCALE).astype(BF16)
        c_sc[...] = jnp.zeros_like(c_sc)
        acc_sc[...] = jnp.zeros_like(acc_sc)

        def step(j, masked, row0=0):
            rows = slice(row0, AQ)
            z2 = _dot(q[rows], _stack_heads(_key_block(k_ref, j)), NT)
            mask = _block_mask(i, j, row0) if masked else None
            a2 = []
            for hd in range(2):
                z = z2[:, hd * QB:(hd + 1) * QB]
                sp = _softplus(z)
                r = _split_dot(jnp.where(mask, sp, 0.0) if masked else sp, um2)
                a = jnp.exp(z - sp - c_sc[hd, rows] - r[:, :QB])
                if masked:
                    a = jnp.where(mask, a, 0.0)
                a2.append(a.astype(BF16))
                c_sc[hd, rows] += r[:, QB:]
            acc_sc[rows] += _dot(jnp.concatenate(a2, axis=1), _stack_heads(_key_block(v_ref, j)))

        for t in reversed(range(KPQ)):
            step(KPQ * i + t, True, t * QB)

        @pl.loop(0, jnp.maximum(i - 1, 0))
        def _(t):
            for u in range(KPQ):
                step(KPQ * (i - t) - 1 - u, False)

        @pl.when(i > 0)
        def _():
            for u in reversed(range(1, KPQ)):
                step(u, False)
            step(0, True)

        head_a = lax.broadcasted_iota(jnp.int32, (AQ, QB), 1) < HEAD_LANES
        o_ref[...] = acc_sc[...].astype(BF16)
        lt_ref[...] = jnp.where(head_a, c_sc[0], c_sc[1])

        @pl.when((p == npair - 1) & (i == nq - 1))
        def _():
            for cp in copies():
                cp.wait()

    def seq(s):
        return pl.BlockSpec((None, LP, QB), lambda p, i: (s, 0, p))

    return pl.pallas_call(
        body,
        name="attn_fwd",
        grid=(npair, nq),
        in_specs=[pl.BlockSpec((None, AQ, QB), lambda p, i: (3, i, p)), seq(4), seq(5)] + [ANY_SPEC] * ne,
        out_specs=[pl.BlockSpec((AQ, QB), lambda p, i: (i, p))] * 2 + [ANY_SPEC] * ne,
        out_shape=[jax.ShapeDtypeStruct((LP, D), BF16), jax.ShapeDtypeStruct((LP, D), F32)]
        + _exchange_shapes(ex_arrs, gathers),
        scratch_shapes=[pltpu.VMEM((2, AQ, QB), F32), pltpu.VMEM((AQ, QB), F32)] + _exchange_sems(ne),
        compiler_params=_params(("arbitrary", "arbitrary")),
    )(hin, hin, hin, *ex_arrs)


def _attn_bwd(hin, do, lt, dhin, ex_arrs, gathers):
    ne = len(ex_arrs)
    npair, nq = D // QB, LP // AQ

    def body(*refs):
        q_ref, k_ref, v_ref, do_ref, lt_ref = refs[:5]
        out_ref = refs[6 + ne]
        psp_sc, pg_sc, dq_sc, dk_acc, dv_acc = refs[7 + 2 * ne:12 + 2 * ne]
        p, i = pl.program_id(0), pl.program_id(1)

        def copies():
            return _exchange_copies(refs[6:6 + ne], refs[7 + ne:7 + 2 * ne], gathers, *refs[12 + 2 * ne:])

        @pl.when((p == 0) & (i == 0))
        def _():
            for cp in copies():
                cp.start()

        @pl.when(i == 0)
        def _():
            dk_acc[...] = jnp.zeros_like(dk_acc)
            dv_acc[...] = jnp.zeros_like(dv_acc)

        um_sp = _cumsum_matrix(True, False)
        um_sp2 = jnp.concatenate([um_sp, um_sp], axis=0)
        um_g = _cumsum_matrix(False, False)
        q = (q_ref[...].astype(F32) * SCALE).astype(BF16)
        do = do_ref[...]
        q_t, do_t = q.T, do.T
        head_a = lax.broadcasted_iota(jnp.int32, (AQ, QB), 1) < HEAD_LANES
        dim_head_a = lax.broadcasted_iota(jnp.int32, (QB, QB), 0) < HEAD_LANES
        lt = lt_ref[...]
        lt_sw = pltpu.roll(lt, HEAD_LANES, 1)
        totals = (jnp.where(head_a, lt, lt_sw), jnp.where(head_a, lt_sw, lt))
        psp_sc[...] = jnp.zeros_like(psp_sc)
        pg_sc[...] = jnp.zeros_like(pg_sc)
        dq_sc[...] = jnp.zeros_like(dq_sc)

        def step(j, masked, row0=0):
            rows = slice(row0, AQ)
            k2 = _stack_heads(_key_block(k_ref, j))
            z2 = _dot(q[rows], k2, NT)
            da2 = _dot(do[rows], _stack_heads(_key_block(v_ref, j)), NT)
            mask = _block_mask(i, j, row0) if masked else None
            a2, dz2 = [], []
            for hd in range(2):
                z = z2[:, hd * QB:(hd + 1) * QB]
                sp = _softplus(z)
                r = _split_dot(jnp.where(mask, sp, 0.0) if masked else sp, um_sp2)
                a = jnp.exp(z - sp - (totals[hd][rows] - psp_sc[hd, rows] - r[:, :QB]))
                if masked:
                    a = jnp.where(mask, a, 0.0)
                g = a * da2[:, hd * QB:(hd + 1) * QB]
                rg = _dot(g.astype(BF16), um_g)
                dz = g - jnp.exp(z - sp) * (g + pg_sc[hd, rows] + rg[:, :QB])
                if masked:
                    dz = jnp.where(mask, dz, 0.0)
                a2.append(a.astype(BF16))
                dz2.append(dz.astype(BF16))
                psp_sc[hd, rows] += r[:, QB:]
                pg_sc[hd, rows] += rg[:, QB:]
            dz2 = jnp.concatenate(dz2, axis=1)
            dq_sc[rows] += _dot(dz2, k2)
            dk2 = _dot(q_t[:, rows], dz2)
            dv2 = _dot(do_t[:, rows], jnp.concatenate(a2, axis=1))
            dk_acc[j] += jnp.where(dim_head_a, dk2[:, :QB], dk2[:, QB:])
            dv_acc[j] += jnp.where(dim_head_a, dv2[:, :QB], dv2[:, QB:])

        @pl.when(i > 0)
        def _():
            step(0, True)
            for u in range(1, KPQ):
                step(u, False)

        @pl.loop(0, jnp.maximum(i - 1, 0))
        def _(t):
            for u in range(KPQ):
                step(KPQ * (t + 1) + u, False)

        for t in range(KPQ):
            step(KPQ * i + t, True, t * QB)

        out_ref[0, pl.ds(pl.multiple_of(i * AQ, AQ), AQ), :] = (dq_sc[...] * SCALE).astype(BF16)

        @pl.when(i == nq - 1)
        def _():
            @pl.loop(0, LP // QB)
            def _(b):
                keys = pl.ds(pl.multiple_of(b * QB, QB), QB)
                out_ref[1, keys, :] = dk_acc[b].T.astype(BF16)
                out_ref[2, keys, :] = dv_acc[b].T.astype(BF16)

        @pl.when((p == npair - 1) & (i == nq - 1))
        def _():
            for cp in copies():
                cp.wait()

    def seq(s):
        return pl.BlockSpec((None, LP, QB), lambda p, i: (s, 0, p))

    blk = pl.BlockSpec((AQ, QB), lambda p, i: (i, p))
    return pl.pallas_call(
        body,
        name="attn_bwd",
        grid=(npair, nq),
        in_specs=[pl.BlockSpec((None, AQ, QB), lambda p, i: (3, i, p)), seq(4), seq(5), blk, blk]
        + [ANY_SPEC] * (1 + ne),
        out_specs=[pl.BlockSpec((3, LP, QB), lambda p, i: (1, 0, p))] + [ANY_SPEC] * ne,
        out_shape=[jax.ShapeDtypeStruct((N_DEV, LP, D), BF16)] + _exchange_shapes(ex_arrs, gathers),
        input_output_aliases={5: 0},
        scratch_shapes=[pltpu.VMEM((2, AQ, QB), F32)] * 2 + [pltpu.VMEM((AQ, QB), F32)]
        + [pltpu.VMEM((LP // QB, QB, QB), F32)] * 2 + _exchange_sems(ne),
        compiler_params=_params(("arbitrary", "arbitrary")),
    )(hin, hin, hin, do, lt, dhin, *ex_arrs)


def _adamw_math(w, g, m, v):
    m_new = ADAM_B1 * m + (1.0 - ADAM_B1) * g
    v_new = ADAM_B2 * v + (1.0 - ADAM_B2) * jnp.square(g)
    m_hat = m_new / (1.0 - ADAM_B1 ** ADAM_STEP)
    v_hat = v_new / (1.0 - ADAM_B2 ** ADAM_STEP)
    return -ADAM_LR * (m_hat / (jnp.sqrt(v_hat) + ADAM_EPS) + ADAM_WD * w), m_new, v_new


def _adamw_small(ws, gs, ms, vs):
    n = len(ws)

    def body(*refs):
        for t in range(n):
            w_ref, g_ref, m_ref, v_ref = (refs[s * n + t] for s in range(4))
            d_ref, nm_ref, nv_ref = (refs[(4 + s) * n + t] for s in range(3))
            d_ref[...], nm_ref[...], nv_ref[...] = _adamw_math(w_ref[...], g_ref[...], m_ref[...], v_ref[...])

    vmem = pl.BlockSpec(memory_space=pltpu.VMEM)
    res = pl.pallas_call(
        body,
        name="adamw_small",
        in_specs=[vmem] * (4 * n),
        out_specs=[vmem] * (3 * n),
        out_shape=[jax.ShapeDtypeStruct(w.shape, F32) for w in ws] * 3,
    )(*ws, *gs, *ms, *vs)
    return res[:n], res[n:2 * n], res[2 * n:]


def _adamw(pieces, w, m, v, *, rows, row_off, tr, name):
    npieces, _, cols = pieces.shape
    ob = row_off // tr

    def body(p_ref, w_ref, m_ref, v_ref, g_ref, d_ref, nm_ref, nv_ref):
        g = p_ref[0].astype(F32)
        for s in range(1, npieces):
            g = g + p_ref[s].astype(F32)
        g_ref[...] = g
        d_ref[...], nm_ref[...], nv_ref[...] = _adamw_math(w_ref[...], g, m_ref[...], v_ref[...])

    spec = pl.BlockSpec((tr, cols), lambda i: (i, 0))
    return pl.pallas_call(
        body,
        name=name,
        grid=(rows // tr,),
        in_specs=[pl.BlockSpec((npieces, tr, cols), lambda i: (0, ob + i, 0)), spec, spec, spec],
        out_specs=[spec] * 4,
        out_shape=[jax.ShapeDtypeStruct((rows, cols), F32)] * 4,
        compiler_params=_params(("parallel",)),
    )(pieces, w, m, v)


def _sum_pieces(pieces, name):
    npieces, rows, cols = pieces.shape

    def body(p_ref, o_ref):
        acc = p_ref[0]
        for s in range(1, npieces):
            acc = acc + p_ref[s]
        o_ref[...] = acc

    return pl.pallas_call(
        body,
        name=name,
        in_specs=[pl.BlockSpec(memory_space=pltpu.VMEM)],
        out_specs=pl.BlockSpec(memory_space=pltpu.VMEM),
        out_shape=jax.ShapeDtypeStruct((rows, cols), pieces.dtype),
    )(pieces)


SMALL_ROWS = 48
CWF_PAD = 384
GRAD_ROWS = 80
G_META, G_PRE_MIX, G_MID, G_POST_FFN, G_CW_MIX, G_B_GATE, G_CW_FFN, G_LOSS = 0, 16, 24, 32, 40, 48, 56, 72


def kernel(x, meta_tokens, g_pre_mix, w_in, conv_w_mix, w_proj_conv, w_proj_attn, b_gate, w_out, g_post_mix, g_pre_ffn, w_up_gate, conv_w_ffn, w_down, g_post_ffn, loss_target, m_meta_tokens, m_g_pre_mix, m_w_in, m_conv_w_mix, m_w_proj_conv, m_w_proj_attn, m_b_gate, m_w_out, m_g_post_mix, m_g_pre_ffn, m_w_up_gate, m_conv_w_ffn, m_w_down, m_g_post_ffn, v_meta_tokens, v_g_pre_mix, v_w_in, v_conv_w_mix, v_w_proj_conv, v_w_proj_attn, v_b_gate, v_w_out, v_g_post_mix, v_g_pre_ffn, v_w_up_gate, v_conv_w_ffn, v_w_down, v_g_post_ffn):
    me = 4 * lax.axis_index("x") + 2 * lax.axis_index("y") + lax.axis_index("c")

    def rows_to(a, n):
        return jnp.pad(a, ((0, n - a.shape[0]), (0, 0)))

    small_shard = jnp.concatenate(
        [meta_tokens, rows_to(conv_w_mix[0], 8), rows_to(b_gate[0], 8),
         rows_to(jnp.pad(conv_w_ffn[0], ((0, 0), (0, CWF_PAD - R_DOWN))).reshape(9, 128), 16)], axis=0)
    wshard = jnp.concatenate([w_proj_conv[0], w_proj_attn[0], w_out[0], w_down[0]], axis=0).astype(BF16)
    w_in_all, small_all = _gather_two_level([w_in[0].astype(BF16), small_shard], "gather_in")

    def unshard(rows):
        return rows.transpose(1, 0, 2).reshape(rows.shape[1], N_DEV * rows.shape[2])

    meta = unshard(small_all[:, 0:16])
    cw_mix = unshard(small_all[:, 16:19])
    bgate = unshard(small_all[:, 24:26])
    cw_ffn = unshard(small_all[:, 32:41].reshape(N_DEV, 3, CWF_PAD)[:, :, :R_DOWN])
    cw4 = cw_ffn.reshape(3, N_FB, FB).transpose(1, 0, 2)
    h0 = jnp.concatenate([jnp.zeros((PAD, D), F32), meta, x[0]], axis=0)

    xn1 = _rms_fwd(h0, g_pre_mix, "rms_pre_mix")
    hin = _mm(xn1, w_in_all, w_rows=D, trans_w=False, out_dtype=BF16, name="mm_in")
    y_conv = _conv_mix_fwd(hin, cw_mix)
    o, lt, wpack, w_ug = _attn_fwd(hin, [wshard, w_up_gate[0].astype(BF16)], (True, True))
    w_pc = wpack[:, O_PC:O_PA].reshape(1, D, D)
    w_pa = wpack[:, O_PA:O_OUT].reshape(1, D, D)
    w_o = wpack[:, O_OUT:O_DOWN].reshape(1, D, D)
    w_dn = wpack[:, O_DOWN:].reshape(N_FB, FB, D)
    bc = _mm(y_conv, w_pc, w_rows=D, trans_w=False, out_dtype=BF16, name="mm_proj_conv")[0]
    ba = _mm(o, w_pa, w_rows=D, trans_w=False, out_dtype=BF16, name="mm_proj_attn")[0]
    merged = _gate_fwd(bc, ba, hin, bgate)
    mix = _mm(merged, w_o, w_rows=D, trans_w=False, out_dtype=F32, name="mm_out")[0]
    h1, xn2 = _resid_rms(h0, mix, g_post_mix, g_pre_ffn)
    ug = _mm(xn2, w_ug, w_rows=D, trans_w=False, out_dtype=BF16, name="mm_up_gate")
    hid = _ffn_act_fwd(ug, cw4)
    ffn = _mm_sum(hid, w_dn, w_rows=FB, trans_w=False, out_dtype=F32, name="mm_down")
    dout, dffn, loss8, dg_post_ffn = _loss_head(h1, ffn, g_post_ffn, loss_target[0])

    dhid = _mm(dffn, w_dn, w_rows=FB, trans_w=True, out_dtype=BF16, name="mm_down_dx")
    gw_dn = _mm_tn(hid, dffn, nb=N_FB, out_dtype=BF16, name="mm_down_dw")
    dug, dcw4 = _ffn_act_bwd(ug, dhid, cw4)
    dug = dug.reshape(2 * N_FB, LP, FB)
    dxn2 = _mm_sum(dug, w_ug, w_rows=D, trans_w=True, out_dtype=F32, name="mm_up_gate_dx")
    gw_ug = _mm_tn(xn2, dug, nb=N_DEV, out_dtype=BF16, name="mm_up_gate_dw")
    dh1, dmix, dg_mid = _mid_bwd(dout, h1, dxn2, mix, g_post_mix, g_pre_ffn)
    dmerged = _mm(dmix, w_o, w_rows=D, trans_w=True, out_dtype=BF16, name="mm_out_dx")[0]
    gw_out = _mm_tn(merged, dmix, nb=1, out_dtype=BF16, name="mm_out_dw")
    dbc, dba, dhin, db_gate = _gate_bwd(dmerged, bc, ba, hin, bgate)
    dy_conv = _mm(dbc, w_pc, w_rows=D, trans_w=True, out_dtype=BF16, name="mm_proj_conv_dx")[0]
    gw_pc = _mm_tn(y_conv, dbc, nb=1, out_dtype=BF16, name="mm_proj_conv_dw")
    do = _mm(dba, w_pa, w_rows=D, trans_w=True, out_dtype=BF16, name="mm_proj_attn_dx")[0]
    gw_pa = _mm_tn(o, dba, nb=1, out_dtype=BF16, name="mm_proj_attn_dw")
    dhin, dcw_mix = _conv_mix_bwd(hin, dy_conv, cw_mix, dhin)
    gpack = jnp.concatenate(
        [gw_pc.reshape(N_DEV, R_PROJ, D), gw_pa.reshape(N_DEV, R_PROJ, D), gw_out.reshape(N_DEV, R_PROJ, D),
         gw_dn.reshape(N_DEV, R_DOWN, D)], axis=1)
    dhin, rpack, rug = _attn_bwd(hin, do, lt, dhin, [gpack, gw_ug], (False, False))
    gw_in = _mm_tn(xn1, dhin, nb=N_DEV, out_dtype=BF16, name="mm_in_dw")
    send_sems, recv_sems, gw_thru, land_thru, token = _scatter_start(gw_in)
    dxn1 = _mm_sum(dhin, w_in_all, w_rows=D, trans_w=True, out_dtype=F32, name="mm_in_dx", after=token)
    dh0, dg_pre_mix = _first_bwd(dh1, h0, dxn1, g_pre_mix)
    dcw_ffn = dcw4[:, :3].transpose(1, 0, 2).reshape(3, D_FF)
    small = jnp.concatenate(
        [dh0[PAD:OFF], dg_pre_mix, dg_mid, dg_post_ffn, dcw_mix, db_gate,
         jnp.pad(dcw_ffn.reshape(-1), (0, 16 * D - 3 * D_FF)).reshape(16, D),
         jnp.pad(loss8, ((0, 0), (0, D - 128)))], axis=0)

    def big(pieces, w, m, v, rows, row_off, tr, name):
        g, d, nm, nv = _adamw(pieces, w[0], m[0], v[0], rows=rows, row_off=row_off, tr=tr, name=name)
        return g[None], d[None], nm[None], nv[None]

    r_pc = big(rpack, w_proj_conv, m_w_proj_conv, v_w_proj_conv, R_PROJ, O_PC, R_PROJ, "adamw_proj_conv")
    r_pa = big(rpack, w_proj_attn, m_w_proj_attn, v_w_proj_attn, R_PROJ, O_PA, R_PROJ, "adamw_proj_attn")
    r_out = big(rpack, w_out, m_w_out, v_w_out, R_PROJ, O_OUT, R_PROJ, "adamw_out")
    r_dn = big(rpack, w_down, m_w_down, v_w_down, R_DOWN, O_DOWN, 32, "adamw_down")
    r_ug = big(rug, w_up_gate, m_w_up_gate, v_w_up_gate, D, 0, 256, "adamw_up_gate")
    (rsmall,) = _exchange([small], (True,), "gather_small_grads",
                          after=(r_pc[1], r_pa[1], r_out[1], r_dn[1], r_ug[1]))
    gs = _sum_pieces(rsmall, "sum_small_grads")
    grad_x = dh0[OFF:]
    loss = gs[G_LOSS, 0]

    def cols(a, width):
        return lax.dynamic_slice_in_dim(a, me * width, width, axis=1)

    g_meta = cols(gs[G_META:G_META + N_META], 128)
    g_gpm, g_gff = gs[G_PRE_MIX:G_PRE_MIX + 1], gs[G_POST_FFN:G_POST_FFN + 1]
    g_gpo, g_gpf = gs[G_MID:G_MID + 1], gs[G_MID + 1:G_MID + 2]
    g_cwm = cols(gs[G_CW_MIX:G_CW_MIX + 3], 128)[None]
    g_bg = cols(gs[G_B_GATE:G_B_GATE + 2], 128)[None]
    g_cwf = cols(gs[G_CW_FFN:G_CW_FFN + 9].reshape(-1)[:3 * D_FF].reshape(3, D_FF), R_DOWN)[None]

    small_w = [meta_tokens, g_pre_mix, conv_w_mix, b_gate, g_post_mix, g_pre_ffn, conv_w_ffn, g_post_ffn]
    small_g = [g_meta, g_gpm, g_cwm, g_bg, g_gpo, g_gpf, g_cwf, g_gff]
    small_m = [m_meta_tokens, m_g_pre_mix, m_conv_w_mix, m_b_gate, m_g_post_mix, m_g_pre_ffn, m_conv_w_ffn, m_g_post_ffn]
    small_v = [v_meta_tokens, v_g_pre_mix, v_conv_w_mix, v_b_gate, v_g_post_mix, v_g_pre_ffn, v_conv_w_ffn, v_g_post_ffn]

    s_g = small_g
    s_d, s_m, s_v = _adamw_small(small_w, small_g, small_m, small_v)

    gw_done, landed = _scatter_wait(send_sems, recv_sems, gw_thru, land_thru, (s_d[0],))
    rin = lax.dynamic_update_index_in_dim(landed, lax.dynamic_index_in_dim(gw_done, me, 0, keepdims=False), me, 0)
    r_in = big(rin, w_in, m_w_in, v_w_in, D, 0, 256, "adamw_in")

    def ordered(k, smalls):
        meta, gpm, cwm, bg, gpo, gpf, cwf, gff = smalls
        return [meta, gpm, r_in[k], cwm, r_pc[k], r_pa[k], bg, r_out[k], gpo, gpf, r_ug[k], cwf, r_dn[k], gff]

    return (loss, grad_x[None], *ordered(0, s_g), *ordered(1, s_d), *ordered(2, s_m), *ordered(3, s_v))
```

```python
import functools
import math

import jax
import jax.numpy as jnp
from jax import lax
from jax.experimental import pallas as pl
from jax.experimental.pallas import tpu as pltpu

F32 = jnp.float32
BF16 = jnp.bfloat16

D = 1024
SEQ = 4096
N_META = 16
PAD = 112
OFF = PAD + N_META
LP = OFF + SEQ
QB = 128
AQ = 384
KPQ = AQ // QB
TM = 384
MM_TM = 1408
HALO = 16
N_DEV = 8
D_FF = 2816
FB = 704
N_FB = D_FF // FB
RMS_EPS = 1e-6
SCALE = 0.125
HEAD_LANES = 64
VMEM_LIMIT = 56 * 1024 * 1024

ADAM_LR = 0.001
ADAM_B1 = 0.9
ADAM_B2 = 0.999
ADAM_EPS = 1e-08
ADAM_WD = 0.01
ADAM_STEP = 10

R_PROJ, R_DOWN = 128, 352
O_PC = 0
O_PA = O_PC + R_PROJ
O_OUT = O_PA + R_PROJ
O_DOWN = O_OUT + R_PROJ
R_PACK = O_DOWN + R_DOWN

NT = (((1,), (1,)), ((), ()))
NN = (((1,), (0,)), ((), ()))
TN = (((0,), (0,)), ((), ()))


def _params(sem):
    return pltpu.CompilerParams(dimension_semantics=sem, vmem_limit_bytes=VMEM_LIMIT)


def _dot(a, b, dn=NN):
    return lax.dot_general(a, b, dn, preferred_element_type=F32)


def _exchange_copies(ins, outs, gathers, send_sems, recv_sems, loc_sems):
    x, y, c = lax.axis_index("x"), lax.axis_index("y"), lax.axis_index("c")
    me = 4 * x + 2 * y + c
    copies = []
    for a, gather in enumerate(gathers):
        copies.append(pltpu.make_async_copy(ins[a] if gather else ins[a].at[me], outs[a].at[me], loc_sems.at[a]))
    for k in range(1, N_DEV):
        px = 1 - x if k & 4 else x
        py = 1 - y if k & 2 else y
        pc = 1 - c if k & 1 else c
        peer = 4 * px + 2 * py + pc
        for a, gather in enumerate(gathers):
            copies.append(pltpu.make_async_remote_copy(
                src_ref=ins[a] if gather else ins[a].at[peer],
                dst_ref=outs[a].at[me],
                send_sem=send_sems.at[a * (N_DEV - 1) + k - 1],
                recv_sem=recv_sems.at[a * (N_DEV - 1) + k - 1],
                device_id=(px, py, pc),
                device_id_type=pl.DeviceIdType.MESH,
            ))
    return copies


def _exchange_shapes(arrs, gathers):
    return [jax.ShapeDtypeStruct((N_DEV,) + (a.shape if g else a.shape[1:]), a.dtype) for a, g in zip(arrs, gathers)]


def _exchange_sems(n):
    return [pltpu.SemaphoreType.DMA((n * (N_DEV - 1),)), pltpu.SemaphoreType.DMA((n * (N_DEV - 1),)),
            pltpu.SemaphoreType.DMA((n,))]


ANY_SPEC = pl.BlockSpec(memory_space=pl.ANY)


def _gather_two_level(arrs, name):
    n = len(arrs)
    per = 7

    def body(*refs):
        ins, outs = refs[:n], refs[n:2 * n]
        send_sems, recv_sems, loc_sems = refs[2 * n:]
        x, y, c = lax.axis_index("x"), lax.axis_index("y"), lax.axis_index("c")
        me, sibling = (x, y, c), (x, y, 1 - c)
        chips = [(1 - x, y), (x, 1 - y), (1 - x, 1 - y)]

        def copy(a, k, block, to, src=None):
            place = outs[a].at[4 * block[0] + 2 * block[1] + block[2]]
            return pltpu.make_async_remote_copy(
                src_ref=place if src is None else src, dst_ref=place,
                send_sem=send_sems.at[a * per + k], recv_sem=recv_sems.at[a * per + k],
                device_id=to, device_id_type=pl.DeviceIdType.MESH)

        mine = [pltpu.make_async_copy(ins[a], outs[a].at[4 * x + 2 * y + c], loc_sems.at[a]) for a in range(n)]
        first = [copy(a, 0, me, sibling, src=ins[a]) for a in range(n)]
        first += [copy(a, 1 + j, me, (*chip, c), src=ins[a]) for j, chip in enumerate(chips) for a in range(n)]
        for cp in mine + first:
            cp.start()
        passed = []
        for j, chip in enumerate(chips):
            for a in range(n):
                copy(a, 1 + j, (*chip, c), me).wait_recv()
                passed.append(copy(a, 4 + j, (*chip, c), sibling))
                passed[-1].start()
        for a in range(n):
            copy(a, 0, sibling, me).wait_recv()
        for j, chip in enumerate(chips):
            for a in range(n):
                copy(a, 4 + j, (*chip, 1 - c), me).wait_recv()
        for cp in first + passed:
            cp.wait_send()
        for cp in mine:
            cp.wait()

    return pl.pallas_call(
        body,
        name=name,
        out_shape=_exchange_shapes(arrs, (True,) * n),
        in_specs=[ANY_SPEC] * n,
        out_specs=[ANY_SPEC] * n,
        scratch_shapes=_exchange_sems(n),
    )(*arrs)


HBM_SPEC = pl.BlockSpec(memory_space=pltpu.HBM)
SEM_SPEC = pl.BlockSpec(memory_space=pltpu.SEMAPHORE)
DATAFLOW = pltpu.SideEffectType.DATAFLOW_SIDE_EFFECTING


def _scatter_copies(g_ref, land_ref, send_sems, recv_sems):
    x, y, c = lax.axis_index("x"), lax.axis_index("y"), lax.axis_index("c")
    me = 4 * x + 2 * y + c
    copies = []
    for k in range(1, N_DEV):
        px = 1 - x if k & 4 else x
        py = 1 - y if k & 2 else y
        pc = 1 - c if k & 1 else c
        copies.append(pltpu.make_async_remote_copy(
            src_ref=g_ref.at[4 * px + 2 * py + pc], dst_ref=land_ref.at[me],
            send_sem=send_sems.at[k - 1], recv_sem=recv_sems.at[k - 1],
            device_id=(px, py, pc), device_id_type=pl.DeviceIdType.MESH))
    return copies


def _scatter_start(g):
    def body(g_ref, land_ref, send_sems, recv_sems, g_thru, land_thru, token):
        for cp in _scatter_copies(g_ref, land_ref, send_sems, recv_sems):
            cp.start()
        token[...] = jnp.zeros_like(token)

    return pl.pallas_call(
        body,
        name="scatter_in_start",
        out_shape=(pltpu.SemaphoreType.DMA((N_DEV - 1,)), pltpu.SemaphoreType.DMA((N_DEV - 1,)),
                   pltpu.HBM(g.shape, g.dtype), pltpu.HBM(g.shape, g.dtype), jax.ShapeDtypeStruct((8, 128), F32)),
        in_specs=(HBM_SPEC, HBM_SPEC),
        out_specs=(SEM_SPEC, SEM_SPEC, HBM_SPEC, HBM_SPEC, pl.BlockSpec(memory_space=pltpu.VMEM)),
        input_output_aliases={0: 2, 1: 3},
        compiler_params=pltpu.CompilerParams(has_side_effects=DATAFLOW),
    )(pltpu.with_memory_space_constraint(g, pltpu.HBM),
      pltpu.with_memory_space_constraint(lax.empty(g.shape, g.dtype), pltpu.HBM))


def _scatter_wait(send_sems, recv_sems, g_thru, land_thru, after):
    def body(g_ref, land_ref, send_sems, recv_sems, *_):
        for cp in _scatter_copies(g_ref, land_ref, send_sems, recv_sems):
            cp.wait_send()
            cp.wait_recv()

    return pl.pallas_call(
        body,
        name="scatter_in_wait",
        out_shape=(pltpu.HBM(g_thru.shape, g_thru.dtype), pltpu.HBM(g_thru.shape, g_thru.dtype)),
        in_specs=(HBM_SPEC, HBM_SPEC, SEM_SPEC, SEM_SPEC) + (ANY_SPEC,) * len(after),
        out_specs=(HBM_SPEC, HBM_SPEC),
        input_output_aliases={0: 0, 1: 1},
        compiler_params=pltpu.CompilerParams(has_side_effects=DATAFLOW),
    )(g_thru, land_thru, send_sems, recv_sems, *after)


def _exchange(arrs, gathers, name, after=()):
    n, na = len(arrs), len(after)

    def body(*refs):
        copies = _exchange_copies(refs[:n], refs[n + na:2 * n + na], gathers, *refs[2 * n + na:])
        for cp in copies:
            cp.start()
        for cp in copies:
            cp.wait()

    return pl.pallas_call(
        body,
        name=name,
        out_shape=_exchange_shapes(arrs, gathers),
        in_specs=[ANY_SPEC] * (n + na),
        out_specs=[ANY_SPEC] * n,
        scratch_shapes=_exchange_sems(n),
    )(*arrs, *after)


def _mm(a, w, *, w_rows, trans_w, out_dtype, name):
    nb, _, wc = w.shape
    m, k = a.shape[-2:]
    n = w_rows if trans_w else wc
    dn = NT if trans_w else NN

    def body(a_ref, w_ref, o_ref):
        o_ref[...] = _dot(a_ref[...], w_ref[...], dn).astype(out_dtype)

    if a.ndim == 2:
        a_spec = pl.BlockSpec((MM_TM, k), lambda j, i: (i, 0))
    else:
        a_spec = pl.BlockSpec((None, MM_TM, k), lambda j, i: (j, i, 0))
    return pl.pallas_call(
        body,
        name=name,
        grid=(nb, m // MM_TM),
        in_specs=[a_spec, pl.BlockSpec((None, w_rows, wc), lambda j, i: (j, 0, 0))],
        out_specs=pl.BlockSpec((None, MM_TM, n), lambda j, i: (j, i, 0)),
        out_shape=jax.ShapeDtypeStruct((nb, m, n), out_dtype),
        compiler_params=_params(("parallel", "parallel")),
    )(a, w)


def _mm_sum(a, w, *, w_rows, trans_w, out_dtype, name, after=()):
    nb, m, k = a.shape
    wc = w.shape[2]
    n = w_rows if trans_w else wc
    dn = NT if trans_w else NN
    after = tuple(after) if isinstance(after, (tuple, list)) else (after,)
    na = len(after)

    def body(*refs):
        a_ref, w_ref, o_ref, acc_ref = refs[0], refs[1], refs[2 + na], refs[3 + na]
        j = pl.program_id(1)

        @pl.when(j == 0)
        def _():
            acc_ref[...] = jnp.zeros_like(acc_ref)

        acc_ref[...] += _dot(a_ref[...], w_ref[...], dn)

        @pl.when(j == nb - 1)
        def _():
            o_ref[...] = acc_ref[...].astype(out_dtype)

    return pl.pallas_call(
        body,
        name=name,
        grid=(m // MM_TM, nb),
        in_specs=[
            pl.BlockSpec((None, MM_TM, k), lambda i, j: (j, i, 0)),
            pl.BlockSpec((None, w_rows, wc), lambda i, j: (j, 0, 0)),
        ] + [ANY_SPEC] * na,
        out_specs=pl.BlockSpec((MM_TM, n), lambda i, j: (i, 0)),
        out_shape=jax.ShapeDtypeStruct((m, n), out_dtype),
        scratch_shapes=[pltpu.VMEM((MM_TM, n), F32)],
        compiler_params=_params(("parallel", "arbitrary")),
    )(a, w, *after)


def _mm_tn(a, b, *, nb, out_dtype, name):
    m, ka = a.shape[-2:]
    n = b.shape[-1]
    steps = m // MM_TM

    def body(a_ref, b_ref, o_ref, acc_ref):
        i = pl.program_id(1)

        @pl.when(i == 0)
        def _():
            acc_ref[...] = jnp.zeros_like(acc_ref)

        acc_ref[...] += _dot(a_ref[...], b_ref[...], TN)

        @pl.when(i == steps - 1)
        def _():
            o_ref[...] = acc_ref[...].astype(out_dtype)

    def spec(arr, cols):
        if arr.ndim == 2:
            return pl.BlockSpec((MM_TM, cols), lambda j, i: (i, 0))
        return pl.BlockSpec((None, MM_TM, cols), lambda j, i: (j, i, 0))

    return pl.pallas_call(
        body,
        name=name,
        grid=(nb, steps),
        in_specs=[spec(a, ka), spec(b, n)],
        out_specs=pl.BlockSpec((None, ka, n), lambda j, i: (j, 0, 0)),
        out_shape=jax.ShapeDtypeStruct((nb, ka, n), out_dtype),
        scratch_shapes=[pltpu.VMEM((ka, n), F32)],
        compiler_params=_params(("parallel", "arbitrary")),
    )(a, b)


def _rstd(x):
    return lax.rsqrt(jnp.mean(x * x, axis=-1, keepdims=True) + RMS_EPS)


def _rms_bwd(x, g, dy):
    r = _rstd(x)
    u = dy * g
    dx = r * u - x * (r * r * r) * jnp.mean(u * x, axis=-1, keepdims=True)
    return dx, dy * x * r


def _row_spec(cols=D, tm=TM):
    return pl.BlockSpec((tm, cols), lambda i: (i, 0))


def _vec_spec(rows=1, cols=D):
    return pl.BlockSpec((rows, cols), lambda i: (0, 0))


def _rms_fwd(x, g, name):
    def body(x_ref, g_ref, o_ref):
        x = x_ref[...]
        o_ref[...] = (x * _rstd(x) * g_ref[...]).astype(BF16)

    return pl.pallas_call(
        body,
        name=name,
        grid=(LP // TM,),
        in_specs=[_row_spec(), _vec_spec()],
        out_specs=_row_spec(),
        out_shape=jax.ShapeDtypeStruct((LP, D), BF16),
        compiler_params=_params(("parallel",)),
    )(x, g)


def _resid_rms(h0, mix, g_post, g_next):
    def body(h0_ref, mix_ref, gp_ref, gn_ref, h1_ref, xn_ref):
        mix = mix_ref[...]
        h1 = h0_ref[...] + mix * _rstd(mix) * gp_ref[...]
        h1_ref[...] = h1
        xn_ref[...] = (h1 * _rstd(h1) * gn_ref[...]).astype(BF16)

    return pl.pallas_call(
        body,
        name="resid_rms",
        grid=(LP // TM,),
        in_specs=[_row_spec(), _row_spec(), _vec_spec(), _vec_spec()],
        out_specs=[_row_spec(), _row_spec()],
        out_shape=[jax.ShapeDtypeStruct((LP, D), F32), jax.ShapeDtypeStruct((LP, D), BF16)],
        compiler_params=_params(("parallel",)),
    )(h0, mix, g_post, g_next)


def _loss_head(h1, ffn, g_post, target):
    nblk = LP // QB

    def body(h1_ref, ffn_ref, g_ref, t_ref, dout_ref, dffn_ref, loss_ref, dg_ref):
        i = pl.program_id(0)

        @pl.when(i == 0)
        def _():
            loss_ref[...] = jnp.zeros_like(loss_ref)
            dg_ref[...] = jnp.zeros_like(dg_ref)

        ffn = ffn_ref[...]
        g = g_ref[...]
        out = h1_ref[...] + ffn * _rstd(ffn) * g
        err = jnp.where(i > 0, out - t_ref[...], 0.0)
        loss_ref[...] += 0.5 * jnp.sum(err * err) / D
        dout = err / D
        dout_ref[...] = dout
        dffn, dg = _rms_bwd(ffn, g, dout)
        dffn_ref[...] = dffn.astype(BF16)
        dg_ref[0:1, :] += jnp.sum(dg, axis=0, keepdims=True)

    return pl.pallas_call(
        body,
        name="loss_head",
        grid=(nblk,),
        in_specs=[
            _row_spec(tm=QB),
            _row_spec(tm=QB),
            _vec_spec(),
            pl.BlockSpec((QB, D), lambda i: (jnp.maximum(i - 1, 0), 0)),
        ],
        out_specs=[_row_spec(tm=QB), _row_spec(tm=QB), _vec_spec(8, 128), _vec_spec(8, D)],
        out_shape=[
            jax.ShapeDtypeStruct((LP, D), F32),
            jax.ShapeDtypeStruct((LP, D), BF16),
            jax.ShapeDtypeStruct((8, 128), F32),
            jax.ShapeDtypeStruct((8, D), F32),
        ],
        compiler_params=_params(("arbitrary",)),
    )(h1, ffn, g_post, target)


def _mid_bwd(dout, h1, dxn2, mix, g_post_mix, g_pre_ffn):
    def body(dout_ref, h1_ref, dxn_ref, mix_ref, gpm_ref, gpf_ref, dh1_ref, dmix_ref, dg_ref):
        i = pl.program_id(0)

        @pl.when(i == 0)
        def _():
            dg_ref[...] = jnp.zeros_like(dg_ref)

        dx, dg_ffn = _rms_bwd(h1_ref[...], gpf_ref[...], dxn_ref[...])
        dh1 = dout_ref[...] + dx
        dh1_ref[...] = dh1
        dmix, dg_mix = _rms_bwd(mix_ref[...], gpm_ref[...], dh1)
        dmix_ref[...] = dmix.astype(BF16)
        dg_ref[0:1, :] += jnp.sum(dg_mix, axis=0, keepdims=True)
        dg_ref[1:2, :] += jnp.sum(dg_ffn, axis=0, keepdims=True)

    return pl.pallas_call(
        body,
        name="mid_bwd",
        grid=(LP // TM,),
        in_specs=[_row_spec(), _row_spec(), _row_spec(), _row_spec(), _vec_spec(), _vec_spec()],
        out_specs=[_row_spec(), _row_spec(), _vec_spec(8, D)],
        out_shape=[
            jax.ShapeDtypeStruct((LP, D), F32),
            jax.ShapeDtypeStruct((LP, D), BF16),
            jax.ShapeDtypeStruct((8, D), F32),
        ],
        compiler_params=_params(("arbitrary",)),
    )(dout, h1, dxn2, mix, g_post_mix, g_pre_ffn)


def _first_bwd(dh1, h0, dxn1, g_pre_mix):
    def body(dh1_ref, h0_ref, dxn_ref, g_ref, dh0_ref, dg_ref):
        i = pl.program_id(0)

        @pl.when(i == 0)
        def _():
            dg_ref[...] = jnp.zeros_like(dg_ref)

        dx, dg = _rms_bwd(h0_ref[...], g_ref[...], dxn_ref[...])
        dh0_ref[...] = dh1_ref[...] + dx
        dg_ref[0:1, :] += jnp.sum(dg, axis=0, keepdims=True)

    return pl.pallas_call(
        body,
        name="first_bwd",
        grid=(LP // TM,),
        in_specs=[_row_spec(), _row_spec(), _row_spec(), _vec_spec()],
        out_specs=[_row_spec(), _vec_spec(8, D)],
        out_shape=[jax.ShapeDtypeStruct((LP, D), F32), jax.ShapeDtypeStruct((8, D), F32)],
        compiler_params=_params(("arbitrary",)),
    )(dh1, h0, dxn1, g_pre_mix)


def _prev_halo(i):
    return jnp.maximum(i * (TM // HALO) - 1, 0)


def _next_halo(i):
    return jnp.minimum((i + 1) * (TM // HALO), LP // HALO - 1)


def _down(x, s):
    return pltpu.roll(x, s, 0)


def _up(x, s):
    return pltpu.roll(x, x.shape[0] - s, 0)


def _conv_mix_fwd(hin, cw):
    def body(b_ref, c_ref, h_ref, cp_ref, hp_ref, w_ref, y_ref):
        i = pl.program_id(0)
        p = c_ref[...].astype(F32) * h_ref[...].astype(F32)
        pp = jnp.where(i > 0, cp_ref[...].astype(F32) * hp_ref[...].astype(F32), 0.0)
        ext = jnp.concatenate([pp, p], axis=0)
        w = [w_ref[t:t + 1, :] for t in range(3)]
        cv = w[2] * ext + w[1] * _down(ext, 1) + w[0] * _down(ext, 2)
        y_ref[...] = (b_ref[...].astype(F32) * cv[HALO:]).astype(BF16)

    def tile(s):
        return pl.BlockSpec((None, TM, D), lambda i: (s, i, 0))

    def prev(s):
        return pl.BlockSpec((None, HALO, D), lambda i: (s, _prev_halo(i), 0))

    return pl.pallas_call(
        body,
        name="conv_mix_fwd",
        grid=(LP // TM,),
        in_specs=[tile(0), tile(1), tile(2), prev(1), prev(2), _vec_spec(3, D)],
        out_specs=_row_spec(),
        out_shape=jax.ShapeDtypeStruct((LP, D), BF16),
        compiler_params=_params(("parallel",)),
    )(hin, hin, hin, hin, hin, cw)


def _conv_mix_bwd(hin, dy, cw, dhin):
    last = LP // TM - 1

    def body(b_ref, c_ref, h_ref, dy_ref, cp_ref, hp_ref, bn_ref, dyn_ref, w_ref, _, out_ref, dw_ref):
        i = pl.program_id(0)

        @pl.when(i == 0)
        def _():
            dw_ref[...] = jnp.zeros_like(dw_ref)

        b = b_ref[...].astype(F32)
        c = c_ref[...].astype(F32)
        h = h_ref[...].astype(F32)
        dy = dy_ref[...].astype(F32)
        w = [w_ref[t:t + 1, :] for t in range(3)]
        p = c * h
        pp = jnp.where(i > 0, cp_ref[...].astype(F32) * hp_ref[...].astype(F32), 0.0)
        ext = jnp.concatenate([pp, p], axis=0)
        p1 = _down(ext, 1)[HALO:]
        p2 = _down(ext, 2)[HALO:]
        cv = w[2] * p + w[1] * p1 + w[0] * p2
        out_ref[0] = (dy * cv).astype(BF16)
        dcv = dy * b
        dcvn = jnp.where(i < last, dyn_ref[...].astype(F32) * bn_ref[...].astype(F32), 0.0)
        dext = jnp.concatenate([dcv, dcvn], axis=0)
        dp = (w[2] * dext + w[1] * _up(dext, 1) + w[0] * _up(dext, 2))[:TM]
        out_ref[1] = (dp * h).astype(BF16)
        out_ref[2] = (dp * c).astype(BF16)
        dw_ref[0:1, :] += jnp.sum(dcv * p2, axis=0, keepdims=True)
        dw_ref[1:2, :] += jnp.sum(dcv * p1, axis=0, keepdims=True)
        dw_ref[2:3, :] += jnp.sum(dcv * p, axis=0, keepdims=True)

    def tile(s):
        return pl.BlockSpec((None, TM, D), lambda i: (s, i, 0))

    def prev(s):
        return pl.BlockSpec((None, HALO, D), lambda i: (s, _prev_halo(i), 0))

    return pl.pallas_call(
        body,
        name="conv_mix_bwd",
        grid=(LP // TM,),
        in_specs=[
            tile(0), tile(1), tile(2), _row_spec(),
            prev(1), prev(2),
            pl.BlockSpec((None, HALO, D), lambda i: (0, _next_halo(i), 0)),
            pl.BlockSpec((HALO, D), lambda i: (_next_halo(i), 0)),
            _vec_spec(3, D),
            pl.BlockSpec(memory_space=pl.ANY),
        ],
        out_specs=[pl.BlockSpec((3, TM, D), lambda i: (0, i, 0)), _vec_spec(8, D)],
        out_shape=[jax.ShapeDtypeStruct((N_DEV, LP, D), BF16), jax.ShapeDtypeStruct((8, D), F32)],
        input_output_aliases={9: 0},
        compiler_params=_params(("arbitrary",)),
    )(hin, hin, hin, dy, hin, hin, hin, dy, cw, dhin)


GELU_K = math.sqrt(2.0 / math.pi)
GELU_A = 0.044715


def _gelu_and_grad(x):
    x2 = x * x
    t = jnp.tanh(x * (GELU_K + (GELU_K * GELU_A) * x2))
    s = 0.5 + 0.5 * t
    grad = s * (1.0 + x * (1.0 - t) * (GELU_K + (3.0 * GELU_K * GELU_A) * x2))
    return x * s, grad


def _ffn_act_fwd(ug, cw4):
    def body(u_ref, g_ref, up_ref, w_ref, o_ref):
        i = pl.program_id(1)
        u = u_ref[...].astype(F32)
        up = jnp.where(i > 0, up_ref[...].astype(F32), 0.0)
        ext = jnp.concatenate([up, u], axis=0)
        w = [w_ref[t:t + 1, :] for t in range(3)]
        uc = (w[2] * ext + w[1] * _down(ext, 1) + w[0] * _down(ext, 2))[HALO:]
        gelu, _ = _gelu_and_grad(uc)
        o_ref[...] = (gelu * g_ref[...].astype(F32)).astype(BF16)

    return pl.pallas_call(
        body,
        name="ffn_act_fwd",
        grid=(N_FB, LP // TM),
        in_specs=[
            pl.BlockSpec((None, TM, FB), lambda j, i: (j, i, 0)),
            pl.BlockSpec((None, TM, FB), lambda j, i: (j + N_FB, i, 0)),
            pl.BlockSpec((None, HALO, FB), lambda j, i: (j, _prev_halo(i), 0)),
            pl.BlockSpec((None, 3, FB), lambda j, i: (j, 0, 0)),
        ],
        out_specs=pl.BlockSpec((None, TM, FB), lambda j, i: (j, i, 0)),
        out_shape=jax.ShapeDtypeStruct((N_FB, LP, FB), BF16),
        compiler_params=_params(("parallel", "parallel")),
    )(ug, ug, ug, cw4)


def _ffn_act_bwd(ug, dhid, cw4):
    last = LP // TM - 1
    n = TM + 2 * HALO

    def body(u_ref, g_ref, dh_ref, up_ref, un_ref, gn_ref, dhn_ref, w_ref, dug_ref, dw_ref):
        i = pl.program_id(1)

        @pl.when(i == 0)
        def _():
            dw_ref[...] = jnp.zeros_like(dw_ref)

        w = [w_ref[t:t + 1, :] for t in range(3)]
        u = u_ref[...].astype(F32)
        up = jnp.where(i > 0, up_ref[...].astype(F32), 0.0)
        ext = jnp.concatenate([up, u, un_ref[...].astype(F32)], axis=0)
        u1 = _down(ext, 1)
        u2 = _down(ext, 2)
        uc = w[2] * ext + w[1] * u1 + w[0] * u2
        gelu, ggrad = _gelu_and_grad(uc)
        zeros = jnp.zeros((HALO, FB), F32)
        gext = jnp.concatenate([zeros, g_ref[...].astype(F32), gn_ref[...].astype(F32)], axis=0)
        dhn = jnp.where(i < last, dhn_ref[...].astype(F32), 0.0)
        dhext = jnp.concatenate([zeros, dh_ref[...].astype(F32), dhn], axis=0)
        dug_ref[1] = (dhext * gelu)[HALO:HALO + TM].astype(BF16)
        duc = dhext * gext * ggrad
        du = w[2] * duc + w[1] * _up(duc, 1) + w[0] * _up(duc, 2)
        dug_ref[0] = du[HALO:HALO + TM].astype(BF16)
        row = lax.broadcasted_iota(jnp.int32, (n, 1), 0)
        own = jnp.where((row >= HALO) & (row < HALO + TM), duc, 0.0)
        dw_ref[0:1, :] += jnp.sum(own * u2, axis=0, keepdims=True)
        dw_ref[1:2, :] += jnp.sum(own * u1, axis=0, keepdims=True)
        dw_ref[2:3, :] += jnp.sum(own * ext, axis=0, keepdims=True)

    def tile(off):
        return pl.BlockSpec((None, TM, FB), lambda j, i: (j + off, i, 0))

    def nxt(off):
        return pl.BlockSpec((None, HALO, FB), lambda j, i: (j + off, _next_halo(i), 0))

    return pl.pallas_call(
        body,
        name="ffn_act_bwd",
        grid=(N_FB, LP // TM),
        in_specs=[
            tile(0), tile(N_FB), tile(0),
            pl.BlockSpec((None, HALO, FB), lambda j, i: (j, _prev_halo(i), 0)),
            nxt(0), nxt(N_FB), nxt(0),
            pl.BlockSpec((None, 3, FB), lambda j, i: (j, 0, 0)),
        ],
        out_specs=[
            pl.BlockSpec((2, None, TM, FB), lambda j, i: (0, j, i, 0)),
            pl.BlockSpec((None, 8, FB), lambda j, i: (j, 0, 0)),
        ],
        out_shape=[jax.ShapeDtypeStruct((2, N_FB, LP, FB), BF16), jax.ShapeDtypeStruct((N_FB, 8, FB), F32)],
        compiler_params=_params(("parallel", "arbitrary")),
    )(ug, ug, dhid, ug, ug, ug, dhid, cw4)


def _gate_fwd(bc, ba, hin, bgate):
    def body(bc_ref, ba_ref, gc_ref, ga_ref, b_ref, o_ref):
        b = b_ref[...]
        sc = jax.nn.sigmoid(gc_ref[...].astype(F32) + b[0:1])
        sa = jax.nn.sigmoid(ga_ref[...].astype(F32) + b[1:2])
        o_ref[...] = (sc * bc_ref[...].astype(F32) + sa * ba_ref[...].astype(F32)).astype(BF16)

    def tile(s):
        return pl.BlockSpec((None, TM, D), lambda i: (s, i, 0))

    return pl.pallas_call(
        body,
        name="gate_fwd",
        grid=(LP // TM,),
        in_specs=[_row_spec(), _row_spec(), tile(6), tile(7), _vec_spec(2, D)],
        out_specs=_row_spec(),
        out_shape=jax.ShapeDtypeStruct((LP, D), BF16),
        compiler_params=_params(("parallel",)),
    )(bc, ba, hin, hin, bgate)


def _gate_bwd(dm, bc, ba, hin, bgate):
    def body(dm_ref, bc_ref, ba_ref, gc_ref, ga_ref, b_ref, dbc_ref, dba_ref, dg_ref, db_ref):
        i = pl.program_id(0)

        @pl.when(i == 0)
        def _():
            db_ref[...] = jnp.zeros_like(db_ref)

        b = b_ref[...]
        dm = dm_ref[...].astype(F32)
        sc = jax.nn.sigmoid(gc_ref[...].astype(F32) + b[0:1])
        sa = jax.nn.sigmoid(ga_ref[...].astype(F32) + b[1:2])
        dbc_ref[...] = (dm * sc).astype(BF16)
        dba_ref[...] = (dm * sa).astype(BF16)
        dgc = dm * bc_ref[...].astype(F32) * sc * (1.0 - sc)
        dga = dm * ba_ref[...].astype(F32) * sa * (1.0 - sa)
        dg_ref[0] = dgc.astype(BF16)
        dg_ref[1] = dga.astype(BF16)
        db_ref[0:1, :] += jnp.sum(dgc, axis=0, keepdims=True)
        db_ref[1:2, :] += jnp.sum(dga, axis=0, keepdims=True)

    def tile(s):
        return pl.BlockSpec((None, TM, D), lambda i: (s, i, 0))

    return pl.pallas_call(
        body,
        name="gate_bwd",
        grid=(LP // TM,),
        in_specs=[_row_spec(), _row_spec(), _row_spec(), tile(6), tile(7), _vec_spec(2, D)],
        out_specs=[_row_spec(), _row_spec(), pl.BlockSpec((2, TM, D), lambda i: (3, i, 0)), _vec_spec(8, D)],
        out_shape=[jax.ShapeDtypeStruct((LP, D), BF16)] * 2
        + [jax.ShapeDtypeStruct((N_DEV, LP, D), BF16), jax.ShapeDtypeStruct((8, D), F32)],
        compiler_params=_params(("arbitrary",)),
    )(dm, bc, ba, hin, hin, bgate)


Z_LINEAR = 30.0


def _softplus(z):
    return jnp.maximum(z, jnp.log(1.0 + jnp.exp(jnp.minimum(z, Z_LINEAR))))


def _cumsum_matrix(inclusive, reverse):
    r = lax.broadcasted_iota(jnp.int32, (QB, 2 * QB), 0)
    c = lax.broadcasted_iota(jnp.int32, (QB, 2 * QB), 1)
    if reverse:
        tri = r > c
    elif inclusive:
        tri = r <= c
    else:
        tri = r < c
    return jnp.where((c >= QB) | tri, 1.0, 0.0).astype(BF16)


def _split_dot(x, m2):
    bits = lax.bitcast_convert_type(x, jnp.uint32) & jnp.uint32(0xFFFF0000)
    hi = lax.bitcast_convert_type(bits, F32)
    return _dot(jnp.concatenate([hi.astype(BF16), (x - hi).astype(BF16)], axis=1), m2)


def _stack_heads(x):
    return jnp.concatenate(_split_heads(x), axis=0)


def _block_mask(i, j, row0):
    row = lax.broadcasted_iota(jnp.int32, (AQ - row0, QB), 0) + (i * AQ + row0)
    col = lax.broadcasted_iota(jnp.int32, (AQ - row0, QB), 1) + j * QB
    return (col < row) & (col >= PAD)


def _key_block(ref, j):
    return ref[pl.ds(pl.multiple_of(j * QB, QB), QB), :]


def _split_heads(x):
    head_a = lax.broadcasted_iota(jnp.int32, x.shape, 1) < HEAD_LANES
    zero = jnp.zeros_like(x)
    return jnp.where(head_a, x, zero), jnp.where(head_a, zero, x)


def _attn_fwd(hin, ex_arrs, gathers):
    ne = len(ex_arrs)
    npair, nq = D // QB, LP // AQ

    def body(*refs):
        q_ref, k_ref, v_ref = refs[:3]
        o_ref, lt_ref = refs[3 + ne:5 + ne]
        c_sc, acc_sc = refs[5 + 2 * ne:7 + 2 * ne]
        p, i = pl.program_id(0), pl.program_id(1)

        def copies():
            return _exchange_copies(refs[3:3 + ne], refs[5 + ne:5 + 2 * ne], gathers, *refs[7 + 2 * ne:])

        @pl.when((p == 0) & (i == 0))
        def _():
            for cp in copies():
                cp.start()

        um = _cumsum_matrix(False, True)
        um2 = jnp.concatenate([um, um], axis=0)
        q = (q_ref[...].astype(F32) * SCALE).astype(BF16)
        c_sc[...] = jnp.zeros_like(c_sc)
        acc_sc[...] = jnp.zeros_like(acc_sc)

        def step(j, masked, row0=0):
            rows = slice(row0, AQ)
            z2 = _dot(q[rows], _stack_heads(_key_block(k_ref, j)), NT)
            mask = _block_mask(i, j, row0) if masked else None
            a2 = []
            for hd in range(2):
                z = z2[:, hd * QB:(hd + 1) * QB]
                sp = _softplus(z)
                r = _split_dot(jnp.where(mask, sp, 0.0) if masked else sp, um2)
                a = jnp.exp(z - sp - c_sc[hd, rows] - r[:, :QB])
                if masked:
                    a = jnp.where(mask, a, 0.0)
                a2.append(a.astype(BF16))
                c_sc[hd, rows] += r[:, QB:]
            acc_sc[rows] += _dot(jnp.concatenate(a2, axis=1), _stack_heads(_key_block(v_ref, j)))

        for t in reversed(range(KPQ)):
            step(KPQ * i + t, True, t * QB)

        groups = jnp.maximum(i - 1, 0)

        @pl.loop(0, groups // 2)
        def _(t):
            for u in range(2 * KPQ):
                step(KPQ * (i - 2 * t) - 1 - u, False)

        @pl.when(groups % 2 == 1)
        def _():
            for u in range(KPQ):
                step(2 * KPQ - 1 - u, False)

        @pl.when(i > 0)
        def _():
            for u in reversed(range(1, KPQ)):
                step(u, False)
            step(0, True)

        head_a = lax.broadcasted_iota(jnp.int32, (AQ, QB), 1) < HEAD_LANES
        o_ref[...] = acc_sc[...].astype(BF16)
        lt_ref[...] = jnp.where(head_a, c_sc[0], c_sc[1])

        @pl.when((p == npair - 1) & (i == nq - 1))
        def _():
            for cp in copies():
                cp.wait()

    def seq(s):
        return pl.BlockSpec((None, LP, QB), lambda p, i: (s, 0, p))

    return pl.pallas_call(
        body,
        name="attn_fwd",
        grid=(npair, nq),
        in_specs=[pl.BlockSpec((None, AQ, QB), lambda p, i: (3, i, p)), seq(4), seq(5)] + [ANY_SPEC] * ne,
        out_specs=[pl.BlockSpec((AQ, QB), lambda p, i: (i, p))] * 2 + [ANY_SPEC] * ne,
        out_shape=[jax.ShapeDtypeStruct((LP, D), BF16), jax.ShapeDtypeStruct((LP, D), F32)]
        + _exchange_shapes(ex_arrs, gathers),
        scratch_shapes=[pltpu.VMEM((2, AQ, QB), F32), pltpu.VMEM((AQ, QB), F32)] + _exchange_sems(ne),
        compiler_params=_params(("arbitrary", "arbitrary")),
    )(hin, hin, hin, *ex_arrs)


def _attn_bwd(hin, do, lt, dhin, ex_arrs, gathers):
    ne = len(ex_arrs)
    npair, nq = D // QB, LP // AQ

    def body(*refs):
        q_ref, k_ref, v_ref, do_ref, lt_ref = refs[:5]
        out_ref = refs[6 + ne]
        psp_sc, pg_sc, dq_sc, dk_acc, dv_acc = refs[7 + 2 * ne:12 + 2 * ne]
        p, i = pl.program_id(0), pl.program_id(1)

        def copies():
            return _exchange_copies(refs[6:6 + ne], refs[7 + ne:7 + 2 * ne], gathers, *refs[12 + 2 * ne:])

        @pl.when((p == 0) & (i == 0))
        def _():
            for cp in copies():
                cp.start()

        @pl.when(i == 0)
        def _():
            dk_acc[...] = jnp.zeros_like(dk_acc)
            dv_acc[...] = jnp.zeros_like(dv_acc)

        um_sp = _cumsum_matrix(True, False)
        um_sp2 = jnp.concatenate([um_sp, um_sp], axis=0)
        um_g = _cumsum_matrix(False, False)
        q = (q_ref[...].astype(F32) * SCALE).astype(BF16)
        do = do_ref[...]
        q_t, do_t = q.T, do.T
        head_a = lax.broadcasted_iota(jnp.int32, (AQ, QB), 1) < HEAD_LANES
        dim_head_a = lax.broadcasted_iota(jnp.int32, (QB, QB), 0) < HEAD_LANES
        lt = lt_ref[...]
        lt_sw = pltpu.roll(lt, HEAD_LANES, 1)
        totals = (jnp.where(head_a, lt, lt_sw), jnp.where(head_a, lt_sw, lt))
        psp_sc[...] = jnp.zeros_like(psp_sc)
        pg_sc[...] = jnp.zeros_like(pg_sc)
        dq_sc[...] = jnp.zeros_like(dq_sc)

        def step(j, masked, row0=0):
            rows = slice(row0, AQ)
            k2 = _stack_heads(_key_block(k_ref, j))
            z2 = _dot(q[rows], k2, NT)
            da2 = _dot(do[rows], _stack_heads(_key_block(v_ref, j)), NT)
            mask = _block_mask(i, j, row0) if masked else None
            a2, dz2 = [], []
            for hd in range(2):
                z = z2[:, hd * QB:(hd + 1) * QB]
                sp = _softplus(z)
                r = _split_dot(jnp.where(mask, sp, 0.0) if masked else sp, um_sp2)
                a = jnp.exp(z - sp - (totals[hd][rows] - psp_sc[hd, rows] - r[:, :QB]))
                if masked:
                    a = jnp.where(mask, a, 0.0)
                g = a * da2[:, hd * QB:(hd + 1) * QB]
                rg = _dot(g.astype(BF16), um_g)
                dz = g - jnp.exp(z - sp) * (g + pg_sc[hd, rows] + rg[:, :QB])
                if masked:
                    dz = jnp.where(mask, dz, 0.0)
                a2.append(a.astype(BF16))
                dz2.append(dz.astype(BF16))
                psp_sc[hd, rows] += r[:, QB:]
                pg_sc[hd, rows] += rg[:, QB:]
            dz2 = jnp.concatenate(dz2, axis=1)
            dq_sc[rows] += _dot(dz2, k2)
            dk2 = _dot(q_t[:, rows], dz2)
            dv2 = _dot(do_t[:, rows], jnp.concatenate(a2, axis=1))
            dk_acc[j] += jnp.where(dim_head_a, dk2[:, :QB], dk2[:, QB:])
            dv_acc[j] += jnp.where(dim_head_a, dv2[:, :QB], dv2[:, QB:])

        @pl.when(i > 0)
        def _():
            step(0, True)
            for u in range(1, KPQ):
                step(u, False)

        groups = jnp.maximum(i - 1, 0)

        @pl.loop(0, groups // 2)
        def _(t):
            for u in range(2 * KPQ):
                step(KPQ * (2 * t + 1) + u, False)

        @pl.when(groups % 2 == 1)
        def _():
            for u in range(KPQ):
                step(KPQ * (i - 1) + u, False)

        for t in range(KPQ):
            step(KPQ * i + t, True, t * QB)

        out_ref[0, pl.ds(pl.multiple_of(i * AQ, AQ), AQ), :] = (dq_sc[...] * SCALE).astype(BF16)

        @pl.when(i == nq - 1)
        def _():
            @pl.loop(0, LP // QB)
            def _(b):
                keys = pl.ds(pl.multiple_of(b * QB, QB), QB)
                out_ref[1, keys, :] = dk_acc[b].T.astype(BF16)
                out_ref[2, keys, :] = dv_acc[b].T.astype(BF16)

        @pl.when((p == npair - 1) & (i == nq - 1))
        def _():
            for cp in copies():
                cp.wait()

    def seq(s):
        return pl.BlockSpec((None, LP, QB), lambda p, i: (s, 0, p))

    blk = pl.BlockSpec((AQ, QB), lambda p, i: (i, p))
    return pl.pallas_call(
        body,
        name="attn_bwd",
        grid=(npair, nq),
        in_specs=[pl.BlockSpec((None, AQ, QB), lambda p, i: (3, i, p)), seq(4), seq(5), blk, blk]
        + [ANY_SPEC] * (1 + ne),
        out_specs=[pl.BlockSpec((3, LP, QB), lambda p, i: (1, 0, p))] + [ANY_SPEC] * ne,
        out_shape=[jax.ShapeDtypeStruct((N_DEV, LP, D), BF16)] + _exchange_shapes(ex_arrs, gathers),
        input_output_aliases={5: 0},
        scratch_shapes=[pltpu.VMEM((2, AQ, QB), F32)] * 2 + [pltpu.VMEM((AQ, QB), F32)]
        + [pltpu.VMEM((LP // QB, QB, QB), F32)] * 2 + _exchange_sems(ne),
        compiler_params=_params(("arbitrary", "arbitrary")),
    )(hin, hin, hin, do, lt, dhin, *ex_arrs)


def _adamw_math(w, g, m, v):
    m_new = ADAM_B1 * m + (1.0 - ADAM_B1) * g
    v_new = ADAM_B2 * v + (1.0 - ADAM_B2) * jnp.square(g)
    m_hat = m_new / (1.0 - ADAM_B1 ** ADAM_STEP)
    v_hat = v_new / (1.0 - ADAM_B2 ** ADAM_STEP)
    return -ADAM_LR * (m_hat / (jnp.sqrt(v_hat) + ADAM_EPS) + ADAM_WD * w), m_new, v_new


def _adamw_small(ws, gs, ms, vs):
    n = len(ws)

    def body(*refs):
        for t in range(n):
            w_ref, g_ref, m_ref, v_ref = (refs[s * n + t] for s in range(4))
            d_ref, nm_ref, nv_ref = (refs[(4 + s) * n + t] for s in range(3))
            d_ref[...], nm_ref[...], nv_ref[...] = _adamw_math(w_ref[...], g_ref[...], m_ref[...], v_ref[...])

    vmem = pl.BlockSpec(memory_space=pltpu.VMEM)
    res = pl.pallas_call(
        body,
        name="adamw_small",
        in_specs=[vmem] * (4 * n),
        out_specs=[vmem] * (3 * n),
        out_shape=[jax.ShapeDtypeStruct(w.shape, F32) for w in ws] * 3,
    )(*ws, *gs, *ms, *vs)
    return res[:n], res[n:2 * n], res[2 * n:]


def _adamw(pieces, w, m, v, *, rows, row_off, tr, name):
    npieces, _, cols = pieces.shape
    ob = row_off // tr

    def body(p_ref, w_ref, m_ref, v_ref, g_ref, d_ref, nm_ref, nv_ref):
        g = p_ref[0].astype(F32)
        for s in range(1, npieces):
            g = g + p_ref[s].astype(F32)
        g_ref[...] = g
        d_ref[...], nm_ref[...], nv_ref[...] = _adamw_math(w_ref[...], g, m_ref[...], v_ref[...])

    spec = pl.BlockSpec((tr, cols), lambda i: (i, 0))
    return pl.pallas_call(
        body,
        name=name,
        grid=(rows // tr,),
        in_specs=[pl.BlockSpec((npieces, tr, cols), lambda i: (0, ob + i, 0)), spec, spec, spec],
        out_specs=[spec] * 4,
        out_shape=[jax.ShapeDtypeStruct((rows, cols), F32)] * 4,
        compiler_params=_params(("parallel",)),
    )(pieces, w, m, v)


def _sum_pieces(pieces, name):
    npieces, rows, cols = pieces.shape

    def body(p_ref, o_ref):
        acc = p_ref[0]
        for s in range(1, npieces):
            acc = acc + p_ref[s]
        o_ref[...] = acc

    return pl.pallas_call(
        body,
        name=name,
        in_specs=[pl.BlockSpec(memory_space=pltpu.VMEM)],
        out_specs=pl.BlockSpec(memory_space=pltpu.VMEM),
        out_shape=jax.ShapeDtypeStruct((rows, cols), pieces.dtype),
    )(pieces)


SMALL_ROWS = 48
CWF_PAD = 384
GRAD_ROWS = 80
G_META, G_PRE_MIX, G_MID, G_POST_FFN, G_CW_MIX, G_B_GATE, G_CW_FFN, G_LOSS = 0, 16, 24, 32, 40, 48, 56, 72


def kernel(x, meta_tokens, g_pre_mix, w_in, conv_w_mix, w_proj_conv, w_proj_attn, b_gate, w_out, g_post_mix, g_pre_ffn, w_up_gate, conv_w_ffn, w_down, g_post_ffn, loss_target, m_meta_tokens, m_g_pre_mix, m_w_in, m_conv_w_mix, m_w_proj_conv, m_w_proj_attn, m_b_gate, m_w_out, m_g_post_mix, m_g_pre_ffn, m_w_up_gate, m_conv_w_ffn, m_w_down, m_g_post_ffn, v_meta_tokens, v_g_pre_mix, v_w_in, v_conv_w_mix, v_w_proj_conv, v_w_proj_attn, v_b_gate, v_w_out, v_g_post_mix, v_g_pre_ffn, v_w_up_gate, v_conv_w_ffn, v_w_down, v_g_post_ffn):
    me = 4 * lax.axis_index("x") + 2 * lax.axis_index("y") + lax.axis_index("c")

    def rows_to(a, n):
        return jnp.pad(a, ((0, n - a.shape[0]), (0, 0)))

    small_shard = jnp.concatenate(
        [meta_tokens, rows_to(conv_w_mix[0], 8), rows_to(b_gate[0], 8),
         rows_to(jnp.pad(conv_w_ffn[0], ((0, 0), (0, CWF_PAD - R_DOWN))).reshape(9, 128), 16)], axis=0)
    wshard = jnp.concatenate([w_proj_conv[0], w_proj_attn[0], w_out[0], w_down[0]], axis=0).astype(BF16)
    w_in_all, small_all = _gather_two_level([w_in[0].astype(BF16), small_shard], "gather_in")

    def unshard(rows):
        return rows.transpose(1, 0, 2).reshape(rows.shape[1], N_DEV * rows.shape[2])

    meta = unshard(small_all[:, 0:16])
    cw_mix = unshard(small_all[:, 16:19])
    bgate = unshard(small_all[:, 24:26])
    cw_ffn = unshard(small_all[:, 32:41].reshape(N_DEV, 3, CWF_PAD)[:, :, :R_DOWN])
    cw4 = cw_ffn.reshape(3, N_FB, FB).transpose(1, 0, 2)
    h0 = jnp.concatenate([jnp.zeros((PAD, D), F32), meta, x[0]], axis=0)

    xn1 = _rms_fwd(h0, g_pre_mix, "rms_pre_mix")
    hin = _mm(xn1, w_in_all, w_rows=D, trans_w=False, out_dtype=BF16, name="mm_in")
    y_conv = _conv_mix_fwd(hin, cw_mix)
    o, lt, wpack, w_ug = _attn_fwd(hin, [wshard, w_up_gate[0].astype(BF16)], (True, True))
    w_pc = wpack[:, O_PC:O_PA].reshape(1, D, D)
    w_pa = wpack[:, O_PA:O_OUT].reshape(1, D, D)
    w_o = wpack[:, O_OUT:O_DOWN].reshape(1, D, D)
    w_dn = wpack[:, O_DOWN:].reshape(N_FB, FB, D)
    bc = _mm(y_conv, w_pc, w_rows=D, trans_w=False, out_dtype=BF16, name="mm_proj_conv")[0]
    ba = _mm(o, w_pa, w_rows=D, trans_w=False, out_dtype=BF16, name="mm_proj_attn")[0]
    merged = _gate_fwd(bc, ba, hin, bgate)
    mix = _mm(merged, w_o, w_rows=D, trans_w=False, out_dtype=F32, name="mm_out")[0]
    h1, xn2 = _resid_rms(h0, mix, g_post_mix, g_pre_ffn)
    ug = _mm(xn2, w_ug, w_rows=D, trans_w=False, out_dtype=BF16, name="mm_up_gate")
    hid = _ffn_act_fwd(ug, cw4)
    ffn = _mm_sum(hid, w_dn, w_rows=FB, trans_w=False, out_dtype=F32, name="mm_down")
    dout, dffn, loss8, dg_post_ffn = _loss_head(h1, ffn, g_post_ffn, loss_target[0])

    dhid = _mm(dffn, w_dn, w_rows=FB, trans_w=True, out_dtype=BF16, name="mm_down_dx")
    gw_dn = _mm_tn(hid, dffn, nb=N_FB, out_dtype=BF16, name="mm_down_dw")
    dug, dcw4 = _ffn_act_bwd(ug, dhid, cw4)
    dug = dug.reshape(2 * N_FB, LP, FB)
    dxn2 = _mm_sum(dug, w_ug, w_rows=D, trans_w=True, out_dtype=F32, name="mm_up_gate_dx")
    gw_ug = _mm_tn(xn2, dug, nb=N_DEV, out_dtype=BF16, name="mm_up_gate_dw")
    dh1, dmix, dg_mid = _mid_bwd(dout, h1, dxn2, mix, g_post_mix, g_pre_ffn)
    dmerged = _mm(dmix, w_o, w_rows=D, trans_w=True, out_dtype=BF16, name="mm_out_dx")[0]
    gw_out = _mm_tn(merged, dmix, nb=1, out_dtype=BF16, name="mm_out_dw")
    dbc, dba, dhin, db_gate = _gate_bwd(dmerged, bc, ba, hin, bgate)
    dy_conv = _mm(dbc, w_pc, w_rows=D, trans_w=True, out_dtype=BF16, name="mm_proj_conv_dx")[0]
    gw_pc = _mm_tn(y_conv, dbc, nb=1, out_dtype=BF16, name="mm_proj_conv_dw")
    do = _mm(dba, w_pa, w_rows=D, trans_w=True, out_dtype=BF16, name="mm_proj_attn_dx")[0]
    gw_pa = _mm_tn(o, dba, nb=1, out_dtype=BF16, name="mm_proj_attn_dw")
    dhin, dcw_mix = _conv_mix_bwd(hin, dy_conv, cw_mix, dhin)
    gpack = jnp.concatenate(
        [gw_pc.reshape(N_DEV, R_PROJ, D), gw_pa.reshape(N_DEV, R_PROJ, D), gw_out.reshape(N_DEV, R_PROJ, D),
         gw_dn.reshape(N_DEV, R_DOWN, D)], axis=1)
    dhin, rpack, rug = _attn_bwd(hin, do, lt, dhin, [gpack, gw_ug], (False, False))
    gw_in = _mm_tn(xn1, dhin, nb=N_DEV, out_dtype=BF16, name="mm_in_dw")
    send_sems, recv_sems, gw_thru, land_thru, token = _scatter_start(gw_in)
    dxn1 = _mm_sum(dhin, w_in_all, w_rows=D, trans_w=True, out_dtype=F32, name="mm_in_dx", after=token)
    dh0, dg_pre_mix = _first_bwd(dh1, h0, dxn1, g_pre_mix)
    dcw_ffn = dcw4[:, :3].transpose(1, 0, 2).reshape(3, D_FF)
    small = jnp.concatenate(
        [dh0[PAD:OFF], dg_pre_mix, dg_mid, dg_post_ffn, dcw_mix, db_gate,
         jnp.pad(dcw_ffn.reshape(-1), (0, 16 * D - 3 * D_FF)).reshape(16, D),
         jnp.pad(loss8, ((0, 0), (0, D - 128)))], axis=0)

    def big(pieces, w, m, v, rows, row_off, tr, name):
        g, d, nm, nv = _adamw(pieces, w[0], m[0], v[0], rows=rows, row_off=row_off, tr=tr, name=name)
        return g[None], d[None], nm[None], nv[None]

    r_pc = big(rpack, w_proj_conv, m_w_proj_conv, v_w_proj_conv, R_PROJ, O_PC, R_PROJ, "adamw_proj_conv")
    r_pa = big(rpack, w_proj_attn, m_w_proj_attn, v_w_proj_attn, R_PROJ, O_PA, R_PROJ, "adamw_proj_attn")
    r_out = big(rpack, w_out, m_w_out, v_w_out, R_PROJ, O_OUT, R_PROJ, "adamw_out")
    r_dn = big(rpack, w_down, m_w_down, v_w_down, R_DOWN, O_DOWN, 32, "adamw_down")
    r_ug = big(rug, w_up_gate, m_w_up_gate, v_w_up_gate, D, 0, 256, "adamw_up_gate")
    (rsmall,) = _exchange([small], (True,), "gather_small_grads",
                          after=(r_pc[1], r_pa[1], r_out[1], r_dn[1], r_ug[1]))
    gs = _sum_pieces(rsmall, "sum_small_grads")
    grad_x = dh0[OFF:]
    loss = gs[G_LOSS, 0]

    def cols(a, width):
        return lax.dynamic_slice_in_dim(a, me * width, width, axis=1)

    g_meta = cols(gs[G_META:G_META + N_META], 128)
    g_gpm, g_gff = gs[G_PRE_MIX:G_PRE_MIX + 1], gs[G_POST_FFN:G_POST_FFN + 1]
    g_gpo, g_gpf = gs[G_MID:G_MID + 1], gs[G_MID + 1:G_MID + 2]
    g_cwm = cols(gs[G_CW_MIX:G_CW_MIX + 3], 128)[None]
    g_bg = cols(gs[G_B_GATE:G_B_GATE + 2], 128)[None]
    g_cwf = cols(gs[G_CW_FFN:G_CW_FFN + 9].reshape(-1)[:3 * D_FF].reshape(3, D_FF), R_DOWN)[None]

    small_w = [meta_tokens, g_pre_mix, conv_w_mix, b_gate, g_post_mix, g_pre_ffn, conv_w_ffn, g_post_ffn]
    small_g = [g_meta, g_gpm, g_cwm, g_bg, g_gpo, g_gpf, g_cwf, g_gff]
    small_m = [m_meta_tokens, m_g_pre_mix, m_conv_w_mix, m_b_gate, m_g_post_mix, m_g_pre_ffn, m_conv_w_ffn, m_g_post_ffn]
    small_v = [v_meta_tokens, v_g_pre_mix, v_conv_w_mix, v_b_gate, v_g_post_mix, v_g_pre_ffn, v_conv_w_ffn, v_g_post_ffn]

    s_g = small_g
    s_d, s_m, s_v = _adamw_small(small_w, small_g, small_m, small_v)

    gw_done, landed = _scatter_wait(send_sems, recv_sems, gw_thru, land_thru, (s_d[0],))
    rin = lax.dynamic_update_index_in_dim(landed, lax.dynamic_index_in_dim(gw_done, me, 0, keepdims=False), me, 0)
    r_in = big(rin, w_in, m_w_in, v_w_in, D, 0, 256, "adamw_in")

    def ordered(k, smalls):
        meta, gpm, cwm, bg, gpo, gpf, cwf, gff = smalls
        return [meta, gpm, r_in[k], cwm, r_pc[k], r_pa[k], bg, r_out[k], gpo, gpf, r_ug[k], cwf, r_dn[k], gff]

    return (loss, grad_x[None], *ordered(0, s_g), *ordered(1, s_d), *ordered(2, s_m), *ordered(3, s_v))
```

```python
import functools
import math

import jax
import jax.numpy as jnp
from jax import lax
from jax.experimental import pallas as pl
from jax.experimental.pallas import tpu as pltpu

F32 = jnp.float32
BF16 = jnp.bfloat16

D = 1024
SEQ = 4096
N_META = 16
PAD = 112
OFF = PAD + N_META
LP = OFF + SEQ
QB = 128
AQ = 384
KPQ = AQ // QB
TM = 384
MM_TM = 1408
HALO = 16
N_DEV = 8
D_FF = 2816
FB = 704
N_FB = D_FF // FB
RMS_EPS = 1e-6
SCALE = 0.125
HEAD_LANES = 64
VMEM_LIMIT = 56 * 1024 * 1024

ADAM_LR = 0.001
ADAM_B1 = 0.9
ADAM_B2 = 0.999
ADAM_EPS = 1e-08
ADAM_WD = 0.01
ADAM_STEP = 10

R_PROJ, R_DOWN = 128, 352
O_PC = 0
O_PA = O_PC + R_PROJ
O_OUT = O_PA + R_PROJ
O_DOWN = O_OUT + R_PROJ
R_PACK = O_DOWN + R_DOWN

NT = (((1,), (1,)), ((), ()))
NN = (((1,), (0,)), ((), ()))
TN = (((0,), (0,)), ((), ()))


def _params(sem):
    return pltpu.CompilerParams(dimension_semantics=sem, vmem_limit_bytes=VMEM_LIMIT)


def _dot(a, b, dn=NN):
    return lax.dot_general(a, b, dn, preferred_element_type=F32)


def _exchange_copies(ins, outs, gathers, send_sems, recv_sems, loc_sems):
    x, y, c = lax.axis_index("x"), lax.axis_index("y"), lax.axis_index("c")
    me = 4 * x + 2 * y + c
    copies = []
    for a, gather in enumerate(gathers):
        copies.append(pltpu.make_async_copy(ins[a] if gather else ins[a].at[me], outs[a].at[me], loc_sems.at[a]))
    for k in range(1, N_DEV):
        px = 1 - x if k & 4 else x
        py = 1 - y if k & 2 else y
        pc = 1 - c if k & 1 else c
        peer = 4 * px + 2 * py + pc
        for a, gather in enumerate(gathers):
            copies.append(pltpu.make_async_remote_copy(
                src_ref=ins[a] if gather else ins[a].at[peer],
                dst_ref=outs[a].at[me],
                send_sem=send_sems.at[a * (N_DEV - 1) + k - 1],
                recv_sem=recv_sems.at[a * (N_DEV - 1) + k - 1],
                device_id=(px, py, pc),
                device_id_type=pl.DeviceIdType.MESH,
            ))
    return copies


def _exchange_shapes(arrs, gathers):
    return [jax.ShapeDtypeStruct((N_DEV,) + (a.shape if g else a.shape[1:]), a.dtype) for a, g in zip(arrs, gathers)]


def _exchange_sems(n):
    return [pltpu.SemaphoreType.DMA((n * (N_DEV - 1),)), pltpu.SemaphoreType.DMA((n * (N_DEV - 1),)),
            pltpu.SemaphoreType.DMA((n,))]


ANY_SPEC = pl.BlockSpec(memory_space=pl.ANY)


def _gather_two_level(arrs, name):
    n = len(arrs)
    per = 7

    def body(*refs):
        ins, outs = refs[:n], refs[n:2 * n]
        send_sems, recv_sems, loc_sems = refs[2 * n:]
        x, y, c = lax.axis_index("x"), lax.axis_index("y"), lax.axis_index("c")
        me, sibling = (x, y, c), (x, y, 1 - c)
        chips = [(1 - x, y), (x, 1 - y), (1 - x, 1 - y)]

        def copy(a, k, block, to, src=None):
            place = outs[a].at[4 * block[0] + 2 * block[1] + block[2]]
            return pltpu.make_async_remote_copy(
                src_ref=place if src is None else src, dst_ref=place,
                send_sem=send_sems.at[a * per + k], recv_sem=recv_sems.at[a * per + k],
                device_id=to, device_id_type=pl.DeviceIdType.MESH)

        mine = [pltpu.make_async_copy(ins[a], outs[a].at[4 * x + 2 * y + c], loc_sems.at[a]) for a in range(n)]
        first = [copy(a, 0, me, sibling, src=ins[a]) for a in range(n)]
        first += [copy(a, 1 + j, me, (*chip, c), src=ins[a]) for j, chip in enumerate(chips) for a in range(n)]
        for cp in mine + first:
            cp.start()
        passed = []
        for j, chip in enumerate(chips):
            for a in range(n):
                copy(a, 1 + j, (*chip, c), me).wait_recv()
                passed.append(copy(a, 4 + j, (*chip, c), sibling))
                passed[-1].start()
        for a in range(n):
            copy(a, 0, sibling, me).wait_recv()
        for j, chip in enumerate(chips):
            for a in range(n):
                copy(a, 4 + j, (*chip, 1 - c), me).wait_recv()
        for cp in first + passed:
            cp.wait_send()
        for cp in mine:
            cp.wait()

    return pl.pallas_call(
        body,
        name=name,
        out_shape=_exchange_shapes(arrs, (True,) * n),
        in_specs=[ANY_SPEC] * n,
        out_specs=[ANY_SPEC] * n,
        scratch_shapes=_exchange_sems(n),
    )(*arrs)


HBM_SPEC = pl.BlockSpec(memory_space=pltpu.HBM)
SEM_SPEC = pl.BlockSpec(memory_space=pltpu.SEMAPHORE)
DATAFLOW = pltpu.SideEffectType.DATAFLOW_SIDE_EFFECTING


def _scatter_copies(g_ref, land_ref, send_sems, recv_sems):
    x, y, c = lax.axis_index("x"), lax.axis_index("y"), lax.axis_index("c")
    me = 4 * x + 2 * y + c
    copies = []
    for k in range(1, N_DEV):
        px = 1 - x if k & 4 else x
        py = 1 - y if k & 2 else y
        pc = 1 - c if k & 1 else c
        copies.append(pltpu.make_async_remote_copy(
            src_ref=g_ref.at[4 * px + 2 * py + pc], dst_ref=land_ref.at[me],
            send_sem=send_sems.at[k - 1], recv_sem=recv_sems.at[k - 1],
            device_id=(px, py, pc), device_id_type=pl.DeviceIdType.MESH))
    return copies


def _scatter_start(g):
    def body(g_ref, land_ref, send_sems, recv_sems, g_thru, land_thru, token):
        for cp in _scatter_copies(g_ref, land_ref, send_sems, recv_sems):
            cp.start()
        token[...] = jnp.zeros_like(token)

    return pl.pallas_call(
        body,
        name="scatter_in_start",
        out_shape=(pltpu.SemaphoreType.DMA((N_DEV - 1,)), pltpu.SemaphoreType.DMA((N_DEV - 1,)),
                   pltpu.HBM(g.shape, g.dtype), pltpu.HBM(g.shape, g.dtype), jax.ShapeDtypeStruct((8, 128), F32)),
        in_specs=(HBM_SPEC, HBM_SPEC),
        out_specs=(SEM_SPEC, SEM_SPEC, HBM_SPEC, HBM_SPEC, pl.BlockSpec(memory_space=pltpu.VMEM)),
        input_output_aliases={0: 2, 1: 3},
        compiler_params=pltpu.CompilerParams(has_side_effects=DATAFLOW),
    )(pltpu.with_memory_space_constraint(g, pltpu.HBM),
      pltpu.with_memory_space_constraint(lax.empty(g.shape, g.dtype), pltpu.HBM))


def _scatter_wait(send_sems, recv_sems, g_thru, land_thru, after):
    def body(g_ref, land_ref, send_sems, recv_sems, *_):
        for cp in _scatter_copies(g_ref, land_ref, send_sems, recv_sems):
            cp.wait_send()
            cp.wait_recv()

    return pl.pallas_call(
        body,
        name="scatter_in_wait",
        out_shape=(pltpu.HBM(g_thru.shape, g_thru.dtype), pltpu.HBM(g_thru.shape, g_thru.dtype)),
        in_specs=(HBM_SPEC, HBM_SPEC, SEM_SPEC, SEM_SPEC) + (ANY_SPEC,) * len(after),
        out_specs=(HBM_SPEC, HBM_SPEC),
        input_output_aliases={0: 0, 1: 1},
        compiler_params=pltpu.CompilerParams(has_side_effects=DATAFLOW),
    )(g_thru, land_thru, send_sems, recv_sems, *after)


def _exchange(arrs, gathers, name, after=()):
    n, na = len(arrs), len(after)

    def body(*refs):
        copies = _exchange_copies(refs[:n], refs[n + na:2 * n + na], gathers, *refs[2 * n + na:])
        for cp in copies:
            cp.start()
        for cp in copies:
            cp.wait()

    return pl.pallas_call(
        body,
        name=name,
        out_shape=_exchange_shapes(arrs, gathers),
        in_specs=[ANY_SPEC] * (n + na),
        out_specs=[ANY_SPEC] * n,
        scratch_shapes=_exchange_sems(n),
    )(*arrs, *after)


def _mm(a, w, *, w_rows, trans_w, out_dtype, name):
    nb, _, wc = w.shape
    m, k = a.shape[-2:]
    n = w_rows if trans_w else wc
    dn = NT if trans_w else NN

    def body(a_ref, w_ref, o_ref):
        o_ref[...] = _dot(a_ref[...], w_ref[...], dn).astype(out_dtype)

    if a.ndim == 2:
        a_spec = pl.BlockSpec((MM_TM, k), lambda j, i: (i, 0))
    else:
        a_spec = pl.BlockSpec((None, MM_TM, k), lambda j, i: (j, i, 0))
    return pl.pallas_call(
        body,
        name=name,
        grid=(nb, m // MM_TM),
        in_specs=[a_spec, pl.BlockSpec((None, w_rows, wc), lambda j, i: (j, 0, 0))],
        out_specs=pl.BlockSpec((None, MM_TM, n), lambda j, i: (j, i, 0)),
        out_shape=jax.ShapeDtypeStruct((nb, m, n), out_dtype),
        compiler_params=_params(("parallel", "parallel")),
    )(a, w)


def _mm_sum(a, w, *, w_rows, trans_w, out_dtype, name, after=()):
    nb, m, k = a.shape
    wc = w.shape[2]
    n = w_rows if trans_w else wc
    dn = NT if trans_w else NN
    after = tuple(after) if isinstance(after, (tuple, list)) else (after,)
    na = len(after)

    def body(*refs):
        a_ref, w_ref, o_ref, acc_ref = refs[0], refs[1], refs[2 + na], refs[3 + na]
        j = pl.program_id(1)

        @pl.when(j == 0)
        def _():
            acc_ref[...] = jnp.zeros_like(acc_ref)

        acc_ref[...] += _dot(a_ref[...], w_ref[...], dn)

        @pl.when(j == nb - 1)
        def _():
            o_ref[...] = acc_ref[...].astype(out_dtype)

    return pl.pallas_call(
        body,
        name=name,
        grid=(m // MM_TM, nb),
        in_specs=[
            pl.BlockSpec((None, MM_TM, k), lambda i, j: (j, i, 0)),
            pl.BlockSpec((None, w_rows, wc), lambda i, j: (j, 0, 0)),
        ] + [ANY_SPEC] * na,
        out_specs=pl.BlockSpec((MM_TM, n), lambda i, j: (i, 0)),
        out_shape=jax.ShapeDtypeStruct((m, n), out_dtype),
        scratch_shapes=[pltpu.VMEM((MM_TM, n), F32)],
        compiler_params=_params(("parallel", "arbitrary")),
    )(a, w, *after)


def _mm_tn(a, b, *, nb, out_dtype, name):
    m, ka = a.shape[-2:]
    n = b.shape[-1]
    steps = m // MM_TM

    def body(a_ref, b_ref, o_ref, acc_ref):
        i = pl.program_id(1)

        @pl.when(i == 0)
        def _():
            acc_ref[...] = jnp.zeros_like(acc_ref)

        acc_ref[...] += _dot(a_ref[...], b_ref[...], TN)

        @pl.when(i == steps - 1)
        def _():
            o_ref[...] = acc_ref[...].astype(out_dtype)

    def spec(arr, cols):
        if arr.ndim == 2:
            return pl.BlockSpec((MM_TM, cols), lambda j, i: (i, 0))
        return pl.BlockSpec((None, MM_TM, cols), lambda j, i: (j, i, 0))

    return pl.pallas_call(
        body,
        name=name,
        grid=(nb, steps),
        in_specs=[spec(a, ka), spec(b, n)],
        out_specs=pl.BlockSpec((None, ka, n), lambda j, i: (j, 0, 0)),
        out_shape=jax.ShapeDtypeStruct((nb, ka, n), out_dtype),
        scratch_shapes=[pltpu.VMEM((ka, n), F32)],
        compiler_params=_params(("parallel", "arbitrary")),
    )(a, b)


def _rstd(x):
    return lax.rsqrt(jnp.mean(x * x, axis=-1, keepdims=True) + RMS_EPS)


def _rms_bwd(x, g, dy):
    r = _rstd(x)
    u = dy * g
    dx = r * u - x * (r * r * r) * jnp.mean(u * x, axis=-1, keepdims=True)
    return dx, dy * x * r


def _row_spec(cols=D, tm=TM):
    return pl.BlockSpec((tm, cols), lambda i: (i, 0))


def _vec_spec(rows=1, cols=D):
    return pl.BlockSpec((rows, cols), lambda i: (0, 0))


def _rms_fwd(x, g, name):
    def body(x_ref, g_ref, o_ref):
        x = x_ref[...]
        o_ref[...] = (x * _rstd(x) * g_ref[...]).astype(BF16)

    return pl.pallas_call(
        body,
        name=name,
        grid=(LP // TM,),
        in_specs=[_row_spec(), _vec_spec()],
        out_specs=_row_spec(),
        out_shape=jax.ShapeDtypeStruct((LP, D), BF16),
        compiler_params=_params(("parallel",)),
    )(x, g)


def _resid_rms(h0, mix, g_post, g_next):
    def body(h0_ref, mix_ref, gp_ref, gn_ref, h1_ref, xn_ref):
        mix = mix_ref[...]
        h1 = h0_ref[...] + mix * _rstd(mix) * gp_ref[...]
        h1_ref[...] = h1
        xn_ref[...] = (h1 * _rstd(h1) * gn_ref[...]).astype(BF16)

    return pl.pallas_call(
        body,
        name="resid_rms",
        grid=(LP // TM,),
        in_specs=[_row_spec(), _row_spec(), _vec_spec(), _vec_spec()],
        out_specs=[_row_spec(), _row_spec()],
        out_shape=[jax.ShapeDtypeStruct((LP, D), F32), jax.ShapeDtypeStruct((LP, D), BF16)],
        compiler_params=_params(("parallel",)),
    )(h0, mix, g_post, g_next)


def _loss_head(h1, ffn, g_post, target):
    nblk = LP // QB

    def body(h1_ref, ffn_ref, g_ref, t_ref, dout_ref, dffn_ref, loss_ref, dg_ref):
        i = pl.program_id(0)

        @pl.when(i == 0)
        def _():
            loss_ref[...] = jnp.zeros_like(loss_ref)
            dg_ref[...] = jnp.zeros_like(dg_ref)

        ffn = ffn_ref[...]
        g = g_ref[...]
        out = h1_ref[...] + ffn * _rstd(ffn) * g
        err = jnp.where(i > 0, out - t_ref[...], 0.0)
        loss_ref[...] += 0.5 * jnp.sum(err * err) / D
        dout = err / D
        dout_ref[...] = dout
        dffn, dg = _rms_bwd(ffn, g, dout)
        dffn_ref[...] = dffn.astype(BF16)
        dg_ref[0:1, :] += jnp.sum(dg, axis=0, keepdims=True)

    return pl.pallas_call(
        body,
        name="loss_head",
        grid=(nblk,),
        in_specs=[
            _row_spec(tm=QB),
            _row_spec(tm=QB),
            _vec_spec(),
            pl.BlockSpec((QB, D), lambda i: (jnp.maximum(i - 1, 0), 0)),
        ],
        out_specs=[_row_spec(tm=QB), _row_spec(tm=QB), _vec_spec(8, 128), _vec_spec(8, D)],
        out_shape=[
            jax.ShapeDtypeStruct((LP, D), F32),
            jax.ShapeDtypeStruct((LP, D), BF16),
            jax.ShapeDtypeStruct((8, 128), F32),
            jax.ShapeDtypeStruct((8, D), F32),
        ],
        compiler_params=_params(("arbitrary",)),
    )(h1, ffn, g_post, target)


def _mid_bwd(dout, h1, dxn2, mix, g_post_mix, g_pre_ffn):
    def body(dout_ref, h1_ref, dxn_ref, mix_ref, gpm_ref, gpf_ref, dh1_ref, dmix_ref, dg_ref):
        i = pl.program_id(0)

        @pl.when(i == 0)
        def _():
            dg_ref[...] = jnp.zeros_like(dg_ref)

        dx, dg_ffn = _rms_bwd(h1_ref[...], gpf_ref[...], dxn_ref[...])
        dh1 = dout_ref[...] + dx
        dh1_ref[...] = dh1
        dmix, dg_mix = _rms_bwd(mix_ref[...], gpm_ref[...], dh1)
        dmix_ref[...] = dmix.astype(BF16)
        dg_ref[0:1, :] += jnp.sum(dg_mix, axis=0, keepdims=True)
        dg_ref[1:2, :] += jnp.sum(dg_ffn, axis=0, keepdims=True)

    return pl.pallas_call(
        body,
        name="mid_bwd",
        grid=(LP // TM,),
        in_specs=[_row_spec(), _row_spec(), _row_spec(), _row_spec(), _vec_spec(), _vec_spec()],
        out_specs=[_row_spec(), _row_spec(), _vec_spec(8, D)],
        out_shape=[
            jax.ShapeDtypeStruct((LP, D), F32),
            jax.ShapeDtypeStruct((LP, D), BF16),
            jax.ShapeDtypeStruct((8, D), F32),
        ],
        compiler_params=_params(("arbitrary",)),
    )(dout, h1, dxn2, mix, g_post_mix, g_pre_ffn)


def _first_bwd(dh1, h0, dxn1, g_pre_mix):
    def body(dh1_ref, h0_ref, dxn_ref, g_ref, dh0_ref, dg_ref):
        i = pl.program_id(0)

        @pl.when(i == 0)
        def _():
            dg_ref[...] = jnp.zeros_like(dg_ref)

        dx, dg = _rms_bwd(h0_ref[...], g_ref[...], dxn_ref[...])
        dh0_ref[...] = dh1_ref[...] + dx
        dg_ref[0:1, :] += jnp.sum(dg, axis=0, keepdims=True)

    return pl.pallas_call(
        body,
        name="first_bwd",
        grid=(LP // TM,),
        in_specs=[_row_spec(), _row_spec(), _row_spec(), _vec_spec()],
        out_specs=[_row_spec(), _vec_spec(8, D)],
        out_shape=[jax.ShapeDtypeStruct((LP, D), F32), jax.ShapeDtypeStruct((8, D), F32)],
        compiler_params=_params(("arbitrary",)),
    )(dh1, h0, dxn1, g_pre_mix)


def _prev_halo(i):
    return jnp.maximum(i * (TM // HALO) - 1, 0)


def _next_halo(i):
    return jnp.minimum((i + 1) * (TM // HALO), LP // HALO - 1)


def _down(x, s):
    return pltpu.roll(x, s, 0)


def _up(x, s):
    return pltpu.roll(x, x.shape[0] - s, 0)


def _conv_mix_fwd(hin, cw):
    def body(b_ref, c_ref, h_ref, cp_ref, hp_ref, w_ref, y_ref):
        i = pl.program_id(0)
        p = c_ref[...].astype(F32) * h_ref[...].astype(F32)
        pp = jnp.where(i > 0, cp_ref[...].astype(F32) * hp_ref[...].astype(F32), 0.0)
        ext = jnp.concatenate([pp, p], axis=0)
        w = [w_ref[t:t + 1, :] for t in range(3)]
        cv = w[2] * ext + w[1] * _down(ext, 1) + w[0] * _down(ext, 2)
        y_ref[...] = (b_ref[...].astype(F32) * cv[HALO:]).astype(BF16)

    def tile(s):
        return pl.BlockSpec((None, TM, D), lambda i: (s, i, 0))

    def prev(s):
        return pl.BlockSpec((None, HALO, D), lambda i: (s, _prev_halo(i), 0))

    return pl.pallas_call(
        body,
        name="conv_mix_fwd",
        grid=(LP // TM,),
        in_specs=[tile(0), tile(1), tile(2), prev(1), prev(2), _vec_spec(3, D)],
        out_specs=_row_spec(),
        out_shape=jax.ShapeDtypeStruct((LP, D), BF16),
        compiler_params=_params(("parallel",)),
    )(hin, hin, hin, hin, hin, cw)


def _conv_mix_bwd(hin, dy, cw, dhin):
    last = LP // TM - 1

    def body(b_ref, c_ref, h_ref, dy_ref, cp_ref, hp_ref, bn_ref, dyn_ref, w_ref, _, out_ref, dw_ref):
        i = pl.program_id(0)

        @pl.when(i == 0)
        def _():
            dw_ref[...] = jnp.zeros_like(dw_ref)

        b = b_ref[...].astype(F32)
        c = c_ref[...].astype(F32)
        h = h_ref[...].astype(F32)
        dy = dy_ref[...].astype(F32)
        w = [w_ref[t:t + 1, :] for t in range(3)]
        p = c * h
        pp = jnp.where(i > 0, cp_ref[...].astype(F32) * hp_ref[...].astype(F32), 0.0)
        ext = jnp.concatenate([pp, p], axis=0)
        p1 = _down(ext, 1)[HALO:]
        p2 = _down(ext, 2)[HALO:]
        cv = w[2] * p + w[1] * p1 + w[0] * p2
        out_ref[0] = (dy * cv).astype(BF16)
        dcv = dy * b
        dcvn = jnp.where(i < last, dyn_ref[...].astype(F32) * bn_ref[...].astype(F32), 0.0)
        dext = jnp.concatenate([dcv, dcvn], axis=0)
        dp = (w[2] * dext + w[1] * _up(dext, 1) + w[0] * _up(dext, 2))[:TM]
        out_ref[1] = (dp * h).astype(BF16)
        out_ref[2] = (dp * c).astype(BF16)
        dw_ref[0:1, :] += jnp.sum(dcv * p2, axis=0, keepdims=True)
        dw_ref[1:2, :] += jnp.sum(dcv * p1, axis=0, keepdims=True)
        dw_ref[2:3, :] += jnp.sum(dcv * p, axis=0, keepdims=True)

    def tile(s):
        return pl.BlockSpec((None, TM, D), lambda i: (s, i, 0))

    def prev(s):
        return pl.BlockSpec((None, HALO, D), lambda i: (s, _prev_halo(i), 0))

    return pl.pallas_call(
        body,
        name="conv_mix_bwd",
        grid=(LP // TM,),
        in_specs=[
            tile(0), tile(1), tile(2), _row_spec(),
            prev(1), prev(2),
            pl.BlockSpec((None, HALO, D), lambda i: (0, _next_halo(i), 0)),
            pl.BlockSpec((HALO, D), lambda i: (_next_halo(i), 0)),
            _vec_spec(3, D),
            pl.BlockSpec(memory_space=pl.ANY),
        ],
        out_specs=[pl.BlockSpec((3, TM, D), lambda i: (0, i, 0)), _vec_spec(8, D)],
        out_shape=[jax.ShapeDtypeStruct((N_DEV, LP, D), BF16), jax.ShapeDtypeStruct((8, D), F32)],
        input_output_aliases={9: 0},
        compiler_params=_params(("arbitrary",)),
    )(hin, hin, hin, dy, hin, hin, hin, dy, cw, dhin)


GELU_K = math.sqrt(2.0 / math.pi)
GELU_A = 0.044715


def _gelu_and_grad(x):
    x2 = x * x
    t = jnp.tanh(x * (GELU_K + (GELU_K * GELU_A) * x2))
    s = 0.5 + 0.5 * t
    grad = s * (1.0 + x * (1.0 - t) * (GELU_K + (3.0 * GELU_K * GELU_A) * x2))
    return x * s, grad


def _ffn_act_fwd(ug, cw4):
    def body(u_ref, g_ref, up_ref, w_ref, o_ref):
        i = pl.program_id(1)
        u = u_ref[...].astype(F32)
        up = jnp.where(i > 0, up_ref[...].astype(F32), 0.0)
        ext = jnp.concatenate([up, u], axis=0)
        w = [w_ref[t:t + 1, :] for t in range(3)]
        uc = (w[2] * ext + w[1] * _down(ext, 1) + w[0] * _down(ext, 2))[HALO:]
        gelu, _ = _gelu_and_grad(uc)
        o_ref[...] = (gelu * g_ref[...].astype(F32)).astype(BF16)

    return pl.pallas_call(
        body,
        name="ffn_act_fwd",
        grid=(N_FB, LP // TM),
        in_specs=[
            pl.BlockSpec((None, TM, FB), lambda j, i: (j, i, 0)),
            pl.BlockSpec((None, TM, FB), lambda j, i: (j + N_FB, i, 0)),
            pl.BlockSpec((None, HALO, FB), lambda j, i: (j, _prev_halo(i), 0)),
            pl.BlockSpec((None, 3, FB), lambda j, i: (j, 0, 0)),
        ],
        out_specs=pl.BlockSpec((None, TM, FB), lambda j, i: (j, i, 0)),
        out_shape=jax.ShapeDtypeStruct((N_FB, LP, FB), BF16),
        compiler_params=_params(("parallel", "parallel")),
    )(ug, ug, ug, cw4)


def _ffn_act_bwd(ug, dhid, cw4):
    last = LP // TM - 1
    n = TM + 2 * HALO

    def body(u_ref, g_ref, dh_ref, up_ref, un_ref, gn_ref, dhn_ref, w_ref, dug_ref, dw_ref):
        i = pl.program_id(1)

        @pl.when(i == 0)
        def _():
            dw_ref[...] = jnp.zeros_like(dw_ref)

        w = [w_ref[t:t + 1, :] for t in range(3)]
        u = u_ref[...].astype(F32)
        up = jnp.where(i > 0, up_ref[...].astype(F32), 0.0)
        ext = jnp.concatenate([up, u, un_ref[...].astype(F32)], axis=0)
        u1 = _down(ext, 1)
        u2 = _down(ext, 2)
        uc = w[2] * ext + w[1] * u1 + w[0] * u2
        gelu, ggrad = _gelu_and_grad(uc)
        zeros = jnp.zeros((HALO, FB), F32)
        gext = jnp.concatenate([zeros, g_ref[...].astype(F32), gn_ref[...].astype(F32)], axis=0)
        dhn = jnp.where(i < last, dhn_ref[...].astype(F32), 0.0)
        dhext = jnp.concatenate([zeros, dh_ref[...].astype(F32), dhn], axis=0)
        dug_ref[1] = (dhext * gelu)[HALO:HALO + TM].astype(BF16)
        duc = dhext * gext * ggrad
        du = w[2] * duc + w[1] * _up(duc, 1) + w[0] * _up(duc, 2)
        dug_ref[0] = du[HALO:HALO + TM].astype(BF16)
        row = lax.broadcasted_iota(jnp.int32, (n, 1), 0)
        own = jnp.where((row >= HALO) & (row < HALO + TM), duc, 0.0)
        dw_ref[0:1, :] += jnp.sum(own * u2, axis=0, keepdims=True)
        dw_ref[1:2, :] += jnp.sum(own * u1, axis=0, keepdims=True)
        dw_ref[2:3, :] += jnp.sum(own * ext, axis=0, keepdims=True)

    def tile(off):
        return pl.BlockSpec((None, TM, FB), lambda j, i: (j + off, i, 0))

    def nxt(off):
        return pl.BlockSpec((None, HALO, FB), lambda j, i: (j + off, _next_halo(i), 0))

    return pl.pallas_call(
        body,
        name="ffn_act_bwd",
        grid=(N_FB, LP // TM),
        in_specs=[
            tile(0), tile(N_FB), tile(0),
            pl.BlockSpec((None, HALO, FB), lambda j, i: (j, _prev_halo(i), 0)),
            nxt(0), nxt(N_FB), nxt(0),
            pl.BlockSpec((None, 3, FB), lambda j, i: (j, 0, 0)),
        ],
        out_specs=[
            pl.BlockSpec((2, None, TM, FB), lambda j, i: (0, j, i, 0)),
            pl.BlockSpec((None, 8, FB), lambda j, i: (j, 0, 0)),
        ],
        out_shape=[jax.ShapeDtypeStruct((2, N_FB, LP, FB), BF16), jax.ShapeDtypeStruct((N_FB, 8, FB), F32)],
        compiler_params=_params(("parallel", "arbitrary")),
    )(ug, ug, dhid, ug, ug, ug, dhid, cw4)


def _gate_fwd(bc, ba, hin, bgate):
    def body(bc_ref, ba_ref, gc_ref, ga_ref, b_ref, o_ref):
        b = b_ref[...]
        sc = jax.nn.sigmoid(gc_ref[...].astype(F32) + b[0:1])
        sa = jax.nn.sigmoid(ga_ref[...].astype(F32) + b[1:2])
        o_ref[...] = (sc * bc_ref[...].astype(F32) + sa * ba_ref[...].astype(F32)).astype(BF16)

    def tile(s):
        return pl.BlockSpec((None, TM, D), lambda i: (s, i, 0))

    return pl.pallas_call(
        body,
        name="gate_fwd",
        grid=(LP // TM,),
        in_specs=[_row_spec(), _row_spec(), tile(6), tile(7), _vec_spec(2, D)],
        out_specs=_row_spec(),
        out_shape=jax.ShapeDtypeStruct((LP, D), BF16),
        compiler_params=_params(("parallel",)),
    )(bc, ba, hin, hin, bgate)


def _gate_bwd(dm, bc, ba, hin, bgate):
    def body(dm_ref, bc_ref, ba_ref, gc_ref, ga_ref, b_ref, dbc_ref, dba_ref, dg_ref, db_ref):
        i = pl.program_id(0)

        @pl.when(i == 0)
        def _():
            db_ref[...] = jnp.zeros_like(db_ref)

        b = b_ref[...]
        dm = dm_ref[...].astype(F32)
        sc = jax.nn.sigmoid(gc_ref[...].astype(F32) + b[0:1])
        sa = jax.nn.sigmoid(ga_ref[...].astype(F32) + b[1:2])
        dbc_ref[...] = (dm * sc).astype(BF16)
        dba_ref[...] = (dm * sa).astype(BF16)
        dgc = dm * bc_ref[...].astype(F32) * sc * (1.0 - sc)
        dga = dm * ba_ref[...].astype(F32) * sa * (1.0 - sa)
        dg_ref[0] = dgc.astype(BF16)
        dg_ref[1] = dga.astype(BF16)
        db_ref[0:1, :] += jnp.sum(dgc, axis=0, keepdims=True)
        db_ref[1:2, :] += jnp.sum(dga, axis=0, keepdims=True)

    def tile(s):
        return pl.BlockSpec((None, TM, D), lambda i: (s, i, 0))

    return pl.pallas_call(
        body,
        name="gate_bwd",
        grid=(LP // TM,),
        in_specs=[_row_spec(), _row_spec(), _row_spec(), tile(6), tile(7), _vec_spec(2, D)],
        out_specs=[_row_spec(), _row_spec(), pl.BlockSpec((2, TM, D), lambda i: (3, i, 0)), _vec_spec(8, D)],
        out_shape=[jax.ShapeDtypeStruct((LP, D), BF16)] * 2
        + [jax.ShapeDtypeStruct((N_DEV, LP, D), BF16), jax.ShapeDtypeStruct((8, D), F32)],
        compiler_params=_params(("arbitrary",)),
    )(dm, bc, ba, hin, hin, bgate)


Z_LINEAR = 30.0


def _softplus(z):
    return jnp.maximum(z, jnp.log(1.0 + jnp.exp(jnp.minimum(z, Z_LINEAR))))


def _cumsum_matrix(inclusive, reverse):
    r = lax.broadcasted_iota(jnp.int32, (QB, 2 * QB), 0)
    c = lax.broadcasted_iota(jnp.int32, (QB, 2 * QB), 1)
    if reverse:
        tri = r > c
    elif inclusive:
        tri = r <= c
    else:
        tri = r < c
    return jnp.where((c >= QB) | tri, 1.0, 0.0).astype(BF16)


def _split_dot(x, m2):
    bits = lax.bitcast_convert_type(x, jnp.uint32) & jnp.uint32(0xFFFF0000)
    hi = lax.bitcast_convert_type(bits, F32)
    return _dot(jnp.concatenate([hi.astype(BF16), (x - hi).astype(BF16)], axis=1), m2)


def _stack_heads(x):
    return jnp.concatenate(_split_heads(x), axis=0)


def _block_mask(i, j, row0):
    row = lax.broadcasted_iota(jnp.int32, (AQ - row0, QB), 0) + (i * AQ + row0)
    col = lax.broadcasted_iota(jnp.int32, (AQ - row0, QB), 1) + j * QB
    return (col < row) & (col >= PAD)


def _key_block(ref, j):
    return ref[pl.ds(pl.multiple_of(j * QB, QB), QB), :]


def _split_heads(x):
    head_a = lax.broadcasted_iota(jnp.int32, x.shape, 1) < HEAD_LANES
    zero = jnp.zeros_like(x)
    return jnp.where(head_a, x, zero), jnp.where(head_a, zero, x)


def _attn_fwd(hin, ex_arrs, gathers):
    ne = len(ex_arrs)
    npair, nq = D // QB, LP // AQ

    def body(*refs):
        q_ref, k_ref, v_ref = refs[:3]
        o_ref, lt_ref = refs[3 + ne:5 + ne]
        c_sc, acc_sc = refs[5 + 2 * ne:7 + 2 * ne]
        p, i = pl.program_id(0), pl.program_id(1)

        def copies():
            return _exchange_copies(refs[3:3 + ne], refs[5 + ne:5 + 2 * ne], gathers, *refs[7 + 2 * ne:])

        @pl.when((p == 0) & (i == 0))
        def _():
            for cp in copies():
                cp.start()

        um = _cumsum_matrix(False, True)
        um2 = jnp.concatenate([um, um], axis=0)
        q = (q_ref[...].astype(F32) * SCALE).astype(BF16)
        c_sc[...] = jnp.zeros_like(c_sc)
        acc_sc[...] = jnp.zeros_like(acc_sc)

        def step(j, masked, row0=0):
            rows = slice(row0, AQ)
            z2 = _dot(q[rows], _stack_heads(_key_block(k_ref, j)), NT)
            mask = _block_mask(i, j, row0) if masked else None
            a2 = []
            for hd in range(2):
                z = z2[:, hd * QB:(hd + 1) * QB]
                sp = _softplus(z)
                r = _split_dot(jnp.where(mask, sp, 0.0) if masked else sp, um2)
                a = jnp.exp(z - sp - c_sc[hd, rows] - r[:, :QB])
                if masked:
                    a = jnp.where(mask, a, 0.0)
                a2.append(a.astype(BF16))
                c_sc[hd, rows] += r[:, QB:]
            acc_sc[rows] += _dot(jnp.concatenate(a2, axis=1), _stack_heads(_key_block(v_ref, j)))

        for t in reversed(range(KPQ)):
            step(KPQ * i + t, True, t * QB)

        groups = jnp.maximum(i - 1, 0)

        @pl.loop(0, groups // 2)
        def _(t):
            for u in range(2 * KPQ):
                step(KPQ * (i - 2 * t) - 1 - u, False)

        @pl.when(groups % 2 == 1)
        def _():
            for u in range(KPQ):
                step(2 * KPQ - 1 - u, False)

        @pl.when(i > 0)
        def _():
            for u in reversed(range(1, KPQ)):
                step(u, False)
            step(0, True)

        head_a = lax.broadcasted_iota(jnp.int32, (AQ, QB), 1) < HEAD_LANES
        o_ref[...] = acc_sc[...].astype(BF16)
        lt_ref[...] = jnp.where(head_a, c_sc[0], c_sc[1])

        @pl.when((p == npair - 1) & (i == nq - 1))
        def _():
            for cp in copies():
                cp.wait()

    def seq(s):
        return pl.BlockSpec((None, LP, QB), lambda p, i: (s, 0, p))

    return pl.pallas_call(
        body,
        name="attn_fwd",
        grid=(npair, nq),
        in_specs=[pl.BlockSpec((None, AQ, QB), lambda p, i: (3, i, p)), seq(4), seq(5)] + [ANY_SPEC] * ne,
        out_specs=[pl.BlockSpec((AQ, QB), lambda p, i: (i, p))] * 2 + [ANY_SPEC] * ne,
        out_shape=[jax.ShapeDtypeStruct((LP, D), BF16), jax.ShapeDtypeStruct((LP, D), F32)]
        + _exchange_shapes(ex_arrs, gathers),
        scratch_shapes=[pltpu.VMEM((2, AQ, QB), F32), pltpu.VMEM((AQ, QB), F32)] + _exchange_sems(ne),
        compiler_params=_params(("arbitrary", "arbitrary")),
    )(hin, hin, hin, *ex_arrs)


def _attn_bwd(hin, do, lt, dhin, ex_arrs, gathers):
    ne = len(ex_arrs)
    npair, nq = D // QB, LP // AQ

    def body(*refs):
        q_ref, k_ref, v_ref, do_ref, lt_ref = refs[:5]
        out_ref = refs[6 + ne]
        psp_sc, pg_sc, dq_sc, dk_acc, dv_acc = refs[7 + 2 * ne:12 + 2 * ne]
        p, i = pl.program_id(0), pl.program_id(1)

        def copies():
            return _exchange_copies(refs[6:6 + ne], refs[7 + ne:7 + 2 * ne], gathers, *refs[12 + 2 * ne:])

        @pl.when((p == 0) & (i == 0))
        def _():
            for cp in copies():
                cp.start()

        @pl.when(i == 0)
        def _():
            dk_acc[...] = jnp.zeros_like(dk_acc)
            dv_acc[...] = jnp.zeros_like(dv_acc)

        um_sp = _cumsum_matrix(True, False)
        um_sp2 = jnp.concatenate([um_sp, um_sp], axis=0)
        um_g = _cumsum_matrix(False, False)
        q = (q_ref[...].astype(F32) * SCALE).astype(BF16)
        do = do_ref[...]
        q_t, do_t = q.T, do.T
        head_a = lax.broadcasted_iota(jnp.int32, (AQ, QB), 1) < HEAD_LANES
        dim_head_a = lax.broadcasted_iota(jnp.int32, (QB, QB), 0) < HEAD_LANES
        lt = lt_ref[...]
        lt_sw = pltpu.roll(lt, HEAD_LANES, 1)
        totals = (jnp.where(head_a, lt, lt_sw), jnp.where(head_a, lt_sw, lt))
        psp_sc[...] = jnp.zeros_like(psp_sc)
        pg_sc[...] = jnp.zeros_like(pg_sc)
        dq_sc[...] = jnp.zeros_like(dq_sc)

        def step(j, masked, row0=0):
            rows = slice(row0, AQ)
            k2 = _stack_heads(_key_block(k_ref, j))
            z2 = _dot(q[rows], k2, NT)
            da2 = _dot(do[rows], _stack_heads(_key_block(v_ref, j)), NT)
            mask = _block_mask(i, j, row0) if masked else None
            a2, dz2 = [], []
            for hd in range(2):
                z = z2[:, hd * QB:(hd + 1) * QB]
                sp = _softplus(z)
                r = _split_dot(jnp.where(mask, sp, 0.0) if masked else sp, um_sp2)
                a = jnp.exp(z - sp - (totals[hd][rows] - psp_sc[hd, rows] - r[:, :QB]))
                if masked:
                    a = jnp.where(mask, a, 0.0)
                g = a * da2[:, hd * QB:(hd + 1) * QB]
                rg = _dot(g.astype(BF16), um_g)
                dz = g - jnp.exp(z - sp) * (g + pg_sc[hd, rows] + rg[:, :QB])
                if masked:
                    dz = jnp.where(mask, dz, 0.0)
                a2.append(a.astype(BF16))
                dz2.append(dz.astype(BF16))
                psp_sc[hd, rows] += r[:, QB:]
                pg_sc[hd, rows] += rg[:, QB:]
            dz2 = jnp.concatenate(dz2, axis=1)
            dq_sc[rows] += _dot(dz2, k2)
            dk2 = _dot(q_t[:, rows], dz2)
            dv2 = _dot(do_t[:, rows], jnp.concatenate(a2, axis=1))
            dk_acc[j] += jnp.where(dim_head_a, dk2[:, :QB], dk2[:, QB:])
            dv_acc[j] += jnp.where(dim_head_a, dv2[:, :QB], dv2[:, QB:])

        @pl.when(i > 0)
        def _():
            step(0, True)
            for u in range(1, KPQ):
                step(u, False)

        groups = jnp.maximum(i - 1, 0)

        @pl.loop(0, groups // 2)
        def _(t):
            for u in range(2 * KPQ):
                step(KPQ * (2 * t + 1) + u, False)

        @pl.when(groups % 2 == 1)
        def _():
            for u in range(KPQ):
                step(KPQ * (i - 1) + u, False)

        for t in range(KPQ):
            step(KPQ * i + t, True, t * QB)

        out_ref[0, pl.ds(pl.multiple_of(i * AQ, AQ), AQ), :] = (dq_sc[...] * SCALE).astype(BF16)

        @pl.when(i == nq - 1)
        def _():
            @pl.loop(0, LP // QB)
            def _(b):
                keys = pl.ds(pl.multiple_of(b * QB, QB), QB)
                out_ref[1, keys, :] = dk_acc[b].T.astype(BF16)
                out_ref[2, keys, :] = dv_acc[b].T.astype(BF16)

        @pl.when((p == npair - 1) & (i == nq - 1))
        def _():
            for cp in copies():
                cp.wait()

    def seq(s):
        return pl.BlockSpec((None, LP, QB), lambda p, i: (s, 0, p))

    blk = pl.BlockSpec((AQ, QB), lambda p, i: (i, p))
    return pl.pallas_call(
        body,
        name="attn_bwd",
        grid=(npair, nq),
        in_specs=[pl.BlockSpec((None, AQ, QB), lambda p, i: (3, i, p)), seq(4), seq(5), blk, blk]
        + [ANY_SPEC] * (1 + ne),
        out_specs=[pl.BlockSpec((3, LP, QB), lambda p, i: (1, 0, p))] + [ANY_SPEC] * ne,
        out_shape=[jax.ShapeDtypeStruct((N_DEV, LP, D), BF16)] + _exchange_shapes(ex_arrs, gathers),
        input_output_aliases={5: 0},
        scratch_shapes=[pltpu.VMEM((2, AQ, QB), F32)] * 2 + [pltpu.VMEM((AQ, QB), F32)]
        + [pltpu.VMEM((LP // QB, QB, QB), F32)] * 2 + _exchange_sems(ne),
        compiler_params=_params(("arbitrary", "arbitrary")),
    )(hin, hin, hin, do, lt, dhin, *ex_arrs)


def _adamw_math(w, g, m, v):
    m_new = ADAM_B1 * m + (1.0 - ADAM_B1) * g
    v_new = ADAM_B2 * v + (1.0 - ADAM_B2) * jnp.square(g)
    m_hat = m_new / (1.0 - ADAM_B1 ** ADAM_STEP)
    v_hat = v_new / (1.0 - ADAM_B2 ** ADAM_STEP)
    return -ADAM_LR * (m_hat / (jnp.sqrt(v_hat) + ADAM_EPS) + ADAM_WD * w), m_new, v_new


def _adamw_small(ws, gs, ms, vs):
    n = len(ws)

    def body(*refs):
        for t in range(n):
            w_ref, g_ref, m_ref, v_ref = (refs[s * n + t] for s in range(4))
            d_ref, nm_ref, nv_ref = (refs[(4 + s) * n + t] for s in range(3))
            d_ref[...], nm_ref[...], nv_ref[...] = _adamw_math(w_ref[...], g_ref[...], m_ref[...], v_ref[...])

    vmem = pl.BlockSpec(memory_space=pltpu.VMEM)
    res = pl.pallas_call(
        body,
        name="adamw_small",
        in_specs=[vmem] * (4 * n),
        out_specs=[vmem] * (3 * n),
        out_shape=[jax.ShapeDtypeStruct(w.shape, F32) for w in ws] * 3,
    )(*ws, *gs, *ms, *vs)
    return res[:n], res[n:2 * n], res[2 * n:]


def _adamw(pieces, w, m, v, *, rows, row_off, tr, name):
    npieces, _, cols = pieces.shape
    ob = row_off // tr

    def body(p_ref, w_ref, m_ref, v_ref, g_ref, d_ref, nm_ref, nv_ref):
        g = p_ref[0].astype(F32)
        for s in range(1, npieces):
            g = g + p_ref[s].astype(F32)
        g_ref[...] = g
        d_ref[...], nm_ref[...], nv_ref[...] = _adamw_math(w_ref[...], g, m_ref[...], v_ref[...])

    spec = pl.BlockSpec((tr, cols), lambda i: (i, 0))
    return pl.pallas_call(
        body,
        name=name,
        grid=(rows // tr,),
        in_specs=[pl.BlockSpec((npieces, tr, cols), lambda i: (0, ob + i, 0)), spec, spec, spec],
        out_specs=[spec] * 4,
        out_shape=[jax.ShapeDtypeStruct((rows, cols), F32)] * 4,
        compiler_params=_params(("parallel",)),
    )(pieces, w, m, v)


def _sum_pieces(pieces, name):
    npieces, rows, cols = pieces.shape

    def body(p_ref, o_ref):
        acc = p_ref[0]
        for s in range(1, npieces):
            acc = acc + p_ref[s]
        o_ref[...] = acc

    return pl.pallas_call(
        body,
        name=name,
        in_specs=[pl.BlockSpec(memory_space=pltpu.VMEM)],
        out_specs=pl.BlockSpec(memory_space=pltpu.VMEM),
        out_shape=jax.ShapeDtypeStruct((rows, cols), pieces.dtype),
    )(pieces)


SMALL_ROWS = 48
CWF_PAD = 384
GRAD_ROWS = 72
G_META, G_PRE_MIX, G_MID, G_POST_FFN, G_CW_MIX, G_B_GATE, G_CW_FFN, G_LOSS = 0, 16, 24, 32, 40, 48, 56, 17


def kernel(x, meta_tokens, g_pre_mix, w_in, conv_w_mix, w_proj_conv, w_proj_attn, b_gate, w_out, g_post_mix, g_pre_ffn, w_up_gate, conv_w_ffn, w_down, g_post_ffn, loss_target, m_meta_tokens, m_g_pre_mix, m_w_in, m_conv_w_mix, m_w_proj_conv, m_w_proj_attn, m_b_gate, m_w_out, m_g_post_mix, m_g_pre_ffn, m_w_up_gate, m_conv_w_ffn, m_w_down, m_g_post_ffn, v_meta_tokens, v_g_pre_mix, v_w_in, v_conv_w_mix, v_w_proj_conv, v_w_proj_attn, v_b_gate, v_w_out, v_g_post_mix, v_g_pre_ffn, v_w_up_gate, v_conv_w_ffn, v_w_down, v_g_post_ffn):
    me = 4 * lax.axis_index("x") + 2 * lax.axis_index("y") + lax.axis_index("c")

    def rows_to(a, n):
        return jnp.pad(a, ((0, n - a.shape[0]), (0, 0)))

    small_shard = jnp.concatenate(
        [meta_tokens, rows_to(conv_w_mix[0], 8), rows_to(b_gate[0], 8),
         rows_to(jnp.pad(conv_w_ffn[0], ((0, 0), (0, CWF_PAD - R_DOWN))).reshape(9, 128), 16)], axis=0)
    wshard = jnp.concatenate([w_proj_conv[0], w_proj_attn[0], w_out[0], w_down[0]], axis=0).astype(BF16)
    w_in_all, small_all = _gather_two_level([w_in[0].astype(BF16), small_shard], "gather_in")

    def unshard(rows):
        return rows.transpose(1, 0, 2).reshape(rows.shape[1], N_DEV * rows.shape[2])

    meta = unshard(small_all[:, 0:16])
    cw_mix = unshard(small_all[:, 16:19])
    bgate = unshard(small_all[:, 24:26])
    cw_ffn = unshard(small_all[:, 32:41].reshape(N_DEV, 3, CWF_PAD)[:, :, :R_DOWN])
    cw4 = cw_ffn.reshape(3, N_FB, FB).transpose(1, 0, 2)
    h0 = jnp.concatenate([jnp.zeros((PAD, D), F32), meta, x[0]], axis=0)

    xn1 = _rms_fwd(h0, g_pre_mix, "rms_pre_mix")
    hin = _mm(xn1, w_in_all, w_rows=D, trans_w=False, out_dtype=BF16, name="mm_in")
    y_conv = _conv_mix_fwd(hin, cw_mix)
    o, lt, wpack, w_ug = _attn_fwd(hin, [wshard, w_up_gate[0].astype(BF16)], (True, True))
    w_pc = wpack[:, O_PC:O_PA].reshape(1, D, D)
    w_pa = wpack[:, O_PA:O_OUT].reshape(1, D, D)
    w_o = wpack[:, O_OUT:O_DOWN].reshape(1, D, D)
    w_dn = wpack[:, O_DOWN:].reshape(N_FB, FB, D)
    bc = _mm(y_conv, w_pc, w_rows=D, trans_w=False, out_dtype=BF16, name="mm_proj_conv")[0]
    ba = _mm(o, w_pa, w_rows=D, trans_w=False, out_dtype=BF16, name="mm_proj_attn")[0]
    merged = _gate_fwd(bc, ba, hin, bgate)
    mix = _mm(merged, w_o, w_rows=D, trans_w=False, out_dtype=F32, name="mm_out")[0]
    h1, xn2 = _resid_rms(h0, mix, g_post_mix, g_pre_ffn)
    ug = _mm(xn2, w_ug, w_rows=D, trans_w=False, out_dtype=BF16, name="mm_up_gate")
    hid = _ffn_act_fwd(ug, cw4)
    ffn = _mm_sum(hid, w_dn, w_rows=FB, trans_w=False, out_dtype=F32, name="mm_down")
    dout, dffn, loss8, dg_post_ffn = _loss_head(h1, ffn, g_post_ffn, loss_target[0])

    dhid = _mm(dffn, w_dn, w_rows=FB, trans_w=True, out_dtype=BF16, name="mm_down_dx")
    gw_dn = _mm_tn(hid, dffn, nb=N_FB, out_dtype=BF16, name="mm_down_dw")
    dug, dcw4 = _ffn_act_bwd(ug, dhid, cw4)
    dug = dug.reshape(2 * N_FB, LP, FB)
    dxn2 = _mm_sum(dug, w_ug, w_rows=D, trans_w=True, out_dtype=F32, name="mm_up_gate_dx")
    gw_ug = _mm_tn(xn2, dug, nb=N_DEV, out_dtype=BF16, name="mm_up_gate_dw")
    dh1, dmix, dg_mid = _mid_bwd(dout, h1, dxn2, mix, g_post_mix, g_pre_ffn)
    dmerged = _mm(dmix, w_o, w_rows=D, trans_w=True, out_dtype=BF16, name="mm_out_dx")[0]
    gw_out = _mm_tn(merged, dmix, nb=1, out_dtype=BF16, name="mm_out_dw")
    dbc, dba, dhin, db_gate = _gate_bwd(dmerged, bc, ba, hin, bgate)
    dy_conv = _mm(dbc, w_pc, w_rows=D, trans_w=True, out_dtype=BF16, name="mm_proj_conv_dx")[0]
    gw_pc = _mm_tn(y_conv, dbc, nb=1, out_dtype=BF16, name="mm_proj_conv_dw")
    do = _mm(dba, w_pa, w_rows=D, trans_w=True, out_dtype=BF16, name="mm_proj_attn_dx")[0]
    gw_pa = _mm_tn(o, dba, nb=1, out_dtype=BF16, name="mm_proj_attn_dw")
    dhin, dcw_mix = _conv_mix_bwd(hin, dy_conv, cw_mix, dhin)
    gpack = jnp.concatenate(
        [gw_pc.reshape(N_DEV, R_PROJ, D), gw_pa.reshape(N_DEV, R_PROJ, D), gw_out.reshape(N_DEV, R_PROJ, D),
         gw_dn.reshape(N_DEV, R_DOWN, D)], axis=1)
    dcw_ffn = dcw4[:, :3].transpose(1, 0, 2).reshape(3, D_FF)
    small_a = jnp.concatenate(
        [dg_mid, dg_post_ffn, dcw_mix, db_gate,
         jnp.pad(dcw_ffn.reshape(-1), (0, 16 * D - 3 * D_FF)).reshape(16, D)], axis=0)
    dhin, rpack, rug, rsmall_a = _attn_bwd(hin, do, lt, dhin, [gpack, gw_ug, small_a], (False, False, True))
    gw_in = _mm_tn(xn1, dhin, nb=N_DEV, out_dtype=BF16, name="mm_in_dw")
    send_sems, recv_sems, gw_thru, land_thru, token = _scatter_start(gw_in)
    dxn1 = _mm_sum(dhin, w_in_all, w_rows=D, trans_w=True, out_dtype=F32, name="mm_in_dx", after=token)
    dh0, dg_pre_mix = _first_bwd(dh1, h0, dxn1, g_pre_mix)
    small_b = jnp.concatenate(
        [dh0[PAD:OFF], lax.dynamic_update_slice(dg_pre_mix, loss8[0:1], (G_LOSS - G_PRE_MIX, 0))], axis=0)

    def big(pieces, w, m, v, rows, row_off, tr, name):
        g, d, nm, nv = _adamw(pieces, w[0], m[0], v[0], rows=rows, row_off=row_off, tr=tr, name=name)
        return g[None], d[None], nm[None], nv[None]

    r_pc = big(rpack, w_proj_conv, m_w_proj_conv, v_w_proj_conv, R_PROJ, O_PC, R_PROJ, "adamw_proj_conv")
    r_pa = big(rpack, w_proj_attn, m_w_proj_attn, v_w_proj_attn, R_PROJ, O_PA, R_PROJ, "adamw_proj_attn")
    r_out = big(rpack, w_out, m_w_out, v_w_out, R_PROJ, O_OUT, R_PROJ, "adamw_out")
    r_dn = big(rpack, w_down, m_w_down, v_w_down, R_DOWN, O_DOWN, 32, "adamw_down")
    r_ug = big(rug, w_up_gate, m_w_up_gate, v_w_up_gate, D, 0, 256, "adamw_up_gate")
    (rsmall_b,) = _exchange([small_b], (True,), "gather_small_grads",
                            after=(r_pc[1], r_pa[1], r_out[1], r_dn[1], r_ug[1]))
    gs = _sum_pieces(jnp.concatenate([rsmall_b, rsmall_a], axis=1), "sum_small_grads")
    grad_x = dh0[OFF:]
    loss = gs[G_LOSS, 0]

    def cols(a, width):
        return lax.dynamic_slice_in_dim(a, me * width, width, axis=1)

    g_meta = cols(gs[G_META:G_META + N_META], 128)
    g_gpm, g_gff = gs[G_PRE_MIX:G_PRE_MIX + 1], gs[G_POST_FFN:G_POST_FFN + 1]
    g_gpo, g_gpf = gs[G_MID:G_MID + 1], gs[G_MID + 1:G_MID + 2]
    g_cwm = cols(gs[G_CW_MIX:G_CW_MIX + 3], 128)[None]
    g_bg = cols(gs[G_B_GATE:G_B_GATE + 2], 128)[None]
    g_cwf = cols(gs[G_CW_FFN:G_CW_FFN + 9].reshape(-1)[:3 * D_FF].reshape(3, D_FF), R_DOWN)[None]

    small_w = [meta_tokens, g_pre_mix, conv_w_mix, b_gate, g_post_mix, g_pre_ffn, conv_w_ffn, g_post_ffn]
    small_g = [g_meta, g_gpm, g_cwm, g_bg, g_gpo, g_gpf, g_cwf, g_gff]
    small_m = [m_meta_tokens, m_g_pre_mix, m_conv_w_mix, m_b_gate, m_g_post_mix, m_g_pre_ffn, m_conv_w_ffn, m_g_post_ffn]
    small_v = [v_meta_tokens, v_g_pre_mix, v_conv_w_mix, v_b_gate, v_g_post_mix, v_g_pre_ffn, v_conv_w_ffn, v_g_post_ffn]

    s_g = small_g
    s_d, s_m, s_v = _adamw_small(small_w, small_g, small_m, small_v)

    gw_done, landed = _scatter_wait(send_sems, recv_sems, gw_thru, land_thru, (s_d[0],))
    rin = lax.dynamic_update_index_in_dim(landed, lax.dynamic_index_in_dim(gw_done, me, 0, keepdims=False), me, 0)
    r_in = big(rin, w_in, m_w_in, v_w_in, D, 0, 256, "adamw_in")

    def ordered(k, smalls):
        meta, gpm, cwm, bg, gpo, gpf, cwf, gff = smalls
        return [meta, gpm, r_in[k], cwm, r_pc[k], r_pa[k], bg, r_out[k], gpo, gpf, r_ug[k], cwf, r_dn[k], gff]

    return (loss, grad_x[None], *ordered(0, s_g), *ordered(1, s_d), *ordered(2, s_m), *ordered(3, s_v))
```

```python
import functools
import math

import jax
import jax.numpy as jnp
from jax import lax
from jax.experimental import pallas as pl
from jax.experimental.pallas import tpu as pltpu

F32 = jnp.float32
BF16 = jnp.bfloat16

D = 1024
SEQ = 4096
N_META = 16
PAD = 112
OFF = PAD + N_META
LP = OFF + SEQ
QB = 128
AQ = 384
KPQ = AQ // QB
TM = 384
MM_TM = 1408
HALO = 16
N_DEV = 8
D_FF = 2816
FB = 704
N_FB = D_FF // FB
RMS_EPS = 1e-6
SCALE = 0.125
HEAD_LANES = 64
VMEM_LIMIT = 56 * 1024 * 1024

ADAM_LR = 0.001
ADAM_B1 = 0.9
ADAM_B2 = 0.999
ADAM_EPS = 1e-08
ADAM_WD = 0.01
ADAM_STEP = 10

R_PROJ, R_DOWN = 128, 352
O_PC = 0
O_PA = O_PC + R_PROJ
O_OUT = O_PA + R_PROJ
O_DOWN = O_OUT + R_PROJ
R_PACK = O_DOWN + R_DOWN

NT = (((1,), (1,)), ((), ()))
NN = (((1,), (0,)), ((), ()))
TN = (((0,), (0,)), ((), ()))


def _params(sem):
    return pltpu.CompilerParams(dimension_semantics=sem, vmem_limit_bytes=VMEM_LIMIT)


def _dot(a, b, dn=NN):
    return lax.dot_general(a, b, dn, preferred_element_type=F32)


def _exchange_copies(ins, outs, gathers, send_sems, recv_sems, loc_sems):
    x, y, c = lax.axis_index("x"), lax.axis_index("y"), lax.axis_index("c")
    me = 4 * x + 2 * y + c
    copies = []
    for a, gather in enumerate(gathers):
        copies.append(pltpu.make_async_copy(ins[a] if gather else ins[a].at[me], outs[a].at[me], loc_sems.at[a]))
    for k in range(1, N_DEV):
        px = 1 - x if k & 4 else x
        py = 1 - y if k & 2 else y
        pc = 1 - c if k & 1 else c
        peer = 4 * px + 2 * py + pc
        for a, gather in enumerate(gathers):
            copies.append(pltpu.make_async_remote_copy(
                src_ref=ins[a] if gather else ins[a].at[peer],
                dst_ref=outs[a].at[me],
                send_sem=send_sems.at[a * (N_DEV - 1) + k - 1],
                recv_sem=recv_sems.at[a * (N_DEV - 1) + k - 1],
                device_id=(px, py, pc),
                device_id_type=pl.DeviceIdType.MESH,
            ))
    return copies


def _exchange_shapes(arrs, gathers):
    return [jax.ShapeDtypeStruct((N_DEV,) + (a.shape if g else a.shape[1:]), a.dtype) for a, g in zip(arrs, gathers)]


def _exchange_sems(n):
    return [pltpu.SemaphoreType.DMA((n * (N_DEV - 1),)), pltpu.SemaphoreType.DMA((n * (N_DEV - 1),)),
            pltpu.SemaphoreType.DMA((n,))]


ANY_SPEC = pl.BlockSpec(memory_space=pl.ANY)


def _gather_two_level(arrs, name):
    n = len(arrs)
    per = 7

    def body(*refs):
        ins, outs = refs[:n], refs[n:2 * n]
        send_sems, recv_sems, loc_sems = refs[2 * n:]
        x, y, c = lax.axis_index("x"), lax.axis_index("y"), lax.axis_index("c")
        me, sibling = (x, y, c), (x, y, 1 - c)
        chips = [(1 - x, y), (x, 1 - y), (1 - x, 1 - y)]

        def copy(a, k, block, to, src=None):
            place = outs[a].at[4 * block[0] + 2 * block[1] + block[2]]
            return pltpu.make_async_remote_copy(
                src_ref=place if src is None else src, dst_ref=place,
                send_sem=send_sems.at[a * per + k], recv_sem=recv_sems.at[a * per + k],
                device_id=to, device_id_type=pl.DeviceIdType.MESH)

        mine = [pltpu.make_async_copy(ins[a], outs[a].at[4 * x + 2 * y + c], loc_sems.at[a]) for a in range(n)]
        first = [copy(a, 0, me, sibling, src=ins[a]) for a in range(n)]
        first += [copy(a, 1 + j, me, (*chip, c), src=ins[a]) for j, chip in enumerate(chips) for a in range(n)]
        for cp in mine + first:
            cp.start()
        passed = []
        for j, chip in enumerate(chips):
            for a in range(n):
                copy(a, 1 + j, (*chip, c), me).wait_recv()
                passed.append(copy(a, 4 + j, (*chip, c), sibling))
                passed[-1].start()
        for a in range(n):
            copy(a, 0, sibling, me).wait_recv()
        for j, chip in enumerate(chips):
            for a in range(n):
                copy(a, 4 + j, (*chip, 1 - c), me).wait_recv()
        for cp in first + passed:
            cp.wait_send()
        for cp in mine:
            cp.wait()

    return pl.pallas_call(
        body,
        name=name,
        out_shape=_exchange_shapes(arrs, (True,) * n),
        in_specs=[ANY_SPEC] * n,
        out_specs=[ANY_SPEC] * n,
        scratch_shapes=_exchange_sems(n),
    )(*arrs)


HBM_SPEC = pl.BlockSpec(memory_space=pltpu.HBM)
SEM_SPEC = pl.BlockSpec(memory_space=pltpu.SEMAPHORE)
DATAFLOW = pltpu.SideEffectType.DATAFLOW_SIDE_EFFECTING


def _scatter_copies(g_ref, land_ref, send_sems, recv_sems, half):
    x, y, c = lax.axis_index("x"), lax.axis_index("y"), lax.axis_index("c")
    me = 4 * x + 2 * y + c
    nr = g_ref.shape[1] // 2
    rows = pl.ds(half * nr, nr)
    copies = []
    for k in range(1, N_DEV):
        px = 1 - x if k & 4 else x
        py = 1 - y if k & 2 else y
        pc = 1 - c if k & 1 else c
        copies.append(pltpu.make_async_remote_copy(
            src_ref=g_ref.at[4 * px + 2 * py + pc, rows], dst_ref=land_ref.at[me, rows],
            send_sem=send_sems.at[k - 1], recv_sem=recv_sems.at[k - 1],
            device_id=(px, py, pc), device_id_type=pl.DeviceIdType.MESH))
    return copies


def _scatter_start(g, land, half, name):
    def body(g_ref, land_ref, send_sems, recv_sems, g_thru, land_thru, token):
        for cp in _scatter_copies(g_ref, land_ref, send_sems, recv_sems, half):
            cp.start()
        token[...] = jnp.zeros_like(token)

    return pl.pallas_call(
        body,
        name=name,
        out_shape=(pltpu.SemaphoreType.DMA((N_DEV - 1,)), pltpu.SemaphoreType.DMA((N_DEV - 1,)),
                   pltpu.HBM(g.shape, g.dtype), pltpu.HBM(g.shape, g.dtype), jax.ShapeDtypeStruct((8, 128), F32)),
        in_specs=(HBM_SPEC, HBM_SPEC),
        out_specs=(SEM_SPEC, SEM_SPEC, HBM_SPEC, HBM_SPEC, pl.BlockSpec(memory_space=pltpu.VMEM)),
        input_output_aliases={0: 2, 1: 3},
        compiler_params=pltpu.CompilerParams(has_side_effects=DATAFLOW),
    )(pltpu.with_memory_space_constraint(g, pltpu.HBM), pltpu.with_memory_space_constraint(land, pltpu.HBM))


def _scatter_wait(send_sems, recv_sems, g_thru, land_thru, after, half, name):
    def body(g_ref, land_ref, send_sems, recv_sems, *_):
        for cp in _scatter_copies(g_ref, land_ref, send_sems, recv_sems, half):
            cp.wait_send()
            cp.wait_recv()

    return pl.pallas_call(
        body,
        name=name,
        out_shape=(pltpu.HBM(g_thru.shape, g_thru.dtype), pltpu.HBM(g_thru.shape, g_thru.dtype)),
        in_specs=(HBM_SPEC, HBM_SPEC, SEM_SPEC, SEM_SPEC) + (ANY_SPEC,) * len(after),
        out_specs=(HBM_SPEC, HBM_SPEC),
        input_output_aliases={0: 0, 1: 1},
        compiler_params=pltpu.CompilerParams(has_side_effects=DATAFLOW),
    )(g_thru, land_thru, send_sems, recv_sems, *after)


def _exchange(arrs, gathers, name, after=()):
    n, na = len(arrs), len(after)

    def body(*refs):
        copies = _exchange_copies(refs[:n], refs[n + na:2 * n + na], gathers, *refs[2 * n + na:])
        for cp in copies:
            cp.start()
        for cp in copies:
            cp.wait()

    return pl.pallas_call(
        body,
        name=name,
        out_shape=_exchange_shapes(arrs, gathers),
        in_specs=[ANY_SPEC] * (n + na),
        out_specs=[ANY_SPEC] * n,
        scratch_shapes=_exchange_sems(n),
    )(*arrs, *after)


def _mm(a, w, *, w_rows, trans_w, out_dtype, name):
    nb, _, wc = w.shape
    m, k = a.shape[-2:]
    n = w_rows if trans_w else wc
    dn = NT if trans_w else NN

    def body(a_ref, w_ref, o_ref):
        o_ref[...] = _dot(a_ref[...], w_ref[...], dn).astype(out_dtype)

    if a.ndim == 2:
        a_spec = pl.BlockSpec((MM_TM, k), lambda j, i: (i, 0))
    else:
        a_spec = pl.BlockSpec((None, MM_TM, k), lambda j, i: (j, i, 0))
    return pl.pallas_call(
        body,
        name=name,
        grid=(nb, m // MM_TM),
        in_specs=[a_spec, pl.BlockSpec((None, w_rows, wc), lambda j, i: (j, 0, 0))],
        out_specs=pl.BlockSpec((None, MM_TM, n), lambda j, i: (j, i, 0)),
        out_shape=jax.ShapeDtypeStruct((nb, m, n), out_dtype),
        compiler_params=_params(("parallel", "parallel")),
    )(a, w)


def _mm_sum(a, w, *, w_rows, trans_w, out_dtype, name, after=()):
    nb, m, k = a.shape
    wc = w.shape[2]
    n = w_rows if trans_w else wc
    dn = NT if trans_w else NN
    after = tuple(after) if isinstance(after, (tuple, list)) else (after,)
    na = len(after)

    def body(*refs):
        a_ref, w_ref, o_ref, acc_ref = refs[0], refs[1], refs[2 + na], refs[3 + na]
        j = pl.program_id(1)

        @pl.when(j == 0)
        def _():
            acc_ref[...] = jnp.zeros_like(acc_ref)

        acc_ref[...] += _dot(a_ref[...], w_ref[...], dn)

        @pl.when(j == nb - 1)
        def _():
            o_ref[...] = acc_ref[...].astype(out_dtype)

    return pl.pallas_call(
        body,
        name=name,
        grid=(m // MM_TM, nb),
        in_specs=[
            pl.BlockSpec((None, MM_TM, k), lambda i, j: (j, i, 0)),
            pl.BlockSpec((None, w_rows, wc), lambda i, j: (j, 0, 0)),
        ] + [ANY_SPEC] * na,
        out_specs=pl.BlockSpec((MM_TM, n), lambda i, j: (i, 0)),
        out_shape=jax.ShapeDtypeStruct((m, n), out_dtype),
        scratch_shapes=[pltpu.VMEM((MM_TM, n), F32)],
        compiler_params=_params(("parallel", "arbitrary")),
    )(a, w, *after)


def _mm_tn_half(a, b, *, nb, out_dtype, name, half, into=None):
    m, ka = a.shape
    n = b.shape[-1]
    kh = ka // 2
    steps = m // MM_TM

    def body(a_ref, b_ref, *rest):
        o_ref, acc_ref = rest[-2:]
        i = pl.program_id(1)

        @pl.when(i == 0)
        def _():
            acc_ref[...] = jnp.zeros_like(acc_ref)

        acc_ref[...] += _dot(a_ref[...], b_ref[...], TN)

        @pl.when(i == steps - 1)
        def _():
            o_ref[...] = acc_ref[...].astype(out_dtype)

    return pl.pallas_call(
        body,
        name=name,
        grid=(nb, steps),
        in_specs=[pl.BlockSpec((MM_TM, kh), lambda j, i: (i, half)),
                  pl.BlockSpec((None, MM_TM, n), lambda j, i: (j, i, 0))] + ([] if into is None else [ANY_SPEC]),
        out_specs=pl.BlockSpec((None, kh, n), lambda j, i: (j, half, 0)),
        out_shape=jax.ShapeDtypeStruct((nb, ka, n), out_dtype),
        input_output_aliases={} if into is None else {2: 0},
        scratch_shapes=[pltpu.VMEM((kh, n), F32)],
        compiler_params=_params(("parallel", "arbitrary")),
    )(a, b, *(() if into is None else (into,)))


def _mm_tn(a, b, *, nb, out_dtype, name):
    m, ka = a.shape[-2:]
    n = b.shape[-1]
    steps = m // MM_TM

    def body(a_ref, b_ref, o_ref, acc_ref):
        i = pl.program_id(1)

        @pl.when(i == 0)
        def _():
            acc_ref[...] = jnp.zeros_like(acc_ref)

        acc_ref[...] += _dot(a_ref[...], b_ref[...], TN)

        @pl.when(i == steps - 1)
        def _():
            o_ref[...] = acc_ref[...].astype(out_dtype)

    def spec(arr, cols):
        if arr.ndim == 2:
            return pl.BlockSpec((MM_TM, cols), lambda j, i: (i, 0))
        return pl.BlockSpec((None, MM_TM, cols), lambda j, i: (j, i, 0))

    return pl.pallas_call(
        body,
        name=name,
        grid=(nb, steps),
        in_specs=[spec(a, ka), spec(b, n)],
        out_specs=pl.BlockSpec((None, ka, n), lambda j, i: (j, 0, 0)),
        out_shape=jax.ShapeDtypeStruct((nb, ka, n), out_dtype),
        scratch_shapes=[pltpu.VMEM((ka, n), F32)],
        compiler_params=_params(("parallel", "arbitrary")),
    )(a, b)


def _rstd(x):
    return lax.rsqrt(jnp.mean(x * x, axis=-1, keepdims=True) + RMS_EPS)


def _rms_bwd(x, g, dy):
    r = _rstd(x)
    u = dy * g
    dx = r * u - x * (r * r * r) * jnp.mean(u * x, axis=-1, keepdims=True)
    return dx, dy * x * r


def _row_spec(cols=D, tm=TM):
    return pl.BlockSpec((tm, cols), lambda i: (i, 0))


def _vec_spec(rows=1, cols=D):
    return pl.BlockSpec((rows, cols), lambda i: (0, 0))


def _rms_fwd(x, g, name):
    def body(x_ref, g_ref, o_ref):
        x = x_ref[...]
        o_ref[...] = (x * _rstd(x) * g_ref[...]).astype(BF16)

    return pl.pallas_call(
        body,
        name=name,
        grid=(LP // TM,),
        in_specs=[_row_spec(), _vec_spec()],
        out_specs=_row_spec(),
        out_shape=jax.ShapeDtypeStruct((LP, D), BF16),
        compiler_params=_params(("parallel",)),
    )(x, g)


def _resid_rms(h0, mix, g_post, g_next):
    def body(h0_ref, mix_ref, gp_ref, gn_ref, h1_ref, xn_ref):
        mix = mix_ref[...]
        h1 = h0_ref[...] + mix * _rstd(mix) * gp_ref[...]
        h1_ref[...] = h1
        xn_ref[...] = (h1 * _rstd(h1) * gn_ref[...]).astype(BF16)

    return pl.pallas_call(
        body,
        name="resid_rms",
        grid=(LP // TM,),
        in_specs=[_row_spec(), _row_spec(), _vec_spec(), _vec_spec()],
        out_specs=[_row_spec(), _row_spec()],
        out_shape=[jax.ShapeDtypeStruct((LP, D), F32), jax.ShapeDtypeStruct((LP, D), BF16)],
        compiler_params=_params(("parallel",)),
    )(h0, mix, g_post, g_next)


def _loss_head(h1, ffn, g_post, target):
    nblk = LP // QB

    def body(h1_ref, ffn_ref, g_ref, t_ref, dout_ref, dffn_ref, loss_ref, dg_ref):
        i = pl.program_id(0)

        @pl.when(i == 0)
        def _():
            loss_ref[...] = jnp.zeros_like(loss_ref)
            dg_ref[...] = jnp.zeros_like(dg_ref)

        ffn = ffn_ref[...]
        g = g_ref[...]
        out = h1_ref[...] + ffn * _rstd(ffn) * g
        err = jnp.where(i > 0, out - t_ref[...], 0.0)
        loss_ref[...] += 0.5 * jnp.sum(err * err) / D
        dout = err / D
        dout_ref[...] = dout
        dffn, dg = _rms_bwd(ffn, g, dout)
        dffn_ref[...] = dffn.astype(BF16)
        dg_ref[0:1, :] += jnp.sum(dg, axis=0, keepdims=True)

    return pl.pallas_call(
        body,
        name="loss_head",
        grid=(nblk,),
        in_specs=[
            _row_spec(tm=QB),
            _row_spec(tm=QB),
            _vec_spec(),
            pl.BlockSpec((QB, D), lambda i: (jnp.maximum(i - 1, 0), 0)),
        ],
        out_specs=[_row_spec(tm=QB), _row_spec(tm=QB), _vec_spec(8, 128), _vec_spec(8, D)],
        out_shape=[
            jax.ShapeDtypeStruct((LP, D), F32),
            jax.ShapeDtypeStruct((LP, D), BF16),
            jax.ShapeDtypeStruct((8, 128), F32),
            jax.ShapeDtypeStruct((8, D), F32),
        ],
        compiler_params=_params(("arbitrary",)),
    )(h1, ffn, g_post, target)


def _mid_bwd(dout, h1, dxn2, mix, g_post_mix, g_pre_ffn):
    def body(dout_ref, h1_ref, dxn_ref, mix_ref, gpm_ref, gpf_ref, dh1_ref, dmix_ref, dg_ref):
        i = pl.program_id(0)

        @pl.when(i == 0)
        def _():
            dg_ref[...] = jnp.zeros_like(dg_ref)

        dx, dg_ffn = _rms_bwd(h1_ref[...], gpf_ref[...], dxn_ref[...])
        dh1 = dout_ref[...] + dx
        dh1_ref[...] = dh1
        dmix, dg_mix = _rms_bwd(mix_ref[...], gpm_ref[...], dh1)
        dmix_ref[...] = dmix.astype(BF16)
        dg_ref[0:1, :] += jnp.sum(dg_mix, axis=0, keepdims=True)
        dg_ref[1:2, :] += jnp.sum(dg_ffn, axis=0, keepdims=True)

    return pl.pallas_call(
        body,
        name="mid_bwd",
        grid=(LP // TM,),
        in_specs=[_row_spec(), _row_spec(), _row_spec(), _row_spec(), _vec_spec(), _vec_spec()],
        out_specs=[_row_spec(), _row_spec(), _vec_spec(8, D)],
        out_shape=[
            jax.ShapeDtypeStruct((LP, D), F32),
            jax.ShapeDtypeStruct((LP, D), BF16),
            jax.ShapeDtypeStruct((8, D), F32),
        ],
        compiler_params=_params(("arbitrary",)),
    )(dout, h1, dxn2, mix, g_post_mix, g_pre_ffn)


def _first_bwd(dh1, h0, dxn1, g_pre_mix):
    def body(dh1_ref, h0_ref, dxn_ref, g_ref, dh0_ref, dg_ref):
        i = pl.program_id(0)

        @pl.when(i == 0)
        def _():
            dg_ref[...] = jnp.zeros_like(dg_ref)

        dx, dg = _rms_bwd(h0_ref[...], g_ref[...], dxn_ref[...])
        dh0_ref[...] = dh1_ref[...] + dx
        dg_ref[0:1, :] += jnp.sum(dg, axis=0, keepdims=True)

    return pl.pallas_call(
        body,
        name="first_bwd",
        grid=(LP // TM,),
        in_specs=[_row_spec(), _row_spec(), _row_spec(), _vec_spec()],
        out_specs=[_row_spec(), _vec_spec(8, D)],
        out_shape=[jax.ShapeDtypeStruct((LP, D), F32), jax.ShapeDtypeStruct((8, D), F32)],
        compiler_params=_params(("arbitrary",)),
    )(dh1, h0, dxn1, g_pre_mix)


def _prev_halo(i):
    return jnp.maximum(i * (TM // HALO) - 1, 0)


def _next_halo(i):
    return jnp.minimum((i + 1) * (TM // HALO), LP // HALO - 1)


def _down(x, s):
    return pltpu.roll(x, s, 0)


def _up(x, s):
    return pltpu.roll(x, x.shape[0] - s, 0)


def _conv_mix_fwd(hin, cw):
    def body(b_ref, c_ref, h_ref, cp_ref, hp_ref, w_ref, y_ref):
        i = pl.program_id(0)
        p = c_ref[...].astype(F32) * h_ref[...].astype(F32)
        pp = jnp.where(i > 0, cp_ref[...].astype(F32) * hp_ref[...].astype(F32), 0.0)
        ext = jnp.concatenate([pp, p], axis=0)
        w = [w_ref[t:t + 1, :] for t in range(3)]
        cv = w[2] * ext + w[1] * _down(ext, 1) + w[0] * _down(ext, 2)
        y_ref[...] = (b_ref[...].astype(F32) * cv[HALO:]).astype(BF16)

    def tile(s):
        return pl.BlockSpec((None, TM, D), lambda i: (s, i, 0))

    def prev(s):
        return pl.BlockSpec((None, HALO, D), lambda i: (s, _prev_halo(i), 0))

    return pl.pallas_call(
        body,
        name="conv_mix_fwd",
        grid=(LP // TM,),
        in_specs=[tile(0), tile(1), tile(2), prev(1), prev(2), _vec_spec(3, D)],
        out_specs=_row_spec(),
        out_shape=jax.ShapeDtypeStruct((LP, D), BF16),
        compiler_params=_params(("parallel",)),
    )(hin, hin, hin, hin, hin, cw)


def _conv_mix_bwd(hin, dy, cw, dhin):
    last = LP // TM - 1

    def body(b_ref, c_ref, h_ref, dy_ref, cp_ref, hp_ref, bn_ref, dyn_ref, w_ref, _, out_ref, dw_ref):
        i = pl.program_id(0)

        @pl.when(i == 0)
        def _():
            dw_ref[...] = jnp.zeros_like(dw_ref)

        b = b_ref[...].astype(F32)
        c = c_ref[...].astype(F32)
        h = h_ref[...].astype(F32)
        dy = dy_ref[...].astype(F32)
        w = [w_ref[t:t + 1, :] for t in range(3)]
        p = c * h
        pp = jnp.where(i > 0, cp_ref[...].astype(F32) * hp_ref[...].astype(F32), 0.0)
        ext = jnp.concatenate([pp, p], axis=0)
        p1 = _down(ext, 1)[HALO:]
        p2 = _down(ext, 2)[HALO:]
        cv = w[2] * p + w[1] * p1 + w[0] * p2
        out_ref[0] = (dy * cv).astype(BF16)
        dcv = dy * b
        dcvn = jnp.where(i < last, dyn_ref[...].astype(F32) * bn_ref[...].astype(F32), 0.0)
        dext = jnp.concatenate([dcv, dcvn], axis=0)
        dp = (w[2] * dext + w[1] * _up(dext, 1) + w[0] * _up(dext, 2))[:TM]
        out_ref[1] = (dp * h).astype(BF16)
        out_ref[2] = (dp * c).astype(BF16)
        dw_ref[0:1, :] += jnp.sum(dcv * p2, axis=0, keepdims=True)
        dw_ref[1:2, :] += jnp.sum(dcv * p1, axis=0, keepdims=True)
        dw_ref[2:3, :] += jnp.sum(dcv * p, axis=0, keepdims=True)

    def tile(s):
        return pl.BlockSpec((None, TM, D), lambda i: (s, i, 0))

    def prev(s):
        return pl.BlockSpec((None, HALO, D), lambda i: (s, _prev_halo(i), 0))

    return pl.pallas_call(
        body,
        name="conv_mix_bwd",
        grid=(LP // TM,),
        in_specs=[
            tile(0), tile(1), tile(2), _row_spec(),
            prev(1), prev(2),
            pl.BlockSpec((None, HALO, D), lambda i: (0, _next_halo(i), 0)),
            pl.BlockSpec((HALO, D), lambda i: (_next_halo(i), 0)),
            _vec_spec(3, D),
            pl.BlockSpec(memory_space=pl.ANY),
        ],
        out_specs=[pl.BlockSpec((3, TM, D), lambda i: (0, i, 0)), _vec_spec(8, D)],
        out_shape=[jax.ShapeDtypeStruct((N_DEV, LP, D), BF16), jax.ShapeDtypeStruct((8, D), F32)],
        input_output_aliases={9: 0},
        compiler_params=_params(("arbitrary",)),
    )(hin, hin, hin, dy, hin, hin, hin, dy, cw, dhin)


GELU_K = math.sqrt(2.0 / math.pi)
GELU_A = 0.044715


def _gelu_and_grad(x):
    x2 = x * x
    t = jnp.tanh(x * (GELU_K + (GELU_K * GELU_A) * x2))
    s = 0.5 + 0.5 * t
    grad = s * (1.0 + x * (1.0 - t) * (GELU_K + (3.0 * GELU_K * GELU_A) * x2))
    return x * s, grad


def _ffn_act_fwd(ug, cw4):
    def body(u_ref, g_ref, up_ref, w_ref, o_ref):
        i = pl.program_id(1)
        u = u_ref[...].astype(F32)
        up = jnp.where(i > 0, up_ref[...].astype(F32), 0.0)
        ext = jnp.concatenate([up, u], axis=0)
        w = [w_ref[t:t + 1, :] for t in range(3)]
        uc = (w[2] * ext + w[1] * _down(ext, 1) + w[0] * _down(ext, 2))[HALO:]
        gelu, _ = _gelu_and_grad(uc)
        o_ref[...] = (gelu * g_ref[...].astype(F32)).astype(BF16)

    return pl.pallas_call(
        body,
        name="ffn_act_fwd",
        grid=(N_FB, LP // TM),
        in_specs=[
            pl.BlockSpec((None, TM, FB), lambda j, i: (j, i, 0)),
            pl.BlockSpec((None, TM, FB), lambda j, i: (j + N_FB, i, 0)),
            pl.BlockSpec((None, HALO, FB), lambda j, i: (j, _prev_halo(i), 0)),
            pl.BlockSpec((None, 3, FB), lambda j, i: (j, 0, 0)),
        ],
        out_specs=pl.BlockSpec((None, TM, FB), lambda j, i: (j, i, 0)),
        out_shape=jax.ShapeDtypeStruct((N_FB, LP, FB), BF16),
        compiler_params=_params(("parallel", "parallel")),
    )(ug, ug, ug, cw4)


def _ffn_act_bwd(ug, dhid, cw4):
    last = LP // TM - 1
    n = TM + 2 * HALO

    def body(u_ref, g_ref, dh_ref, up_ref, un_ref, gn_ref, dhn_ref, w_ref, dug_ref, dw_ref):
        i = pl.program_id(1)

        @pl.when(i == 0)
        def _():
            dw_ref[...] = jnp.zeros_like(dw_ref)

        w = [w_ref[t:t + 1, :] for t in range(3)]
        u = u_ref[...].astype(F32)
        up = jnp.where(i > 0, up_ref[...].astype(F32), 0.0)
        ext = jnp.concatenate([up, u, un_ref[...].astype(F32)], axis=0)
        u1 = _down(ext, 1)
        u2 = _down(ext, 2)
        uc = w[2] * ext + w[1] * u1 + w[0] * u2
        gelu, ggrad = _gelu_and_grad(uc)
        zeros = jnp.zeros((HALO, FB), F32)
        gext = jnp.concatenate([zeros, g_ref[...].astype(F32), gn_ref[...].astype(F32)], axis=0)
        dhn = jnp.where(i < last, dhn_ref[...].astype(F32), 0.0)
        dhext = jnp.concatenate([zeros, dh_ref[...].astype(F32), dhn], axis=0)
        dug_ref[1] = (dhext * gelu)[HALO:HALO + TM].astype(BF16)
        duc = dhext * gext * ggrad
        du = w[2] * duc + w[1] * _up(duc, 1) + w[0] * _up(duc, 2)
        dug_ref[0] = du[HALO:HALO + TM].astype(BF16)
        row = lax.broadcasted_iota(jnp.int32, (n, 1), 0)
        own = jnp.where((row >= HALO) & (row < HALO + TM), duc, 0.0)
        dw_ref[0:1, :] += jnp.sum(own * u2, axis=0, keepdims=True)
        dw_ref[1:2, :] += jnp.sum(own * u1, axis=0, keepdims=True)
        dw_ref[2:3, :] += jnp.sum(own * ext, axis=0, keepdims=True)

    def tile(off):
        return pl.BlockSpec((None, TM, FB), lambda j, i: (j + off, i, 0))

    def nxt(off):
        return pl.BlockSpec((None, HALO, FB), lambda j, i: (j + off, _next_halo(i), 0))

    return pl.pallas_call(
        body,
        name="ffn_act_bwd",
        grid=(N_FB, LP // TM),
        in_specs=[
            tile(0), tile(N_FB), tile(0),
            pl.BlockSpec((None, HALO, FB), lambda j, i: (j, _prev_halo(i), 0)),
            nxt(0), nxt(N_FB), nxt(0),
            pl.BlockSpec((None, 3, FB), lambda j, i: (j, 0, 0)),
        ],
        out_specs=[
            pl.BlockSpec((2, None, TM, FB), lambda j, i: (0, j, i, 0)),
            pl.BlockSpec((None, 8, FB), lambda j, i: (j, 0, 0)),
        ],
        out_shape=[jax.ShapeDtypeStruct((2, N_FB, LP, FB), BF16), jax.ShapeDtypeStruct((N_FB, 8, FB), F32)],
        compiler_params=_params(("parallel", "arbitrary")),
    )(ug, ug, dhid, ug, ug, ug, dhid, cw4)


def _gate_fwd(bc, ba, hin, bgate):
    def body(bc_ref, ba_ref, gc_ref, ga_ref, b_ref, o_ref):
        b = b_ref[...]
        sc = jax.nn.sigmoid(gc_ref[...].astype(F32) + b[0:1])
        sa = jax.nn.sigmoid(ga_ref[...].astype(F32) + b[1:2])
        o_ref[...] = (sc * bc_ref[...].astype(F32) + sa * ba_ref[...].astype(F32)).astype(BF16)

    def tile(s):
        return pl.BlockSpec((None, TM, D), lambda i: (s, i, 0))

    return pl.pallas_call(
        body,
        name="gate_fwd",
        grid=(LP // TM,),
        in_specs=[_row_spec(), _row_spec(), tile(6), tile(7), _vec_spec(2, D)],
        out_specs=_row_spec(),
        out_shape=jax.ShapeDtypeStruct((LP, D), BF16),
        compiler_params=_params(("parallel",)),
    )(bc, ba, hin, hin, bgate)


def _gate_bwd(dm, bc, ba, hin, bgate):
    def body(dm_ref, bc_ref, ba_ref, gc_ref, ga_ref, b_ref, dbc_ref, dba_ref, dg_ref, db_ref):
        i = pl.program_id(0)

        @pl.when(i == 0)
        def _():
            db_ref[...] = jnp.zeros_like(db_ref)

        b = b_ref[...]
        dm = dm_ref[...].astype(F32)
        sc = jax.nn.sigmoid(gc_ref[...].astype(F32) + b[0:1])
        sa = jax.nn.sigmoid(ga_ref[...].astype(F32) + b[1:2])
        dbc_ref[...] = (dm * sc).astype(BF16)
        dba_ref[...] = (dm * sa).astype(BF16)
        dgc = dm * bc_ref[...].astype(F32) * sc * (1.0 - sc)
        dga = dm * ba_ref[...].astype(F32) * sa * (1.0 - sa)
        dg_ref[0] = dgc.astype(BF16)
        dg_ref[1] = dga.astype(BF16)
        db_ref[0:1, :] += jnp.sum(dgc, axis=0, keepdims=True)
        db_ref[1:2, :] += jnp.sum(dga, axis=0, keepdims=True)

    def tile(s):
        return pl.BlockSpec((None, TM, D), lambda i: (s, i, 0))

    return pl.pallas_call(
        body,
        name="gate_bwd",
        grid=(LP // TM,),
        in_specs=[_row_spec(), _row_spec(), _row_spec(), tile(6), tile(7), _vec_spec(2, D)],
        out_specs=[_row_spec(), _row_spec(), pl.BlockSpec((2, TM, D), lambda i: (3, i, 0)), _vec_spec(8, D)],
        out_shape=[jax.ShapeDtypeStruct((LP, D), BF16)] * 2
        + [jax.ShapeDtypeStruct((N_DEV, LP, D), BF16), jax.ShapeDtypeStruct((8, D), F32)],
        compiler_params=_params(("arbitrary",)),
    )(dm, bc, ba, hin, hin, bgate)


Z_LINEAR = 30.0


def _softplus(z):
    return jnp.maximum(z, jnp.log(1.0 + jnp.exp(jnp.minimum(z, Z_LINEAR))))


def _cumsum_matrix(inclusive, reverse):
    r = lax.broadcasted_iota(jnp.int32, (QB, 2 * QB), 0)
    c = lax.broadcasted_iota(jnp.int32, (QB, 2 * QB), 1)
    if reverse:
        tri = r > c
    elif inclusive:
        tri = r <= c
    else:
        tri = r < c
    return jnp.where((c >= QB) | tri, 1.0, 0.0).astype(BF16)


def _split_dot(x, m2):
    bits = lax.bitcast_convert_type(x, jnp.uint32) & jnp.uint32(0xFFFF0000)
    hi = lax.bitcast_convert_type(bits, F32)
    return _dot(jnp.concatenate([hi.astype(BF16), (x - hi).astype(BF16)], axis=1), m2)


def _stack_heads(x):
    return jnp.concatenate(_split_heads(x), axis=0)


def _block_mask(i, j, row0):
    row = lax.broadcasted_iota(jnp.int32, (AQ - row0, QB), 0) + (i * AQ + row0)
    col = lax.broadcasted_iota(jnp.int32, (AQ - row0, QB), 1) + j * QB
    return (col < row) & (col >= PAD)


def _key_block(ref, j):
    return ref[pl.ds(pl.multiple_of(j * QB, QB), QB), :]


def _split_heads(x):
    head_a = lax.broadcasted_iota(jnp.int32, x.shape, 1) < HEAD_LANES
    zero = jnp.zeros_like(x)
    return jnp.where(head_a, x, zero), jnp.where(head_a, zero, x)


def _attn_fwd(hin, ex_arrs, gathers):
    ne = len(ex_arrs)
    npair, nq = D // QB, LP // AQ

    def body(*refs):
        q_ref, k_ref, v_ref = refs[:3]
        o_ref, lt_ref = refs[3 + ne:5 + ne]
        c_sc, acc_sc = refs[5 + 2 * ne:7 + 2 * ne]
        p, i = pl.program_id(0), pl.program_id(1)

        def copies():
            return _exchange_copies(refs[3:3 + ne], refs[5 + ne:5 + 2 * ne], gathers, *refs[7 + 2 * ne:])

        @pl.when((p == 0) & (i == 0))
        def _():
            for cp in copies():
                cp.start()

        um = _cumsum_matrix(False, True)
        um2 = jnp.concatenate([um, um], axis=0)
        q = (q_ref[...].astype(F32) * SCALE).astype(BF16)
        c_sc[...] = jnp.zeros_like(c_sc)
        acc_sc[...] = jnp.zeros_like(acc_sc)

        def step(j, masked, row0=0):
            rows = slice(row0, AQ)
            z2 = _dot(q[rows], _stack_heads(_key_block(k_ref, j)), NT)
            mask = _block_mask(i, j, row0) if masked else None
            a2 = []
            for hd in range(2):
                z = z2[:, hd * QB:(hd + 1) * QB]
                sp = _softplus(z)
                r = _split_dot(jnp.where(mask, sp, 0.0) if masked else sp, um2)
                a = jnp.exp(z - sp - c_sc[hd, rows] - r[:, :QB])
                if masked:
                    a = jnp.where(mask, a, 0.0)
                a2.append(a.astype(BF16))
                c_sc[hd, rows] += r[:, QB:]
            acc_sc[rows] += _dot(jnp.concatenate(a2, axis=1), _stack_heads(_key_block(v_ref, j)))

        for t in reversed(range(KPQ)):
            step(KPQ * i + t, True, t * QB)

        groups = jnp.maximum(i - 1, 0)

        @pl.loop(0, groups // 2)
        def _(t):
            for u in range(2 * KPQ):
                step(KPQ * (i - 2 * t) - 1 - u, False)

        @pl.when(groups % 2 == 1)
        def _():
            for u in range(KPQ):
                step(2 * KPQ - 1 - u, False)

        @pl.when(i > 0)
        def _():
            for u in reversed(range(1, KPQ)):
                step(u, False)
            step(0, True)

        head_a = lax.broadcasted_iota(jnp.int32, (AQ, QB), 1) < HEAD_LANES
        o_ref[...] = acc_sc[...].astype(BF16)
        lt_ref[...] = jnp.where(head_a, c_sc[0], c_sc[1])

        @pl.when((p == npair - 1) & (i == nq - 1))
        def _():
            for cp in copies():
                cp.wait()

    def seq(s):
        return pl.BlockSpec((None, LP, QB), lambda p, i: (s, 0, p))

    return pl.pallas_call(
        body,
        name="attn_fwd",
        grid=(npair, nq),
        in_specs=[pl.BlockSpec((None, AQ, QB), lambda p, i: (3, i, p)), seq(4), seq(5)] + [ANY_SPEC] * ne,
        out_specs=[pl.BlockSpec((AQ, QB), lambda p, i: (i, p))] * 2 + [ANY_SPEC] * ne,
        out_shape=[jax.ShapeDtypeStruct((LP, D), BF16), jax.ShapeDtypeStruct((LP, D), F32)]
        + _exchange_shapes(ex_arrs, gathers),
        scratch_shapes=[pltpu.VMEM((2, AQ, QB), F32), pltpu.VMEM((AQ, QB), F32)] + _exchange_sems(ne),
        compiler_params=_params(("arbitrary", "arbitrary")),
    )(hin, hin, hin, *ex_arrs)


def _attn_bwd(hin, do, lt, dhin, ex_arrs, gathers):
    ne = len(ex_arrs)
    npair, nq = D // QB, LP // AQ

    def body(*refs):
        q_ref, k_ref, v_ref, do_ref, lt_ref = refs[:5]
        out_ref = refs[6 + ne]
        psp_sc, pg_sc, dq_sc, dk_acc, dv_acc = refs[7 + 2 * ne:12 + 2 * ne]
        p, i = pl.program_id(0), pl.program_id(1)

        def copies():
            return _exchange_copies(refs[6:6 + ne], refs[7 + ne:7 + 2 * ne], gathers, *refs[12 + 2 * ne:])

        @pl.when((p == 0) & (i == 0))
        def _():
            for cp in copies():
                cp.start()

        @pl.when(i == 0)
        def _():
            dk_acc[...] = jnp.zeros_like(dk_acc)
            dv_acc[...] = jnp.zeros_like(dv_acc)

        um_sp = _cumsum_matrix(True, False)
        um_sp2 = jnp.concatenate([um_sp, um_sp], axis=0)
        um_g = _cumsum_matrix(False, False)
        q = (q_ref[...].astype(F32) * SCALE).astype(BF16)
        do = do_ref[...]
        q_t, do_t = q.T, do.T
        head_a = lax.broadcasted_iota(jnp.int32, (AQ, QB), 1) < HEAD_LANES
        dim_head_a = lax.broadcasted_iota(jnp.int32, (QB, QB), 0) < HEAD_LANES
        lt = lt_ref[...]
        lt_sw = pltpu.roll(lt, HEAD_LANES, 1)
        totals = (jnp.where(head_a, lt, lt_sw), jnp.where(head_a, lt_sw, lt))
        psp_sc[...] = jnp.zeros_like(psp_sc)
        pg_sc[...] = jnp.zeros_like(pg_sc)
        dq_sc[...] = jnp.zeros_like(dq_sc)

        def step(j, masked, row0=0):
            rows = slice(row0, AQ)
            k2 = _stack_heads(_key_block(k_ref, j))
            z2 = _dot(q[rows], k2, NT)
            da2 = _dot(do[rows], _stack_heads(_key_block(v_ref, j)), NT)
            mask = _block_mask(i, j, row0) if masked else None
            a2, dz2 = [], []
            for hd in range(2):
                z = z2[:, hd * QB:(hd + 1) * QB]
                sp = _softplus(z)
                r = _split_dot(jnp.where(mask, sp, 0.0) if masked else sp, um_sp2)
                a = jnp.exp(z - sp - (totals[hd][rows] - psp_sc[hd, rows] - r[:, :QB]))
                if masked:
                    a = jnp.where(mask, a, 0.0)
                g = a * da2[:, hd * QB:(hd + 1) * QB]
                rg = _dot(g.astype(BF16), um_g)
                dz = g - jnp.exp(z - sp) * (g + pg_sc[hd, rows] + rg[:, :QB])
                if masked:
                    dz = jnp.where(mask, dz, 0.0)
                a2.append(a.astype(BF16))
                dz2.append(dz.astype(BF16))
                psp_sc[hd, rows] += r[:, QB:]
                pg_sc[hd, rows] += rg[:, QB:]
            dz2 = jnp.concatenate(dz2, axis=1)
            dq_sc[rows] += _dot(dz2, k2)
            dk2 = _dot(q_t[:, rows], dz2)
            dv2 = _dot(do_t[:, rows], jnp.concatenate(a2, axis=1))
            dk_acc[j] += jnp.where(dim_head_a, dk2[:, :QB], dk2[:, QB:])
            dv_acc[j] += jnp.where(dim_head_a, dv2[:, :QB], dv2[:, QB:])

        @pl.when(i > 0)
        def _():
            step(0, True)
            for u in range(1, KPQ):
                step(u, False)

        groups = jnp.maximum(i - 1, 0)

        @pl.loop(0, groups // 2)
        def _(t):
            for u in range(2 * KPQ):
                step(KPQ * (2 * t + 1) + u, False)

        @pl.when(groups % 2 == 1)
        def _():
            for u in range(KPQ):
                step(KPQ * (i - 1) + u, False)

        for t in range(KPQ):
            step(KPQ * i + t, True, t * QB)

        out_ref[0, pl.ds(pl.multiple_of(i * AQ, AQ), AQ), :] = (dq_sc[...] * SCALE).astype(BF16)

        @pl.when(i == nq - 1)
        def _():
            @pl.loop(0, LP // QB)
            def _(b):
                keys = pl.ds(pl.multiple_of(b * QB, QB), QB)
                out_ref[1, keys, :] = dk_acc[b].T.astype(BF16)
                out_ref[2, keys, :] = dv_acc[b].T.astype(BF16)

        @pl.when((p == npair - 1) & (i == nq - 1))
        def _():
            for cp in copies():
                cp.wait()

    def seq(s):
        return pl.BlockSpec((None, LP, QB), lambda p, i: (s, 0, p))

    blk = pl.BlockSpec((AQ, QB), lambda p, i: (i, p))
    return pl.pallas_call(
        body,
        name="attn_bwd",
        grid=(npair, nq),
        in_specs=[pl.BlockSpec((None, AQ, QB), lambda p, i: (3, i, p)), seq(4), seq(5), blk, blk]
        + [ANY_SPEC] * (1 + ne),
        out_specs=[pl.BlockSpec((3, LP, QB), lambda p, i: (1, 0, p))] + [ANY_SPEC] * ne,
        out_shape=[jax.ShapeDtypeStruct((N_DEV, LP, D), BF16)] + _exchange_shapes(ex_arrs, gathers),
        input_output_aliases={5: 0},
        scratch_shapes=[pltpu.VMEM((2, AQ, QB), F32)] * 2 + [pltpu.VMEM((AQ, QB), F32)]
        + [pltpu.VMEM((LP // QB, QB, QB), F32)] * 2 + _exchange_sems(ne),
        compiler_params=_params(("arbitrary", "arbitrary")),
    )(hin, hin, hin, do, lt, dhin, *ex_arrs)


def _adamw_math(w, g, m, v):
    m_new = ADAM_B1 * m + (1.0 - ADAM_B1) * g
    v_new = ADAM_B2 * v + (1.0 - ADAM_B2) * jnp.square(g)
    m_hat = m_new / (1.0 - ADAM_B1 ** ADAM_STEP)
    v_hat = v_new / (1.0 - ADAM_B2 ** ADAM_STEP)
    return -ADAM_LR * (m_hat / (jnp.sqrt(v_hat) + ADAM_EPS) + ADAM_WD * w), m_new, v_new


def _adamw_small(ws, gs, ms, vs):
    n = len(ws)

    def body(*refs):
        for t in range(n):
            w_ref, g_ref, m_ref, v_ref = (refs[s * n + t] for s in range(4))
            d_ref, nm_ref, nv_ref = (refs[(4 + s) * n + t] for s in range(3))
            d_ref[...], nm_ref[...], nv_ref[...] = _adamw_math(w_ref[...], g_ref[...], m_ref[...], v_ref[...])

    vmem = pl.BlockSpec(memory_space=pltpu.VMEM)
    res = pl.pallas_call(
        body,
        name="adamw_small",
        in_specs=[vmem] * (4 * n),
        out_specs=[vmem] * (3 * n),
        out_shape=[jax.ShapeDtypeStruct(w.shape, F32) for w in ws] * 3,
    )(*ws, *gs, *ms, *vs)
    return res[:n], res[n:2 * n], res[2 * n:]


def _adamw(pieces, w, m, v, *, rows, row_off, tr, name):
    npieces, _, cols = pieces.shape
    ob = row_off // tr

    def body(p_ref, w_ref, m_ref, v_ref, g_ref, d_ref, nm_ref, nv_ref):
        g = p_ref[0].astype(F32)
        for s in range(1, npieces):
            g = g + p_ref[s].astype(F32)
        g_ref[...] = g
        d_ref[...], nm_ref[...], nv_ref[...] = _adamw_math(w_ref[...], g, m_ref[...], v_ref[...])

    spec = pl.BlockSpec((tr, cols), lambda i: (i, 0))
    return pl.pallas_call(
        body,
        name=name,
        grid=(rows // tr,),
        in_specs=[pl.BlockSpec((npieces, tr, cols), lambda i: (0, ob + i, 0)), spec, spec, spec],
        out_specs=[spec] * 4,
        out_shape=[jax.ShapeDtypeStruct((rows, cols), F32)] * 4,
        compiler_params=_params(("parallel",)),
    )(pieces, w, m, v)


def _sum_pieces(pieces, name):
    npieces, rows, cols = pieces.shape

    def body(p_ref, o_ref):
        acc = p_ref[0]
        for s in range(1, npieces):
            acc = acc + p_ref[s]
        o_ref[...] = acc

    return pl.pallas_call(
        body,
        name=name,
        in_specs=[pl.BlockSpec(memory_space=pltpu.VMEM)],
        out_specs=pl.BlockSpec(memory_space=pltpu.VMEM),
        out_shape=jax.ShapeDtypeStruct((rows, cols), pieces.dtype),
    )(pieces)


SMALL_ROWS = 48
CWF_PAD = 384
GRAD_ROWS = 72
G_META, G_PRE_MIX, G_MID, G_POST_FFN, G_CW_MIX, G_B_GATE, G_CW_FFN, G_LOSS = 0, 16, 24, 32, 40, 48, 56, 17


def kernel(x, meta_tokens, g_pre_mix, w_in, conv_w_mix, w_proj_conv, w_proj_attn, b_gate, w_out, g_post_mix, g_pre_ffn, w_up_gate, conv_w_ffn, w_down, g_post_ffn, loss_target, m_meta_tokens, m_g_pre_mix, m_w_in, m_conv_w_mix, m_w_proj_conv, m_w_proj_attn, m_b_gate, m_w_out, m_g_post_mix, m_g_pre_ffn, m_w_up_gate, m_conv_w_ffn, m_w_down, m_g_post_ffn, v_meta_tokens, v_g_pre_mix, v_w_in, v_conv_w_mix, v_w_proj_conv, v_w_proj_attn, v_b_gate, v_w_out, v_g_post_mix, v_g_pre_ffn, v_w_up_gate, v_conv_w_ffn, v_w_down, v_g_post_ffn):
    me = 4 * lax.axis_index("x") + 2 * lax.axis_index("y") + lax.axis_index("c")

    def rows_to(a, n):
        return jnp.pad(a, ((0, n - a.shape[0]), (0, 0)))

    small_shard = jnp.concatenate(
        [meta_tokens, rows_to(conv_w_mix[0], 8), rows_to(b_gate[0], 8),
         rows_to(jnp.pad(conv_w_ffn[0], ((0, 0), (0, CWF_PAD - R_DOWN))).reshape(9, 128), 16)], axis=0)
    wshard = jnp.concatenate([w_proj_conv[0], w_proj_attn[0], w_out[0], w_down[0]], axis=0).astype(BF16)
    w_in_all, small_all = _gather_two_level([w_in[0].astype(BF16), small_shard], "gather_in")

    def unshard(rows):
        return rows.transpose(1, 0, 2).reshape(rows.shape[1], N_DEV * rows.shape[2])

    meta = unshard(small_all[:, 0:16])
    cw_mix = unshard(small_all[:, 16:19])
    bgate = unshard(small_all[:, 24:26])
    cw_ffn = unshard(small_all[:, 32:41].reshape(N_DEV, 3, CWF_PAD)[:, :, :R_DOWN])
    cw4 = cw_ffn.reshape(3, N_FB, FB).transpose(1, 0, 2)
    h0 = jnp.concatenate([jnp.zeros((PAD, D), F32), meta, x[0]], axis=0)

    xn1 = _rms_fwd(h0, g_pre_mix, "rms_pre_mix")
    hin = _mm(xn1, w_in_all, w_rows=D, trans_w=False, out_dtype=BF16, name="mm_in")
    y_conv = _conv_mix_fwd(hin, cw_mix)
    o, lt, wpack, w_ug = _attn_fwd(hin, [wshard, w_up_gate[0].astype(BF16)], (True, True))
    w_pc = wpack[:, O_PC:O_PA].reshape(1, D, D)
    w_pa = wpack[:, O_PA:O_OUT].reshape(1, D, D)
    w_o = wpack[:, O_OUT:O_DOWN].reshape(1, D, D)
    w_dn = wpack[:, O_DOWN:].reshape(N_FB, FB, D)
    bc = _mm(y_conv, w_pc, w_rows=D, trans_w=False, out_dtype=BF16, name="mm_proj_conv")[0]
    ba = _mm(o, w_pa, w_rows=D, trans_w=False, out_dtype=BF16, name="mm_proj_attn")[0]
    merged = _gate_fwd(bc, ba, hin, bgate)
    mix = _mm(merged, w_o, w_rows=D, trans_w=False, out_dtype=F32, name="mm_out")[0]
    h1, xn2 = _resid_rms(h0, mix, g_post_mix, g_pre_ffn)
    ug = _mm(xn2, w_ug, w_rows=D, trans_w=False, out_dtype=BF16, name="mm_up_gate")
    hid = _ffn_act_fwd(ug, cw4)
    ffn = _mm_sum(hid, w_dn, w_rows=FB, trans_w=False, out_dtype=F32, name="mm_down")
    dout, dffn, loss8, dg_post_ffn = _loss_head(h1, ffn, g_post_ffn, loss_target[0])

    dhid = _mm(dffn, w_dn, w_rows=FB, trans_w=True, out_dtype=BF16, name="mm_down_dx")
    gw_dn = _mm_tn(hid, dffn, nb=N_FB, out_dtype=BF16, name="mm_down_dw")
    dug, dcw4 = _ffn_act_bwd(ug, dhid, cw4)
    dug = dug.reshape(2 * N_FB, LP, FB)
    dxn2 = _mm_sum(dug, w_ug, w_rows=D, trans_w=True, out_dtype=F32, name="mm_up_gate_dx")
    gw_ug = _mm_tn(xn2, dug, nb=N_DEV, out_dtype=BF16, name="mm_up_gate_dw")
    dh1, dmix, dg_mid = _mid_bwd(dout, h1, dxn2, mix, g_post_mix, g_pre_ffn)
    dmerged = _mm(dmix, w_o, w_rows=D, trans_w=True, out_dtype=BF16, name="mm_out_dx")[0]
    gw_out = _mm_tn(merged, dmix, nb=1, out_dtype=BF16, name="mm_out_dw")
    dbc, dba, dhin, db_gate = _gate_bwd(dmerged, bc, ba, hin, bgate)
    dy_conv = _mm(dbc, w_pc, w_rows=D, trans_w=True, out_dtype=BF16, name="mm_proj_conv_dx")[0]
    gw_pc = _mm_tn(y_conv, dbc, nb=1, out_dtype=BF16, name="mm_proj_conv_dw")
    do = _mm(dba, w_pa, w_rows=D, trans_w=True, out_dtype=BF16, name="mm_proj_attn_dx")[0]
    gw_pa = _mm_tn(o, dba, nb=1, out_dtype=BF16, name="mm_proj_attn_dw")
    dhin, dcw_mix = _conv_mix_bwd(hin, dy_conv, cw_mix, dhin)
    gpack = jnp.concatenate(
        [gw_pc.reshape(N_DEV, R_PROJ, D), gw_pa.reshape(N_DEV, R_PROJ, D), gw_out.reshape(N_DEV, R_PROJ, D),
         gw_dn.reshape(N_DEV, R_DOWN, D)], axis=1)
    dcw_ffn = dcw4[:, :3].transpose(1, 0, 2).reshape(3, D_FF)
    small_a = jnp.concatenate(
        [dg_mid, dg_post_ffn, dcw_mix, db_gate,
         jnp.pad(dcw_ffn.reshape(-1), (0, 16 * D - 3 * D_FF)).reshape(16, D)], axis=0)
    dhin, rpack, rug, rsmall_a = _attn_bwd(hin, do, lt, dhin, [gpack, gw_ug, small_a], (False, False, True))
    gw_half = _mm_tn_half(xn1, dhin, nb=N_DEV, out_dtype=BF16, name="mm_in_dw_a", half=0)
    sems_a = _scatter_start(gw_half, lax.empty(gw_half.shape, BF16), 0, "scatter_in_start_a")
    gw_in = _mm_tn_half(xn1, dhin, nb=N_DEV, out_dtype=BF16, name="mm_in_dw_b", half=1, into=sems_a[2])
    sems_b = _scatter_start(gw_in, sems_a[3], 1, "scatter_in_start_b")
    dxn1 = _mm_sum(dhin, w_in_all, w_rows=D, trans_w=True, out_dtype=F32, name="mm_in_dx", after=sems_b[4])
    dh0, dg_pre_mix = _first_bwd(dh1, h0, dxn1, g_pre_mix)
    small_b = jnp.concatenate(
        [dh0[PAD:OFF], lax.dynamic_update_slice(dg_pre_mix, loss8[0:1], (G_LOSS - G_PRE_MIX, 0))], axis=0)

    def big(pieces, w, m, v, rows, row_off, tr, name):
        g, d, nm, nv = _adamw(pieces, w[0], m[0], v[0], rows=rows, row_off=row_off, tr=tr, name=name)
        return g[None], d[None], nm[None], nv[None]

    r_pc = big(rpack, w_proj_conv, m_w_proj_conv, v_w_proj_conv, R_PROJ, O_PC, R_PROJ, "adamw_proj_conv")
    r_pa = big(rpack, w_proj_attn, m_w_proj_attn, v_w_proj_attn, R_PROJ, O_PA, R_PROJ, "adamw_proj_attn")
    r_out = big(rpack, w_out, m_w_out, v_w_out, R_PROJ, O_OUT, R_PROJ, "adamw_out")
    r_dn = big(rpack, w_down, m_w_down, v_w_down, R_DOWN, O_DOWN, 32, "adamw_down")
    r_ug = big(rug, w_up_gate, m_w_up_gate, v_w_up_gate, D, 0, 256, "adamw_up_gate")
    (rsmall_b,) = _exchange([small_b], (True,), "gather_small_grads",
                            after=(r_pc[1], r_pa[1], r_out[1], r_dn[1], r_ug[1]))
    gs = _sum_pieces(jnp.concatenate([rsmall_b, rsmall_a], axis=1), "sum_small_grads")
    grad_x = dh0[OFF:]
    loss = gs[G_LOSS, 0]

    def cols(a, width):
        return lax.dynamic_slice_in_dim(a, me * width, width, axis=1)

    g_meta = cols(gs[G_META:G_META + N_META], 128)
    g_gpm, g_gff = gs[G_PRE_MIX:G_PRE_MIX + 1], gs[G_POST_FFN:G_POST_FFN + 1]
    g_gpo, g_gpf = gs[G_MID:G_MID + 1], gs[G_MID + 1:G_MID + 2]
    g_cwm = cols(gs[G_CW_MIX:G_CW_MIX + 3], 128)[None]
    g_bg = cols(gs[G_B_GATE:G_B_GATE + 2], 128)[None]
    g_cwf = cols(gs[G_CW_FFN:G_CW_FFN + 9].reshape(-1)[:3 * D_FF].reshape(3, D_FF), R_DOWN)[None]

    small_w = [meta_tokens, g_pre_mix, conv_w_mix, b_gate, g_post_mix, g_pre_ffn, conv_w_ffn, g_post_ffn]
    small_g = [g_meta, g_gpm, g_cwm, g_bg, g_gpo, g_gpf, g_cwf, g_gff]
    small_m = [m_meta_tokens, m_g_pre_mix, m_conv_w_mix, m_b_gate, m_g_post_mix, m_g_pre_ffn, m_conv_w_ffn, m_g_post_ffn]
    small_v = [v_meta_tokens, v_g_pre_mix, v_conv_w_mix, v_b_gate, v_g_post_mix, v_g_pre_ffn, v_conv_w_ffn, v_g_post_ffn]

    s_g = small_g
    s_d, s_m, s_v = _adamw_small(small_w, small_g, small_m, small_v)

    gw_done, landed = _scatter_wait(sems_a[0], sems_a[1], sems_b[2], sems_b[3], (s_d[0],), 0, "scatter_in_wait_a")
    gw_done, landed = _scatter_wait(sems_b[0], sems_b[1], gw_done, landed, (), 1, "scatter_in_wait_b")
    rin = lax.dynamic_update_index_in_dim(landed, lax.dynamic_index_in_dim(gw_done, me, 0, keepdims=False), me, 0)
    r_in = big(rin, w_in, m_w_in, v_w_in, D, 0, 256, "adamw_in")

    def ordered(k, smalls):
        meta, gpm, cwm, bg, gpo, gpf, cwf, gff = smalls
        return [meta, gpm, r_in[k], cwm, r_pc[k], r_pa[k], bg, r_out[k], gpo, gpf, r_ug[k], cwf, r_dn[k], gff]

    return (loss, grad_x[None], *ordered(0, s_g), *ordered(1, s_d), *ordered(2, s_m), *ordered(3, s_v))
```

```python
import functools
import math

import jax
import jax.numpy as jnp
from jax import lax
from jax.experimental import pallas as pl
from jax.experimental.pallas import tpu as pltpu

F32 = jnp.float32
BF16 = jnp.bfloat16

D = 1024
SEQ = 4096
N_META = 16
PAD = 112
OFF = PAD + N_META
LP = OFF + SEQ
QB = 128
AQ = 384
KPQ = AQ // QB
TM = 384
MM_TM = 1408
HALO = 16
N_DEV = 8
D_FF = 2816
FB = 704
N_FB = D_FF // FB
RMS_EPS = 1e-6
SCALE = 0.125
HEAD_LANES = 64
VMEM_LIMIT = 56 * 1024 * 1024

ADAM_LR = 0.001
ADAM_B1 = 0.9
ADAM_B2 = 0.999
ADAM_EPS = 1e-08
ADAM_WD = 0.01
ADAM_STEP = 10

R_PROJ, R_DOWN = 128, 352
O_PC = 0
O_PA = O_PC + R_PROJ
O_OUT = O_PA + R_PROJ
O_DOWN = O_OUT + R_PROJ
R_PACK = O_DOWN + R_DOWN

NT = (((1,), (1,)), ((), ()))
NN = (((1,), (0,)), ((), ()))
TN = (((0,), (0,)), ((), ()))


def _params(sem):
    return pltpu.CompilerParams(dimension_semantics=sem, vmem_limit_bytes=VMEM_LIMIT)


def _dot(a, b, dn=NN):
    return lax.dot_general(a, b, dn, preferred_element_type=F32)


def _exchange_copies(ins, outs, gathers, send_sems, recv_sems, loc_sems):
    x, y, c = lax.axis_index("x"), lax.axis_index("y"), lax.axis_index("c")
    me = 4 * x + 2 * y + c
    copies = []
    for a, gather in enumerate(gathers):
        copies.append(pltpu.make_async_copy(ins[a] if gather else ins[a].at[me], outs[a].at[me], loc_sems.at[a]))
    for k in range(1, N_DEV):
        px = 1 - x if k & 4 else x
        py = 1 - y if k & 2 else y
        pc = 1 - c if k & 1 else c
        peer = 4 * px + 2 * py + pc
        for a, gather in enumerate(gathers):
            copies.append(pltpu.make_async_remote_copy(
                src_ref=ins[a] if gather else ins[a].at[peer],
                dst_ref=outs[a].at[me],
                send_sem=send_sems.at[a * (N_DEV - 1) + k - 1],
                recv_sem=recv_sems.at[a * (N_DEV - 1) + k - 1],
                device_id=(px, py, pc),
                device_id_type=pl.DeviceIdType.MESH,
            ))
    return copies


def _exchange_shapes(arrs, gathers):
    return [jax.ShapeDtypeStruct((N_DEV,) + (a.shape if g else a.shape[1:]), a.dtype) for a, g in zip(arrs, gathers)]


def _exchange_sems(n):
    return [pltpu.SemaphoreType.DMA((n * (N_DEV - 1),)), pltpu.SemaphoreType.DMA((n * (N_DEV - 1),)),
            pltpu.SemaphoreType.DMA((n,))]


ANY_SPEC = pl.BlockSpec(memory_space=pl.ANY)


def _gather_two_level(arrs, name):
    n = len(arrs)
    per = 7

    def body(*refs):
        ins, outs = refs[:n], refs[n:2 * n]
        send_sems, recv_sems, loc_sems = refs[2 * n:]
        x, y, c = lax.axis_index("x"), lax.axis_index("y"), lax.axis_index("c")
        me, sibling = (x, y, c), (x, y, 1 - c)
        chips = [(1 - x, y), (x, 1 - y), (1 - x, 1 - y)]

        def copy(a, k, block, to, src=None):
            place = outs[a].at[4 * block[0] + 2 * block[1] + block[2]]
            return pltpu.make_async_remote_copy(
                src_ref=place if src is None else src, dst_ref=place,
                send_sem=send_sems.at[a * per + k], recv_sem=recv_sems.at[a * per + k],
                device_id=to, device_id_type=pl.DeviceIdType.MESH)

        mine = [pltpu.make_async_copy(ins[a], outs[a].at[4 * x + 2 * y + c], loc_sems.at[a]) for a in range(n)]
        first = [copy(a, 0, me, sibling, src=ins[a]) for a in range(n)]
        first += [copy(a, 1 + j, me, (*chip, c), src=ins[a]) for j, chip in enumerate(chips) for a in range(n)]
        for cp in mine + first:
            cp.start()
        passed = []
        for j, chip in enumerate(chips):
            for a in range(n):
                copy(a, 1 + j, (*chip, c), me).wait_recv()
                passed.append(copy(a, 4 + j, (*chip, c), sibling))
                passed[-1].start()
        for a in range(n):
            copy(a, 0, sibling, me).wait_recv()
        for j, chip in enumerate(chips):
            for a in range(n):
                copy(a, 4 + j, (*chip, 1 - c), me).wait_recv()
        for cp in first + passed:
            cp.wait_send()
        for cp in mine:
            cp.wait()

    return pl.pallas_call(
        body,
        name=name,
        out_shape=_exchange_shapes(arrs, (True,) * n),
        in_specs=[ANY_SPEC] * n,
        out_specs=[ANY_SPEC] * n,
        scratch_shapes=_exchange_sems(n),
    )(*arrs)


HBM_SPEC = pl.BlockSpec(memory_space=pltpu.HBM)
SEM_SPEC = pl.BlockSpec(memory_space=pltpu.SEMAPHORE)
DATAFLOW = pltpu.SideEffectType.DATAFLOW_SIDE_EFFECTING


def _scatter_copies(g_ref, land_ref, send_sems, recv_sems, half):
    x, y, c = lax.axis_index("x"), lax.axis_index("y"), lax.axis_index("c")
    me = 4 * x + 2 * y + c
    nr = g_ref.shape[1] // 2
    rows = pl.ds(half * nr, nr)
    copies = []
    for k in range(1, N_DEV):
        px = 1 - x if k & 4 else x
        py = 1 - y if k & 2 else y
        pc = 1 - c if k & 1 else c
        copies.append(pltpu.make_async_remote_copy(
            src_ref=g_ref.at[4 * px + 2 * py + pc, rows], dst_ref=land_ref.at[me, rows],
            send_sem=send_sems.at[k - 1], recv_sem=recv_sems.at[k - 1],
            device_id=(px, py, pc), device_id_type=pl.DeviceIdType.MESH))
    return copies


def _scatter_start(g, land, half, name):
    def body(g_ref, land_ref, send_sems, recv_sems, g_thru, land_thru, token):
        for cp in _scatter_copies(g_ref, land_ref, send_sems, recv_sems, half):
            cp.start()
        token[...] = jnp.zeros_like(token)

    return pl.pallas_call(
        body,
        name=name,
        out_shape=(pltpu.SemaphoreType.DMA((N_DEV - 1,)), pltpu.SemaphoreType.DMA((N_DEV - 1,)),
                   pltpu.HBM(g.shape, g.dtype), pltpu.HBM(g.shape, g.dtype), jax.ShapeDtypeStruct((8, 128), F32)),
        in_specs=(HBM_SPEC, HBM_SPEC),
        out_specs=(SEM_SPEC, SEM_SPEC, HBM_SPEC, HBM_SPEC, pl.BlockSpec(memory_space=pltpu.VMEM)),
        input_output_aliases={0: 2, 1: 3},
        compiler_params=pltpu.CompilerParams(has_side_effects=DATAFLOW),
    )(pltpu.with_memory_space_constraint(g, pltpu.HBM), pltpu.with_memory_space_constraint(land, pltpu.HBM))


def _scatter_wait(send_sems, recv_sems, g_thru, land_thru, after, half, name):
    def body(g_ref, land_ref, send_sems, recv_sems, *_):
        for cp in _scatter_copies(g_ref, land_ref, send_sems, recv_sems, half):
            cp.wait_send()
            cp.wait_recv()

    return pl.pallas_call(
        body,
        name=name,
        out_shape=(pltpu.HBM(g_thru.shape, g_thru.dtype), pltpu.HBM(g_thru.shape, g_thru.dtype)),
        in_specs=(HBM_SPEC, HBM_SPEC, SEM_SPEC, SEM_SPEC) + (ANY_SPEC,) * len(after),
        out_specs=(HBM_SPEC, HBM_SPEC),
        input_output_aliases={0: 0, 1: 1},
        compiler_params=pltpu.CompilerParams(has_side_effects=DATAFLOW),
    )(g_thru, land_thru, send_sems, recv_sems, *after)


def _exchange(arrs, gathers, name, after=()):
    n, na = len(arrs), len(after)

    def body(*refs):
        copies = _exchange_copies(refs[:n], refs[n + na:2 * n + na], gathers, *refs[2 * n + na:])
        for cp in copies:
            cp.start()
        for cp in copies:
            cp.wait()

    return pl.pallas_call(
        body,
        name=name,
        out_shape=_exchange_shapes(arrs, gathers),
        in_specs=[ANY_SPEC] * (n + na),
        out_specs=[ANY_SPEC] * n,
        scratch_shapes=_exchange_sems(n),
    )(*arrs, *after)


def _mm(a, w, *, w_rows, trans_w, out_dtype, name):
    nb, _, wc = w.shape
    m, k = a.shape[-2:]
    n = w_rows if trans_w else wc
    dn = NT if trans_w else NN

    def body(a_ref, w_ref, o_ref):
        o_ref[...] = _dot(a_ref[...], w_ref[...], dn).astype(out_dtype)

    if a.ndim == 2:
        a_spec = pl.BlockSpec((MM_TM, k), lambda j, i: (i, 0))
    else:
        a_spec = pl.BlockSpec((None, MM_TM, k), lambda j, i: (j, i, 0))
    return pl.pallas_call(
        body,
        name=name,
        grid=(nb, m // MM_TM),
        in_specs=[a_spec, pl.BlockSpec((None, w_rows, wc), lambda j, i: (j, 0, 0))],
        out_specs=pl.BlockSpec((None, MM_TM, n), lambda j, i: (j, i, 0)),
        out_shape=jax.ShapeDtypeStruct((nb, m, n), out_dtype),
        compiler_params=_params(("parallel", "parallel")),
    )(a, w)


def _mm_sum(a, w, *, w_rows, trans_w, out_dtype, name, after=()):
    nb, m, k = a.shape
    wc = w.shape[2]
    n = w_rows if trans_w else wc
    dn = NT if trans_w else NN
    after = tuple(after) if isinstance(after, (tuple, list)) else (after,)
    na = len(after)

    def body(*refs):
        a_ref, w_ref, o_ref, acc_ref = refs[0], refs[1], refs[2 + na], refs[3 + na]
        j = pl.program_id(1)

        @pl.when(j == 0)
        def _():
            acc_ref[...] = jnp.zeros_like(acc_ref)

        acc_ref[...] += _dot(a_ref[...], w_ref[...], dn)

        @pl.when(j == nb - 1)
        def _():
            o_ref[...] = acc_ref[...].astype(out_dtype)

    return pl.pallas_call(
        body,
        name=name,
        grid=(m // MM_TM, nb),
        in_specs=[
            pl.BlockSpec((None, MM_TM, k), lambda i, j: (j, i, 0)),
            pl.BlockSpec((None, w_rows, wc), lambda i, j: (j, 0, 0)),
        ] + [ANY_SPEC] * na,
        out_specs=pl.BlockSpec((MM_TM, n), lambda i, j: (i, 0)),
        out_shape=jax.ShapeDtypeStruct((m, n), out_dtype),
        scratch_shapes=[pltpu.VMEM((MM_TM, n), F32)],
        compiler_params=_params(("parallel", "arbitrary")),
    )(a, w, *after)


def _mm_tn_half(a, b, *, nb, out_dtype, name, half, into=None):
    m, ka = a.shape
    n = b.shape[-1]
    kh = ka // 2
    steps = m // MM_TM

    def body(a_ref, b_ref, *rest):
        o_ref, acc_ref = rest[-2:]
        i, j = pl.program_id(0), pl.program_id(1)
        prod = _dot(a_ref[...], b_ref[...], TN)

        @pl.when(i == 0)
        def _():
            acc_ref[j] = prod

        @pl.when(i > 0)
        def _():
            acc_ref[j] += prod

        @pl.when(i == steps - 1)
        def _():
            o_ref[j] = acc_ref[j].astype(out_dtype)

    return pl.pallas_call(
        body,
        name=name,
        grid=(steps, nb),
        in_specs=[pl.BlockSpec((MM_TM, kh), lambda i, j: (i, half)),
                  pl.BlockSpec((None, MM_TM, n), lambda i, j: (j, i, 0))] + ([] if into is None else [ANY_SPEC]),
        out_specs=pl.BlockSpec((nb, kh, n), lambda i, j: (0, half, 0)),
        out_shape=jax.ShapeDtypeStruct((nb, ka, n), out_dtype),
        input_output_aliases={} if into is None else {2: 0},
        scratch_shapes=[pltpu.VMEM((nb, kh, n), F32)],
        compiler_params=_params(("arbitrary", "arbitrary")),
    )(a, b, *(() if into is None else (into,)))


def _mm_tn(a, b, *, nb, out_dtype, name):
    m, ka = a.shape[-2:]
    n = b.shape[-1]
    steps = m // MM_TM

    def body(a_ref, b_ref, o_ref, acc_ref):
        i = pl.program_id(1)

        @pl.when(i == 0)
        def _():
            acc_ref[...] = jnp.zeros_like(acc_ref)

        acc_ref[...] += _dot(a_ref[...], b_ref[...], TN)

        @pl.when(i == steps - 1)
        def _():
            o_ref[...] = acc_ref[...].astype(out_dtype)

    def spec(arr, cols):
        if arr.ndim == 2:
            return pl.BlockSpec((MM_TM, cols), lambda j, i: (i, 0))
        return pl.BlockSpec((None, MM_TM, cols), lambda j, i: (j, i, 0))

    return pl.pallas_call(
        body,
        name=name,
        grid=(nb, steps),
        in_specs=[spec(a, ka), spec(b, n)],
        out_specs=pl.BlockSpec((None, ka, n), lambda j, i: (j, 0, 0)),
        out_shape=jax.ShapeDtypeStruct((nb, ka, n), out_dtype),
        scratch_shapes=[pltpu.VMEM((ka, n), F32)],
        compiler_params=_params(("parallel", "arbitrary")),
    )(a, b)


def _rstd(x):
    return lax.rsqrt(jnp.mean(x * x, axis=-1, keepdims=True) + RMS_EPS)


def _rms_bwd(x, g, dy):
    r = _rstd(x)
    u = dy * g
    dx = r * u - x * (r * r * r) * jnp.mean(u * x, axis=-1, keepdims=True)
    return dx, dy * x * r


def _row_spec(cols=D, tm=TM):
    return pl.BlockSpec((tm, cols), lambda i: (i, 0))


def _vec_spec(rows=1, cols=D):
    return pl.BlockSpec((rows, cols), lambda i: (0, 0))


def _rms_fwd(x, g, name):
    def body(x_ref, g_ref, o_ref):
        x = x_ref[...]
        o_ref[...] = (x * _rstd(x) * g_ref[...]).astype(BF16)

    return pl.pallas_call(
        body,
        name=name,
        grid=(LP // TM,),
        in_specs=[_row_spec(), _vec_spec()],
        out_specs=_row_spec(),
        out_shape=jax.ShapeDtypeStruct((LP, D), BF16),
        compiler_params=_params(("parallel",)),
    )(x, g)


def _resid_rms(h0, mix, g_post, g_next):
    def body(h0_ref, mix_ref, gp_ref, gn_ref, h1_ref, xn_ref):
        mix = mix_ref[...]
        h1 = h0_ref[...] + mix * _rstd(mix) * gp_ref[...]
        h1_ref[...] = h1
        xn_ref[...] = (h1 * _rstd(h1) * gn_ref[...]).astype(BF16)

    return pl.pallas_call(
        body,
        name="resid_rms",
        grid=(LP // TM,),
        in_specs=[_row_spec(), _row_spec(), _vec_spec(), _vec_spec()],
        out_specs=[_row_spec(), _row_spec()],
        out_shape=[jax.ShapeDtypeStruct((LP, D), F32), jax.ShapeDtypeStruct((LP, D), BF16)],
        compiler_params=_params(("parallel",)),
    )(h0, mix, g_post, g_next)


def _loss_head(h1, ffn, g_post, target):
    nblk = LP // QB

    def body(h1_ref, ffn_ref, g_ref, t_ref, dout_ref, dffn_ref, loss_ref, dg_ref):
        i = pl.program_id(0)

        @pl.when(i == 0)
        def _():
            loss_ref[...] = jnp.zeros_like(loss_ref)
            dg_ref[...] = jnp.zeros_like(dg_ref)

        ffn = ffn_ref[...]
        g = g_ref[...]
        out = h1_ref[...] + ffn * _rstd(ffn) * g
        err = jnp.where(i > 0, out - t_ref[...], 0.0)
        loss_ref[...] += 0.5 * jnp.sum(err * err) / D
        dout = err / D
        dout_ref[...] = dout
        dffn, dg = _rms_bwd(ffn, g, dout)
        dffn_ref[...] = dffn.astype(BF16)
        dg_ref[0:1, :] += jnp.sum(dg, axis=0, keepdims=True)

    return pl.pallas_call(
        body,
        name="loss_head",
        grid=(nblk,),
        in_specs=[
            _row_spec(tm=QB),
            _row_spec(tm=QB),
            _vec_spec(),
            pl.BlockSpec((QB, D), lambda i: (jnp.maximum(i - 1, 0), 0)),
        ],
        out_specs=[_row_spec(tm=QB), _row_spec(tm=QB), _vec_spec(8, 128), _vec_spec(8, D)],
        out_shape=[
            jax.ShapeDtypeStruct((LP, D), F32),
            jax.ShapeDtypeStruct((LP, D), BF16),
            jax.ShapeDtypeStruct((8, 128), F32),
            jax.ShapeDtypeStruct((8, D), F32),
        ],
        compiler_params=_params(("arbitrary",)),
    )(h1, ffn, g_post, target)


def _mid_bwd(dout, h1, dxn2, mix, g_post_mix, g_pre_ffn):
    def body(dout_ref, h1_ref, dxn_ref, mix_ref, gpm_ref, gpf_ref, dh1_ref, dmix_ref, dg_ref):
        i = pl.program_id(0)

        @pl.when(i == 0)
        def _():
            dg_ref[...] = jnp.zeros_like(dg_ref)

        dx, dg_ffn = _rms_bwd(h1_ref[...], gpf_ref[...], dxn_ref[...])
        dh1 = dout_ref[...] + dx
        dh1_ref[...] = dh1
        dmix, dg_mix = _rms_bwd(mix_ref[...], gpm_ref[...], dh1)
        dmix_ref[...] = dmix.astype(BF16)
        dg_ref[0:1, :] += jnp.sum(dg_mix, axis=0, keepdims=True)
        dg_ref[1:2, :] += jnp.sum(dg_ffn, axis=0, keepdims=True)

    return pl.pallas_call(
        body,
        name="mid_bwd",
        grid=(LP // TM,),
        in_specs=[_row_spec(), _row_spec(), _row_spec(), _row_spec(), _vec_spec(), _vec_spec()],
        out_specs=[_row_spec(), _row_spec(), _vec_spec(8, D)],
        out_shape=[
            jax.ShapeDtypeStruct((LP, D), F32),
            jax.ShapeDtypeStruct((LP, D), BF16),
            jax.ShapeDtypeStruct((8, D), F32),
        ],
        compiler_params=_params(("arbitrary",)),
    )(dout, h1, dxn2, mix, g_post_mix, g_pre_ffn)


def _first_bwd(dh1, h0, dxn1, g_pre_mix):
    def body(dh1_ref, h0_ref, dxn_ref, g_ref, dh0_ref, dg_ref):
        i = pl.program_id(0)

        @pl.when(i == 0)
        def _():
            dg_ref[...] = jnp.zeros_like(dg_ref)

        dx, dg = _rms_bwd(h0_ref[...], g_ref[...], dxn_ref[...])
        dh0_ref[...] = dh1_ref[...] + dx
        dg_ref[0:1, :] += jnp.sum(dg, axis=0, keepdims=True)

    return pl.pallas_call(
        body,
        name="first_bwd",
        grid=(LP // TM,),
        in_specs=[_row_spec(), _row_spec(), _row_spec(), _vec_spec()],
        out_specs=[_row_spec(), _vec_spec(8, D)],
        out_shape=[jax.ShapeDtypeStruct((LP, D), F32), jax.ShapeDtypeStruct((8, D), F32)],
        compiler_params=_params(("arbitrary",)),
    )(dh1, h0, dxn1, g_pre_mix)


def _prev_halo(i):
    return jnp.maximum(i * (TM // HALO) - 1, 0)


def _next_halo(i):
    return jnp.minimum((i + 1) * (TM // HALO), LP // HALO - 1)


def _down(x, s):
    return pltpu.roll(x, s, 0)


def _up(x, s):
    return pltpu.roll(x, x.shape[0] - s, 0)


def _conv_mix_fwd(hin, cw):
    def body(b_ref, c_ref, h_ref, cp_ref, hp_ref, w_ref, y_ref):
        i = pl.program_id(0)
        p = c_ref[...].astype(F32) * h_ref[...].astype(F32)
        pp = jnp.where(i > 0, cp_ref[...].astype(F32) * hp_ref[...].astype(F32), 0.0)
        ext = jnp.concatenate([pp, p], axis=0)
        w = [w_ref[t:t + 1, :] for t in range(3)]
        cv = w[2] * ext + w[1] * _down(ext, 1) + w[0] * _down(ext, 2)
        y_ref[...] = (b_ref[...].astype(F32) * cv[HALO:]).astype(BF16)

    def tile(s):
        return pl.BlockSpec((None, TM, D), lambda i: (s, i, 0))

    def prev(s):
        return pl.BlockSpec((None, HALO, D), lambda i: (s, _prev_halo(i), 0))

    return pl.pallas_call(
        body,
        name="conv_mix_fwd",
        grid=(LP // TM,),
        in_specs=[tile(0), tile(1), tile(2), prev(1), prev(2), _vec_spec(3, D)],
        out_specs=_row_spec(),
        out_shape=jax.ShapeDtypeStruct((LP, D), BF16),
        compiler_params=_params(("parallel",)),
    )(hin, hin, hin, hin, hin, cw)


def _conv_mix_bwd(hin, dy, cw, dhin):
    last = LP // TM - 1

    def body(b_ref, c_ref, h_ref, dy_ref, cp_ref, hp_ref, bn_ref, dyn_ref, w_ref, _, out_ref, dw_ref):
        i = pl.program_id(0)

        @pl.when(i == 0)
        def _():
            dw_ref[...] = jnp.zeros_like(dw_ref)

        b = b_ref[...].astype(F32)
        c = c_ref[...].astype(F32)
        h = h_ref[...].astype(F32)
        dy = dy_ref[...].astype(F32)
        w = [w_ref[t:t + 1, :] for t in range(3)]
        p = c * h
        pp = jnp.where(i > 0, cp_ref[...].astype(F32) * hp_ref[...].astype(F32), 0.0)
        ext = jnp.concatenate([pp, p], axis=0)
        p1 = _down(ext, 1)[HALO:]
        p2 = _down(ext, 2)[HALO:]
        cv = w[2] * p + w[1] * p1 + w[0] * p2
        out_ref[0] = (dy * cv).astype(BF16)
        dcv = dy * b
        dcvn = jnp.where(i < last, dyn_ref[...].astype(F32) * bn_ref[...].astype(F32), 0.0)
        dext = jnp.concatenate([dcv, dcvn], axis=0)
        dp = (w[2] * dext + w[1] * _up(dext, 1) + w[0] * _up(dext, 2))[:TM]
        out_ref[1] = (dp * h).astype(BF16)
        out_ref[2] = (dp * c).astype(BF16)
        dw_ref[0:1, :] += jnp.sum(dcv * p2, axis=0, keepdims=True)
        dw_ref[1:2, :] += jnp.sum(dcv * p1, axis=0, keepdims=True)
        dw_ref[2:3, :] += jnp.sum(dcv * p, axis=0, keepdims=True)

    def tile(s):
        return pl.BlockSpec((None, TM, D), lambda i: (s, i, 0))

    def prev(s):
        return pl.BlockSpec((None, HALO, D), lambda i: (s, _prev_halo(i), 0))

    return pl.pallas_call(
        body,
        name="conv_mix_bwd",
        grid=(LP // TM,),
        in_specs=[
            tile(0), tile(1), tile(2), _row_spec(),
            prev(1), prev(2),
            pl.BlockSpec((None, HALO, D), lambda i: (0, _next_halo(i), 0)),
            pl.BlockSpec((HALO, D), lambda i: (_next_halo(i), 0)),
            _vec_spec(3, D),
            pl.BlockSpec(memory_space=pl.ANY),
        ],
        out_specs=[pl.BlockSpec((3, TM, D), lambda i: (0, i, 0)), _vec_spec(8, D)],
        out_shape=[jax.ShapeDtypeStruct((N_DEV, LP, D), BF16), jax.ShapeDtypeStruct((8, D), F32)],
        input_output_aliases={9: 0},
        compiler_params=_params(("arbitrary",)),
    )(hin, hin, hin, dy, hin, hin, hin, dy, cw, dhin)


GELU_K = math.sqrt(2.0 / math.pi)
GELU_A = 0.044715


def _gelu_and_grad(x):
    x2 = x * x
    t = jnp.tanh(x * (GELU_K + (GELU_K * GELU_A) * x2))
    s = 0.5 + 0.5 * t
    grad = s * (1.0 + x * (1.0 - t) * (GELU_K + (3.0 * GELU_K * GELU_A) * x2))
    return x * s, grad


def _ffn_act_fwd(ug, cw4):
    def body(u_ref, g_ref, up_ref, w_ref, o_ref):
        i = pl.program_id(1)
        u = u_ref[...].astype(F32)
        up = jnp.where(i > 0, up_ref[...].astype(F32), 0.0)
        ext = jnp.concatenate([up, u], axis=0)
        w = [w_ref[t:t + 1, :] for t in range(3)]
        uc = (w[2] * ext + w[1] * _down(ext, 1) + w[0] * _down(ext, 2))[HALO:]
        gelu, _ = _gelu_and_grad(uc)
        o_ref[...] = (gelu * g_ref[...].astype(F32)).astype(BF16)

    return pl.pallas_call(
        body,
        name="ffn_act_fwd",
        grid=(N_FB, LP // TM),
        in_specs=[
            pl.BlockSpec((None, TM, FB), lambda j, i: (j, i, 0)),
            pl.BlockSpec((None, TM, FB), lambda j, i: (j + N_FB, i, 0)),
            pl.BlockSpec((None, HALO, FB), lambda j, i: (j, _prev_halo(i), 0)),
            pl.BlockSpec((None, 3, FB), lambda j, i: (j, 0, 0)),
        ],
        out_specs=pl.BlockSpec((None, TM, FB), lambda j, i: (j, i, 0)),
        out_shape=jax.ShapeDtypeStruct((N_FB, LP, FB), BF16),
        compiler_params=_params(("parallel", "parallel")),
    )(ug, ug, ug, cw4)


def _ffn_act_bwd(ug, dhid, cw4):
    last = LP // TM - 1
    n = TM + 2 * HALO

    def body(u_ref, g_ref, dh_ref, up_ref, un_ref, gn_ref, dhn_ref, w_ref, dug_ref, dw_ref):
        i = pl.program_id(1)

        @pl.when(i == 0)
        def _():
            dw_ref[...] = jnp.zeros_like(dw_ref)

        w = [w_ref[t:t + 1, :] for t in range(3)]
        u = u_ref[...].astype(F32)
        up = jnp.where(i > 0, up_ref[...].astype(F32), 0.0)
        ext = jnp.concatenate([up, u, un_ref[...].astype(F32)], axis=0)
        u1 = _down(ext, 1)
        u2 = _down(ext, 2)
        uc = w[2] * ext + w[1] * u1 + w[0] * u2
        gelu, ggrad = _gelu_and_grad(uc)
        zeros = jnp.zeros((HALO, FB), F32)
        gext = jnp.concatenate([zeros, g_ref[...].astype(F32), gn_ref[...].astype(F32)], axis=0)
        dhn = jnp.where(i < last, dhn_ref[...].astype(F32), 0.0)
        dhext = jnp.concatenate([zeros, dh_ref[...].astype(F32), dhn], axis=0)
        dug_ref[1] = (dhext * gelu)[HALO:HALO + TM].astype(BF16)
        duc = dhext * gext * ggrad
        du = w[2] * duc + w[1] * _up(duc, 1) + w[0] * _up(duc, 2)
        dug_ref[0] = du[HALO:HALO + TM].astype(BF16)
        row = lax.broadcasted_iota(jnp.int32, (n, 1), 0)
        own = jnp.where((row >= HALO) & (row < HALO + TM), duc, 0.0)
        dw_ref[0:1, :] += jnp.sum(own * u2, axis=0, keepdims=True)
        dw_ref[1:2, :] += jnp.sum(own * u1, axis=0, keepdims=True)
        dw_ref[2:3, :] += jnp.sum(own * ext, axis=0, keepdims=True)

    def tile(off):
        return pl.BlockSpec((None, TM, FB), lambda j, i: (j + off, i, 0))

    def nxt(off):
        return pl.BlockSpec((None, HALO, FB), lambda j, i: (j + off, _next_halo(i), 0))

    return pl.pallas_call(
        body,
        name="ffn_act_bwd",
        grid=(N_FB, LP // TM),
        in_specs=[
            tile(0), tile(N_FB), tile(0),
            pl.BlockSpec((None, HALO, FB), lambda j, i: (j, _prev_halo(i), 0)),
            nxt(0), nxt(N_FB), nxt(0),
            pl.BlockSpec((None, 3, FB), lambda j, i: (j, 0, 0)),
        ],
        out_specs=[
            pl.BlockSpec((2, None, TM, FB), lambda j, i: (0, j, i, 0)),
            pl.BlockSpec((None, 8, FB), lambda j, i: (j, 0, 0)),
        ],
        out_shape=[jax.ShapeDtypeStruct((2, N_FB, LP, FB), BF16), jax.ShapeDtypeStruct((N_FB, 8, FB), F32)],
        compiler_params=_params(("parallel", "arbitrary")),
    )(ug, ug, dhid, ug, ug, ug, dhid, cw4)


def _gate_fwd(bc, ba, hin, bgate):
    def body(bc_ref, ba_ref, gc_ref, ga_ref, b_ref, o_ref):
        b = b_ref[...]
        sc = jax.nn.sigmoid(gc_ref[...].astype(F32) + b[0:1])
        sa = jax.nn.sigmoid(ga_ref[...].astype(F32) + b[1:2])
        o_ref[...] = (sc * bc_ref[...].astype(F32) + sa * ba_ref[...].astype(F32)).astype(BF16)

    def tile(s):
        return pl.BlockSpec((None, TM, D), lambda i: (s, i, 0))

    return pl.pallas_call(
        body,
        name="gate_fwd",
        grid=(LP // TM,),
        in_specs=[_row_spec(), _row_spec(), tile(6), tile(7), _vec_spec(2, D)],
        out_specs=_row_spec(),
        out_shape=jax.ShapeDtypeStruct((LP, D), BF16),
        compiler_params=_params(("parallel",)),
    )(bc, ba, hin, hin, bgate)


def _gate_bwd(dm, bc, ba, hin, bgate):
    def body(dm_ref, bc_ref, ba_ref, gc_ref, ga_ref, b_ref, dbc_ref, dba_ref, dg_ref, db_ref):
        i = pl.program_id(0)

        @pl.when(i == 0)
        def _():
            db_ref[...] = jnp.zeros_like(db_ref)

        b = b_ref[...]
        dm = dm_ref[...].astype(F32)
        sc = jax.nn.sigmoid(gc_ref[...].astype(F32) + b[0:1])
        sa = jax.nn.sigmoid(ga_ref[...].astype(F32) + b[1:2])
        dbc_ref[...] = (dm * sc).astype(BF16)
        dba_ref[...] = (dm * sa).astype(BF16)
        dgc = dm * bc_ref[...].astype(F32) * sc * (1.0 - sc)
        dga = dm * ba_ref[...].astype(F32) * sa * (1.0 - sa)
        dg_ref[0] = dgc.astype(BF16)
        dg_ref[1] = dga.astype(BF16)
        db_ref[0:1, :] += jnp.sum(dgc, axis=0, keepdims=True)
        db_ref[1:2, :] += jnp.sum(dga, axis=0, keepdims=True)

    def tile(s):
        return pl.BlockSpec((None, TM, D), lambda i: (s, i, 0))

    return pl.pallas_call(
        body,
        name="gate_bwd",
        grid=(LP // TM,),
        in_specs=[_row_spec(), _row_spec(), _row_spec(), tile(6), tile(7), _vec_spec(2, D)],
        out_specs=[_row_spec(), _row_spec(), pl.BlockSpec((2, TM, D), lambda i: (3, i, 0)), _vec_spec(8, D)],
        out_shape=[jax.ShapeDtypeStruct((LP, D), BF16)] * 2
        + [jax.ShapeDtypeStruct((N_DEV, LP, D), BF16), jax.ShapeDtypeStruct((8, D), F32)],
        compiler_params=_params(("arbitrary",)),
    )(dm, bc, ba, hin, hin, bgate)


Z_LINEAR = 30.0


def _softplus(z):
    return jnp.maximum(z, jnp.log(1.0 + jnp.exp(jnp.minimum(z, Z_LINEAR))))


def _cumsum_matrix(inclusive, reverse):
    r = lax.broadcasted_iota(jnp.int32, (QB, 2 * QB), 0)
    c = lax.broadcasted_iota(jnp.int32, (QB, 2 * QB), 1)
    if reverse:
        tri = r > c
    elif inclusive:
        tri = r <= c
    else:
        tri = r < c
    return jnp.where((c >= QB) | tri, 1.0, 0.0).astype(BF16)


def _split_dot(x, m2):
    bits = lax.bitcast_convert_type(x, jnp.uint32) & jnp.uint32(0xFFFF0000)
    hi = lax.bitcast_convert_type(bits, F32)
    return _dot(jnp.concatenate([hi.astype(BF16), (x - hi).astype(BF16)], axis=1), m2)


def _stack_heads(x):
    return jnp.concatenate(_split_heads(x), axis=0)


def _block_mask(i, j, row0):
    row = lax.broadcasted_iota(jnp.int32, (AQ - row0, QB), 0) + (i * AQ + row0)
    col = lax.broadcasted_iota(jnp.int32, (AQ - row0, QB), 1) + j * QB
    return (col < row) & (col >= PAD)


def _key_block(ref, j):
    return ref[pl.ds(pl.multiple_of(j * QB, QB), QB), :]


def _split_heads(x):
    head_a = lax.broadcasted_iota(jnp.int32, x.shape, 1) < HEAD_LANES
    zero = jnp.zeros_like(x)
    return jnp.where(head_a, x, zero), jnp.where(head_a, zero, x)


def _attn_fwd(hin, ex_arrs, gathers):
    ne = len(ex_arrs)
    npair, nq = D // QB, LP // AQ

    def body(*refs):
        q_ref, k_ref, v_ref = refs[:3]
        o_ref, lt_ref = refs[3 + ne:5 + ne]
        c_sc, acc_sc = refs[5 + 2 * ne:7 + 2 * ne]
        p, i = pl.program_id(0), pl.program_id(1)

        def copies():
            return _exchange_copies(refs[3:3 + ne], refs[5 + ne:5 + 2 * ne], gathers, *refs[7 + 2 * ne:])

        @pl.when((p == 0) & (i == 0))
        def _():
            for cp in copies():
                cp.start()

        um = _cumsum_matrix(False, True)
        um2 = jnp.concatenate([um, um], axis=0)
        q = (q_ref[...].astype(F32) * SCALE).astype(BF16)
        c_sc[...] = jnp.zeros_like(c_sc)
        acc_sc[...] = jnp.zeros_like(acc_sc)

        def step(j, masked, row0=0):
            rows = slice(row0, AQ)
            z2 = _dot(q[rows], _stack_heads(_key_block(k_ref, j)), NT)
            mask = _block_mask(i, j, row0) if masked else None
            a2 = []
            for hd in range(2):
                z = z2[:, hd * QB:(hd + 1) * QB]
                sp = _softplus(z)
                r = _split_dot(jnp.where(mask, sp, 0.0) if masked else sp, um2)
                a = jnp.exp(z - sp - c_sc[hd, rows] - r[:, :QB])
                if masked:
                    a = jnp.where(mask, a, 0.0)
                a2.append(a.astype(BF16))
                c_sc[hd, rows] += r[:, QB:]
            acc_sc[rows] += _dot(jnp.concatenate(a2, axis=1), _stack_heads(_key_block(v_ref, j)))

        for t in reversed(range(KPQ)):
            step(KPQ * i + t, True, t * QB)

        groups = jnp.maximum(i - 1, 0)

        @pl.loop(0, groups // 2)
        def _(t):
            for u in range(2 * KPQ):
                step(KPQ * (i - 2 * t) - 1 - u, False)

        @pl.when(groups % 2 == 1)
        def _():
            for u in range(KPQ):
                step(2 * KPQ - 1 - u, False)

        @pl.when(i > 0)
        def _():
            for u in reversed(range(1, KPQ)):
                step(u, False)
            step(0, True)

        head_a = lax.broadcasted_iota(jnp.int32, (AQ, QB), 1) < HEAD_LANES
        o_ref[...] = acc_sc[...].astype(BF16)
        lt_ref[...] = jnp.where(head_a, c_sc[0], c_sc[1])

        @pl.when((p == npair - 1) & (i == nq - 1))
        def _():
            for cp in copies():
                cp.wait()

    def seq(s):
        return pl.BlockSpec((None, LP, QB), lambda p, i: (s, 0, p))

    return pl.pallas_call(
        body,
        name="attn_fwd",
        grid=(npair, nq),
        in_specs=[pl.BlockSpec((None, AQ, QB), lambda p, i: (3, i, p)), seq(4), seq(5)] + [ANY_SPEC] * ne,
        out_specs=[pl.BlockSpec((AQ, QB), lambda p, i: (i, p))] * 2 + [ANY_SPEC] * ne,
        out_shape=[jax.ShapeDtypeStruct((LP, D), BF16), jax.ShapeDtypeStruct((LP, D), F32)]
        + _exchange_shapes(ex_arrs, gathers),
        scratch_shapes=[pltpu.VMEM((2, AQ, QB), F32), pltpu.VMEM((AQ, QB), F32)] + _exchange_sems(ne),
        compiler_params=_params(("arbitrary", "arbitrary")),
    )(hin, hin, hin, *ex_arrs)


def _attn_bwd(hin, do, lt, dhin, ex_arrs, gathers):
    ne = len(ex_arrs)
    npair, nq = D // QB, LP // AQ

    def body(*refs):
        q_ref, k_ref, v_ref, do_ref, lt_ref = refs[:5]
        out_ref = refs[6 + ne]
        psp_sc, pg_sc, dq_sc, dk_acc, dv_acc = refs[7 + 2 * ne:12 + 2 * ne]
        p, i = pl.program_id(0), pl.program_id(1)

        def copies():
            return _exchange_copies(refs[6:6 + ne], refs[7 + ne:7 + 2 * ne], gathers, *refs[12 + 2 * ne:])

        @pl.when((p == 0) & (i == 0))
        def _():
            for cp in copies():
                cp.start()

        @pl.when(i == 0)
        def _():
            dk_acc[...] = jnp.zeros_like(dk_acc)
            dv_acc[...] = jnp.zeros_like(dv_acc)

        um_sp = _cumsum_matrix(True, False)
        um_sp2 = jnp.concatenate([um_sp, um_sp], axis=0)
        um_g = _cumsum_matrix(False, False)
        q = (q_ref[...].astype(F32) * SCALE).astype(BF16)
        do = do_ref[...]
        q_t, do_t = q.T, do.T
        head_a = lax.broadcasted_iota(jnp.int32, (AQ, QB), 1) < HEAD_LANES
        dim_head_a = lax.broadcasted_iota(jnp.int32, (QB, QB), 0) < HEAD_LANES
        lt = lt_ref[...]
        lt_sw = pltpu.roll(lt, HEAD_LANES, 1)
        totals = (jnp.where(head_a, lt, lt_sw), jnp.where(head_a, lt_sw, lt))
        psp_sc[...] = jnp.zeros_like(psp_sc)
        pg_sc[...] = jnp.zeros_like(pg_sc)
        dq_sc[...] = jnp.zeros_like(dq_sc)

        def step(j, masked, row0=0):
            rows = slice(row0, AQ)
            k2 = _stack_heads(_key_block(k_ref, j))
            z2 = _dot(q[rows], k2, NT)
            da2 = _dot(do[rows], _stack_heads(_key_block(v_ref, j)), NT)
            mask = _block_mask(i, j, row0) if masked else None
            a2, dz2 = [], []
            for hd in range(2):
                z = z2[:, hd * QB:(hd + 1) * QB]
                sp = _softplus(z)
                r = _split_dot(jnp.where(mask, sp, 0.0) if masked else sp, um_sp2)
                a = jnp.exp(z - sp - (totals[hd][rows] - psp_sc[hd, rows] - r[:, :QB]))
                if masked:
                    a = jnp.where(mask, a, 0.0)
                g = a * da2[:, hd * QB:(hd + 1) * QB]
                rg = _dot(g.astype(BF16), um_g)
                dz = g - jnp.exp(z - sp) * (g + pg_sc[hd, rows] + rg[:, :QB])
                if masked:
                    dz = jnp.where(mask, dz, 0.0)
                a2.append(a.astype(BF16))
                dz2.append(dz.astype(BF16))
                psp_sc[hd, rows] += r[:, QB:]
                pg_sc[hd, rows] += rg[:, QB:]
            dz2 = jnp.concatenate(dz2, axis=1)
            dq_sc[rows] += _dot(dz2, k2)
            dk2 = _dot(q_t[:, rows], dz2)
            dv2 = _dot(do_t[:, rows], jnp.concatenate(a2, axis=1))
            dk_acc[j] += jnp.where(dim_head_a, dk2[:, :QB], dk2[:, QB:])
            dv_acc[j] += jnp.where(dim_head_a, dv2[:, :QB], dv2[:, QB:])

        @pl.when(i > 0)
        def _():
            step(0, True)
            for u in range(1, KPQ):
                step(u, False)

        groups = jnp.maximum(i - 1, 0)

        @pl.loop(0, groups // 2)
        def _(t):
            for u in range(2 * KPQ):
                step(KPQ * (2 * t + 1) + u, False)

        @pl.when(groups % 2 == 1)
        def _():
            for u in range(KPQ):
                step(KPQ * (i - 1) + u, False)

        for t in range(KPQ):
            step(KPQ * i + t, True, t * QB)

        out_ref[0, pl.ds(pl.multiple_of(i * AQ, AQ), AQ), :] = (dq_sc[...] * SCALE).astype(BF16)

        @pl.when(i == nq - 1)
        def _():
            @pl.loop(0, LP // QB)
            def _(b):
                keys = pl.ds(pl.multiple_of(b * QB, QB), QB)
                out_ref[1, keys, :] = dk_acc[b].T.astype(BF16)
                out_ref[2, keys, :] = dv_acc[b].T.astype(BF16)

        @pl.when((p == npair - 1) & (i == nq - 1))
        def _():
            for cp in copies():
                cp.wait()

    def seq(s):
        return pl.BlockSpec((None, LP, QB), lambda p, i: (s, 0, p))

    blk = pl.BlockSpec((AQ, QB), lambda p, i: (i, p))
    return pl.pallas_call(
        body,
        name="attn_bwd",
        grid=(npair, nq),
        in_specs=[pl.BlockSpec((None, AQ, QB), lambda p, i: (3, i, p)), seq(4), seq(5), blk, blk]
        + [ANY_SPEC] * (1 + ne),
        out_specs=[pl.BlockSpec((3, LP, QB), lambda p, i: (1, 0, p))] + [ANY_SPEC] * ne,
        out_shape=[jax.ShapeDtypeStruct((N_DEV, LP, D), BF16)] + _exchange_shapes(ex_arrs, gathers),
        input_output_aliases={5: 0},
        scratch_shapes=[pltpu.VMEM((2, AQ, QB), F32)] * 2 + [pltpu.VMEM((AQ, QB), F32)]
        + [pltpu.VMEM((LP // QB, QB, QB), F32)] * 2 + _exchange_sems(ne),
        compiler_params=_params(("arbitrary", "arbitrary")),
    )(hin, hin, hin, do, lt, dhin, *ex_arrs)


def _adamw_math(w, g, m, v):
    m_new = ADAM_B1 * m + (1.0 - ADAM_B1) * g
    v_new = ADAM_B2 * v + (1.0 - ADAM_B2) * jnp.square(g)
    m_hat = m_new / (1.0 - ADAM_B1 ** ADAM_STEP)
    v_hat = v_new / (1.0 - ADAM_B2 ** ADAM_STEP)
    return -ADAM_LR * (m_hat / (jnp.sqrt(v_hat) + ADAM_EPS) + ADAM_WD * w), m_new, v_new


def _adamw_small(ws, gs, ms, vs):
    n = len(ws)

    def body(*refs):
        for t in range(n):
            w_ref, g_ref, m_ref, v_ref = (refs[s * n + t] for s in range(4))
            d_ref, nm_ref, nv_ref = (refs[(4 + s) * n + t] for s in range(3))
            d_ref[...], nm_ref[...], nv_ref[...] = _adamw_math(w_ref[...], g_ref[...], m_ref[...], v_ref[...])

    vmem = pl.BlockSpec(memory_space=pltpu.VMEM)
    res = pl.pallas_call(
        body,
        name="adamw_small",
        in_specs=[vmem] * (4 * n),
        out_specs=[vmem] * (3 * n),
        out_shape=[jax.ShapeDtypeStruct(w.shape, F32) for w in ws] * 3,
    )(*ws, *gs, *ms, *vs)
    return res[:n], res[n:2 * n], res[2 * n:]


def _adamw(pieces, w, m, v, *, rows, row_off, tr, name):
    npieces, _, cols = pieces.shape
    ob = row_off // tr

    def body(p_ref, w_ref, m_ref, v_ref, g_ref, d_ref, nm_ref, nv_ref):
        g = p_ref[0].astype(F32)
        for s in range(1, npieces):
            g = g + p_ref[s].astype(F32)
        g_ref[...] = g
        d_ref[...], nm_ref[...], nv_ref[...] = _adamw_math(w_ref[...], g, m_ref[...], v_ref[...])

    spec = pl.BlockSpec((tr, cols), lambda i: (i, 0))
    return pl.pallas_call(
        body,
        name=name,
        grid=(rows // tr,),
        in_specs=[pl.BlockSpec((npieces, tr, cols), lambda i: (0, ob + i, 0)), spec, spec, spec],
        out_specs=[spec] * 4,
        out_shape=[jax.ShapeDtypeStruct((rows, cols), F32)] * 4,
        compiler_params=_params(("parallel",)),
    )(pieces, w, m, v)


def _sum_pieces(pieces, name):
    npieces, rows, cols = pieces.shape

    def body(p_ref, o_ref):
        acc = p_ref[0]
        for s in range(1, npieces):
            acc = acc + p_ref[s]
        o_ref[...] = acc

    return pl.pallas_call(
        body,
        name=name,
        in_specs=[pl.BlockSpec(memory_space=pltpu.VMEM)],
        out_specs=pl.BlockSpec(memory_space=pltpu.VMEM),
        out_shape=jax.ShapeDtypeStruct((rows, cols), pieces.dtype),
    )(pieces)


SMALL_ROWS = 48
CWF_PAD = 384
GRAD_ROWS = 72
G_META, G_PRE_MIX, G_MID, G_POST_FFN, G_CW_MIX, G_B_GATE, G_CW_FFN, G_LOSS = 0, 16, 24, 32, 40, 48, 56, 17


def kernel(x, meta_tokens, g_pre_mix, w_in, conv_w_mix, w_proj_conv, w_proj_attn, b_gate, w_out, g_post_mix, g_pre_ffn, w_up_gate, conv_w_ffn, w_down, g_post_ffn, loss_target, m_meta_tokens, m_g_pre_mix, m_w_in, m_conv_w_mix, m_w_proj_conv, m_w_proj_attn, m_b_gate, m_w_out, m_g_post_mix, m_g_pre_ffn, m_w_up_gate, m_conv_w_ffn, m_w_down, m_g_post_ffn, v_meta_tokens, v_g_pre_mix, v_w_in, v_conv_w_mix, v_w_proj_conv, v_w_proj_attn, v_b_gate, v_w_out, v_g_post_mix, v_g_pre_ffn, v_w_up_gate, v_conv_w_ffn, v_w_down, v_g_post_ffn):
    me = 4 * lax.axis_index("x") + 2 * lax.axis_index("y") + lax.axis_index("c")

    def rows_to(a, n):
        return jnp.pad(a, ((0, n - a.shape[0]), (0, 0)))

    small_shard = jnp.concatenate(
        [meta_tokens, rows_to(conv_w_mix[0], 8), rows_to(b_gate[0], 8),
         rows_to(jnp.pad(conv_w_ffn[0], ((0, 0), (0, CWF_PAD - R_DOWN))).reshape(9, 128), 16)], axis=0)
    wshard = jnp.concatenate([w_proj_conv[0], w_proj_attn[0], w_out[0], w_down[0]], axis=0).astype(BF16)
    w_in_all, small_all = _gather_two_level([w_in[0].astype(BF16), small_shard], "gather_in")

    def unshard(rows):
        return rows.transpose(1, 0, 2).reshape(rows.shape[1], N_DEV * rows.shape[2])

    meta = unshard(small_all[:, 0:16])
    cw_mix = unshard(small_all[:, 16:19])
    bgate = unshard(small_all[:, 24:26])
    cw_ffn = unshard(small_all[:, 32:41].reshape(N_DEV, 3, CWF_PAD)[:, :, :R_DOWN])
    cw4 = cw_ffn.reshape(3, N_FB, FB).transpose(1, 0, 2)
    h0 = jnp.concatenate([jnp.zeros((PAD, D), F32), meta, x[0]], axis=0)

    xn1 = _rms_fwd(h0, g_pre_mix, "rms_pre_mix")
    hin = _mm(xn1, w_in_all, w_rows=D, trans_w=False, out_dtype=BF16, name="mm_in")
    y_conv = _conv_mix_fwd(hin, cw_mix)
    o, lt, wpack, w_ug = _attn_fwd(hin, [wshard, w_up_gate[0].astype(BF16)], (True, True))
    w_pc = wpack[:, O_PC:O_PA].reshape(1, D, D)
    w_pa = wpack[:, O_PA:O_OUT].reshape(1, D, D)
    w_o = wpack[:, O_OUT:O_DOWN].reshape(1, D, D)
    w_dn = wpack[:, O_DOWN:].reshape(N_FB, FB, D)
    bc = _mm(y_conv, w_pc, w_rows=D, trans_w=False, out_dtype=BF16, name="mm_proj_conv")[0]
    ba = _mm(o, w_pa, w_rows=D, trans_w=False, out_dtype=BF16, name="mm_proj_attn")[0]
    merged = _gate_fwd(bc, ba, hin, bgate)
    mix = _mm(merged, w_o, w_rows=D, trans_w=False, out_dtype=F32, name="mm_out")[0]
    h1, xn2 = _resid_rms(h0, mix, g_post_mix, g_pre_ffn)
    ug = _mm(xn2, w_ug, w_rows=D, trans_w=False, out_dtype=BF16, name="mm_up_gate")
    hid = _ffn_act_fwd(ug, cw4)
    ffn = _mm_sum(hid, w_dn, w_rows=FB, trans_w=False, out_dtype=F32, name="mm_down")
    dout, dffn, loss8, dg_post_ffn = _loss_head(h1, ffn, g_post_ffn, loss_target[0])

    dhid = _mm(dffn, w_dn, w_rows=FB, trans_w=True, out_dtype=BF16, name="mm_down_dx")
    gw_dn = _mm_tn(hid, dffn, nb=N_FB, out_dtype=BF16, name="mm_down_dw")
    dug, dcw4 = _ffn_act_bwd(ug, dhid, cw4)
    dug = dug.reshape(2 * N_FB, LP, FB)
    dxn2 = _mm_sum(dug, w_ug, w_rows=D, trans_w=True, out_dtype=F32, name="mm_up_gate_dx")
    gw_ug = _mm_tn(xn2, dug, nb=N_DEV, out_dtype=BF16, name="mm_up_gate_dw")
    dh1, dmix, dg_mid = _mid_bwd(dout, h1, dxn2, mix, g_post_mix, g_pre_ffn)
    dmerged = _mm(dmix, w_o, w_rows=D, trans_w=True, out_dtype=BF16, name="mm_out_dx")[0]
    gw_out = _mm_tn(merged, dmix, nb=1, out_dtype=BF16, name="mm_out_dw")
    dbc, dba, dhin, db_gate = _gate_bwd(dmerged, bc, ba, hin, bgate)
    dy_conv = _mm(dbc, w_pc, w_rows=D, trans_w=True, out_dtype=BF16, name="mm_proj_conv_dx")[0]
    gw_pc = _mm_tn(y_conv, dbc, nb=1, out_dtype=BF16, name="mm_proj_conv_dw")
    do = _mm(dba, w_pa, w_rows=D, trans_w=True, out_dtype=BF16, name="mm_proj_attn_dx")[0]
    gw_pa = _mm_tn(o, dba, nb=1, out_dtype=BF16, name="mm_proj_attn_dw")
    dhin, dcw_mix = _conv_mix_bwd(hin, dy_conv, cw_mix, dhin)
    gpack = jnp.concatenate(
        [gw_pc.reshape(N_DEV, R_PROJ, D), gw_pa.reshape(N_DEV, R_PROJ, D), gw_out.reshape(N_DEV, R_PROJ, D),
         gw_dn.reshape(N_DEV, R_DOWN, D)], axis=1)
    dcw_ffn = dcw4[:, :3].transpose(1, 0, 2).reshape(3, D_FF)
    small_a = jnp.concatenate(
        [dg_mid, dg_post_ffn, dcw_mix, db_gate,
         jnp.pad(dcw_ffn.reshape(-1), (0, 16 * D - 3 * D_FF)).reshape(16, D)], axis=0)
    dhin, rpack, rug, rsmall_a = _attn_bwd(hin, do, lt, dhin, [gpack, gw_ug, small_a], (False, False, True))
    gw_half = _mm_tn_half(xn1, dhin, nb=N_DEV, out_dtype=BF16, name="mm_in_dw_a", half=0)
    sems_a = _scatter_start(gw_half, lax.empty(gw_half.shape, BF16), 0, "scatter_in_start_a")
    gw_in = _mm_tn_half(xn1, dhin, nb=N_DEV, out_dtype=BF16, name="mm_in_dw_b", half=1, into=sems_a[2])
    sems_b = _scatter_start(gw_in, sems_a[3], 1, "scatter_in_start_b")
    dxn1 = _mm_sum(dhin, w_in_all, w_rows=D, trans_w=True, out_dtype=F32, name="mm_in_dx", after=sems_b[4])
    dh0, dg_pre_mix = _first_bwd(dh1, h0, dxn1, g_pre_mix)
    small_b = jnp.concatenate(
        [dh0[PAD:OFF], lax.dynamic_update_slice(dg_pre_mix, loss8[0:1], (G_LOSS - G_PRE_MIX, 0))], axis=0)

    def big(pieces, w, m, v, rows, row_off, tr, name):
        g, d, nm, nv = _adamw(pieces, w[0], m[0], v[0], rows=rows, row_off=row_off, tr=tr, name=name)
        return g[None], d[None], nm[None], nv[None]

    r_pc = big(rpack, w_proj_conv, m_w_proj_conv, v_w_proj_conv, R_PROJ, O_PC, R_PROJ, "adamw_proj_conv")
    r_pa = big(rpack, w_proj_attn, m_w_proj_attn, v_w_proj_attn, R_PROJ, O_PA, R_PROJ, "adamw_proj_attn")
    r_out = big(rpack, w_out, m_w_out, v_w_out, R_PROJ, O_OUT, R_PROJ, "adamw_out")
    r_dn = big(rpack, w_down, m_w_down, v_w_down, R_DOWN, O_DOWN, 32, "adamw_down")
    r_ug = big(rug, w_up_gate, m_w_up_gate, v_w_up_gate, D, 0, 256, "adamw_up_gate")
    (rsmall_b,) = _exchange([small_b], (True,), "gather_small_grads",
                            after=(r_pc[1], r_pa[1], r_out[1], r_dn[1], r_ug[1]))
    gs = _sum_pieces(jnp.concatenate([rsmall_b, rsmall_a], axis=1), "sum_small_grads")
    grad_x = dh0[OFF:]
    loss = gs[G_LOSS, 0]

    def cols(a, width):
        return lax.dynamic_slice_in_dim(a, me * width, width, axis=1)

    g_meta = cols(gs[G_META:G_META + N_META], 128)
    g_gpm, g_gff = gs[G_PRE_MIX:G_PRE_MIX + 1], gs[G_POST_FFN:G_POST_FFN + 1]
    g_gpo, g_gpf = gs[G_MID:G_MID + 1], gs[G_MID + 1:G_MID + 2]
    g_cwm = cols(gs[G_CW_MIX:G_CW_MIX + 3], 128)[None]
    g_bg = cols(gs[G_B_GATE:G_B_GATE + 2], 128)[None]
    g_cwf = cols(gs[G_CW_FFN:G_CW_FFN + 9].reshape(-1)[:3 * D_FF].reshape(3, D_FF), R_DOWN)[None]

    small_w = [meta_tokens, g_pre_mix, conv_w_mix, b_gate, g_post_mix, g_pre_ffn, conv_w_ffn, g_post_ffn]
    small_g = [g_meta, g_gpm, g_cwm, g_bg, g_gpo, g_gpf, g_cwf, g_gff]
    small_m = [m_meta_tokens, m_g_pre_mix, m_conv_w_mix, m_b_gate, m_g_post_mix, m_g_pre_ffn, m_conv_w_ffn, m_g_post_ffn]
    small_v = [v_meta_tokens, v_g_pre_mix, v_conv_w_mix, v_b_gate, v_g_post_mix, v_g_pre_ffn, v_conv_w_ffn, v_g_post_ffn]

    s_g = small_g
    s_d, s_m, s_v = _adamw_small(small_w, small_g, small_m, small_v)

    gw_done, landed = _scatter_wait(sems_a[0], sems_a[1], sems_b[2], sems_b[3], (s_d[0],), 0, "scatter_in_wait_a")
    gw_done, landed = _scatter_wait(sems_b[0], sems_b[1], gw_done, landed, (), 1, "scatter_in_wait_b")
    rin = lax.dynamic_update_index_in_dim(landed, lax.dynamic_index_in_dim(gw_done, me, 0, keepdims=False), me, 0)
    r_in = big(rin, w_in, m_w_in, v_w_in, D, 0, 256, "adamw_in")

    def ordered(k, smalls):
        meta, gpm, cwm, bg, gpo, gpf, cwf, gff = smalls
        return [meta, gpm, r_in[k], cwm, r_pc[k], r_pa[k], bg, r_out[k], gpo, gpf, r_ug[k], cwf, r_dn[k], gff]

    return (loss, grad_x[None], *ordered(0, s_g), *ordered(1, s_d), *ordered(2, s_m), *ordered(3, s_v))
```
